```python
import jax
import jax.numpy as jnp
from jax import lax
import numpy as np

D_MODEL = 2048
BATCH = 2
SEQ = 8192
DEPTH = 4

N_MIXERS = 3
D_FF = 5632
HALF_STEP = 0.5
NORM_EPS = 1e-6
NEG_INF = -1e30
ROPE_THETA = 500000.0

A_HEADS = 16
A_HEAD_DIM = 128
A_ROT_DIM = A_HEAD_DIM // 4
A_PATTERNS = ((128, 1), (512, 4), (2048, 16))
A_GROUPS = len(A_PATTERNS)
A_BLOCK = 128

B_HEADS = 16
B_Q_RANK = 1536
B_KV_RANK = 512
B_NOPE_DIM = 128
B_ROPE_DIM = 64
B_V_DIM = 128
B_BLOCK = 128

C_WIDTH = D_MODEL
C_CONV = 3

kernel_name = 'hybrid_dilated_mla_shortconv_macaron'


def rmsnorm(x, gain):
    xf = x.astype(jnp.float32)
    y = xf * lax.rsqrt(jnp.mean(xf * xf, axis=-1, keepdims=True) + NORM_EPS)
    return (y * gain.astype(jnp.float32)).astype(x.dtype)


def swiglu(x, w13, w2):
    gate, up = jnp.split(x @ w13, 2, axis=-1)
    return (jax.nn.silu(gate) * up) @ w2


def rope(t, positions, rot_dim):
    half = rot_dim // 2
    inv_freq = ROPE_THETA ** (-jnp.arange(half, dtype=jnp.float32) / half)
    ang = positions.astype(jnp.float32)[:, :, None, None] * inv_freq
    cos, sin = jnp.cos(ang), jnp.sin(ang)
    tf = t.astype(jnp.float32)
    t1, t2, rest = tf[..., :half], tf[..., half:rot_dim], tf[..., rot_dim:]
    out = jnp.concatenate([t1 * cos - t2 * sin, t2 * cos + t1 * sin, rest], axis=-1)
    return out.astype(t.dtype)


def dilated_branch(q, k, v, span, dilation):
    bsz, seq, heads, dh = q.shape
    sub = seq // dilation
    nb = -(-sub // A_BLOCK)
    pad = nb * A_BLOCK - sub

    def to_blocks(a):
        a = a.reshape(bsz, sub, dilation, heads, dh).transpose(0, 2, 1, 3, 4)
        a = jnp.pad(a, ((0, 0), (0, 0), (0, pad), (0, 0), (0, 0)))
        return a.reshape(bsz, dilation, nb, A_BLOCK, heads, dh)

    def with_prev(a):
        prev = jnp.pad(a, ((0, 0), (0, 0), (1, 0), (0, 0), (0, 0), (0, 0)))[:, :, :-1]
        return jnp.concatenate([prev, a], axis=3)

    qb = to_blocks(q)
    kw = with_prev(to_blocks(k))
    vw = with_prev(to_blocks(v))
    s = jnp.einsum('bcnqhd,bcnkhd->bcnhqk', qb, kw).astype(jnp.float32) * (dh ** -0.5)
    qi = jnp.arange(A_BLOCK)[:, None] + A_BLOCK
    ki = jnp.arange(2 * A_BLOCK)[None, :]
    dist = qi - ki
    band = (dist >= 0) & (dist <= span)
    has_prev = (jnp.arange(nb) > 0)[:, None, None] | (ki >= A_BLOCK)[None]
    mask = band[None] & has_prev
    s = jnp.where(mask[:, None], s, NEG_INF)
    m = jnp.max(s, axis=-1, keepdims=True)
    p = jnp.exp(s - m)
    l = jnp.sum(p, axis=-1, keepdims=True)
    o = jnp.einsum('bcnhqk,bcnkhd->bcnqhd', p / l, vw.astype(jnp.float32))
    lse = (m + jnp.log(l))[..., 0]

    o = o.reshape(bsz, dilation, nb * A_BLOCK, heads, dh)[:, :, :sub]
    o = o.transpose(0, 2, 1, 3, 4).reshape(bsz, seq, heads, dh)
    lse = lse.transpose(0, 1, 2, 4, 3).reshape(bsz, dilation, nb * A_BLOCK, heads)[:, :, :sub]
    lse = lse.transpose(0, 2, 1, 3).reshape(bsz, seq, heads)
    return o, lse


def dilated_mixer(h, positions, w_qkv, w_o):
    bsz, seq, _ = h.shape
    qkv = (h @ w_qkv).reshape(bsz, seq, A_GROUPS, 3, A_HEADS, A_HEAD_DIM)
    outs, lses = [], []
    for g, (window, dilation) in enumerate(A_PATTERNS):
        q = rope(qkv[:, :, g, 0], positions, A_ROT_DIM)
        k = rope(qkv[:, :, g, 1], positions, A_ROT_DIM)
        o, lse = dilated_branch(q, k, qkv[:, :, g, 2], window // dilation, dilation)
        outs.append(o)
        lses.append(lse)
    wts = jax.nn.softmax(jnp.stack(lses), axis=0)[..., None]
    o = jnp.sum(wts * jnp.stack(outs), axis=0)
    return o.reshape(bsz, seq, A_HEADS * A_HEAD_DIM).astype(h.dtype) @ w_o


def mla_mixer(h, positions, w_in, q_norm, w_uq, kv_norm, w_ukv, w_o):
    bsz, seq, _ = h.shape
    c_q, c_kv, k_rope = jnp.split(h @ w_in, [B_Q_RANK, B_Q_RANK + B_KV_RANK], axis=-1)
    q = (rmsnorm(c_q, q_norm) @ w_uq).reshape(bsz, seq, B_HEADS, B_NOPE_DIM + B_ROPE_DIM)
    q_nope = q[..., :B_NOPE_DIM]
    q_rope = rope(q[..., B_NOPE_DIM:], positions, B_ROPE_DIM)
    k_rope = rope(k_rope[:, :, None, :], positions, B_ROPE_DIM)[:, :, 0]
    kv = (rmsnorm(c_kv, kv_norm) @ w_ukv).reshape(bsz, seq, B_HEADS, B_NOPE_DIM + B_V_DIM)
    k_nope, v = kv[..., :B_NOPE_DIM], kv[..., B_NOPE_DIM:]
    vf = v.astype(jnp.float32)
    scale = (B_NOPE_DIM + B_ROPE_DIM) ** -0.5
    nq = seq // B_BLOCK
    key_pos = jnp.arange(seq)

    def blocks(a):
        return a.reshape((bsz, nq, B_BLOCK) + a.shape[2:]).swapaxes(0, 1)

    def attend(args):
        start, qn, qr = args
        s = (jnp.einsum('bqhd,bkhd->bhqk', qn, k_nope)
             + jnp.einsum('bqhd,bkd->bhqk', qr, k_rope)).astype(jnp.float32) * scale
        causal = (start + jnp.arange(B_BLOCK))[:, None] >= key_pos[None, :]
        p = jax.nn.softmax(jnp.where(causal, s, NEG_INF), axis=-1)
        return jnp.einsum('bhqk,bkhd->bqhd', p, vf)

    o = lax.map(attend, (jnp.arange(nq) * B_BLOCK, blocks(q_nope), blocks(q_rope)))
    o = o.swapaxes(0, 1).reshape(bsz, seq, B_HEADS * B_V_DIM)
    return o.astype(h.dtype) @ w_o


def short_conv_mixer(h, w_in, conv_w, w_out):
    b_gate, c_gate, u = jnp.split(h @ w_in, 3, axis=-1)
    z = lax.conv_general_dilated(c_gate * u, conv_w[:, None, :], window_strides=(1,),
                                 padding=[(C_CONV - 1, 0)],
                                 dimension_numbers=('NWC', 'WIO', 'NWC'),
                                 feature_group_count=C_WIDTH)
    return (b_gate * z) @ w_out


def setup_inputs(seed: int = 0) -> dict:
    key = jax.random.key(seed)
    keys = jax.random.split(key, 128)
    counter = [0]

    def nk():
        counter[0] += 1
        return keys[counter[0] - 1]

    def dense(fan_in, fan_out):
        return jax.random.normal(nk(), (fan_in, fan_out), jnp.float32) * fan_in ** -0.5

    def gain(n):
        return 1.0 + 0.02 * jax.random.normal(nk(), (n,), jnp.float32)

    inp = {}
    inp['x'] = jax.random.normal(nk(), (BATCH, SEQ, D_MODEL), jnp.float32)
    offset = jax.random.randint(nk(), (BATCH, 1), 0, 4096, dtype=jnp.int32)
    inp['positions'] = offset + jnp.arange(SEQ, dtype=jnp.int32)[None, :]
    for i in range(DEPTH):
        p = 'l%d_' % i
        inp[p + 'ffn1_norm'] = gain(D_MODEL)
        inp[p + 'ffn1_w13'] = dense(D_MODEL, 2 * D_FF)
        inp[p + 'ffn1_w2'] = dense(D_FF, D_MODEL)
        inp[p + 'mix_norm'] = gain(D_MODEL)
        kind = i % N_MIXERS
        if kind == 0:
            inp[p + 'a_w_qkv'] = dense(D_MODEL, A_GROUPS * 3 * A_HEADS * A_HEAD_DIM)
            inp[p + 'a_w_o'] = dense(A_HEADS * A_HEAD_DIM, D_MODEL)
        elif kind == 1:
            inp[p + 'b_w_in'] = dense(D_MODEL, B_Q_RANK + B_KV_RANK + B_ROPE_DIM)
            inp[p + 'b_q_norm'] = gain(B_Q_RANK)
            inp[p + 'b_w_uq'] = dense(B_Q_RANK, B_HEADS * (B_NOPE_DIM + B_ROPE_DIM))
            inp[p + 'b_kv_norm'] = gain(B_KV_RANK)
            inp[p + 'b_w_ukv'] = dense(B_KV_RANK, B_HEADS * (B_NOPE_DIM + B_V_DIM))
            inp[p + 'b_w_o'] = dense(B_HEADS * B_V_DIM, D_MODEL)
        else:
            inp[p + 'c_w_in'] = dense(D_MODEL, 3 * C_WIDTH)
            inp[p + 'c_conv_w'] = jax.random.normal(nk(), (C_CONV, C_WIDTH), jnp.float32) * C_CONV ** -0.5
            inp[p + 'c_w_out'] = dense(C_WIDTH, D_MODEL)
        inp[p + 'ffn2_norm'] = gain(D_MODEL)
        inp[p + 'ffn2_w13'] = dense(D_MODEL, 2 * D_FF)
        inp[p + 'ffn2_w2'] = dense(D_FF, D_MODEL)
    inp['final_norm'] = gain(D_MODEL)
    return inp


def reference(x, positions,
              l0_ffn1_norm, l0_ffn1_w13, l0_ffn1_w2, l0_mix_norm, l0_a_w_qkv, l0_a_w_o,
              l0_ffn2_norm, l0_ffn2_w13, l0_ffn2_w2,
              l1_ffn1_norm, l1_ffn1_w13, l1_ffn1_w2, l1_mix_norm, l1_b_w_in, l1_b_q_norm,
              l1_b_w_uq, l1_b_kv_norm, l1_b_w_ukv, l1_b_w_o,
              l1_ffn2_norm, l1_ffn2_w13, l1_ffn2_w2,
              l2_ffn1_norm, l2_ffn1_w13, l2_ffn1_w2, l2_mix_norm, l2_c_w_in, l2_c_conv_w, l2_c_w_out,
              l2_ffn2_norm, l2_ffn2_w13, l2_ffn2_w2,
              l3_ffn1_norm, l3_ffn1_w13, l3_ffn1_w2, l3_mix_norm, l3_a_w_qkv, l3_a_w_o,
              l3_ffn2_norm, l3_ffn2_w13, l3_ffn2_w2,
              final_norm):
    ffn = [
        (l0_ffn1_norm, l0_ffn1_w13, l0_ffn1_w2, l0_mix_norm, l0_ffn2_norm, l0_ffn2_w13, l0_ffn2_w2),
        (l1_ffn1_norm, l1_ffn1_w13, l1_ffn1_w2, l1_mix_norm, l1_ffn2_norm, l1_ffn2_w13, l1_ffn2_w2),
        (l2_ffn1_norm, l2_ffn1_w13, l2_ffn1_w2, l2_mix_norm, l2_ffn2_norm, l2_ffn2_w13, l2_ffn2_w2),
        (l3_ffn1_norm, l3_ffn1_w13, l3_ffn1_w2, l3_mix_norm, l3_ffn2_norm, l3_ffn2_w13, l3_ffn2_w2),
    ]
    mixers = [
        lambda t: dilated_mixer(t, positions, l0_a_w_qkv, l0_a_w_o),
        lambda t: mla_mixer(t, positions, l1_b_w_in, l1_b_q_norm, l1_b_w_uq,
                            l1_b_kv_norm, l1_b_w_ukv, l1_b_w_o),
        lambda t: short_conv_mixer(t, l2_c_w_in, l2_c_conv_w, l2_c_w_out),
        lambda t: dilated_mixer(t, positions, l3_a_w_qkv, l3_a_w_o),
    ]
    h = x
    for i in range(DEPTH):
        n1, w13a, w2a, nm, n2, w13b, w2b = ffn[i]
        h = h + HALF_STEP * swiglu(rmsnorm(h, n1), w13a, w2a)
        h = h + mixers[i](rmsnorm(h, nm))
        h = h + HALF_STEP * swiglu(rmsnorm(h, n2), w13b, w2b)
    return rmsnorm(h, final_norm)
```

```python
import functools

import jax
import jax.numpy as jnp
from jax import lax
from jax.experimental import pallas as pl
from jax.experimental.pallas import tpu as pltpu

F32 = jnp.float32
BF16 = jnp.bfloat16

D_MODEL = 2048
BATCH = 2
SEQ = 8192
TOKENS = BATCH * SEQ
D_FF = 5632
HALF_STEP = 0.5
NORM_EPS = 1e-6
NEG_INF = -1e30
ROPE_THETA = 500000.0

A_HEADS = 16
A_HEAD_DIM = 128
A_ROT_HALF = A_HEAD_DIM // 8
A_PATTERNS = ((128, 1), (512, 4), (2048, 16))
A_GROUPS = len(A_PATTERNS)
A_BLOCK = 128
A_QKV_COLS = A_GROUPS * 3 * A_HEADS * A_HEAD_DIM

B_HEADS = 16
B_Q_RANK = 1536
B_KV_RANK = 512
B_NOPE_DIM = 128
B_ROPE_DIM = 64
B_ROT_HALF = B_ROPE_DIM // 2
B_V_DIM = 128
B_SLOT = 256
B_IN_COLS = 2176

C_CONV = 3

LANES = 128
VMEM_LIMIT = 56 * 1024 * 1024


def _cparams(*sem):
    return pltpu.CompilerParams(dimension_semantics=sem, vmem_limit_bytes=VMEM_LIMIT)


def _rms(x, gain):
    ms = jnp.mean(x * x, axis=-1, keepdims=True)
    return x * lax.rsqrt(ms + NORM_EPS) * gain


def _dot(a, b):
    return jnp.dot(a, b, preferred_element_type=F32)


def _dot_t(a, b):
    return lax.dot_general(a, b, (((1,), (1,)), ((), ())), preferred_element_type=F32)


def _rope128(y, cos, sin_lo, sin_hi, half):
    return (y * cos + pltpu.roll(y, LANES - half, 1) * sin_lo + pltpu.roll(y, half, 1) * sin_hi)


def _rope_tab_kernel(pos_ref, inva_ref, invb_ref, ca, la, ha, cb, lb, hb):
    pos = pos_ref[...].astype(F32)
    lane = lax.broadcasted_iota(jnp.int32, (1, LANES), 1)
    for inv_ref, half, c_ref, lo_ref, hi_ref in ((inva_ref, A_ROT_HALF, ca, la, ha),
                                                 (invb_ref, B_ROT_HALF, cb, lb, hb)):
        ang = pos * inv_ref[...]
        sn = jnp.sin(ang)
        c_ref[...] = jnp.cos(ang)
        lo_ref[...] = jnp.where(lane < half, -sn, 0.0)
        hi_ref[...] = jnp.where((lane >= half) & (lane < 2 * half), sn, 0.0)


def _rope_tables(positions):
    def inv_lanes(half):
        inv = ROPE_THETA ** (-jnp.arange(half, dtype=F32) / half)
        return jnp.concatenate([inv, inv, jnp.zeros((LANES - 2 * half,), F32)]).reshape(1, LANES)

    rows = 2048
    tab = jax.ShapeDtypeStruct((TOKENS, LANES), F32)
    row_spec = pl.BlockSpec((rows, LANES), lambda i: (i, 0))
    inv_spec = pl.BlockSpec((1, LANES), lambda i: (0, 0))
    return pl.pallas_call(
        _rope_tab_kernel,
        out_shape=(tab,) * 6,
        grid=(TOKENS // rows,),
        in_specs=[pl.BlockSpec((rows, 1), lambda i: (i, 0)), inv_spec, inv_spec],
        out_specs=(row_spec,) * 6,
        compiler_params=_cparams("parallel"),
        name="rope_tables",
    )(positions.reshape(TOKENS, 1), inv_lanes(A_ROT_HALF), inv_lanes(B_ROT_HALF))


def _ffn_kernel(x_ref, g_ref, w1_ref, w3_ref, w2_ref, fg_ref, o_ref, xn_ref, acc_ref, *, final):
    j = pl.program_id(1)

    @pl.when(j == 0)
    def _():
        xn_ref[...] = _rms(x_ref[...], g_ref[...]).astype(BF16)
        acc_ref[...] = jnp.zeros_like(acc_ref)

    xn = xn_ref[...]
    gate = _dot(xn, w1_ref[...])
    up = _dot(xn, w3_ref[...])
    act = (gate * (1.0 / (1.0 + jnp.exp(-gate))) * up).astype(BF16)
    acc_ref[...] += _dot(act, w2_ref[...])

    @pl.when(j == pl.num_programs(1) - 1)
    def _():
        y = x_ref[...] + HALF_STEP * acc_ref[...]
        if final:
            y = _rms(y, fg_ref[...])
        o_ref[...] = y


def _ffn(h, gain, w13, w2, final_gain=None):
    tm, tf = 512, 512
    nj = D_FF // tf
    final = final_gain is not None
    fg = final_gain if final else gain
    return pl.pallas_call(
        functools.partial(_ffn_kernel, final=final),
        out_shape=jax.ShapeDtypeStruct((TOKENS, D_MODEL), F32),
        grid=(TOKENS // tm, nj),
        in_specs=[
            pl.BlockSpec((tm, D_MODEL), lambda i, j: (i, 0)),
            pl.BlockSpec((1, D_MODEL), lambda i, j: (0, 0)),
            pl.BlockSpec((D_MODEL, tf), lambda i, j: (0, j)),
            pl.BlockSpec((D_MODEL, tf), lambda i, j: (0, j + nj)),
            pl.BlockSpec((tf, D_MODEL), lambda i, j: (j, 0)),
            pl.BlockSpec((1, D_MODEL), lambda i, j: (0, 0)),
        ],
        out_specs=pl.BlockSpec((tm, D_MODEL), lambda i, j: (i, 0)),
        scratch_shapes=[pltpu.VMEM((tm, D_MODEL), BF16), pltpu.VMEM((tm, D_MODEL), F32)],
        compiler_params=_cparams("parallel", "arbitrary"),
        name="ffn",
    )(h, gain.reshape(1, D_MODEL), w13, w13, w2, fg.reshape(1, D_MODEL))


def _out_proj_kernel(x_ref, w_ref, r_ref, o_ref):
    o_ref[...] = r_ref[...] + _dot(x_ref[...], w_ref[...])


def _out_proj(x, w, res):
    k = x.shape[1]
    tm, tn = 1024, 1024
    return pl.pallas_call(
        _out_proj_kernel,
        out_shape=jax.ShapeDtypeStruct((TOKENS, D_MODEL), F32),
        grid=(TOKENS // tm, D_MODEL // tn),
        in_specs=[
            pl.BlockSpec((tm, k), lambda i, j: (i, 0)),
            pl.BlockSpec((k, tn), lambda i, j: (0, j)),
            pl.BlockSpec((tm, tn), lambda i, j: (i, j)),
        ],
        out_specs=pl.BlockSpec((tm, tn), lambda i, j: (i, j)),
        compiler_params=_cparams("parallel", "parallel"),
        name="out_proj",
    )(x, w, res)


def _qkv_a_kernel(x_ref, g_ref, w_ref, cos_ref, lo_ref, hi_ref, o_ref, xn_ref, *, tn):
    j = pl.program_id(1)

    @pl.when(j == 0)
    def _():
        xn_ref[...] = _rms(x_ref[...], g_ref[...]).astype(BF16)

    y = _dot(xn_ref[...], w_ref[...])
    tiles_per_part = (A_HEADS * A_HEAD_DIM) // tn
    part = (j // tiles_per_part) % 3

    @pl.when(part == 2)
    def _():
        o_ref[...] = y.astype(BF16)

    @pl.when(part != 2)
    def _():
        scale = jnp.where(part == 0, A_HEAD_DIM ** -0.5, 1.0).astype(F32)
        cos, lo, hi = cos_ref[...], lo_ref[...], hi_ref[...]
        for hh in range(tn // LANES):
            sl = slice(hh * LANES, (hh + 1) * LANES)
            o_ref[:, sl] = (_rope128(y[:, sl], cos, lo, hi, A_ROT_HALF) * scale).astype(BF16)


def _qkv_a(h, gain, w_qkv, tabs):
    tm, tn = 1024, 1024
    tab_spec = pl.BlockSpec((tm, LANES), lambda i, j: (i, 0))
    return pl.pallas_call(
        functools.partial(_qkv_a_kernel, tn=tn),
        out_shape=jax.ShapeDtypeStruct((TOKENS, A_QKV_COLS), BF16),
        grid=(TOKENS // tm, A_QKV_COLS // tn),
        in_specs=[
            pl.BlockSpec((tm, D_MODEL), lambda i, j: (i, 0)),
            pl.BlockSpec((1, D_MODEL), lambda i, j: (0, 0)),
            pl.BlockSpec((D_MODEL, tn), lambda i, j: (0, j)),
            tab_spec, tab_spec, tab_spec,
        ],
        out_specs=pl.BlockSpec((tm, tn), lambda i, j: (i, j)),
        scratch_shapes=[pltpu.VMEM((tm, D_MODEL), BF16)],
        compiler_params=_cparams("parallel", "arbitrary"),
        name="qkv_a",
    )(h, gain.reshape(1, D_MODEL), w_qkv, *tabs)


def _attn_a_kernel(q_ref, kc_ref, kp_ref, vc_ref, vp_ref, o_ref, lse_ref):
    has_prev = pl.program_id(2) > 0
    row = lax.broadcasted_iota(jnp.int32, (A_BLOCK, A_BLOCK), 0)
    col = lax.broadcasted_iota(jnp.int32, (A_BLOCK, A_BLOCK), 1)
    mask_prev = jnp.logical_and(col >= row, has_prev)
    mask_cur = col <= row
    lse_ref[0] = jnp.zeros((A_BLOCK, LANES), F32)
    for h in range(A_HEADS):
        sl = slice(h * A_HEAD_DIM, (h + 1) * A_HEAD_DIM)
        q = q_ref[0, :, sl]
        s_prev = jnp.where(mask_prev, _dot_t(q, kp_ref[0, :, sl]), NEG_INF)
        s_cur = jnp.where(mask_cur, _dot_t(q, kc_ref[0, :, sl]), NEG_INF)
        m = jnp.maximum(jnp.max(s_prev, axis=-1, keepdims=True), jnp.max(s_cur, axis=-1, keepdims=True))
        p_prev = jnp.exp(s_prev - m)
        p_cur = jnp.exp(s_cur - m)
        l = jnp.sum(p_prev, axis=-1, keepdims=True) + jnp.sum(p_cur, axis=-1, keepdims=True)
        o = _dot(p_prev.astype(BF16), vp_ref[0, :, sl]) + _dot(p_cur.astype(BF16), vc_ref[0, :, sl])
        o_ref[0, :, sl] = o / l
        lse_ref[0, :, h:h + 1] = m + jnp.log(l)


def _attn_a(qkv, group, window, dilation):
    assert window // dilation == A_BLOCK, "band width must equal the query block"
    sub = SEQ // dilation
    nb = sub // A_BLOCK
    hd = A_HEADS * A_HEAD_DIM
    qkv_v = qkv.reshape(BATCH, sub, dilation * A_QKV_COLS)
    per_res = A_QKV_COLS // hd

    def spec(part, prev):
        def idx(b, c, n):
            return (b, jnp.maximum(n - 1, 0) if prev else n, c * per_res + group * 3 + part)
        return pl.BlockSpec((1, A_BLOCK, hd), idx)

    o, lse = pl.pallas_call(
        _attn_a_kernel,
        out_shape=(jax.ShapeDtypeStruct((BATCH, sub, dilation * hd), F32),
                   jax.ShapeDtypeStruct((BATCH, sub, dilation * LANES), F32)),
        grid=(BATCH, dilation, nb),
        in_specs=[spec(0, False), spec(1, False), spec(1, True), spec(2, False), spec(2, True)],
        out_specs=(pl.BlockSpec((1, A_BLOCK, hd), lambda b, c, n: (b, n, c)),
                   pl.BlockSpec((1, A_BLOCK, LANES), lambda b, c, n: (b, n, c))),
        compiler_params=_cparams("parallel", "parallel", "arbitrary"),
        name="attn_a_d%d" % dilation,
    )(qkv_v, qkv_v, qkv_v, qkv_v, qkv_v)
    return o.reshape(TOKENS, hd), lse.reshape(TOKENS, LANES)


def _comb_a_kernel(o0, o1, o2, l0, l1, l2, out_ref):
    a = [l0[...], l1[...], l2[...]]
    m = jnp.maximum(jnp.maximum(a[0], a[1]), a[2])
    e = [jnp.exp(t - m) for t in a]
    den = e[0] + e[1] + e[2]
    w = [t / den for t in e]
    rows = out_ref.shape[0]
    for h in range(A_HEADS):
        sl = slice(h * A_HEAD_DIM, (h + 1) * A_HEAD_DIM)
        acc = jnp.zeros((rows, A_HEAD_DIM), F32)
        for wg, og in zip(w, (o0, o1, o2)):
            acc = acc + jnp.broadcast_to(wg[:, h:h + 1], (rows, A_HEAD_DIM)) * og[:, sl]
        out_ref[:, sl] = acc.astype(BF16)


def _comb_a(outs, lses):
    tm = 512
    hd = A_HEADS * A_HEAD_DIM
    o_spec = pl.BlockSpec((tm, hd), lambda i: (i, 0))
    l_spec = pl.BlockSpec((tm, LANES), lambda i: (i, 0))
    return pl.pallas_call(
        _comb_a_kernel,
        out_shape=jax.ShapeDtypeStruct((TOKENS, hd), BF16),
        grid=(TOKENS // tm,),
        in_specs=[o_spec] * 3 + [l_spec] * 3,
        out_specs=o_spec,
        compiler_params=_cparams("parallel"),
        name="comb_a",
    )(*outs, *lses)


def _mixer_a(h, gain, w_qkv, w_o, tabs_a):
    qkv = _qkv_a(h, gain, w_qkv, tabs_a)
    outs, lses = [], []
    for g, (window, dilation) in enumerate(A_PATTERNS):
        o, lse = _attn_a(qkv, g, window, dilation)
        outs.append(o)
        lses.append(lse)
    return _out_proj(_comb_a(outs, lses), w_o, h)


def _b_in_kernel(x_ref, g_ref, w_ref, o_ref):
    o_ref[...] = _dot(_rms(x_ref[...], g_ref[...]).astype(BF16), w_ref[...])


def _b_in(h, gain, w_in):
    tm = 512
    return pl.pallas_call(
        _b_in_kernel,
        out_shape=jax.ShapeDtypeStruct((TOKENS, B_IN_COLS), F32),
        grid=(TOKENS // tm,),
        in_specs=[
            pl.BlockSpec((tm, D_MODEL), lambda i: (i, 0)),
            pl.BlockSpec((1, D_MODEL), lambda i: (0, 0)),
            pl.BlockSpec((D_MODEL, B_IN_COLS), lambda i: (0, 0)),
        ],
        out_specs=pl.BlockSpec((tm, B_IN_COLS), lambda i: (i, 0)),
        compiler_params=_cparams("parallel"),
        name="b_in",
    )(h, gain.reshape(1, D_MODEL), w_in)


def _b_q_kernel(c_ref, g_ref, w_ref, cos_ref, lo_ref, hi_ref, o_ref, xn_ref, *, slots):
    @pl.when(pl.program_id(1) == 0)
    def _():
        xn_ref[...] = _rms(c_ref[...], g_ref[...]).astype(BF16)

    y = _dot(xn_ref[...], w_ref[...])
    scale = (B_NOPE_DIM + B_ROPE_DIM) ** -0.5
    cos, lo, hi = cos_ref[...], lo_ref[...], hi_ref[...]
    for s in range(slots):
        nope = y[:, s * B_SLOT:s * B_SLOT + LANES]
        rot = _rope128(y[:, s * B_SLOT + LANES:(s + 1) * B_SLOT], cos, lo, hi, B_ROT_HALF)
        o_ref[s, :, :LANES] = (nope * scale).astype(BF16)
        o_ref[s, :, LANES:] = (rot * scale).astype(BF16)


def _b_q(c, gain, w_q, tabs_b):
    tm, slots = 1024, 4
    tn = slots * B_SLOT
    tab_spec = pl.BlockSpec((tm, LANES), lambda i, j: (i, 0))
    return pl.pallas_call(
        functools.partial(_b_q_kernel, slots=slots),
        out_shape=jax.ShapeDtypeStruct((B_HEADS, TOKENS, B_SLOT), BF16),
        grid=(TOKENS // tm, B_HEADS // slots),
        in_specs=[
            pl.BlockSpec((tm, B_Q_RANK), lambda i, j: (i, 0)),
            pl.BlockSpec((1, B_Q_RANK), lambda i, j: (0, 0)),
            pl.BlockSpec((B_Q_RANK, tn), lambda i, j: (0, j)),
            tab_spec, tab_spec, tab_spec,
        ],
        out_specs=pl.BlockSpec((slots, tm, B_SLOT), lambda i, j: (j, i, 0)),
        scratch_shapes=[pltpu.VMEM((tm, B_Q_RANK), BF16)],
        compiler_params=_cparams("parallel", "arbitrary"),
        name="b_q",
    )(c, gain.reshape(1, B_Q_RANK), w_q, *tabs_b)


def _b_kv_kernel(c_ref, g_ref, wk_ref, wv_ref, cr_ref, cos_ref, lo_ref, hi_ref, k_ref, v_ref, *, slots):
    xn = _rms(c_ref[...], g_ref[...]).astype(BF16)
    yk = _dot(xn, wk_ref[...])
    yv = _dot(xn, wv_ref[...])
    kr = _rope128(cr_ref[...], cos_ref[...], lo_ref[...], hi_ref[...], B_ROT_HALF).astype(BF16)
    for s in range(slots):
        sl = slice(s * LANES, (s + 1) * LANES)
        k_ref[s, :, :LANES] = yk[:, sl].astype(BF16)
        k_ref[s, :, LANES:] = kr
        v_ref[s] = yv[:, sl].astype(BF16)


def _b_kv(c, gain, w_k, w_v, tabs_b):
    tm, slots = 1024, 4
    tn = slots * LANES
    kv_block = B_Q_RANK // B_KV_RANK
    rope_block = (B_Q_RANK + B_KV_RANK) // LANES
    tab_spec = pl.BlockSpec((tm, LANES), lambda i, j: (i, 0))
    return pl.pallas_call(
        functools.partial(_b_kv_kernel, slots=slots),
        out_shape=(jax.ShapeDtypeStruct((B_HEADS, TOKENS, B_SLOT), BF16),
                   jax.ShapeDtypeStruct((B_HEADS, TOKENS, B_V_DIM), BF16)),
        grid=(TOKENS // tm, B_HEADS // slots),
        in_specs=[
            pl.BlockSpec((tm, B_KV_RANK), lambda i, j: (i, kv_block)),
            pl.BlockSpec((1, B_KV_RANK), lambda i, j: (0, 0)),
            pl.BlockSpec((B_KV_RANK, tn), lambda i, j: (0, j)),
            pl.BlockSpec((B_KV_RANK, tn), lambda i, j: (0, j)),
            pl.BlockSpec((tm, LANES), lambda i, j: (i, rope_block)),
            tab_spec, tab_spec, tab_spec,
        ],
        out_specs=(pl.BlockSpec((slots, tm, B_SLOT), lambda i, j: (j, i, 0)),
                   pl.BlockSpec((slots, tm, B_V_DIM), lambda i, j: (j, i, 0))),
        compiler_params=_cparams("parallel", "parallel"),
        name="b_kv",
    )(c, gain.reshape(1, B_KV_RANK), w_k, w_v, c, *tabs_b)


def _attn_b_kernel(q_ref, k_ref, v_ref, o_ref, m_ref, l_ref, acc_ref, *, tq):
    qi = pl.program_id(2)
    q = q_ref[0]
    m_ref[...] = jnp.full_like(m_ref, NEG_INF)
    l_ref[...] = jnp.zeros_like(l_ref)
    acc_ref[...] = jnp.zeros_like(acc_ref)

    def step(ki, masked):
        start = pl.multiple_of(ki * tq, tq)
        s = _dot_t(q, k_ref[0, pl.ds(start, tq), :])
        if masked:
            row = lax.broadcasted_iota(jnp.int32, (tq, tq), 0)
            col = lax.broadcasted_iota(jnp.int32, (tq, tq), 1)
            s = jnp.where(row >= col, s, NEG_INF)
        m_prev = m_ref[...]
        m_new = jnp.maximum(m_prev, jnp.max(s, axis=-1, keepdims=True))
        alpha = jnp.exp(m_prev - m_new)
        p = jnp.exp(s - m_new)
        l_ref[...] = alpha * l_ref[...] + jnp.sum(p, axis=-1, keepdims=True)
        acc_ref[...] = alpha * acc_ref[...] + _dot(p.astype(BF16), v_ref[0, pl.ds(start, tq), :])
        m_ref[...] = m_new

    def body(ki, carry):
        step(ki, False)
        return carry

    lax.fori_loop(0, qi, body, 0)
    step(qi, True)
    o_ref[...] = (acc_ref[...] / l_ref[...]).astype(BF16)


def _attn_b(q3, k3, v3):
    tq = 512
    nq = SEQ // tq
    return pl.pallas_call(
        functools.partial(_attn_b_kernel, tq=tq),
        out_shape=jax.ShapeDtypeStruct((TOKENS, B_HEADS * B_V_DIM), BF16),
        grid=(BATCH, B_HEADS, nq),
        in_specs=[
            pl.BlockSpec((1, tq, B_SLOT), lambda b, h, qi: (h, b * nq + qi, 0)),
            pl.BlockSpec((1, SEQ, B_SLOT), lambda b, h, qi: (h, b, 0)),
            pl.BlockSpec((1, SEQ, B_V_DIM), lambda b, h, qi: (h, b, 0)),
        ],
        out_specs=pl.BlockSpec((tq, B_V_DIM), lambda b, h, qi: (b * nq + qi, h)),
        scratch_shapes=[pltpu.VMEM((tq, 1), F32), pltpu.VMEM((tq, 1), F32),
                        pltpu.VMEM((tq, B_V_DIM), F32)],
        compiler_params=_cparams("parallel", "parallel", "arbitrary"),
        name="attn_b",
    )(q3, k3, v3)


def _mixer_b(h, gain, w_in, q_norm, w_q, kv_norm, w_k, w_v, w_o, tabs_b):
    c = _b_in(h, gain, w_in)
    q3 = _b_q(c, q_norm, w_q, tabs_b)
    k3, v3 = _b_kv(c, kv_norm, w_k, w_v, tabs_b)
    return _out_proj(_attn_b(q3, k3, v3), w_o, h)


def _c_in_kernel(x_ref, g_ref, wb_ref, wc_ref, wu_ref, cw_ref, o_ref, xn_ref, carry_ref, *, tm):
    i = pl.program_id(0)
    j = pl.program_id(1)

    @pl.when(j == 0)
    def _():
        xn_ref[...] = _rms(x_ref[...], g_ref[...]).astype(BF16)

    xn = xn_ref[...]
    b_gate = _dot(xn, wb_ref[...])
    cu = _dot(xn, wc_ref[...]) * _dot(xn, wu_ref[...])
    w0, w1, w2 = cw_ref[0:1, :], cw_ref[1:2, :], cw_ref[2:3, :]
    z = w2 * cu + w1 * pltpu.roll(cu, 1, 0) + w0 * pltpu.roll(cu, 2, 0)
    o_ref[...] = (b_gate * z).astype(BF16)

    first = (i * tm) % SEQ == 0
    tail = jnp.where(first, 0.0, carry_ref[j])
    both = jnp.concatenate([tail, cu[0:8, :]], axis=0)
    z8 = w2 * both[8:16] + w1 * both[7:15] + w0 * both[6:14]
    o_ref[0:8, :] = (b_gate[0:8, :] * z8).astype(BF16)
    carry_ref[j] = cu[tm - 8:tm, :]


def _c_in(h, gain, w_in, conv_w):
    tm, tn = 1024, 512
    nj = D_MODEL // tn
    return pl.pallas_call(
        functools.partial(_c_in_kernel, tm=tm),
        out_shape=jax.ShapeDtypeStruct((TOKENS, D_MODEL), BF16),
        grid=(TOKENS // tm, nj),
        in_specs=[
            pl.BlockSpec((tm, D_MODEL), lambda i, j: (i, 0)),
            pl.BlockSpec((1, D_MODEL), lambda i, j: (0, 0)),
            pl.BlockSpec((D_MODEL, tn), lambda i, j: (0, j)),
            pl.BlockSpec((D_MODEL, tn), lambda i, j: (0, j + nj)),
            pl.BlockSpec((D_MODEL, tn), lambda i, j: (0, j + 2 * nj)),
            pl.BlockSpec((C_CONV, tn), lambda i, j: (0, j)),
        ],
        out_specs=pl.BlockSpec((tm, tn), lambda i, j: (i, j)),
        scratch_shapes=[pltpu.VMEM((tm, D_MODEL), BF16), pltpu.VMEM((nj, 8, tn), F32)],
        compiler_params=_cparams("arbitrary", "arbitrary"),
        name="c_in",
    )(h, gain.reshape(1, D_MODEL), w_in, w_in, w_in, conv_w)


def _mixer_c(h, gain, w_in, conv_w, w_out):
    return _out_proj(_c_in(h, gain, w_in, conv_w), w_out, h)


def kernel(x, positions, l0_ffn1_norm, l0_ffn1_w13, l0_ffn1_w2, l0_mix_norm, l0_a_w_qkv, l0_a_w_o, l0_ffn2_norm, l0_ffn2_w13, l0_ffn2_w2, l1_ffn1_norm, l1_ffn1_w13, l1_ffn1_w2, l1_mix_norm, l1_b_w_in, l1_b_q_norm, l1_b_w_uq, l1_b_kv_norm, l1_b_w_ukv, l1_b_w_o, l1_ffn2_norm, l1_ffn2_w13, l1_ffn2_w2, l2_ffn1_norm, l2_ffn1_w13, l2_ffn1_w2, l2_mix_norm, l2_c_w_in, l2_c_conv_w, l2_c_w_out, l2_ffn2_norm, l2_ffn2_w13, l2_ffn2_w2, l3_ffn1_norm, l3_ffn1_w13, l3_ffn1_w2, l3_mix_norm, l3_a_w_qkv, l3_a_w_o, l3_ffn2_norm, l3_ffn2_w13, l3_ffn2_w2, final_norm):
    bf = lambda w: w.astype(BF16)
    tabs = _rope_tables(positions)
    tabs_a, tabs_b = tabs[:3], tabs[3:]

    w_uq = l1_b_w_uq.reshape(B_Q_RANK, B_HEADS, B_NOPE_DIM + B_ROPE_DIM)
    w_q = jnp.pad(w_uq, ((0, 0), (0, 0), (0, B_SLOT - B_NOPE_DIM - B_ROPE_DIM)))
    w_q = bf(w_q.reshape(B_Q_RANK, B_HEADS * B_SLOT))
    w_ukv = l1_b_w_ukv.reshape(B_KV_RANK, B_HEADS, B_NOPE_DIM + B_V_DIM)
    w_k = bf(w_ukv[:, :, :B_NOPE_DIM].reshape(B_KV_RANK, B_HEADS * B_NOPE_DIM))
    w_v = bf(w_ukv[:, :, B_NOPE_DIM:].reshape(B_KV_RANK, B_HEADS * B_V_DIM))
    w_b_in = bf(jnp.pad(l1_b_w_in, ((0, 0), (0, B_IN_COLS - l1_b_w_in.shape[1]))))

    h = x.reshape(TOKENS, D_MODEL)

    h = _ffn(h, l0_ffn1_norm, bf(l0_ffn1_w13), bf(l0_ffn1_w2))
    h = _mixer_a(h, l0_mix_norm, bf(l0_a_w_qkv), bf(l0_a_w_o), tabs_a)
    h = _ffn(h, l0_ffn2_norm, bf(l0_ffn2_w13), bf(l0_ffn2_w2))

    h = _ffn(h, l1_ffn1_norm, bf(l1_ffn1_w13), bf(l1_ffn1_w2))
    h = _mixer_b(h, l1_mix_norm, w_b_in, l1_b_q_norm, w_q, l1_b_kv_norm, w_k, w_v, bf(l1_b_w_o), tabs_b)
    h = _ffn(h, l1_ffn2_norm, bf(l1_ffn2_w13), bf(l1_ffn2_w2))

    h = _ffn(h, l2_ffn1_norm, bf(l2_ffn1_w13), bf(l2_ffn1_w2))
    h = _mixer_c(h, l2_mix_norm, bf(l2_c_w_in), l2_c_conv_w, bf(l2_c_w_out))
    h = _ffn(h, l2_ffn2_norm, bf(l2_ffn2_w13), bf(l2_ffn2_w2))

    h = _ffn(h, l3_ffn1_norm, bf(l3_ffn1_w13), bf(l3_ffn1_w2))
    h = _mixer_a(h, l3_mix_norm, bf(l3_a_w_qkv), bf(l3_a_w_o), tabs_a)
    h = _ffn(h, l3_ffn2_norm, bf(l3_ffn2_w13), bf(l3_ffn2_w2), final_gain=final_norm)

    return h.reshape(BATCH, SEQ, D_MODEL)
```

```python
import functools

import jax
import jax.numpy as jnp
from jax import lax
from jax.experimental import pallas as pl
from jax.experimental.pallas import tpu as pltpu

F32 = jnp.float32
BF16 = jnp.bfloat16

D_MODEL = 2048
BATCH = 2
SEQ = 8192
TOKENS = BATCH * SEQ
D_FF = 5632
HALF_STEP = 0.5
NORM_EPS = 1e-6
NEG_INF = -1e30
ROPE_THETA = 500000.0

A_HEADS = 16
A_HEAD_DIM = 128
A_ROT_HALF = A_HEAD_DIM // 8
A_PATTERNS = ((128, 1), (512, 4), (2048, 16))
A_BLOCK = 128
A_TILE = 2048

B_HEADS = 16
B_Q_RANK = 1536
B_KV_RANK = 512
B_NOPE_DIM = 128
B_ROPE_DIM = 64
B_ROT_HALF = B_ROPE_DIM // 2
B_V_DIM = 128
B_SLOT = 256
B_IN_COLS = 2176

C_CONV = 3

LANES = 128
VMEM_LIMIT = 56 * 1024 * 1024


def _cparams(*sem):
    return pltpu.CompilerParams(dimension_semantics=sem, vmem_limit_bytes=VMEM_LIMIT)


def _rms(x, gain):
    ms = jnp.mean(x * x, axis=-1, keepdims=True)
    return x * lax.rsqrt(ms + NORM_EPS) * gain


def _dot(a, b):
    return jnp.dot(a, b, preferred_element_type=F32)


def _dot_t(a, b):
    return lax.dot_general(a, b, (((1,), (1,)), ((), ())), preferred_element_type=F32)


def _rope128(y, cos, sin_lo, sin_hi, half):
    return (y * cos + pltpu.roll(y, LANES - half, 1) * sin_lo + pltpu.roll(y, half, 1) * sin_hi)


def _rope_tab_kernel(pos_ref, inv_ref, c_ref, lo_ref, hi_ref, *, half):
    ang = pos_ref[...].astype(F32) * inv_ref[...]
    lane = lax.broadcasted_iota(jnp.int32, (1, LANES), 1)
    sn = jnp.sin(ang)
    c_ref[...] = jnp.cos(ang)
    lo_ref[...] = jnp.where(lane < half, -sn, 0.0)
    hi_ref[...] = jnp.where((lane >= half) & (lane < 2 * half), sn, 0.0)


def _rope_tables(pos_flat, half):
    inv = ROPE_THETA ** (-jnp.arange(half, dtype=F32) / half)
    inv = jnp.concatenate([inv, inv, jnp.zeros((LANES - 2 * half,), F32)]).reshape(1, LANES)
    n = pos_flat.shape[0]
    rows = 2048
    tab = jax.ShapeDtypeStruct((n, LANES), F32)
    row_spec = pl.BlockSpec((rows, LANES), lambda i: (i, 0))
    return pl.pallas_call(
        functools.partial(_rope_tab_kernel, half=half),
        out_shape=(tab,) * 3,
        grid=(n // rows,),
        in_specs=[pl.BlockSpec((rows, 1), lambda i: (i, 0)), pl.BlockSpec((1, LANES), lambda i: (0, 0))],
        out_specs=(row_spec,) * 3,
        compiler_params=_cparams("parallel"),
        name="rope_tables",
    )(pos_flat.reshape(n, 1), inv)


def _ffn_kernel(x_ref, g_ref, w1_ref, w3_ref, w2_ref, fg_ref, o_ref, xn_ref, acc_ref, *, final):
    j = pl.program_id(1)

    @pl.when(j == 0)
    def _():
        xn_ref[...] = _rms(x_ref[...], g_ref[...]).astype(BF16)
        acc_ref[...] = jnp.zeros_like(acc_ref)

    xn = xn_ref[...]
    gate = _dot(xn, w1_ref[...])
    up = _dot(xn, w3_ref[...])
    act = (gate * (1.0 / (1.0 + jnp.exp(-gate))) * up).astype(BF16)
    acc_ref[...] += _dot(act, w2_ref[...])

    @pl.when(j == pl.num_programs(1) - 1)
    def _():
        y = x_ref[...] + HALF_STEP * acc_ref[...]
        if final:
            y = _rms(y, fg_ref[...])
        o_ref[...] = y


def _ffn(h, gain, w13, w2, final_gain=None):
    tm, tf = 512, 512
    nj = D_FF // tf
    final = final_gain is not None
    fg = final_gain if final else gain
    return pl.pallas_call(
        functools.partial(_ffn_kernel, final=final),
        out_shape=jax.ShapeDtypeStruct((TOKENS, D_MODEL), F32),
        grid=(TOKENS // tm, nj),
        in_specs=[
            pl.BlockSpec((tm, D_MODEL), lambda i, j: (i, 0)),
            pl.BlockSpec((1, D_MODEL), lambda i, j: (0, 0)),
            pl.BlockSpec((D_MODEL, tf), lambda i, j: (0, j)),
            pl.BlockSpec((D_MODEL, tf), lambda i, j: (0, j + nj)),
            pl.BlockSpec((tf, D_MODEL), lambda i, j: (j, 0)),
            pl.BlockSpec((1, D_MODEL), lambda i, j: (0, 0)),
        ],
        out_specs=pl.BlockSpec((tm, D_MODEL), lambda i, j: (i, 0)),
        scratch_shapes=[pltpu.VMEM((tm, D_MODEL), BF16), pltpu.VMEM((tm, D_MODEL), F32)],
        compiler_params=_cparams("parallel", "arbitrary"),
        name="ffn",
    )(h, gain.reshape(1, D_MODEL), w13, w13, w2, fg.reshape(1, D_MODEL))


def _out_proj_kernel(x_ref, w_ref, r_ref, o_ref):
    o_ref[...] = r_ref[...] + _dot(x_ref[...], w_ref[...])


def _out_proj(x, w, res):
    k = x.shape[1]
    tm, tn = 1024, 1024
    return pl.pallas_call(
        _out_proj_kernel,
        out_shape=jax.ShapeDtypeStruct((TOKENS, D_MODEL), F32),
        grid=(TOKENS // tm, D_MODEL // tn),
        in_specs=[
            pl.BlockSpec((tm, k), lambda i, j: (i, 0)),
            pl.BlockSpec((k, tn), lambda i, j: (0, j)),
            pl.BlockSpec((tm, tn), lambda i, j: (i, j)),
        ],
        out_specs=pl.BlockSpec((tm, tn), lambda i, j: (i, j)),
        compiler_params=_cparams("parallel", "parallel"),
        name="out_proj",
    )(x, w, res)


def _norm_perm_a_kernel(x_ref, g_ref, *refs, tm):
    o_refs, xs_ref = refs[:-1], refs[-1]
    xn = _rms(x_ref[...], g_ref[...])
    nchunk = D_MODEL // LANES
    for k in range(nchunk):
        xs_ref[k] = xn[:, k * LANES:(k + 1) * LANES]
    for (_, d), o_ref in zip(A_PATTERNS, o_refs):
        if d == 1:
            o_ref[...] = xn.astype(BF16)
        else:
            for c in range(d):
                for k in range(nchunk):
                    o_ref[0, c, 0, :, k * LANES:(k + 1) * LANES] = (
                        xs_ref[k, pl.ds(c, tm // d, stride=d), :].astype(BF16))


def _norm_perm_a(h, gain):
    tm = 512
    per_tile = A_TILE // tm
    shapes, specs = [], []
    for _, d in A_PATTERNS:
        if d == 1:
            shapes.append(jax.ShapeDtypeStruct((TOKENS, D_MODEL), BF16))
            specs.append(pl.BlockSpec((tm, D_MODEL), lambda i: (i, 0)))
        else:
            shapes.append(jax.ShapeDtypeStruct((TOKENS // A_TILE, d, per_tile, tm // d, D_MODEL), BF16))
            specs.append(pl.BlockSpec((1, d, 1, tm // d, D_MODEL),
                                      lambda i: (i // per_tile, 0, i % per_tile, 0, 0)))
    outs = pl.pallas_call(
        functools.partial(_norm_perm_a_kernel, tm=tm),
        out_shape=tuple(shapes),
        grid=(TOKENS // tm,),
        in_specs=[pl.BlockSpec((tm, D_MODEL), lambda i: (i, 0)), pl.BlockSpec((1, D_MODEL), lambda i: (0, 0))],
        out_specs=tuple(specs),
        scratch_shapes=[pltpu.VMEM((D_MODEL // LANES, tm, LANES), F32)],
        compiler_params=_cparams("parallel"),
        name="norm_perm_a",
    )(h, gain.reshape(1, D_MODEL))
    return [o.reshape(TOKENS, D_MODEL) for o in outs]


def _qkv_a_kernel(x_ref, w_ref, cos_ref, lo_ref, hi_ref, o_ref, *, tn):
    y = _dot(x_ref[...], w_ref[...])
    part = pl.program_id(1) // ((A_HEADS * A_HEAD_DIM) // tn)

    @pl.when(part == 2)
    def _():
        o_ref[...] = y.astype(BF16)

    @pl.when(part != 2)
    def _():
        scale = jnp.where(part == 0, A_HEAD_DIM ** -0.5, 1.0).astype(F32)
        cos, lo, hi = cos_ref[...], lo_ref[...], hi_ref[...]
        for hh in range(tn // LANES):
            sl = slice(hh * LANES, (hh + 1) * LANES)
            o_ref[:, sl] = (_rope128(y[:, sl], cos, lo, hi, A_ROT_HALF) * scale).astype(BF16)


def _qkv_a(xn, w_qkv, tabs, group):
    tm, tn = 1024, 1024
    cols = 3 * A_HEADS * A_HEAD_DIM
    nj = cols // tn
    ni = TOKENS // tm
    tab_spec = pl.BlockSpec((tm, LANES), lambda i, j: (group * ni + i, 0))
    return pl.pallas_call(
        functools.partial(_qkv_a_kernel, tn=tn),
        out_shape=jax.ShapeDtypeStruct((TOKENS, cols), BF16),
        grid=(ni, nj),
        in_specs=[
            pl.BlockSpec((tm, D_MODEL), lambda i, j: (i, 0)),
            pl.BlockSpec((D_MODEL, tn), lambda i, j: (0, group * nj + j)),
            tab_spec, tab_spec, tab_spec,
        ],
        out_specs=pl.BlockSpec((tm, tn), lambda i, j: (i, j)),
        compiler_params=_cparams("parallel", "parallel"),
        name="qkv_a",
    )(xn, w_qkv, *tabs)


def _attn_a_kernel(q_ref, kc_ref, kp_ref, vc_ref, vp_ref, o_ref, lse_ref, kf_ref, vf_ref, os_ref, ls_ref,
                   *, d, heads):
    rows = A_TILE // d
    nblk = rows // A_BLOCK
    for c in range(d):
        kf_ref[c, :A_BLOCK] = kp_ref[0, c]
        kf_ref[c, A_BLOCK:] = kc_ref[0, c]
        vf_ref[c, :A_BLOCK] = vp_ref[0, c]
        vf_ref[c, A_BLOCK:] = vc_ref[0, c]
    first_tile = pl.program_id(1) == 0
    r = lax.broadcasted_iota(jnp.int32, (A_BLOCK, 2 * A_BLOCK), 0)
    kk = lax.broadcasted_iota(jnp.int32, (A_BLOCK, 2 * A_BLOCK), 1)
    band = (kk >= r) & (kk <= r + A_BLOCK)
    lane = lax.broadcasted_iota(jnp.int32, (A_BLOCK, LANES), 1)

    def unit_group(u, carry):
        c = u // nblk
        blk = u % nblk
        start = pl.multiple_of(blk * A_BLOCK, A_BLOCK)
        no_prev = jnp.logical_and(first_tile, blk == 0)
        mask = jnp.logical_and(band, jnp.logical_or(kk >= A_BLOCK, jnp.logical_not(no_prev)))
        q = q_ref[0, c, pl.ds(start, A_BLOCK), :]
        k = kf_ref[c, pl.ds(start, 2 * A_BLOCK), :]
        v = vf_ref[c, pl.ds(start, 2 * A_BLOCK), :]
        hs = [slice(h * A_HEAD_DIM, (h + 1) * A_HEAD_DIM) for h in range(heads)]
        scores = [_dot_t(q[:, sl], k[:, sl]) for sl in hs]
        lse_tile = jnp.zeros((A_BLOCK, LANES), F32)
        out_rows = pl.ds(blk * (A_BLOCK * d) + c, A_BLOCK, stride=d) if d > 1 else pl.ds(start, A_BLOCK)
        for h, sl in enumerate(hs):
            s = jnp.where(mask, scores[h], NEG_INF)
            m = jnp.max(s, axis=-1, keepdims=True)
            p = jnp.exp(s - m)
            l = jnp.sum(p, axis=-1, keepdims=True)
            os_ref[h, out_rows, :] = _dot(p.astype(BF16), v[:, sl]) / l
            lse_tile = jnp.where(lane == h, m + jnp.log(l), lse_tile)
        ls_ref[out_rows, :] = lse_tile
        return carry

    lax.fori_loop(0, d * nblk, unit_group, 0)
    for h in range(heads):
        o_ref[:, h * A_HEAD_DIM:(h + 1) * A_HEAD_DIM] = os_ref[h]
    lse_ref[...] = ls_ref[...]


def _attn_a(qkv, window, dilation):
    assert window // dilation == A_BLOCK, "band width must equal the query block"
    d = dilation
    heads = 4
    hw = heads * A_HEAD_DIM
    ngrp = A_HEADS // heads
    rows = A_TILE // d
    tiles = SEQ // A_TILE
    qkv_v = qkv.reshape(BATCH, tiles, d, rows, 3 * A_HEADS * A_HEAD_DIM)
    last_blk = rows // A_BLOCK - 1

    def cur(part):
        return pl.BlockSpec((None, 1, d, rows, hw), lambda b, n, g: (b, n, 0, 0, part * ngrp + g))

    def prev(part):
        return pl.BlockSpec((None, 1, d, A_BLOCK, hw),
                            lambda b, n, g: (b, jnp.maximum(n - 1, 0), 0, last_blk, part * ngrp + g))

    o, lse = pl.pallas_call(
        functools.partial(_attn_a_kernel, d=d, heads=heads),
        out_shape=(jax.ShapeDtypeStruct((TOKENS, A_HEADS * A_HEAD_DIM), F32),
                   jax.ShapeDtypeStruct((TOKENS, ngrp * LANES), F32)),
        grid=(BATCH, tiles, ngrp),
        in_specs=[cur(0), cur(1), prev(1), cur(2), prev(2)],
        out_specs=(pl.BlockSpec((A_TILE, hw), lambda b, n, g: (b * tiles + n, g)),
                   pl.BlockSpec((A_TILE, LANES), lambda b, n, g: (b * tiles + n, g))),
        scratch_shapes=[pltpu.VMEM((d, rows + A_BLOCK, hw), BF16), pltpu.VMEM((d, rows + A_BLOCK, hw), BF16),
                        pltpu.VMEM((heads, A_TILE, A_HEAD_DIM), F32), pltpu.VMEM((A_TILE, LANES), F32)],
        compiler_params=_cparams("parallel", "parallel", "parallel"),
        name="attn_a_d%d" % d,
    )(qkv_v, qkv_v, qkv_v, qkv_v, qkv_v)
    return o, lse


def _comb_a_kernel(o0, o1, o2, l0, l1, l2, out_ref, *, heads):
    a = [l0[...], l1[...], l2[...]]
    m = jnp.maximum(jnp.maximum(a[0], a[1]), a[2])
    e = [jnp.exp(t - m) for t in a]
    den = e[0] + e[1] + e[2]
    w = [t / den for t in e]
    rows = out_ref.shape[0]
    for h in range(A_HEADS):
        sl = slice(h * A_HEAD_DIM, (h + 1) * A_HEAD_DIM)
        col = (h // heads) * LANES + h % heads
        acc = jnp.zeros((rows, A_HEAD_DIM), F32)
        for wg, og in zip(w, (o0, o1, o2)):
            acc = acc + jnp.broadcast_to(wg[:, col:col + 1], (rows, A_HEAD_DIM)) * og[:, sl]
        out_ref[:, sl] = acc.astype(BF16)


def _comb_a(outs, lses):
    tm = 512
    hd = A_HEADS * A_HEAD_DIM
    lw = lses[0].shape[1]
    o_spec = pl.BlockSpec((tm, hd), lambda i: (i, 0))
    l_spec = pl.BlockSpec((tm, lw), lambda i: (i, 0))
    return pl.pallas_call(
        functools.partial(_comb_a_kernel, heads=A_HEADS * LANES // lw),
        out_shape=jax.ShapeDtypeStruct((TOKENS, hd), BF16),
        grid=(TOKENS // tm,),
        in_specs=[o_spec] * 3 + [l_spec] * 3,
        out_specs=o_spec,
        compiler_params=_cparams("parallel"),
        name="comb_a",
    )(*outs, *lses)


def _mixer_a(h, gain, w_qkv, w_o, tabs_a):
    outs, lses = [], []
    for g, ((window, dilation), xn) in enumerate(zip(A_PATTERNS, _norm_perm_a(h, gain))):
        o, lse = _attn_a(_qkv_a(xn, w_qkv, tabs_a, g), window, dilation)
        outs.append(o)
        lses.append(lse)
    return _out_proj(_comb_a(outs, lses), w_o, h)


def _b_in_kernel(x_ref, g_ref, w_ref, o_ref):
    o_ref[...] = _dot(_rms(x_ref[...], g_ref[...]).astype(BF16), w_ref[...])


def _b_in(h, gain, w_in):
    tm = 512
    return pl.pallas_call(
        _b_in_kernel,
        out_shape=jax.ShapeDtypeStruct((TOKENS, B_IN_COLS), F32),
        grid=(TOKENS // tm,),
        in_specs=[
            pl.BlockSpec((tm, D_MODEL), lambda i: (i, 0)),
            pl.BlockSpec((1, D_MODEL), lambda i: (0, 0)),
            pl.BlockSpec((D_MODEL, B_IN_COLS), lambda i: (0, 0)),
        ],
        out_specs=pl.BlockSpec((tm, B_IN_COLS), lambda i: (i, 0)),
        compiler_params=_cparams("parallel"),
        name="b_in",
    )(h, gain.reshape(1, D_MODEL), w_in)


def _b_q_kernel(c_ref, g_ref, w_ref, cos_ref, lo_ref, hi_ref, o_ref, xn_ref, *, slots):
    @pl.when(pl.program_id(1) == 0)
    def _():
        xn_ref[...] = _rms(c_ref[...], g_ref[...]).astype(BF16)

    y = _dot(xn_ref[...], w_ref[...])
    scale = (B_NOPE_DIM + B_ROPE_DIM) ** -0.5
    cos, lo, hi = cos_ref[...], lo_ref[...], hi_ref[...]
    for s in range(slots):
        nope = y[:, s * B_SLOT:s * B_SLOT + LANES]
        rot = _rope128(y[:, s * B_SLOT + LANES:(s + 1) * B_SLOT], cos, lo, hi, B_ROT_HALF)
        o_ref[s, :, :LANES] = (nope * scale).astype(BF16)
        o_ref[s, :, LANES:] = (rot * scale).astype(BF16)


def _b_q(c, gain, w_q, tabs_b):
    tm, slots = 1024, 4
    tn = slots * B_SLOT
    tab_spec = pl.BlockSpec((tm, LANES), lambda i, j: (i, 0))
    return pl.pallas_call(
        functools.partial(_b_q_kernel, slots=slots),
        out_shape=jax.ShapeDtypeStruct((B_HEADS, TOKENS, B_SLOT), BF16),
        grid=(TOKENS // tm, B_HEADS // slots),
        in_specs=[
            pl.BlockSpec((tm, B_Q_RANK), lambda i, j: (i, 0)),
            pl.BlockSpec((1, B_Q_RANK), lambda i, j: (0, 0)),
            pl.BlockSpec((B_Q_RANK, tn), lambda i, j: (0, j)),
            tab_spec, tab_spec, tab_spec,
        ],
        out_specs=pl.BlockSpec((slots, tm, B_SLOT), lambda i, j: (j, i, 0)),
        scratch_shapes=[pltpu.VMEM((tm, B_Q_RANK), BF16)],
        compiler_params=_cparams("parallel", "arbitrary"),
        name="b_q",
    )(c, gain.reshape(1, B_Q_RANK), w_q, *tabs_b)


def _b_kv_kernel(c_ref, g_ref, wk_ref, wv_ref, cr_ref, cos_ref, lo_ref, hi_ref, k_ref, v_ref, *, slots):
    xn = _rms(c_ref[...], g_ref[...]).astype(BF16)
    yk = _dot(xn, wk_ref[...])
    yv = _dot(xn, wv_ref[...])
    kr = _rope128(cr_ref[...], cos_ref[...], lo_ref[...], hi_ref[...], B_ROT_HALF).astype(BF16)
    for s in range(slots):
        sl = slice(s * LANES, (s + 1) * LANES)
        k_ref[s, :, :LANES] = yk[:, sl].astype(BF16)
        k_ref[s, :, LANES:] = kr
        v_ref[s] = yv[:, sl].astype(BF16)


def _b_kv(c, gain, w_k, w_v, tabs_b):
    tm, slots = 1024, 4
    tn = slots * LANES
    kv_block = B_Q_RANK // B_KV_RANK
    rope_block = (B_Q_RANK + B_KV_RANK) // LANES
    tab_spec = pl.BlockSpec((tm, LANES), lambda i, j: (i, 0))
    return pl.pallas_call(
        functools.partial(_b_kv_kernel, slots=slots),
        out_shape=(jax.ShapeDtypeStruct((B_HEADS, TOKENS, B_SLOT), BF16),
                   jax.ShapeDtypeStruct((B_HEADS, TOKENS, B_V_DIM), BF16)),
        grid=(TOKENS // tm, B_HEADS // slots),
        in_specs=[
            pl.BlockSpec((tm, B_KV_RANK), lambda i, j: (i, kv_block)),
            pl.BlockSpec((1, B_KV_RANK), lambda i, j: (0, 0)),
            pl.BlockSpec((B_KV_RANK, tn), lambda i, j: (0, j)),
            pl.BlockSpec((B_KV_RANK, tn), lambda i, j: (0, j)),
            pl.BlockSpec((tm, LANES), lambda i, j: (i, rope_block)),
            tab_spec, tab_spec, tab_spec,
        ],
        out_specs=(pl.BlockSpec((slots, tm, B_SLOT), lambda i, j: (j, i, 0)),
                   pl.BlockSpec((slots, tm, B_V_DIM), lambda i, j: (j, i, 0))),
        compiler_params=_cparams("parallel", "parallel"),
        name="b_kv",
    )(c, gain.reshape(1, B_KV_RANK), w_k, w_v, c, *tabs_b)


def _attn_b_kernel(q_ref, k_ref, v_ref, o_ref, s0_ref, s1_ref, m_ref, l_ref, acc_ref, *, tq):
    qi = pl.program_id(2)
    q = q_ref[0]
    m_ref[...] = jnp.full_like(m_ref, NEG_INF)
    l_ref[...] = jnp.zeros_like(l_ref)
    acc_ref[...] = jnp.zeros_like(acc_ref)

    def scores(ki):
        return _dot_t(q, k_ref[0, pl.ds(pl.multiple_of(ki * tq, tq), tq), :])

    def fold(s_ref, ki, masked):
        s = s_ref[...]
        if masked:
            row = qi * tq + lax.broadcasted_iota(jnp.int32, (tq, tq), 0)
            col = ki * tq + lax.broadcasted_iota(jnp.int32, (tq, tq), 1)
            s = jnp.where(row >= col, s, NEG_INF)
        m_prev = m_ref[...]
        m_new = jnp.maximum(m_prev, jnp.max(s, axis=-1, keepdims=True))
        alpha = jnp.exp(m_prev - m_new)
        p = jnp.exp(s - pltpu.repeat(m_new, tq // LANES, 1))
        l_ref[...] = alpha * l_ref[...] + jnp.sum(p, axis=-1, keepdims=True)
        v = v_ref[0, pl.ds(pl.multiple_of(ki * tq, tq), tq), :]
        acc_ref[...] = alpha * acc_ref[...] + _dot(p.astype(BF16), v)
        m_ref[...] = m_new

    last_pair = qi // 2
    s0_ref[...] = scores(0)

    def body(t, carry):
        s1_ref[...] = scores(2 * t + 1)
        fold(s0_ref, 2 * t, False)
        s0_ref[...] = scores(2 * t + 2)
        fold(s1_ref, 2 * t + 1, False)
        return carry

    lax.fori_loop(0, last_pair, body, 0)
    s1_ref[...] = scores(2 * last_pair + 1)
    fold(s0_ref, 2 * last_pair, True)
    fold(s1_ref, 2 * last_pair + 1, True)
    o_ref[...] = (acc_ref[...] / l_ref[...]).astype(BF16)


def _attn_b(q3, k3, v3):
    tq = 512
    nq = SEQ // tq
    assert nq % 2 == 0, "the paired sweep reads key block qi + 1 when qi is even"
    return pl.pallas_call(
        functools.partial(_attn_b_kernel, tq=tq),
        out_shape=jax.ShapeDtypeStruct((TOKENS, B_HEADS * B_V_DIM), BF16),
        grid=(BATCH, B_HEADS, nq),
        in_specs=[
            pl.BlockSpec((1, tq, B_SLOT), lambda b, h, qi: (h, b * nq + qi, 0)),
            pl.BlockSpec((1, SEQ, B_SLOT), lambda b, h, qi: (h, b, 0)),
            pl.BlockSpec((1, SEQ, B_V_DIM), lambda b, h, qi: (h, b, 0)),
        ],
        out_specs=pl.BlockSpec((tq, B_V_DIM), lambda b, h, qi: (b * nq + qi, h)),
        scratch_shapes=[pltpu.VMEM((tq, tq), F32), pltpu.VMEM((tq, tq), F32),
                        pltpu.VMEM((tq, LANES), F32), pltpu.VMEM((tq, LANES), F32),
                        pltpu.VMEM((tq, B_V_DIM), F32)],
        compiler_params=_cparams("parallel", "parallel", "arbitrary"),
        name="attn_b",
    )(q3, k3, v3)


def _mixer_b(h, gain, w_in, q_norm, w_q, kv_norm, w_k, w_v, w_o, tabs_b):
    c = _b_in(h, gain, w_in)
    q3 = _b_q(c, q_norm, w_q, tabs_b)
    k3, v3 = _b_kv(c, kv_norm, w_k, w_v, tabs_b)
    return _out_proj(_attn_b(q3, k3, v3), w_o, h)


def _c_in_kernel(x_ref, g_ref, wb_ref, wc_ref, wu_ref, cw_ref, o_ref, xn_ref, carry_ref, *, tm):
    i = pl.program_id(0)
    j = pl.program_id(1)

    @pl.when(j == 0)
    def _():
        xn_ref[...] = _rms(x_ref[...], g_ref[...]).astype(BF16)

    xn = xn_ref[...]
    b_gate = _dot(xn, wb_ref[...])
    cu = _dot(xn, wc_ref[...]) * _dot(xn, wu_ref[...])
    w0, w1, w2 = cw_ref[0:1, :], cw_ref[1:2, :], cw_ref[2:3, :]
    z = w2 * cu + w1 * pltpu.roll(cu, 1, 0) + w0 * pltpu.roll(cu, 2, 0)
    o_ref[...] = (b_gate * z).astype(BF16)

    first = (i * tm) % SEQ == 0
    tail = jnp.where(first, 0.0, carry_ref[j])
    both = jnp.concatenate([tail, cu[0:8, :]], axis=0)
    z8 = w2 * both[8:16] + w1 * both[7:15] + w0 * both[6:14]
    o_ref[0:8, :] = (b_gate[0:8, :] * z8).astype(BF16)
    carry_ref[j] = cu[tm - 8:tm, :]


def _c_in(h, gain, w_in, conv_w):
    tm, tn = 1024, 512
    nj = D_MODEL // tn
    return pl.pallas_call(
        functools.partial(_c_in_kernel, tm=tm),
        out_shape=jax.ShapeDtypeStruct((TOKENS, D_MODEL), BF16),
        grid=(TOKENS // tm, nj),
        in_specs=[
            pl.BlockSpec((tm, D_MODEL), lambda i, j: (i, 0)),
            pl.BlockSpec((1, D_MODEL), lambda i, j: (0, 0)),
            pl.BlockSpec((D_MODEL, tn), lambda i, j: (0, j)),
            pl.BlockSpec((D_MODEL, tn), lambda i, j: (0, j + nj)),
            pl.BlockSpec((D_MODEL, tn), lambda i, j: (0, j + 2 * nj)),
            pl.BlockSpec((C_CONV, tn), lambda i, j: (0, j)),
        ],
        out_specs=pl.BlockSpec((tm, tn), lambda i, j: (i, j)),
        scratch_shapes=[pltpu.VMEM((tm, D_MODEL), BF16), pltpu.VMEM((nj, 8, tn), F32)],
        compiler_params=_cparams("arbitrary", "arbitrary"),
        name="c_in",
    )(h, gain.reshape(1, D_MODEL), w_in, w_in, w_in, conv_w)


def _mixer_c(h, gain, w_in, conv_w, w_out):
    return _out_proj(_c_in(h, gain, w_in, conv_w), w_out, h)


def kernel(x, positions, l0_ffn1_norm, l0_ffn1_w13, l0_ffn1_w2, l0_mix_norm, l0_a_w_qkv, l0_a_w_o, l0_ffn2_norm, l0_ffn2_w13, l0_ffn2_w2, l1_ffn1_norm, l1_ffn1_w13, l1_ffn1_w2, l1_mix_norm, l1_b_w_in, l1_b_q_norm, l1_b_w_uq, l1_b_kv_norm, l1_b_w_ukv, l1_b_w_o, l1_ffn2_norm, l1_ffn2_w13, l1_ffn2_w2, l2_ffn1_norm, l2_ffn1_w13, l2_ffn1_w2, l2_mix_norm, l2_c_w_in, l2_c_conv_w, l2_c_w_out, l2_ffn2_norm, l2_ffn2_w13, l2_ffn2_w2, l3_ffn1_norm, l3_ffn1_w13, l3_ffn1_w2, l3_mix_norm, l3_a_w_qkv, l3_a_w_o, l3_ffn2_norm, l3_ffn2_w13, l3_ffn2_w2, final_norm):
    bf = lambda w: w.astype(BF16)
    pos_tiles = positions.reshape(TOKENS // A_TILE, A_TILE)
    pos_a = [pos_tiles.reshape(-1, A_TILE // d, d).swapaxes(1, 2).reshape(TOKENS) for _, d in A_PATTERNS]
    tabs_a = _rope_tables(jnp.concatenate(pos_a), A_ROT_HALF)
    tabs_b = _rope_tables(positions.reshape(TOKENS), B_ROT_HALF)

    w_uq = l1_b_w_uq.reshape(B_Q_RANK, B_HEADS, B_NOPE_DIM + B_ROPE_DIM)
    w_q = jnp.pad(w_uq, ((0, 0), (0, 0), (0, B_SLOT - B_NOPE_DIM - B_ROPE_DIM)))
    w_q = bf(w_q.reshape(B_Q_RANK, B_HEADS * B_SLOT))
    w_ukv = l1_b_w_ukv.reshape(B_KV_RANK, B_HEADS, B_NOPE_DIM + B_V_DIM)
    w_k = bf(w_ukv[:, :, :B_NOPE_DIM].reshape(B_KV_RANK, B_HEADS * B_NOPE_DIM))
    w_v = bf(w_ukv[:, :, B_NOPE_DIM:].reshape(B_KV_RANK, B_HEADS * B_V_DIM))
    w_b_in = bf(jnp.pad(l1_b_w_in, ((0, 0), (0, B_IN_COLS - l1_b_w_in.shape[1]))))

    h = x.reshape(TOKENS, D_MODEL)

    h = _ffn(h, l0_ffn1_norm, bf(l0_ffn1_w13), bf(l0_ffn1_w2))
    h = _mixer_a(h, l0_mix_norm, bf(l0_a_w_qkv), bf(l0_a_w_o), tabs_a)
    h = _ffn(h, l0_ffn2_norm, bf(l0_ffn2_w13), bf(l0_ffn2_w2))

    h = _ffn(h, l1_ffn1_norm, bf(l1_ffn1_w13), bf(l1_ffn1_w2))
    h = _mixer_b(h, l1_mix_norm, w_b_in, l1_b_q_norm, w_q, l1_b_kv_norm, w_k, w_v, bf(l1_b_w_o), tabs_b)
    h = _ffn(h, l1_ffn2_norm, bf(l1_ffn2_w13), bf(l1_ffn2_w2))

    h = _ffn(h, l2_ffn1_norm, bf(l2_ffn1_w13), bf(l2_ffn1_w2))
    h = _mixer_c(h, l2_mix_norm, bf(l2_c_w_in), l2_c_conv_w, bf(l2_c_w_out))
    h = _ffn(h, l2_ffn2_norm, bf(l2_ffn2_w13), bf(l2_ffn2_w2))

    h = _ffn(h, l3_ffn1_norm, bf(l3_ffn1_w13), bf(l3_ffn1_w2))
    h = _mixer_a(h, l3_mix_norm, bf(l3_a_w_qkv), bf(l3_a_w_o), tabs_a)
    h = _ffn(h, l3_ffn2_norm, bf(l3_ffn2_w13), bf(l3_ffn2_w2), final_gain=final_norm)

    return h.reshape(BATCH, SEQ, D_MODEL)
```

```python
import functools

import jax
import jax.numpy as jnp
from jax import lax
from jax.experimental import pallas as pl
from jax.experimental.pallas import tpu as pltpu

F32 = jnp.float32
BF16 = jnp.bfloat16

D_MODEL = 2048
BATCH = 2
SEQ = 8192
TOKENS = BATCH * SEQ
D_FF = 5632
HALF_STEP = 0.5
NORM_EPS = 1e-6
NEG_INF = -1e30
ROPE_THETA = 500000.0

A_HEADS = 16
A_HEAD_DIM = 128
A_ROT_HALF = A_HEAD_DIM // 8
A_PATTERNS = ((128, 1), (512, 4), (2048, 16))
A_BLOCK = 128
A_TILE = 2048

B_HEADS = 16
B_Q_RANK = 1536
B_KV_RANK = 512
B_NOPE_DIM = 128
B_ROPE_DIM = 64
B_ROT_HALF = B_ROPE_DIM // 2
B_V_DIM = 128
B_SLOT = 256
B_IN_COLS = 2176

C_CONV = 3

LANES = 128
MXU_COLS = 256
ROPE_PAIR = LANES // 2
VMEM_LIMIT = 56 * 1024 * 1024


def _cparams(*sem):
    return pltpu.CompilerParams(dimension_semantics=sem, vmem_limit_bytes=VMEM_LIMIT)


def _rms(x, gain):
    ms = jnp.mean(x * x, axis=-1, keepdims=True)
    return x * lax.rsqrt(ms + NORM_EPS) * gain


def _dot(a, b):
    return jnp.dot(a, b, preferred_element_type=F32)


def _dot_t(a, b):
    return lax.dot_general(a, b, (((1,), (1,)), ((), ())), preferred_element_type=F32)


def _rope128(y, cos, sin):
    return y * cos + pltpu.roll(y, ROPE_PAIR, 1) * sin


def _rope_lanes(w, half):
    return jnp.concatenate([w[..., :half], w[..., 2 * half:ROPE_PAIR + half], w[..., half:2 * half],
                            w[..., ROPE_PAIR + half:]], axis=-1)


def _rope_tab_kernel(pos_ref, inv_ref, c_ref, s_ref):
    ang = pos_ref[...].astype(F32) * inv_ref[...]
    lane = lax.broadcasted_iota(jnp.int32, (1, LANES), 1)
    sn = jnp.sin(ang)
    c_ref[...] = jnp.cos(ang)
    s_ref[...] = jnp.where(lane < ROPE_PAIR, -sn, sn)


def _rope_tables(pos_flat, half):
    inv = ROPE_THETA ** (-jnp.arange(half, dtype=F32) / half)
    gap = jnp.zeros((ROPE_PAIR - half,), F32)
    inv = jnp.concatenate([inv, gap, inv, gap]).reshape(1, LANES)
    n = pos_flat.shape[0]
    rows = 2048
    tab = jax.ShapeDtypeStruct((n, LANES), F32)
    row_spec = pl.BlockSpec((rows, LANES), lambda i: (i, 0))
    return pl.pallas_call(
        _rope_tab_kernel,
        out_shape=(tab,) * 2,
        grid=(n // rows,),
        in_specs=[pl.BlockSpec((rows, 1), lambda i: (i, 0)), pl.BlockSpec((1, LANES), lambda i: (0, 0))],
        out_specs=(row_spec,) * 2,
        compiler_params=_cparams("parallel"),
        name="rope_tables",
    )(pos_flat.reshape(n, 1), inv)


def _ffn_kernel(x_ref, g_ref, w1_ref, w3_ref, w2_ref, fg_ref, o_ref, xn_ref, acc_ref, *, final):
    j = pl.program_id(1)

    @pl.when(j == 0)
    def _():
        xn_ref[...] = _rms(x_ref[...], g_ref[...]).astype(BF16)
        acc_ref[...] = jnp.zeros_like(acc_ref)

    xn = xn_ref[...]
    gate = _dot(xn, w1_ref[...])
    up = _dot(xn, w3_ref[...])
    act = (gate * (1.0 / (1.0 + jnp.exp(-gate))) * up).astype(BF16)
    acc_ref[...] += _dot(act, w2_ref[...])

    @pl.when(j == pl.num_programs(1) - 1)
    def _():
        y = x_ref[...] + HALF_STEP * acc_ref[...]
        if final:
            y = _rms(y, fg_ref[...])
        o_ref[...] = y


def _ffn(h, gain, w13, w2, final_gain=None):
    tm, tf = 512, 512
    nj = D_FF // tf
    final = final_gain is not None
    fg = final_gain if final else gain
    return pl.pallas_call(
        functools.partial(_ffn_kernel, final=final),
        out_shape=jax.ShapeDtypeStruct((TOKENS, D_MODEL), F32),
        grid=(TOKENS // tm, nj),
        in_specs=[
            pl.BlockSpec((tm, D_MODEL), lambda i, j: (i, 0)),
            pl.BlockSpec((1, D_MODEL), lambda i, j: (0, 0)),
            pl.BlockSpec((D_MODEL, tf), lambda i, j: (0, j)),
            pl.BlockSpec((D_MODEL, tf), lambda i, j: (0, j + nj)),
            pl.BlockSpec((tf, D_MODEL), lambda i, j: (j, 0)),
            pl.BlockSpec((1, D_MODEL), lambda i, j: (0, 0)),
        ],
        out_specs=pl.BlockSpec((tm, D_MODEL), lambda i, j: (i, 0)),
        scratch_shapes=[pltpu.VMEM((tm, D_MODEL), BF16), pltpu.VMEM((tm, D_MODEL), F32)],
        compiler_params=_cparams("parallel", "arbitrary"),
        name="ffn",
    )(h, gain.reshape(1, D_MODEL), w13, w13, w2, fg.reshape(1, D_MODEL))


def _out_proj_kernel(x_ref, w_ref, r_ref, o_ref):
    o_ref[...] = r_ref[...] + _dot(x_ref[...], w_ref[...])


def _out_proj(x, w, res):
    k = x.shape[1]
    tm, tn = 1024, 1024
    return pl.pallas_call(
        _out_proj_kernel,
        out_shape=jax.ShapeDtypeStruct((TOKENS, D_MODEL), F32),
        grid=(TOKENS // tm, D_MODEL // tn),
        in_specs=[
            pl.BlockSpec((tm, k), lambda i, j: (i, 0)),
            pl.BlockSpec((k, tn), lambda i, j: (0, j)),
            pl.BlockSpec((tm, tn), lambda i, j: (i, j)),
        ],
        out_specs=pl.BlockSpec((tm, tn), lambda i, j: (i, j)),
        compiler_params=_cparams("parallel", "parallel"),
        name="out_proj",
    )(x, w, res)


def _norm_perm_a_kernel(x_ref, g_ref, *refs, tm):
    o_refs, xs_ref = refs[:-1], refs[-1]
    xn = _rms(x_ref[...], g_ref[...])
    nchunk = D_MODEL // LANES
    for k in range(nchunk):
        xs_ref[k] = xn[:, k * LANES:(k + 1) * LANES]
    for (_, d), o_ref in zip(A_PATTERNS, o_refs):
        if d == 1:
            o_ref[...] = xn.astype(BF16)
        else:
            for c in range(d):
                for k in range(nchunk):
                    o_ref[0, c, 0, :, k * LANES:(k + 1) * LANES] = (
                        xs_ref[k, pl.ds(c, tm // d, stride=d), :].astype(BF16))


def _norm_perm_a(h, gain):
    tm = 512
    per_tile = A_TILE // tm
    shapes, specs = [], []
    for _, d in A_PATTERNS:
        if d == 1:
            shapes.append(jax.ShapeDtypeStruct((TOKENS, D_MODEL), BF16))
            specs.append(pl.BlockSpec((tm, D_MODEL), lambda i: (i, 0)))
        else:
            shapes.append(jax.ShapeDtypeStruct((TOKENS // A_TILE, d, per_tile, tm // d, D_MODEL), BF16))
            specs.append(pl.BlockSpec((1, d, 1, tm // d, D_MODEL),
                                      lambda i: (i // per_tile, 0, i % per_tile, 0, 0)))
    outs = pl.pallas_call(
        functools.partial(_norm_perm_a_kernel, tm=tm),
        out_shape=tuple(shapes),
        grid=(TOKENS // tm,),
        in_specs=[pl.BlockSpec((tm, D_MODEL), lambda i: (i, 0)), pl.BlockSpec((1, D_MODEL), lambda i: (0, 0))],
        out_specs=tuple(specs),
        scratch_shapes=[pltpu.VMEM((D_MODEL // LANES, tm, LANES), F32)],
        compiler_params=_cparams("parallel"),
        name="norm_perm_a",
    )(h, gain.reshape(1, D_MODEL))
    return [o.reshape(TOKENS, D_MODEL) for o in outs]


def _qkv_a_kernel(x_ref, w_ref, cos_ref, sin_ref, o_ref, *, tn):
    part = pl.program_id(1) // ((A_HEADS * A_HEAD_DIM) // tn)
    scale = jnp.where(part == 0, A_HEAD_DIM ** -0.5, 1.0).astype(F32)
    cos, sin = cos_ref[...], sin_ref[...]
    x = x_ref[...]
    for cc in range(tn // MXU_COLS):
        y = _dot(x, w_ref[:, cc * MXU_COLS:(cc + 1) * MXU_COLS])
        for hh in range(MXU_COLS // LANES):
            sl = slice(cc * MXU_COLS + hh * LANES, cc * MXU_COLS + (hh + 1) * LANES)
            y_h = y[:, hh * LANES:(hh + 1) * LANES]
            o_ref[:, sl] = (_rope128(y_h, cos, sin) * scale).astype(BF16)


def _qkv_a(xn, w_qkv, tabs, group):
    tm, tn = 1024, 1024
    cols = 3 * A_HEADS * A_HEAD_DIM
    nj = cols // tn
    ni = TOKENS // tm
    v_first = 2 * (A_HEADS * A_HEAD_DIM) // tn
    tab_spec = pl.BlockSpec((tm, LANES),
                            lambda i, j: (jnp.where(j >= v_first, len(A_PATTERNS) * ni, group * ni + i), 0))
    return pl.pallas_call(
        functools.partial(_qkv_a_kernel, tn=tn),
        out_shape=jax.ShapeDtypeStruct((TOKENS, cols), BF16),
        grid=(ni, nj),
        in_specs=[
            pl.BlockSpec((tm, D_MODEL), lambda i, j: (i, 0)),
            pl.BlockSpec((D_MODEL, tn), lambda i, j: (0, group * nj + j)),
            tab_spec, tab_spec,
        ],
        out_specs=pl.BlockSpec((tm, tn), lambda i, j: (i, j)),
        compiler_params=_cparams("parallel", "parallel"),
        name="qkv_a",
    )(xn, w_qkv, *tabs)


def _attn_a_kernel(q_ref, kc_ref, kp_ref, vc_ref, vp_ref, o_ref, lse_ref, *, d, heads, per_trip):
    rows = A_TILE // d
    nblk = rows // A_BLOCK
    first_tile = pl.program_id(1) == 0
    r = lax.broadcasted_iota(jnp.int32, (A_BLOCK, 2 * A_BLOCK), 0)
    kk = lax.broadcasted_iota(jnp.int32, (A_BLOCK, 2 * A_BLOCK), 1)
    band = (kk >= r) & (kk <= r + A_BLOCK)
    lane = lax.broadcasted_iota(jnp.int32, (A_BLOCK, LANES), 1)
    hs = [slice(h * A_HEAD_DIM, (h + 1) * A_HEAD_DIM) for h in range(heads)]

    def trip(t, carry):
        loaded = []
        for i in range(per_trip):
            u = t * per_trip + i
            c = u // nblk
            blk = u % nblk
            start = pl.multiple_of(blk * A_BLOCK, A_BLOCK)
            before = pl.multiple_of(jnp.maximum(start - A_BLOCK, 0), A_BLOCK)
            q = q_ref[0, c, pl.ds(start, A_BLOCK), :]
            k = jnp.concatenate([jnp.where(blk == 0, kp_ref[0, c], kc_ref[0, c, pl.ds(before, A_BLOCK), :]),
                                 kc_ref[0, c, pl.ds(start, A_BLOCK), :]], axis=0)
            v = jnp.concatenate([jnp.where(blk == 0, vp_ref[0, c], vc_ref[0, c, pl.ds(before, A_BLOCK), :]),
                                 vc_ref[0, c, pl.ds(start, A_BLOCK), :]], axis=0)
            no_prev = jnp.logical_and(first_tile, blk == 0)
            mask = jnp.logical_and(band, jnp.logical_or(kk >= A_BLOCK, jnp.logical_not(no_prev)))
            out_rows = (pl.ds(blk * (A_BLOCK * d) + c, A_BLOCK, stride=d) if d > 1
                        else pl.ds(start, A_BLOCK))
            loaded.append(([_dot_t(q[:, sl], k[:, sl]) for sl in hs], v, mask, out_rows))
        for scores, v, mask, out_rows in loaded:
            lse_tile = jnp.zeros((A_BLOCK, LANES), F32)
            for h, sl in enumerate(hs):
                s = jnp.where(mask, scores[h], NEG_INF)
                m = jnp.max(s, axis=-1, keepdims=True)
                p = jnp.exp(s - m)
                l = jnp.sum(p, axis=-1, keepdims=True)
                o_ref[h, out_rows, :] = _dot(p.astype(BF16), v[:, sl]) / l
                lse_tile = jnp.where(lane == h, m + jnp.log(l), lse_tile)
            lse_ref[out_rows, :] = lse_tile
        return carry

    lax.fori_loop(0, d * nblk // per_trip, trip, 0)


def _attn_a(qkv, window, dilation):
    assert window // dilation == A_BLOCK, "band width must equal the query block"
    d = dilation
    heads = 4
    hw = heads * A_HEAD_DIM
    ngrp = A_HEADS // heads
    rows = A_TILE // d
    tiles = SEQ // A_TILE
    qkv_v = qkv.reshape(BATCH, tiles, d, rows, 3 * A_HEADS * A_HEAD_DIM)
    last_blk = rows // A_BLOCK - 1

    def cur(part):
        return pl.BlockSpec((None, 1, d, rows, hw), lambda b, n, g: (b, n, 0, 0, part * ngrp + g))

    def prev(part):
        return pl.BlockSpec((None, 1, d, A_BLOCK, hw),
                            lambda b, n, g: (b, jnp.maximum(n - 1, 0), 0, last_blk, part * ngrp + g))

    return pl.pallas_call(
        functools.partial(_attn_a_kernel, d=d, heads=heads, per_trip=2),
        out_shape=(jax.ShapeDtypeStruct((A_HEADS, TOKENS, A_HEAD_DIM), F32),
                   jax.ShapeDtypeStruct((TOKENS, ngrp * LANES), F32)),
        grid=(BATCH, tiles, ngrp),
        in_specs=[cur(0), cur(1), prev(1), cur(2), prev(2)],
        out_specs=(pl.BlockSpec((heads, A_TILE, A_HEAD_DIM), lambda b, n, g: (g, b * tiles + n, 0)),
                   pl.BlockSpec((A_TILE, LANES), lambda b, n, g: (b * tiles + n, g))),
        compiler_params=_cparams("parallel", "parallel", "parallel"),
        name="attn_a_d%d" % d,
    )(qkv_v, qkv_v, qkv_v, qkv_v, qkv_v)


def _comb_a_kernel(o0, o1, o2, l0, l1, l2, out_ref, *, heads):
    a = [l0[...], l1[...], l2[...]]
    m = jnp.maximum(jnp.maximum(a[0], a[1]), a[2])
    e = [jnp.exp(t - m) for t in a]
    den = e[0] + e[1] + e[2]
    w = [t / den for t in e]
    rows = out_ref.shape[0]
    for h in range(A_HEADS):
        col = (h // heads) * LANES + h % heads
        acc = jnp.zeros((rows, A_HEAD_DIM), F32)
        for wg, og in zip(w, (o0, o1, o2)):
            acc = acc + jnp.broadcast_to(wg[:, col:col + 1], (rows, A_HEAD_DIM)) * og[h]
        out_ref[:, h * A_HEAD_DIM:(h + 1) * A_HEAD_DIM] = acc.astype(BF16)


def _comb_a(outs, lses):
    tm = 512
    lw = lses[0].shape[1]
    o_spec = pl.BlockSpec((A_HEADS, tm, A_HEAD_DIM), lambda i: (0, i, 0))
    l_spec = pl.BlockSpec((tm, lw), lambda i: (i, 0))
    return pl.pallas_call(
        functools.partial(_comb_a_kernel, heads=A_HEADS * LANES // lw),
        out_shape=jax.ShapeDtypeStruct((TOKENS, A_HEADS * A_HEAD_DIM), BF16),
        grid=(TOKENS // tm,),
        in_specs=[o_spec] * 3 + [l_spec] * 3,
        out_specs=pl.BlockSpec((tm, A_HEADS * A_HEAD_DIM), lambda i: (i, 0)),
        compiler_params=_cparams("parallel"),
        name="comb_a",
    )(*outs, *lses)


def _mixer_a(h, gain, w_qkv, w_o, tabs_a):
    outs, lses = [], []
    for g, ((window, dilation), xn) in enumerate(zip(A_PATTERNS, _norm_perm_a(h, gain))):
        o, lse = _attn_a(_qkv_a(xn, w_qkv, tabs_a, g), window, dilation)
        outs.append(o)
        lses.append(lse)
    return _out_proj(_comb_a(outs, lses), w_o, h)


def _b_in_kernel(x_ref, g_ref, w_ref, o_ref):
    o_ref[...] = _dot(_rms(x_ref[...], g_ref[...]).astype(BF16), w_ref[...])


def _b_in(h, gain, w_in):
    tm = 512
    return pl.pallas_call(
        _b_in_kernel,
        out_shape=jax.ShapeDtypeStruct((TOKENS, B_IN_COLS), F32),
        grid=(TOKENS // tm,),
        in_specs=[
            pl.BlockSpec((tm, D_MODEL), lambda i: (i, 0)),
            pl.BlockSpec((1, D_MODEL), lambda i: (0, 0)),
            pl.BlockSpec((D_MODEL, B_IN_COLS), lambda i: (0, 0)),
        ],
        out_specs=pl.BlockSpec((tm, B_IN_COLS), lambda i: (i, 0)),
        compiler_params=_cparams("parallel"),
        name="b_in",
    )(h, gain.reshape(1, D_MODEL), w_in)


def _b_q_kernel(c_ref, g_ref, w_ref, cos_ref, sin_ref, o_ref, xn_ref, *, slots):
    @pl.when(pl.program_id(1) == 0)
    def _():
        xn_ref[...] = _rms(c_ref[...], g_ref[...]).astype(BF16)

    y = _dot(xn_ref[...], w_ref[...])
    scale = (B_NOPE_DIM + B_ROPE_DIM) ** -0.5
    cos, sin = cos_ref[...], sin_ref[...]
    for s in range(slots):
        nope = y[:, s * B_SLOT:s * B_SLOT + LANES]
        rot = _rope128(y[:, s * B_SLOT + LANES:(s + 1) * B_SLOT], cos, sin)
        o_ref[s, :, :LANES] = (nope * scale).astype(BF16)
        o_ref[s, :, LANES:] = (rot * scale).astype(BF16)


def _b_q(c, gain, w_q, tabs_b):
    tm, slots = 1024, 4
    tn = slots * B_SLOT
    tab_spec = pl.BlockSpec((tm, LANES), lambda i, j: (i, 0))
    return pl.pallas_call(
        functools.partial(_b_q_kernel, slots=slots),
        out_shape=jax.ShapeDtypeStruct((B_HEADS, TOKENS, B_SLOT), BF16),
        grid=(TOKENS // tm, B_HEADS // slots),
        in_specs=[
            pl.BlockSpec((tm, B_Q_RANK), lambda i, j: (i, 0)),
            pl.BlockSpec((1, B_Q_RANK), lambda i, j: (0, 0)),
            pl.BlockSpec((B_Q_RANK, tn), lambda i, j: (0, j)),
            tab_spec, tab_spec,
        ],
        out_specs=pl.BlockSpec((slots, tm, B_SLOT), lambda i, j: (j, i, 0)),
        scratch_shapes=[pltpu.VMEM((tm, B_Q_RANK), BF16)],
        compiler_params=_cparams("parallel", "arbitrary"),
        name="b_q",
    )(c, gain.reshape(1, B_Q_RANK), w_q, *tabs_b)


def _b_kv_kernel(c_ref, g_ref, wk_ref, wv_ref, cr_ref, cos_ref, sin_ref, k_ref, v_ref, *, slots):
    xn = _rms(c_ref[...], g_ref[...]).astype(BF16)
    yk = _dot(xn, wk_ref[...])
    yv = _dot(xn, wv_ref[...])
    kr = _rope128(cr_ref[...], cos_ref[...], sin_ref[...]).astype(BF16)
    for s in range(slots):
        sl = slice(s * LANES, (s + 1) * LANES)
        k_ref[s, :, :LANES] = yk[:, sl].astype(BF16)
        k_ref[s, :, LANES:] = kr
        v_ref[s] = yv[:, sl].astype(BF16)


def _b_kv(c, gain, w_k, w_v, tabs_b):
    tm, slots = 1024, 4
    tn = slots * LANES
    kv_block = B_Q_RANK // B_KV_RANK
    rope_block = (B_Q_RANK + B_KV_RANK) // LANES
    tab_spec = pl.BlockSpec((tm, LANES), lambda i, j: (i, 0))
    return pl.pallas_call(
        functools.partial(_b_kv_kernel, slots=slots),
        out_shape=(jax.ShapeDtypeStruct((B_HEADS, TOKENS, B_SLOT), BF16),
                   jax.ShapeDtypeStruct((B_HEADS, TOKENS, B_V_DIM), BF16)),
        grid=(TOKENS // tm, B_HEADS // slots),
        in_specs=[
            pl.BlockSpec((tm, B_KV_RANK), lambda i, j: (i, kv_block)),
            pl.BlockSpec((1, B_KV_RANK), lambda i, j: (0, 0)),
            pl.BlockSpec((B_KV_RANK, tn), lambda i, j: (0, j)),
            pl.BlockSpec((B_KV_RANK, tn), lambda i, j: (0, j)),
            pl.BlockSpec((tm, LANES), lambda i, j: (i, rope_block)),
            tab_spec, tab_spec,
        ],
        out_specs=(pl.BlockSpec((slots, tm, B_SLOT), lambda i, j: (j, i, 0)),
                   pl.BlockSpec((slots, tm, B_V_DIM), lambda i, j: (j, i, 0))),
        compiler_params=_cparams("parallel", "parallel"),
        name="b_kv",
    )(c, gain.reshape(1, B_KV_RANK), w_k, w_v, c, *tabs_b)


def _attn_b_kernel(q_ref, k_ref, v_ref, o_ref, s0_ref, s1_ref, m_ref, l_ref, acc_ref, *, tq):
    qi = pl.program_id(2)
    q = q_ref[0]
    m_ref[...] = jnp.full_like(m_ref, NEG_INF)
    l_ref[...] = jnp.zeros_like(l_ref)
    acc_ref[...] = jnp.zeros_like(acc_ref)

    def scores(ki):
        return _dot_t(q, k_ref[0, pl.ds(pl.multiple_of(ki * tq, tq), tq), :])

    def fold(s_ref, ki, diagonal):
        s = s_ref[...]
        if diagonal:
            row = lax.broadcasted_iota(jnp.int32, (tq, tq), 0)
            col = lax.broadcasted_iota(jnp.int32, (tq, tq), 1)
            s = jnp.where(row >= col, s, NEG_INF)
        m_prev = m_ref[...]
        m_new = jnp.maximum(m_prev, jnp.max(s, axis=-1, keepdims=True))
        alpha = jnp.exp(m_prev - m_new)
        p = jnp.exp(s - jnp.concatenate([m_new] * (tq // LANES), axis=1))
        l_ref[...] = alpha * l_ref[...] + jnp.sum(p, axis=-1, keepdims=True)
        v = v_ref[0, pl.ds(pl.multiple_of(ki * tq, tq), tq), :]
        acc_ref[...] = alpha * acc_ref[...] + _dot(p.astype(BF16), v)
        m_ref[...] = m_new

    pairs = qi // 2
    s0_ref[...] = scores(0)

    def body(t, carry):
        s1_ref[...] = scores(2 * t + 1)
        fold(s0_ref, 2 * t, False)
        s0_ref[...] = scores(2 * t + 2)
        fold(s1_ref, 2 * t + 1, False)
        return carry

    lax.fori_loop(0, pairs, body, 0)

    @pl.when(qi % 2 == 1)
    def _():
        s1_ref[...] = scores(qi)
        fold(s0_ref, qi - 1, False)
        fold(s1_ref, qi, True)

    @pl.when(qi % 2 == 0)
    def _():
        fold(s0_ref, qi, True)

    o_ref[...] = (acc_ref[...] / l_ref[...]).astype(BF16)


def _attn_b(q3, k3, v3):
    tq = 512
    nq = SEQ // tq
    return pl.pallas_call(
        functools.partial(_attn_b_kernel, tq=tq),
        out_shape=jax.ShapeDtypeStruct((TOKENS, B_HEADS * B_V_DIM), BF16),
        grid=(BATCH, B_HEADS, nq),
        in_specs=[
            pl.BlockSpec((1, tq, B_SLOT), lambda b, h, qi: (h, b * nq + qi, 0)),
            pl.BlockSpec((1, SEQ, B_SLOT), lambda b, h, qi: (h, b, 0)),
            pl.BlockSpec((1, SEQ, B_V_DIM), lambda b, h, qi: (h, b, 0)),
        ],
        out_specs=pl.BlockSpec((tq, B_V_DIM), lambda b, h, qi: (b * nq + qi, h)),
        scratch_shapes=[pltpu.VMEM((tq, tq), F32), pltpu.VMEM((tq, tq), F32),
                        pltpu.VMEM((tq, LANES), F32), pltpu.VMEM((tq, LANES), F32),
                        pltpu.VMEM((tq, B_V_DIM), F32)],
        compiler_params=_cparams("parallel", "parallel", "arbitrary"),
        name="attn_b",
    )(q3, k3, v3)


def _mixer_b(h, gain, w_in, q_norm, w_q, kv_norm, w_k, w_v, w_o, tabs_b):
    c = _b_in(h, gain, w_in)
    q3 = _b_q(c, q_norm, w_q, tabs_b)
    k3, v3 = _b_kv(c, kv_norm, w_k, w_v, tabs_b)
    return _out_proj(_attn_b(q3, k3, v3), w_o, h)


def _c_in_kernel(x_ref, g_ref, wb_ref, wc_ref, wu_ref, cw_ref, o_ref, xn_ref, carry_ref, *, tm):
    i = pl.program_id(0)
    j = pl.program_id(1)

    @pl.when(j == 0)
    def _():
        xn_ref[...] = _rms(x_ref[...], g_ref[...]).astype(BF16)

    xn = xn_ref[...]
    b_gate = _dot(xn, wb_ref[...])
    cu = _dot(xn, wc_ref[...]) * _dot(xn, wu_ref[...])
    w0, w1, w2 = cw_ref[0:1, :], cw_ref[1:2, :], cw_ref[2:3, :]
    z = w2 * cu + w1 * pltpu.roll(cu, 1, 0) + w0 * pltpu.roll(cu, 2, 0)
    o_ref[...] = (b_gate * z).astype(BF16)

    first = (i * tm) % SEQ == 0
    tail = jnp.where(first, 0.0, carry_ref[j])
    both = jnp.concatenate([tail, cu[0:8, :]], axis=0)
    z8 = w2 * both[8:16] + w1 * both[7:15] + w0 * both[6:14]
    o_ref[0:8, :] = (b_gate[0:8, :] * z8).astype(BF16)
    carry_ref[j] = cu[tm - 8:tm, :]


def _c_in(h, gain, w_in, conv_w):
    tm, tn = 1024, 512
    nj = D_MODEL // tn
    return pl.pallas_call(
        functools.partial(_c_in_kernel, tm=tm),
        out_shape=jax.ShapeDtypeStruct((TOKENS, D_MODEL), BF16),
        grid=(TOKENS // tm, nj),
        in_specs=[
            pl.BlockSpec((tm, D_MODEL), lambda i, j: (i, 0)),
            pl.BlockSpec((1, D_MODEL), lambda i, j: (0, 0)),
            pl.BlockSpec((D_MODEL, tn), lambda i, j: (0, j)),
            pl.BlockSpec((D_MODEL, tn), lambda i, j: (0, j + nj)),
            pl.BlockSpec((D_MODEL, tn), lambda i, j: (0, j + 2 * nj)),
            pl.BlockSpec((C_CONV, tn), lambda i, j: (0, j)),
        ],
        out_specs=pl.BlockSpec((tm, tn), lambda i, j: (i, j)),
        scratch_shapes=[pltpu.VMEM((tm, D_MODEL), BF16), pltpu.VMEM((nj, 8, tn), F32)],
        compiler_params=_cparams("arbitrary", "arbitrary"),
        name="c_in",
    )(h, gain.reshape(1, D_MODEL), w_in, w_in, w_in, conv_w)


def _mixer_c(h, gain, w_in, conv_w, w_out):
    return _out_proj(_c_in(h, gain, w_in, conv_w), w_out, h)


def kernel(x, positions, l0_ffn1_norm, l0_ffn1_w13, l0_ffn1_w2, l0_mix_norm, l0_a_w_qkv, l0_a_w_o, l0_ffn2_norm, l0_ffn2_w13, l0_ffn2_w2, l1_ffn1_norm, l1_ffn1_w13, l1_ffn1_w2, l1_mix_norm, l1_b_w_in, l1_b_q_norm, l1_b_w_uq, l1_b_kv_norm, l1_b_w_ukv, l1_b_w_o, l1_ffn2_norm, l1_ffn2_w13, l1_ffn2_w2, l2_ffn1_norm, l2_ffn1_w13, l2_ffn1_w2, l2_mix_norm, l2_c_w_in, l2_c_conv_w, l2_c_w_out, l2_ffn2_norm, l2_ffn2_w13, l2_ffn2_w2, l3_ffn1_norm, l3_ffn1_w13, l3_ffn1_w2, l3_mix_norm, l3_a_w_qkv, l3_a_w_o, l3_ffn2_norm, l3_ffn2_w13, l3_ffn2_w2, final_norm):
    bf = lambda w: w.astype(BF16)
    pos_tiles = positions.reshape(TOKENS // A_TILE, A_TILE)
    pos_a = [pos_tiles.reshape(-1, A_TILE // d, d).swapaxes(1, 2).reshape(TOKENS) for _, d in A_PATTERNS]
    tabs_a = _rope_tables(jnp.concatenate(pos_a + [jnp.zeros((A_TILE,), positions.dtype)]), A_ROT_HALF)
    tabs_b = _rope_tables(positions.reshape(TOKENS), B_ROT_HALF)

    def rope_group(w):
        return _rope_lanes(jnp.pad(w, [(0, 0)] * (w.ndim - 1) + [(0, LANES - B_ROPE_DIM)]), B_ROT_HALF)

    w_uq = l1_b_w_uq.reshape(B_Q_RANK, B_HEADS, B_NOPE_DIM + B_ROPE_DIM)
    w_q = jnp.concatenate([w_uq[..., :B_NOPE_DIM], rope_group(w_uq[..., B_NOPE_DIM:])], axis=-1)
    w_q = bf(w_q.reshape(B_Q_RANK, B_HEADS * B_SLOT))
    w_ukv = l1_b_w_ukv.reshape(B_KV_RANK, B_HEADS, B_NOPE_DIM + B_V_DIM)
    w_k = bf(w_ukv[:, :, :B_NOPE_DIM].reshape(B_KV_RANK, B_HEADS * B_NOPE_DIM))
    w_v = bf(w_ukv[:, :, B_NOPE_DIM:].reshape(B_KV_RANK, B_HEADS * B_V_DIM))
    n_c = B_Q_RANK + B_KV_RANK
    w_b_in = bf(jnp.concatenate([l1_b_w_in[:, :n_c], rope_group(l1_b_w_in[:, n_c:])], axis=-1))

    def a_qkv(w):
        w5 = w.reshape(D_MODEL, len(A_PATTERNS), 3, A_HEADS, A_HEAD_DIM)
        w5 = jnp.concatenate([_rope_lanes(w5[:, :, :2], A_ROT_HALF), w5[:, :, 2:]], axis=2)
        return bf(w5.reshape(w.shape))

    h = x.reshape(TOKENS, D_MODEL)

    h = _ffn(h, l0_ffn1_norm, bf(l0_ffn1_w13), bf(l0_ffn1_w2))
    h = _mixer_a(h, l0_mix_norm, a_qkv(l0_a_w_qkv), bf(l0_a_w_o), tabs_a)
    h = _ffn(h, l0_ffn2_norm, bf(l0_ffn2_w13), bf(l0_ffn2_w2))

    h = _ffn(h, l1_ffn1_norm, bf(l1_ffn1_w13), bf(l1_ffn1_w2))
    h = _mixer_b(h, l1_mix_norm, w_b_in, l1_b_q_norm, w_q, l1_b_kv_norm, w_k, w_v, bf(l1_b_w_o), tabs_b)
    h = _ffn(h, l1_ffn2_norm, bf(l1_ffn2_w13), bf(l1_ffn2_w2))

    h = _ffn(h, l2_ffn1_norm, bf(l2_ffn1_w13), bf(l2_ffn1_w2))
    h = _mixer_c(h, l2_mix_norm, bf(l2_c_w_in), l2_c_conv_w, bf(l2_c_w_out))
    h = _ffn(h, l2_ffn2_norm, bf(l2_ffn2_w13), bf(l2_ffn2_w2))

    h = _ffn(h, l3_ffn1_norm, bf(l3_ffn1_w13), bf(l3_ffn1_w2))
    h = _mixer_a(h, l3_mix_norm, a_qkv(l3_a_w_qkv), bf(l3_a_w_o), tabs_a)
    h = _ffn(h, l3_ffn2_norm, bf(l3_ffn2_w13), bf(l3_ffn2_w2), final_gain=final_norm)

    return h.reshape(BATCH, SEQ, D_MODEL)
```

```python
import functools

import jax
import jax.numpy as jnp
from jax import lax
from jax.experimental import pallas as pl
from jax.experimental.pallas import tpu as pltpu

F32 = jnp.float32
BF16 = jnp.bfloat16

D_MODEL = 2048
BATCH = 2
SEQ = 8192
TOKENS = BATCH * SEQ
D_FF = 5632
HALF_STEP = 0.5
NORM_EPS = 1e-6
NEG_INF = -1e30
ROPE_THETA = 500000.0

A_HEADS = 16
A_HEAD_DIM = 128
A_ROT_HALF = A_HEAD_DIM // 8
A_PATTERNS = ((128, 1), (512, 4), (2048, 16))
A_BLOCK = 128
A_TILE = 2048

B_HEADS = 16
B_Q_RANK = 1536
B_KV_RANK = 512
B_NOPE_DIM = 128
B_ROPE_DIM = 64
B_ROT_HALF = B_ROPE_DIM // 2
B_V_DIM = 128
B_SLOT = 256
B_IN_COLS = 2176

C_CONV = 3

LANES = 128
MXU_COLS = 256
ROPE_PAIR = LANES // 2
VMEM_LIMIT = 56 * 1024 * 1024


def _cparams(*sem):
    return pltpu.CompilerParams(dimension_semantics=sem, vmem_limit_bytes=VMEM_LIMIT)


def _rms(x, gain):
    ms = jnp.mean(x * x, axis=-1, keepdims=True)
    return x * lax.rsqrt(ms + NORM_EPS) * gain


def _dot(a, b):
    return jnp.dot(a, b, preferred_element_type=F32)


def _dot_t(a, b):
    return lax.dot_general(a, b, (((1,), (1,)), ((), ())), preferred_element_type=F32)


def _rope128(y, cos, sin):
    return y * cos + pltpu.roll(y, ROPE_PAIR, 1) * sin


def _rope_lanes(w, half):
    return jnp.concatenate([w[..., :half], w[..., 2 * half:ROPE_PAIR + half], w[..., half:2 * half],
                            w[..., ROPE_PAIR + half:]], axis=-1)


def _rope_tab_kernel(pos_ref, inv_ref, c_ref, s_ref):
    ang = pos_ref[...].astype(F32) * inv_ref[...]
    lane = lax.broadcasted_iota(jnp.int32, (1, LANES), 1)
    sn = jnp.sin(ang)
    c_ref[...] = jnp.cos(ang)
    s_ref[...] = jnp.where(lane < ROPE_PAIR, -sn, sn)


def _rope_tables(pos_flat, half):
    inv = ROPE_THETA ** (-jnp.arange(half, dtype=F32) / half)
    gap = jnp.zeros((ROPE_PAIR - half,), F32)
    inv = jnp.concatenate([inv, gap, inv, gap]).reshape(1, LANES)
    n = pos_flat.shape[0]
    rows = 2048
    tab = jax.ShapeDtypeStruct((n, LANES), F32)
    row_spec = pl.BlockSpec((rows, LANES), lambda i: (i, 0))
    return pl.pallas_call(
        _rope_tab_kernel,
        out_shape=(tab,) * 2,
        grid=(n // rows,),
        in_specs=[pl.BlockSpec((rows, 1), lambda i: (i, 0)), pl.BlockSpec((1, LANES), lambda i: (0, 0))],
        out_specs=(row_spec,) * 2,
        compiler_params=_cparams("parallel"),
        name="rope_tables",
    )(pos_flat.reshape(n, 1), inv)


def _ffn_kernel(x_ref, xnext_ref, g_ref, w1_ref, w3_ref, w2_ref, fg_ref, o_ref, xn_ref, *, final):
    i = pl.program_id(0)
    j = pl.program_id(1)
    last = pl.num_programs(1) - 1
    slot = i % 2

    @pl.when(jnp.logical_and(i == 0, j == 0))
    def _():
        xn_ref[0] = _rms(x_ref[...], g_ref[...]).astype(BF16)

    def half_step():
        xn = xn_ref[slot]
        gate = _dot(xn, w1_ref[...])
        up = _dot(xn, w3_ref[...])
        act = (gate * (1.0 / (1.0 + jnp.exp(-gate))) * up).astype(BF16)
        return HALF_STEP * _dot(act, w2_ref[...])

    @pl.when(j == 0)
    def _():
        o_ref[...] = x_ref[...] + half_step()

    @pl.when(jnp.logical_and(j > 0, j < last))
    def _():
        o_ref[...] += half_step()

    @pl.when(j == last)
    def _():
        y = o_ref[...] + half_step()
        xn_ref[1 - slot] = _rms(xnext_ref[...], g_ref[...]).astype(BF16)
        if final:
            y = _rms(y, fg_ref[...])
        o_ref[...] = y


def _ffn(h, gain, w13, w2, final_gain=None):
    tm, tf = 512, 512
    nj = D_FF // tf
    ni = TOKENS // tm
    final = final_gain is not None
    fg = final_gain if final else gain
    return pl.pallas_call(
        functools.partial(_ffn_kernel, final=final),
        out_shape=jax.ShapeDtypeStruct((TOKENS, D_MODEL), F32),
        grid=(ni, nj),
        in_specs=[
            pl.BlockSpec((tm, D_MODEL), lambda i, j: (i, 0)),
            pl.BlockSpec((tm, D_MODEL), lambda i, j: (jnp.minimum(i + 1, ni - 1), 0)),
            pl.BlockSpec((1, D_MODEL), lambda i, j: (0, 0)),
            pl.BlockSpec((D_MODEL, tf), lambda i, j: (0, j)),
            pl.BlockSpec((D_MODEL, tf), lambda i, j: (0, j + nj)),
            pl.BlockSpec((tf, D_MODEL), lambda i, j: (j, 0)),
            pl.BlockSpec((1, D_MODEL), lambda i, j: (0, 0)),
        ],
        out_specs=pl.BlockSpec((tm, D_MODEL), lambda i, j: (i, 0)),
        scratch_shapes=[pltpu.VMEM((2, tm, D_MODEL), BF16)],
        compiler_params=_cparams("arbitrary", "arbitrary"),
        name="ffn",
    )(h, h, gain.reshape(1, D_MODEL), w13, w13, w2, fg.reshape(1, D_MODEL))


def _out_proj_kernel(x_ref, w_ref, r_ref, o_ref):
    o_ref[...] = r_ref[...] + _dot(x_ref[...], w_ref[...])


def _out_proj(x, w, res):
    k = x.shape[1]
    tm, tn = 1024, 1024
    return pl.pallas_call(
        _out_proj_kernel,
        out_shape=jax.ShapeDtypeStruct((TOKENS, D_MODEL), F32),
        grid=(TOKENS // tm, D_MODEL // tn),
        in_specs=[
            pl.BlockSpec((tm, k), lambda i, j: (i, 0)),
            pl.BlockSpec((k, tn), lambda i, j: (0, j)),
            pl.BlockSpec((tm, tn), lambda i, j: (i, j)),
        ],
        out_specs=pl.BlockSpec((tm, tn), lambda i, j: (i, j)),
        compiler_params=_cparams("parallel", "parallel"),
        name="out_proj",
    )(x, w, res)


def _norm_perm_a_kernel(x_ref, g_ref, *refs, tm):
    o_refs, xs_ref = refs[:-1], refs[-1]
    xn = _rms(x_ref[...], g_ref[...])
    nchunk = D_MODEL // LANES
    for k in range(nchunk):
        xs_ref[k] = xn[:, k * LANES:(k + 1) * LANES]
    for (_, d), o_ref in zip(A_PATTERNS, o_refs):
        if d == 1:
            o_ref[...] = xn.astype(BF16)
        else:
            for c in range(d):
                for k in range(nchunk):
                    o_ref[0, c, 0, :, k * LANES:(k + 1) * LANES] = (
                        xs_ref[k, pl.ds(c, tm // d, stride=d), :].astype(BF16))


def _norm_perm_a(h, gain):
    tm = 512
    per_tile = A_TILE // tm
    shapes, specs = [], []
    for _, d in A_PATTERNS:
        if d == 1:
            shapes.append(jax.ShapeDtypeStruct((TOKENS, D_MODEL), BF16))
            specs.append(pl.BlockSpec((tm, D_MODEL), lambda i: (i, 0)))
        else:
            shapes.append(jax.ShapeDtypeStruct((TOKENS // A_TILE, d, per_tile, tm // d, D_MODEL), BF16))
            specs.append(pl.BlockSpec((1, d, 1, tm // d, D_MODEL),
                                      lambda i: (i // per_tile, 0, i % per_tile, 0, 0)))
    outs = pl.pallas_call(
        functools.partial(_norm_perm_a_kernel, tm=tm),
        out_shape=tuple(shapes),
        grid=(TOKENS // tm,),
        in_specs=[pl.BlockSpec((tm, D_MODEL), lambda i: (i, 0)), pl.BlockSpec((1, D_MODEL), lambda i: (0, 0))],
        out_specs=tuple(specs),
        scratch_shapes=[pltpu.VMEM((D_MODEL // LANES, tm, LANES), F32)],
        compiler_params=_cparams("parallel"),
        name="norm_perm_a",
    )(h, gain.reshape(1, D_MODEL))
    return [o.reshape(TOKENS, D_MODEL) for o in outs]


def _qkv_w_kernel(w_ref, o_ref):
    part = pl.program_id(1) % 3

    @pl.when(part == 2)
    def _():
        o_ref[...] = w_ref[...].astype(BF16)

    @pl.when(part != 2)
    def _():
        lane = lax.broadcasted_iota(jnp.int32, (1, LANES), 1)
        stay = (lane < A_ROT_HALF) | (lane >= ROPE_PAIR + A_ROT_HALF)
        for hh in range(A_HEADS):
            sl = slice(hh * A_HEAD_DIM, (hh + 1) * A_HEAD_DIM)
            y = w_ref[:, sl]
            moved = jnp.where(lane < ROPE_PAIR, pltpu.roll(y, LANES - A_ROT_HALF, 1),
                              pltpu.roll(y, ROPE_PAIR - A_ROT_HALF, 1))
            o_ref[:, sl] = jnp.where(stay, y, moved).astype(BF16)


def _qkv_w(w_qkv):
    rows = 256
    hd = A_HEADS * A_HEAD_DIM
    spec = pl.BlockSpec((rows, hd), lambda i, j: (i, j))
    return pl.pallas_call(
        _qkv_w_kernel,
        out_shape=jax.ShapeDtypeStruct(w_qkv.shape, BF16),
        grid=(D_MODEL // rows, w_qkv.shape[1] // hd),
        in_specs=[spec],
        out_specs=spec,
        compiler_params=_cparams("parallel", "parallel"),
        name="qkv_w",
    )(w_qkv)


def _qkv_a_kernel(x_ref, w_ref, cos_ref, sin_ref, o_ref, *, tn):
    part = pl.program_id(1) // ((A_HEADS * A_HEAD_DIM) // tn)
    scale = jnp.where(part == 0, A_HEAD_DIM ** -0.5, 1.0).astype(F32)
    cos, sin = cos_ref[...], sin_ref[...]
    x = x_ref[...]
    for cc in range(tn // MXU_COLS):
        y = _dot(x, w_ref[:, cc * MXU_COLS:(cc + 1) * MXU_COLS])
        for hh in range(MXU_COLS // LANES):
            sl = slice(cc * MXU_COLS + hh * LANES, cc * MXU_COLS + (hh + 1) * LANES)
            y_h = y[:, hh * LANES:(hh + 1) * LANES]
            o_ref[:, sl] = (_rope128(y_h, cos, sin) * scale).astype(BF16)


def _qkv_a(xn, w_qkv, tabs, group):
    tm, tn = 1024, 1024
    cols = 3 * A_HEADS * A_HEAD_DIM
    nj = cols // tn
    ni = TOKENS // tm
    v_first = 2 * (A_HEADS * A_HEAD_DIM) // tn
    tab_spec = pl.BlockSpec((tm, LANES),
                            lambda i, j: (jnp.where(j >= v_first, len(A_PATTERNS) * ni, group * ni + i), 0))
    return pl.pallas_call(
        functools.partial(_qkv_a_kernel, tn=tn),
        out_shape=jax.ShapeDtypeStruct((TOKENS, cols), BF16),
        grid=(ni, nj),
        in_specs=[
            pl.BlockSpec((tm, D_MODEL), lambda i, j: (i, 0)),
            pl.BlockSpec((D_MODEL, tn), lambda i, j: (0, group * nj + j)),
            tab_spec, tab_spec,
        ],
        out_specs=pl.BlockSpec((tm, tn), lambda i, j: (i, j)),
        compiler_params=_cparams("parallel", "parallel"),
        name="qkv_a",
    )(xn, w_qkv, *tabs)


def _attn_a_kernel(q_ref, kc_ref, kp_ref, vc_ref, vp_ref, o_ref, lse_ref, *, d, heads, per_trip):
    rows = A_TILE // d
    nblk = rows // A_BLOCK
    first_tile = pl.program_id(1) == 0
    r = lax.broadcasted_iota(jnp.int32, (A_BLOCK, 2 * A_BLOCK), 0)
    kk = lax.broadcasted_iota(jnp.int32, (A_BLOCK, 2 * A_BLOCK), 1)
    band = (kk >= r) & (kk <= r + A_BLOCK)
    lane = lax.broadcasted_iota(jnp.int32, (A_BLOCK, LANES), 1)
    hs = [slice(h * A_HEAD_DIM, (h + 1) * A_HEAD_DIM) for h in range(heads)]

    def trip(t, carry):
        loaded = []
        for i in range(per_trip):
            u = t * per_trip + i
            c = u // nblk
            blk = u % nblk
            start = pl.multiple_of(blk * A_BLOCK, A_BLOCK)
            before = pl.multiple_of(jnp.maximum(start - A_BLOCK, 0), A_BLOCK)
            q = q_ref[0, c, pl.ds(start, A_BLOCK), :]
            k = jnp.concatenate([jnp.where(blk == 0, kp_ref[0, c], kc_ref[0, c, pl.ds(before, A_BLOCK), :]),
                                 kc_ref[0, c, pl.ds(start, A_BLOCK), :]], axis=0)
            v = jnp.concatenate([jnp.where(blk == 0, vp_ref[0, c], vc_ref[0, c, pl.ds(before, A_BLOCK), :]),
                                 vc_ref[0, c, pl.ds(start, A_BLOCK), :]], axis=0)
            no_prev = jnp.logical_and(first_tile, blk == 0)
            mask = jnp.logical_and(band, jnp.logical_or(kk >= A_BLOCK, jnp.logical_not(no_prev)))
            out_rows = (pl.ds(blk * (A_BLOCK * d) + c, A_BLOCK, stride=d) if d > 1
                        else pl.ds(start, A_BLOCK))
            loaded.append(([_dot_t(q[:, sl], k[:, sl]) for sl in hs], v, mask, out_rows))
        for scores, v, mask, out_rows in loaded:
            lse_tile = jnp.zeros((A_BLOCK, LANES), F32)
            for h, sl in enumerate(hs):
                s = jnp.where(mask, scores[h], NEG_INF)
                m = jnp.max(s, axis=-1, keepdims=True)
                p = jnp.exp(s - m)
                l = jnp.sum(p, axis=-1, keepdims=True)
                o_ref[h, out_rows, :] = _dot(p.astype(BF16), v[:, sl]) / l
                lse_tile = jnp.where(lane == h, m + jnp.log(l), lse_tile)
            lse_ref[out_rows, :] = lse_tile
        return carry

    lax.fori_loop(0, d * nblk // per_trip, trip, 0)


def _attn_a(qkv, window, dilation):
    assert window // dilation == A_BLOCK, "band width must equal the query block"
    d = dilation
    heads = 4
    hw = heads * A_HEAD_DIM
    ngrp = A_HEADS // heads
    rows = A_TILE // d
    tiles = SEQ // A_TILE
    qkv_v = qkv.reshape(BATCH, tiles, d, rows, 3 * A_HEADS * A_HEAD_DIM)
    last_blk = rows // A_BLOCK - 1

    def cur(part):
        return pl.BlockSpec((None, 1, d, rows, hw), lambda b, n, g: (b, n, 0, 0, part * ngrp + g))

    def prev(part):
        return pl.BlockSpec((None, 1, d, A_BLOCK, hw),
                            lambda b, n, g: (b, jnp.maximum(n - 1, 0), 0, last_blk, part * ngrp + g))

    return pl.pallas_call(
        functools.partial(_attn_a_kernel, d=d, heads=heads, per_trip=4),
        out_shape=(jax.ShapeDtypeStruct((A_HEADS, TOKENS, A_HEAD_DIM), F32),
                   jax.ShapeDtypeStruct((TOKENS, ngrp * LANES), F32)),
        grid=(BATCH, tiles, ngrp),
        in_specs=[cur(0), cur(1), prev(1), cur(2), prev(2)],
        out_specs=(pl.BlockSpec((heads, A_TILE, A_HEAD_DIM), lambda b, n, g: (g, b * tiles + n, 0)),
                   pl.BlockSpec((A_TILE, LANES), lambda b, n, g: (b * tiles + n, g))),
        compiler_params=_cparams("parallel", "parallel", "parallel"),
        name="attn_a_d%d" % d,
    )(qkv_v, qkv_v, qkv_v, qkv_v, qkv_v)


def _comb_a_kernel(o0, o1, o2, l0, l1, l2, out_ref, *, heads):
    a = [l0[...], l1[...], l2[...]]
    m = jnp.maximum(jnp.maximum(a[0], a[1]), a[2])
    e = [jnp.exp(t - m) for t in a]
    den = e[0] + e[1] + e[2]
    w = [t / den for t in e]
    rows = out_ref.shape[0]
    for h in range(A_HEADS):
        col = (h // heads) * LANES + h % heads
        acc = jnp.zeros((rows, A_HEAD_DIM), F32)
        for wg, og in zip(w, (o0, o1, o2)):
            acc = acc + jnp.broadcast_to(wg[:, col:col + 1], (rows, A_HEAD_DIM)) * og[h]
        out_ref[:, h * A_HEAD_DIM:(h + 1) * A_HEAD_DIM] = acc.astype(BF16)


def _comb_a(outs, lses):
    tm = 512
    lw = lses[0].shape[1]
    o_spec = pl.BlockSpec((A_HEADS, tm, A_HEAD_DIM), lambda i: (0, i, 0))
    l_spec = pl.BlockSpec((tm, lw), lambda i: (i, 0))
    return pl.pallas_call(
        functools.partial(_comb_a_kernel, heads=A_HEADS * LANES // lw),
        out_shape=jax.ShapeDtypeStruct((TOKENS, A_HEADS * A_HEAD_DIM), BF16),
        grid=(TOKENS // tm,),
        in_specs=[o_spec] * 3 + [l_spec] * 3,
        out_specs=pl.BlockSpec((tm, A_HEADS * A_HEAD_DIM), lambda i: (i, 0)),
        compiler_params=_cparams("parallel"),
        name="comb_a",
    )(*outs, *lses)


def _mixer_a(h, gain, w_qkv, w_o, tabs_a):
    outs, lses = [], []
    for g, ((window, dilation), xn) in enumerate(zip(A_PATTERNS, _norm_perm_a(h, gain))):
        o, lse = _attn_a(_qkv_a(xn, w_qkv, tabs_a, g), window, dilation)
        outs.append(o)
        lses.append(lse)
    return _out_proj(_comb_a(outs, lses), w_o, h)


def _b_in_kernel(x_ref, g_ref, w_ref, o_ref):
    o_ref[...] = _dot(_rms(x_ref[...], g_ref[...]).astype(BF16), w_ref[...])


def _b_in(h, gain, w_in):
    tm = 512
    return pl.pallas_call(
        _b_in_kernel,
        out_shape=jax.ShapeDtypeStruct((TOKENS, B_IN_COLS), F32),
        grid=(TOKENS // tm,),
        in_specs=[
            pl.BlockSpec((tm, D_MODEL), lambda i: (i, 0)),
            pl.BlockSpec((1, D_MODEL), lambda i: (0, 0)),
            pl.BlockSpec((D_MODEL, B_IN_COLS), lambda i: (0, 0)),
        ],
        out_specs=pl.BlockSpec((tm, B_IN_COLS), lambda i: (i, 0)),
        compiler_params=_cparams("parallel"),
        name="b_in",
    )(h, gain.reshape(1, D_MODEL), w_in)


def _b_q_kernel(c_ref, g_ref, w_ref, cos_ref, sin_ref, o_ref, xn_ref, *, slots):
    @pl.when(pl.program_id(1) == 0)
    def _():
        xn_ref[...] = _rms(c_ref[...], g_ref[...]).astype(BF16)

    y = _dot(xn_ref[...], w_ref[...])
    scale = (B_NOPE_DIM + B_ROPE_DIM) ** -0.5
    cos, sin = cos_ref[...], sin_ref[...]
    for s in range(slots):
        nope = y[:, s * B_SLOT:s * B_SLOT + LANES]
        rot = _rope128(y[:, s * B_SLOT + LANES:(s + 1) * B_SLOT], cos, sin)
        o_ref[s, :, :LANES] = (nope * scale).astype(BF16)
        o_ref[s, :, LANES:] = (rot * scale).astype(BF16)


def _b_q(c, gain, w_q, tabs_b):
    tm, slots = 1024, 4
    tn = slots * B_SLOT
    tab_spec = pl.BlockSpec((tm, LANES), lambda i, j: (i, 0))
    return pl.pallas_call(
        functools.partial(_b_q_kernel, slots=slots),
        out_shape=jax.ShapeDtypeStruct((B_HEADS, TOKENS, B_SLOT), BF16),
        grid=(TOKENS // tm, B_HEADS // slots),
        in_specs=[
            pl.BlockSpec((tm, B_Q_RANK), lambda i, j: (i, 0)),
            pl.BlockSpec((1, B_Q_RANK), lambda i, j: (0, 0)),
            pl.BlockSpec((B_Q_RANK, tn), lambda i, j: (0, j)),
            tab_spec, tab_spec,
        ],
        out_specs=pl.BlockSpec((slots, tm, B_SLOT), lambda i, j: (j, i, 0)),
        scratch_shapes=[pltpu.VMEM((tm, B_Q_RANK), BF16)],
        compiler_params=_cparams("parallel", "arbitrary"),
        name="b_q",
    )(c, gain.reshape(1, B_Q_RANK), w_q, *tabs_b)


def _b_kv_kernel(c_ref, g_ref, wk_ref, wv_ref, cr_ref, cos_ref, sin_ref, k_ref, v_ref, *, slots):
    xn = _rms(c_ref[...], g_ref[...]).astype(BF16)
    yk = _dot(xn, wk_ref[...])
    yv = _dot(xn, wv_ref[...])
    kr = _rope128(cr_ref[...], cos_ref[...], sin_ref[...]).astype(BF16)
    for s in range(slots):
        sl = slice(s * LANES, (s + 1) * LANES)
        k_ref[s, :, :LANES] = yk[:, sl].astype(BF16)
        k_ref[s, :, LANES:] = kr
        v_ref[s] = yv[:, sl].astype(BF16)


def _b_kv(c, gain, w_k, w_v, tabs_b):
    tm, slots = 1024, 4
    tn = slots * LANES
    kv_block = B_Q_RANK // B_KV_RANK
    rope_block = (B_Q_RANK + B_KV_RANK) // LANES
    tab_spec = pl.BlockSpec((tm, LANES), lambda i, j: (i, 0))
    return pl.pallas_call(
        functools.partial(_b_kv_kernel, slots=slots),
        out_shape=(jax.ShapeDtypeStruct((B_HEADS, TOKENS, B_SLOT), BF16),
                   jax.ShapeDtypeStruct((B_HEADS, TOKENS, B_V_DIM), BF16)),
        grid=(TOKENS // tm, B_HEADS // slots),
        in_specs=[
            pl.BlockSpec((tm, B_KV_RANK), lambda i, j: (i, kv_block)),
            pl.BlockSpec((1, B_KV_RANK), lambda i, j: (0, 0)),
            pl.BlockSpec((B_KV_RANK, tn), lambda i, j: (0, j)),
            pl.BlockSpec((B_KV_RANK, tn), lambda i, j: (0, j)),
            pl.BlockSpec((tm, LANES), lambda i, j: (i, rope_block)),
            tab_spec, tab_spec,
        ],
        out_specs=(pl.BlockSpec((slots, tm, B_SLOT), lambda i, j: (j, i, 0)),
                   pl.BlockSpec((slots, tm, B_V_DIM), lambda i, j: (j, i, 0))),
        compiler_params=_cparams("parallel", "parallel"),
        name="b_kv",
    )(c, gain.reshape(1, B_KV_RANK), w_k, w_v, c, *tabs_b)


def _attn_b_kernel(q_ref, k_ref, v_ref, o_ref, s0_ref, s1_ref, m_ref, l_ref, acc_ref, *, tq):
    qi = pl.program_id(2)
    q = q_ref[0]
    m_ref[...] = jnp.full_like(m_ref, NEG_INF)
    l_ref[...] = jnp.zeros_like(l_ref)
    acc_ref[...] = jnp.zeros_like(acc_ref)

    def scores(ki):
        return _dot_t(q, k_ref[0, pl.ds(pl.multiple_of(ki * tq, tq), tq), :])

    def fold(s_ref, ki, diagonal):
        s = s_ref[...]
        if diagonal:
            row = lax.broadcasted_iota(jnp.int32, (tq, tq), 0)
            col = lax.broadcasted_iota(jnp.int32, (tq, tq), 1)
            s = jnp.where(row >= col, s, NEG_INF)
        m_prev = m_ref[...]
        m_new = jnp.maximum(m_prev, jnp.max(s, axis=-1, keepdims=True))
        alpha = jnp.exp(m_prev - m_new)
        p = jnp.exp(s - jnp.concatenate([m_new] * (tq // LANES), axis=1))
        l_ref[...] = alpha * l_ref[...] + jnp.sum(p, axis=-1, keepdims=True)
        v = v_ref[0, pl.ds(pl.multiple_of(ki * tq, tq), tq), :]
        acc_ref[...] = alpha * acc_ref[...] + _dot(p.astype(BF16), v)
        m_ref[...] = m_new

    pairs = qi // 2
    s0_ref[...] = scores(0)

    def body(t, carry):
        s1_ref[...] = scores(2 * t + 1)
        fold(s0_ref, 2 * t, False)
        s0_ref[...] = scores(2 * t + 2)
        fold(s1_ref, 2 * t + 1, False)
        return carry

    lax.fori_loop(0, pairs, body, 0)

    @pl.when(qi % 2 == 1)
    def _():
        s1_ref[...] = scores(qi)
        fold(s0_ref, qi - 1, False)
        fold(s1_ref, qi, True)

    @pl.when(qi % 2 == 0)
    def _():
        fold(s0_ref, qi, True)

    o_ref[...] = (acc_ref[...] / l_ref[...]).astype(BF16)


def _attn_b(q3, k3, v3):
    tq = 512
    nq = SEQ // tq
    return pl.pallas_call(
        functools.partial(_attn_b_kernel, tq=tq),
        out_shape=jax.ShapeDtypeStruct((TOKENS, B_HEADS * B_V_DIM), BF16),
        grid=(BATCH, B_HEADS, nq),
        in_specs=[
            pl.BlockSpec((1, tq, B_SLOT), lambda b, h, qi: (h, b * nq + qi, 0)),
            pl.BlockSpec((1, SEQ, B_SLOT), lambda b, h, qi: (h, b, 0)),
            pl.BlockSpec((1, SEQ, B_V_DIM), lambda b, h, qi: (h, b, 0)),
        ],
        out_specs=pl.BlockSpec((tq, B_V_DIM), lambda b, h, qi: (b * nq + qi, h)),
        scratch_shapes=[pltpu.VMEM((tq, tq), F32), pltpu.VMEM((tq, tq), F32),
                        pltpu.VMEM((tq, LANES), F32), pltpu.VMEM((tq, LANES), F32),
                        pltpu.VMEM((tq, B_V_DIM), F32)],
        compiler_params=_cparams("parallel", "parallel", "arbitrary"),
        name="attn_b",
    )(q3, k3, v3)


def _mixer_b(h, gain, w_in, q_norm, w_q, kv_norm, w_k, w_v, w_o, tabs_b):
    c = _b_in(h, gain, w_in)
    q3 = _b_q(c, q_norm, w_q, tabs_b)
    k3, v3 = _b_kv(c, kv_norm, w_k, w_v, tabs_b)
    return _out_proj(_attn_b(q3, k3, v3), w_o, h)


def _c_in_kernel(x_ref, g_ref, wb_ref, wc_ref, wu_ref, cw_ref, o_ref, xn_ref, carry_ref, *, tm):
    i = pl.program_id(0)
    j = pl.program_id(1)

    @pl.when(j == 0)
    def _():
        xn_ref[...] = _rms(x_ref[...], g_ref[...]).astype(BF16)

    xn = xn_ref[...]
    b_gate = _dot(xn, wb_ref[...])
    cu = _dot(xn, wc_ref[...]) * _dot(xn, wu_ref[...])
    w0, w1, w2 = cw_ref[0:1, :], cw_ref[1:2, :], cw_ref[2:3, :]
    z = w2 * cu + w1 * pltpu.roll(cu, 1, 0) + w0 * pltpu.roll(cu, 2, 0)
    o_ref[...] = (b_gate * z).astype(BF16)

    first = (i * tm) % SEQ == 0
    tail = jnp.where(first, 0.0, carry_ref[j])
    both = jnp.concatenate([tail, cu[0:8, :]], axis=0)
    z8 = w2 * both[8:16] + w1 * both[7:15] + w0 * both[6:14]
    o_ref[0:8, :] = (b_gate[0:8, :] * z8).astype(BF16)
    carry_ref[j] = cu[tm - 8:tm, :]


def _c_in(h, gain, w_in, conv_w):
    tm, tn = 1024, 512
    nj = D_MODEL // tn
    return pl.pallas_call(
        functools.partial(_c_in_kernel, tm=tm),
        out_shape=jax.ShapeDtypeStruct((TOKENS, D_MODEL), BF16),
        grid=(TOKENS // tm, nj),
        in_specs=[
            pl.BlockSpec((tm, D_MODEL), lambda i, j: (i, 0)),
            pl.BlockSpec((1, D_MODEL), lambda i, j: (0, 0)),
            pl.BlockSpec((D_MODEL, tn), lambda i, j: (0, j)),
            pl.BlockSpec((D_MODEL, tn), lambda i, j: (0, j + nj)),
            pl.BlockSpec((D_MODEL, tn), lambda i, j: (0, j + 2 * nj)),
            pl.BlockSpec((C_CONV, tn), lambda i, j: (0, j)),
        ],
        out_specs=pl.BlockSpec((tm, tn), lambda i, j: (i, j)),
        scratch_shapes=[pltpu.VMEM((tm, D_MODEL), BF16), pltpu.VMEM((nj, 8, tn), F32)],
        compiler_params=_cparams("arbitrary", "arbitrary"),
        name="c_in",
    )(h, gain.reshape(1, D_MODEL), w_in, w_in, w_in, conv_w)


def _mixer_c(h, gain, w_in, conv_w, w_out):
    return _out_proj(_c_in(h, gain, w_in, conv_w), w_out, h)


def kernel(x, positions, l0_ffn1_norm, l0_ffn1_w13, l0_ffn1_w2, l0_mix_norm, l0_a_w_qkv, l0_a_w_o, l0_ffn2_norm, l0_ffn2_w13, l0_ffn2_w2, l1_ffn1_norm, l1_ffn1_w13, l1_ffn1_w2, l1_mix_norm, l1_b_w_in, l1_b_q_norm, l1_b_w_uq, l1_b_kv_norm, l1_b_w_ukv, l1_b_w_o, l1_ffn2_norm, l1_ffn2_w13, l1_ffn2_w2, l2_ffn1_norm, l2_ffn1_w13, l2_ffn1_w2, l2_mix_norm, l2_c_w_in, l2_c_conv_w, l2_c_w_out, l2_ffn2_norm, l2_ffn2_w13, l2_ffn2_w2, l3_ffn1_norm, l3_ffn1_w13, l3_ffn1_w2, l3_mix_norm, l3_a_w_qkv, l3_a_w_o, l3_ffn2_norm, l3_ffn2_w13, l3_ffn2_w2, final_norm):
    bf = lambda w: w.astype(BF16)
    pos_tiles = positions.reshape(TOKENS // A_TILE, A_TILE)
    pos_a = [pos_tiles.reshape(-1, A_TILE // d, d).swapaxes(1, 2).reshape(TOKENS) for _, d in A_PATTERNS]
    tabs_a = _rope_tables(jnp.concatenate(pos_a + [jnp.zeros((A_TILE,), positions.dtype)]), A_ROT_HALF)
    tabs_b = _rope_tables(positions.reshape(TOKENS), B_ROT_HALF)

    def rope_group(w):
        return _rope_lanes(jnp.pad(w, [(0, 0)] * (w.ndim - 1) + [(0, LANES - B_ROPE_DIM)]), B_ROT_HALF)

    w_uq = l1_b_w_uq.reshape(B_Q_RANK, B_HEADS, B_NOPE_DIM + B_ROPE_DIM)
    w_q = jnp.concatenate([w_uq[..., :B_NOPE_DIM], rope_group(w_uq[..., B_NOPE_DIM:])], axis=-1)
    w_q = bf(w_q.reshape(B_Q_RANK, B_HEADS * B_SLOT))
    w_ukv = l1_b_w_ukv.reshape(B_KV_RANK, B_HEADS, B_NOPE_DIM + B_V_DIM)
    w_k = bf(w_ukv[:, :, :B_NOPE_DIM].reshape(B_KV_RANK, B_HEADS * B_NOPE_DIM))
    w_v = bf(w_ukv[:, :, B_NOPE_DIM:].reshape(B_KV_RANK, B_HEADS * B_V_DIM))
    n_c = B_Q_RANK + B_KV_RANK
    w_b_in = bf(jnp.concatenate([l1_b_w_in[:, :n_c], rope_group(l1_b_w_in[:, n_c:])], axis=-1))

    h = x.reshape(TOKENS, D_MODEL)

    h = _ffn(h, l0_ffn1_norm, bf(l0_ffn1_w13), bf(l0_ffn1_w2))
    h = _mixer_a(h, l0_mix_norm, _qkv_w(l0_a_w_qkv), bf(l0_a_w_o), tabs_a)
    h = _ffn(h, l0_ffn2_norm, bf(l0_ffn2_w13), bf(l0_ffn2_w2))

    h = _ffn(h, l1_ffn1_norm, bf(l1_ffn1_w13), bf(l1_ffn1_w2))
    h = _mixer_b(h, l1_mix_norm, w_b_in, l1_b_q_norm, w_q, l1_b_kv_norm, w_k, w_v, bf(l1_b_w_o), tabs_b)
    h = _ffn(h, l1_ffn2_norm, bf(l1_ffn2_w13), bf(l1_ffn2_w2))

    h = _ffn(h, l2_ffn1_norm, bf(l2_ffn1_w13), bf(l2_ffn1_w2))
    h = _mixer_c(h, l2_mix_norm, bf(l2_c_w_in), l2_c_conv_w, bf(l2_c_w_out))
    h = _ffn(h, l2_ffn2_norm, bf(l2_ffn2_w13), bf(l2_ffn2_w2))

    h = _ffn(h, l3_ffn1_norm, bf(l3_ffn1_w13), bf(l3_ffn1_w2))
    h = _mixer_a(h, l3_mix_norm, _qkv_w(l3_a_w_qkv), bf(l3_a_w_o), tabs_a)
    h = _ffn(h, l3_ffn2_norm, bf(l3_ffn2_w13), bf(l3_ffn2_w2), final_gain=final_norm)

    return h.reshape(BATCH, SEQ, D_MODEL)
```

```python
import functools

import jax
import jax.numpy as jnp
from jax import lax
from jax.experimental import pallas as pl
from jax.experimental.pallas import tpu as pltpu

F32 = jnp.float32
BF16 = jnp.bfloat16

D_MODEL = 2048
BATCH = 2
SEQ = 8192
TOKENS = BATCH * SEQ
D_FF = 5632
HALF_STEP = 0.5
NORM_EPS = 1e-6
NEG_INF = -1e30
ROPE_THETA = 500000.0

A_HEADS = 16
A_HEAD_DIM = 128
A_ROT_HALF = A_HEAD_DIM // 8
A_PATTERNS = ((128, 1), (512, 4), (2048, 16))
A_BLOCK = 128
A_TILE = 2048

B_HEADS = 16
B_Q_RANK = 1536
B_KV_RANK = 512
B_NOPE_DIM = 128
B_ROPE_DIM = 64
B_ROT_HALF = B_ROPE_DIM // 2
B_V_DIM = 128
B_SLOT = 256
B_VX = 256
B_IN_COLS = 2176

C_CONV = 3

LANES = 128
MXU_COLS = 256
ROPE_PAIR = LANES // 2
VMEM_LIMIT = 56 * 1024 * 1024


def _cparams(*sem):
    return pltpu.CompilerParams(dimension_semantics=sem, vmem_limit_bytes=VMEM_LIMIT)


def _rms(x, gain):
    ms = jnp.mean(x * x, axis=-1, keepdims=True)
    return x * lax.rsqrt(ms + NORM_EPS) * gain


def _dot(a, b):
    return jnp.dot(a, b, preferred_element_type=F32)


def _dot_t(a, b):
    return lax.dot_general(a, b, (((1,), (1,)), ((), ())), preferred_element_type=F32)


def _rope128(y, cos, sin):
    return y * cos + pltpu.roll(y, ROPE_PAIR, 1) * sin


def _rope_lanes(w, half):
    return jnp.concatenate([w[..., :half], w[..., 2 * half:ROPE_PAIR + half], w[..., half:2 * half],
                            w[..., ROPE_PAIR + half:]], axis=-1)


def _rope_tab_kernel(pos_ref, inv_ref, c_ref, s_ref):
    ang = pos_ref[...].astype(F32) * inv_ref[...]
    lane = lax.broadcasted_iota(jnp.int32, (1, LANES), 1)
    sn = jnp.sin(ang)
    c_ref[...] = jnp.cos(ang)
    s_ref[...] = jnp.where(lane < ROPE_PAIR, -sn, sn)


def _rope_tables(pos_flat, half):
    inv = ROPE_THETA ** (-jnp.arange(half, dtype=F32) / half)
    gap = jnp.zeros((ROPE_PAIR - half,), F32)
    inv = jnp.concatenate([inv, gap, inv, gap]).reshape(1, LANES)
    n = pos_flat.shape[0]
    rows = 2048
    tab = jax.ShapeDtypeStruct((n, LANES), F32)
    row_spec = pl.BlockSpec((rows, LANES), lambda i: (i, 0))
    return pl.pallas_call(
        _rope_tab_kernel,
        out_shape=(tab,) * 2,
        grid=(n // rows,),
        in_specs=[pl.BlockSpec((rows, 1), lambda i: (i, 0)), pl.BlockSpec((1, LANES), lambda i: (0, 0))],
        out_specs=(row_spec,) * 2,
        compiler_params=_cparams("parallel"),
        name="rope_tables",
    )(pos_flat.reshape(n, 1), inv)


def _ffn_kernel(x_ref, g_ref, w1_ref, w3_ref, w2_ref, fg_ref, o_ref, xn_ref, acc_ref, *, final):
    j = pl.program_id(1)

    @pl.when(j == 0)
    def _():
        xn_ref[...] = _rms(x_ref[...], g_ref[...]).astype(BF16)
        acc_ref[...] = jnp.zeros_like(acc_ref)

    xn = xn_ref[...]
    gate = _dot(xn, w1_ref[...])
    up = _dot(xn, w3_ref[...])
    act = (gate * (1.0 / (1.0 + jnp.exp(-gate))) * up).astype(BF16)
    acc_ref[...] += _dot(act, w2_ref[...])

    @pl.when(j == pl.num_programs(1) - 1)
    def _():
        y = x_ref[...] + HALF_STEP * acc_ref[...]
        if final:
            y = _rms(y, fg_ref[...])
        o_ref[...] = y


def _ffn(h, gain, w13, w2, final_gain=None):
    tm, tf = 512, 512
    nj = D_FF // tf
    ni = TOKENS // tm
    final = final_gain is not None
    fg = final_gain if final else gain
    return pl.pallas_call(
        functools.partial(_ffn_kernel, final=final),
        out_shape=jax.ShapeDtypeStruct((TOKENS, D_MODEL), F32),
        grid=(ni, nj),
        in_specs=[
            pl.BlockSpec((tm, D_MODEL), lambda i, j: (i, 0)),
            pl.BlockSpec((1, D_MODEL), lambda i, j: (0, 0)),
            pl.BlockSpec((D_MODEL, tf), lambda i, j: (0, j)),
            pl.BlockSpec((D_MODEL, tf), lambda i, j: (0, j + nj)),
            pl.BlockSpec((tf, D_MODEL), lambda i, j: (j, 0)),
            pl.BlockSpec((1, D_MODEL), lambda i, j: (0, 0)),
        ],
        out_specs=pl.BlockSpec((tm, D_MODEL), lambda i, j: (i, 0)),
        scratch_shapes=[pltpu.VMEM((tm, D_MODEL), BF16), pltpu.VMEM((tm, D_MODEL), F32)],
        compiler_params=_cparams("parallel", "arbitrary"),
        name="ffn",
    )(h, gain.reshape(1, D_MODEL), w13, w13, w2, fg.reshape(1, D_MODEL))


def _out_proj_kernel(x_ref, w_ref, r_ref, o_ref):
    o_ref[...] = r_ref[...] + _dot(x_ref[...], w_ref[...])


def _out_proj(x, w, res):
    k = x.shape[1]
    tm, tn = 1024, 1024
    return pl.pallas_call(
        _out_proj_kernel,
        out_shape=jax.ShapeDtypeStruct((TOKENS, D_MODEL), F32),
        grid=(TOKENS // tm, D_MODEL // tn),
        in_specs=[
            pl.BlockSpec((tm, k), lambda i, j: (i, 0)),
            pl.BlockSpec((k, tn), lambda i, j: (0, j)),
            pl.BlockSpec((tm, tn), lambda i, j: (i, j)),
        ],
        out_specs=pl.BlockSpec((tm, tn), lambda i, j: (i, j)),
        compiler_params=_cparams("parallel", "parallel"),
        name="out_proj",
    )(x, w, res)


def _norm_perm_a_kernel(x_ref, g_ref, *refs, tm):
    o_refs, xs_ref = refs[:-1], refs[-1]
    xn = _rms(x_ref[...], g_ref[...])
    nchunk = D_MODEL // LANES
    for k in range(nchunk):
        xs_ref[k] = xn[:, k * LANES:(k + 1) * LANES]
    for (_, d), o_ref in zip(A_PATTERNS, o_refs):
        if d == 1:
            o_ref[...] = xn.astype(BF16)
        else:
            for c in range(d):
                for k in range(nchunk):
                    o_ref[0, c, 0, :, k * LANES:(k + 1) * LANES] = (
                        xs_ref[k, pl.ds(c, tm // d, stride=d), :].astype(BF16))


def _norm_perm_a(h, gain):
    tm = 512
    per_tile = A_TILE // tm
    shapes, specs = [], []
    for _, d in A_PATTERNS:
        if d == 1:
            shapes.append(jax.ShapeDtypeStruct((TOKENS, D_MODEL), BF16))
            specs.append(pl.BlockSpec((tm, D_MODEL), lambda i: (i, 0)))
        else:
            shapes.append(jax.ShapeDtypeStruct((TOKENS // A_TILE, d, per_tile, tm // d, D_MODEL), BF16))
            specs.append(pl.BlockSpec((1, d, 1, tm // d, D_MODEL),
                                      lambda i: (i // per_tile, 0, i % per_tile, 0, 0)))
    outs = pl.pallas_call(
        functools.partial(_norm_perm_a_kernel, tm=tm),
        out_shape=tuple(shapes),
        grid=(TOKENS // tm,),
        in_specs=[pl.BlockSpec((tm, D_MODEL), lambda i: (i, 0)), pl.BlockSpec((1, D_MODEL), lambda i: (0, 0))],
        out_specs=tuple(specs),
        scratch_shapes=[pltpu.VMEM((D_MODEL // LANES, tm, LANES), F32)],
        compiler_params=_cparams("parallel"),
        name="norm_perm_a",
    )(h, gain.reshape(1, D_MODEL))
    return [o.reshape(TOKENS, D_MODEL) for o in outs]


def _qkv_w_kernel(w_ref, o_ref):
    part = pl.program_id(1) % 3

    @pl.when(part == 2)
    def _():
        o_ref[...] = w_ref[...].astype(BF16)

    @pl.when(part != 2)
    def _():
        lane = lax.broadcasted_iota(jnp.int32, (1, LANES), 1)
        stay = (lane < A_ROT_HALF) | (lane >= ROPE_PAIR + A_ROT_HALF)
        for hh in range(A_HEADS):
            sl = slice(hh * A_HEAD_DIM, (hh + 1) * A_HEAD_DIM)
            y = w_ref[:, sl]
            moved = jnp.where(lane < ROPE_PAIR, pltpu.roll(y, LANES - A_ROT_HALF, 1),
                              pltpu.roll(y, ROPE_PAIR - A_ROT_HALF, 1))
            o_ref[:, sl] = jnp.where(stay, y, moved).astype(BF16)


def _qkv_w(w_qkv):
    rows = 256
    hd = A_HEADS * A_HEAD_DIM
    spec = pl.BlockSpec((rows, hd), lambda i, j: (i, j))
    return pl.pallas_call(
        _qkv_w_kernel,
        out_shape=jax.ShapeDtypeStruct(w_qkv.shape, BF16),
        grid=(D_MODEL // rows, w_qkv.shape[1] // hd),
        in_specs=[spec],
        out_specs=spec,
        compiler_params=_cparams("parallel", "parallel"),
        name="qkv_w",
    )(w_qkv)


def _qkv_a_kernel(x_ref, w_ref, cos_ref, sin_ref, o_ref, *, tn):
    part = pl.program_id(1) // ((A_HEADS * A_HEAD_DIM) // tn)
    scale = jnp.where(part == 0, A_HEAD_DIM ** -0.5, 1.0).astype(F32)
    cos, sin = cos_ref[...], sin_ref[...]
    x = x_ref[...]
    for cc in range(tn // MXU_COLS):
        y = _dot(x, w_ref[:, cc * MXU_COLS:(cc + 1) * MXU_COLS])
        for hh in range(MXU_COLS // LANES):
            sl = slice(cc * MXU_COLS + hh * LANES, cc * MXU_COLS + (hh + 1) * LANES)
            y_h = y[:, hh * LANES:(hh + 1) * LANES]
            o_ref[:, sl] = (_rope128(y_h, cos, sin) * scale).astype(BF16)


def _qkv_a(xn, w_qkv, tabs, group):
    tm, tn = 1024, 1024
    cols = 3 * A_HEADS * A_HEAD_DIM
    nj = cols // tn
    ni = TOKENS // tm
    v_first = 2 * (A_HEADS * A_HEAD_DIM) // tn
    tab_spec = pl.BlockSpec((tm, LANES),
                            lambda i, j: (jnp.where(j >= v_first, len(A_PATTERNS) * ni, group * ni + i), 0))
    return pl.pallas_call(
        functools.partial(_qkv_a_kernel, tn=tn),
        out_shape=jax.ShapeDtypeStruct((TOKENS, cols), BF16),
        grid=(ni, nj),
        in_specs=[
            pl.BlockSpec((tm, D_MODEL), lambda i, j: (i, 0)),
            pl.BlockSpec((D_MODEL, tn), lambda i, j: (0, group * nj + j)),
            tab_spec, tab_spec,
        ],
        out_specs=pl.BlockSpec((tm, tn), lambda i, j: (i, j)),
        compiler_params=_cparams("parallel", "parallel"),
        name="qkv_a",
    )(xn, w_qkv, *tabs)


def _attn_a_kernel(q_ref, kc_ref, kp_ref, vc_ref, vp_ref, o_ref, lse_ref, *, d, heads, per_trip):
    rows = A_TILE // d
    nblk = rows // A_BLOCK
    first_tile = pl.program_id(1) == 0
    r = lax.broadcasted_iota(jnp.int32, (A_BLOCK, 2 * A_BLOCK), 0)
    kk = lax.broadcasted_iota(jnp.int32, (A_BLOCK, 2 * A_BLOCK), 1)
    band = (kk >= r) & (kk <= r + A_BLOCK)
    lane = lax.broadcasted_iota(jnp.int32, (A_BLOCK, LANES), 1)
    hs = [slice(h * A_HEAD_DIM, (h + 1) * A_HEAD_DIM) for h in range(heads)]

    def trip(t, carry):
        loaded = []
        for i in range(per_trip):
            u = t * per_trip + i
            c = u // nblk
            blk = u % nblk
            start = pl.multiple_of(blk * A_BLOCK, A_BLOCK)
            before = pl.multiple_of(jnp.maximum(start - A_BLOCK, 0), A_BLOCK)
            q = q_ref[0, c, pl.ds(start, A_BLOCK), :]
            k = jnp.concatenate([jnp.where(blk == 0, kp_ref[0, c], kc_ref[0, c, pl.ds(before, A_BLOCK), :]),
                                 kc_ref[0, c, pl.ds(start, A_BLOCK), :]], axis=0)
            v = jnp.concatenate([jnp.where(blk == 0, vp_ref[0, c], vc_ref[0, c, pl.ds(before, A_BLOCK), :]),
                                 vc_ref[0, c, pl.ds(start, A_BLOCK), :]], axis=0)
            no_prev = jnp.logical_and(first_tile, blk == 0)
            mask = jnp.logical_and(band, jnp.logical_or(kk >= A_BLOCK, jnp.logical_not(no_prev)))
            out_rows = (pl.ds(blk * (A_BLOCK * d) + c, A_BLOCK, stride=d) if d > 1
                        else pl.ds(start, A_BLOCK))
            loaded.append(([_dot_t(q[:, sl], k[:, sl]) for sl in hs], v, mask, out_rows))
        for scores, v, mask, out_rows in loaded:
            lse_tile = jnp.zeros((A_BLOCK, LANES), F32)
            for h, sl in enumerate(hs):
                s = jnp.where(mask, scores[h], NEG_INF)
                m = jnp.max(s, axis=-1, keepdims=True)
                p = jnp.exp(s - m)
                l = jnp.sum(p, axis=-1, keepdims=True)
                o_ref[h, out_rows, :] = _dot(p.astype(BF16), v[:, sl]) / l
                lse_tile = jnp.where(lane == h, m + jnp.log(l), lse_tile)
            lse_ref[out_rows, :] = lse_tile
        return carry

    lax.fori_loop(0, d * nblk // per_trip, trip, 0)


def _attn_a(qkv, window, dilation):
    assert window // dilation == A_BLOCK, "band width must equal the query block"
    d = dilation
    heads = 4
    hw = heads * A_HEAD_DIM
    ngrp = A_HEADS // heads
    rows = A_TILE // d
    tiles = SEQ // A_TILE
    qkv_v = qkv.reshape(BATCH, tiles, d, rows, 3 * A_HEADS * A_HEAD_DIM)
    last_blk = rows // A_BLOCK - 1

    def cur(part):
        return pl.BlockSpec((None, 1, d, rows, hw), lambda b, n, g: (b, n, 0, 0, part * ngrp + g))

    def prev(part):
        return pl.BlockSpec((None, 1, d, A_BLOCK, hw),
                            lambda b, n, g: (b, jnp.maximum(n - 1, 0), 0, last_blk, part * ngrp + g))

    return pl.pallas_call(
        functools.partial(_attn_a_kernel, d=d, heads=heads, per_trip=4),
        out_shape=(jax.ShapeDtypeStruct((A_HEADS, TOKENS, A_HEAD_DIM), F32),
                   jax.ShapeDtypeStruct((TOKENS, ngrp * LANES), F32)),
        grid=(BATCH, tiles, ngrp),
        in_specs=[cur(0), cur(1), prev(1), cur(2), prev(2)],
        out_specs=(pl.BlockSpec((heads, A_TILE, A_HEAD_DIM), lambda b, n, g: (g, b * tiles + n, 0)),
                   pl.BlockSpec((A_TILE, LANES), lambda b, n, g: (b * tiles + n, g))),
        compiler_params=_cparams("parallel", "parallel", "parallel"),
        name="attn_a_d%d" % d,
    )(qkv_v, qkv_v, qkv_v, qkv_v, qkv_v)


def _comb_a_kernel(o0, o1, o2, l0, l1, l2, out_ref, *, heads):
    a = [l0[...], l1[...], l2[...]]
    m = jnp.maximum(jnp.maximum(a[0], a[1]), a[2])
    e = [jnp.exp(t - m) for t in a]
    den = e[0] + e[1] + e[2]
    w = [t / den for t in e]
    rows = out_ref.shape[0]
    for h in range(A_HEADS):
        col = (h // heads) * LANES + h % heads
        acc = jnp.zeros((rows, A_HEAD_DIM), F32)
        for wg, og in zip(w, (o0, o1, o2)):
            acc = acc + jnp.broadcast_to(wg[:, col:col + 1], (rows, A_HEAD_DIM)) * og[h]
        out_ref[:, h * A_HEAD_DIM:(h + 1) * A_HEAD_DIM] = acc.astype(BF16)


def _comb_a(outs, lses):
    tm = 512
    lw = lses[0].shape[1]
    o_spec = pl.BlockSpec((A_HEADS, tm, A_HEAD_DIM), lambda i: (0, i, 0))
    l_spec = pl.BlockSpec((tm, lw), lambda i: (i, 0))
    return pl.pallas_call(
        functools.partial(_comb_a_kernel, heads=A_HEADS * LANES // lw),
        out_shape=jax.ShapeDtypeStruct((TOKENS, A_HEADS * A_HEAD_DIM), BF16),
        grid=(TOKENS // tm,),
        in_specs=[o_spec] * 3 + [l_spec] * 3,
        out_specs=pl.BlockSpec((tm, A_HEADS * A_HEAD_DIM), lambda i: (i, 0)),
        compiler_params=_cparams("parallel"),
        name="comb_a",
    )(*outs, *lses)


def _mixer_a(h, gain, w_qkv, w_o, tabs_a):
    outs, lses = [], []
    for g, ((window, dilation), xn) in enumerate(zip(A_PATTERNS, _norm_perm_a(h, gain))):
        o, lse = _attn_a(_qkv_a(xn, w_qkv, tabs_a, g), window, dilation)
        outs.append(o)
        lses.append(lse)
    return _out_proj(_comb_a(outs, lses), w_o, h)


def _b_in_kernel(x_ref, g_ref, w_ref, o_ref):
    o_ref[...] = _dot(_rms(x_ref[...], g_ref[...]).astype(BF16), w_ref[...])


def _b_in(h, gain, w_in):
    tm = 512
    return pl.pallas_call(
        _b_in_kernel,
        out_shape=jax.ShapeDtypeStruct((TOKENS, B_IN_COLS), F32),
        grid=(TOKENS // tm,),
        in_specs=[
            pl.BlockSpec((tm, D_MODEL), lambda i: (i, 0)),
            pl.BlockSpec((1, D_MODEL), lambda i: (0, 0)),
            pl.BlockSpec((D_MODEL, B_IN_COLS), lambda i: (0, 0)),
        ],
        out_specs=pl.BlockSpec((tm, B_IN_COLS), lambda i: (i, 0)),
        compiler_params=_cparams("parallel"),
        name="b_in",
    )(h, gain.reshape(1, D_MODEL), w_in)


def _b_q_kernel(c_ref, g_ref, w_ref, cos_ref, sin_ref, o_ref, xn_ref, *, slots):
    @pl.when(pl.program_id(1) == 0)
    def _():
        xn_ref[...] = _rms(c_ref[...], g_ref[...]).astype(BF16)

    y = _dot(xn_ref[...], w_ref[...])
    scale = (B_NOPE_DIM + B_ROPE_DIM) ** -0.5
    cos, sin = cos_ref[...], sin_ref[...]
    for s in range(slots):
        nope = y[:, s * B_SLOT:s * B_SLOT + LANES]
        rot = _rope128(y[:, s * B_SLOT + LANES:(s + 1) * B_SLOT], cos, sin)
        o_ref[s, :, :LANES] = (nope * scale).astype(BF16)
        o_ref[s, :, LANES:] = (rot * scale).astype(BF16)


def _b_q(c, gain, w_q, tabs_b):
    tm, slots = 1024, 4
    tn = slots * B_SLOT
    tab_spec = pl.BlockSpec((tm, LANES), lambda i, j: (i, 0))
    return pl.pallas_call(
        functools.partial(_b_q_kernel, slots=slots),
        out_shape=jax.ShapeDtypeStruct((B_HEADS, TOKENS, B_SLOT), BF16),
        grid=(TOKENS // tm, B_HEADS // slots),
        in_specs=[
            pl.BlockSpec((tm, B_Q_RANK), lambda i, j: (i, 0)),
            pl.BlockSpec((1, B_Q_RANK), lambda i, j: (0, 0)),
            pl.BlockSpec((B_Q_RANK, tn), lambda i, j: (0, j)),
            tab_spec, tab_spec,
        ],
        out_specs=pl.BlockSpec((slots, tm, B_SLOT), lambda i, j: (j, i, 0)),
        scratch_shapes=[pltpu.VMEM((tm, B_Q_RANK), BF16)],
        compiler_params=_cparams("parallel", "arbitrary"),
        name="b_q",
    )(c, gain.reshape(1, B_Q_RANK), w_q, *tabs_b)


def _b_kv_kernel(c_ref, g_ref, wk_ref, wv_ref, cr_ref, cos_ref, sin_ref, k_ref, v_ref, *, slots):
    xn = _rms(c_ref[...], g_ref[...]).astype(BF16)
    yk = _dot(xn, wk_ref[...])
    yv = _dot(xn, wv_ref[...])
    kr = _rope128(cr_ref[...], cos_ref[...], sin_ref[...]).astype(BF16)
    lane = lax.broadcasted_iota(jnp.int32, (c_ref.shape[0], B_VX - B_V_DIM), 1)
    ones_col = jnp.where(lane == 0, 1.0, 0.0).astype(BF16)
    for s in range(slots):
        sl = slice(s * LANES, (s + 1) * LANES)
        k_ref[s, :, :LANES] = yk[:, sl].astype(BF16)
        k_ref[s, :, LANES:] = kr
        v_ref[s, :, :B_V_DIM] = yv[:, sl].astype(BF16)
        v_ref[s, :, B_V_DIM:] = ones_col


def _b_kv(c, gain, w_k, w_v, tabs_b):
    tm, slots = 1024, 4
    tn = slots * LANES
    kv_block = B_Q_RANK // B_KV_RANK
    rope_block = (B_Q_RANK + B_KV_RANK) // LANES
    tab_spec = pl.BlockSpec((tm, LANES), lambda i, j: (i, 0))
    return pl.pallas_call(
        functools.partial(_b_kv_kernel, slots=slots),
        out_shape=(jax.ShapeDtypeStruct((B_HEADS, TOKENS, B_SLOT), BF16),
                   jax.ShapeDtypeStruct((B_HEADS, TOKENS, B_VX), BF16)),
        grid=(TOKENS // tm, B_HEADS // slots),
        in_specs=[
            pl.BlockSpec((tm, B_KV_RANK), lambda i, j: (i, kv_block)),
            pl.BlockSpec((1, B_KV_RANK), lambda i, j: (0, 0)),
            pl.BlockSpec((B_KV_RANK, tn), lambda i, j: (0, j)),
            pl.BlockSpec((B_KV_RANK, tn), lambda i, j: (0, j)),
            pl.BlockSpec((tm, LANES), lambda i, j: (i, rope_block)),
            tab_spec, tab_spec,
        ],
        out_specs=(pl.BlockSpec((slots, tm, B_SLOT), lambda i, j: (j, i, 0)),
                   pl.BlockSpec((slots, tm, B_VX), lambda i, j: (j, i, 0))),
        compiler_params=_cparams("parallel", "parallel"),
        name="b_kv",
    )(c, gain.reshape(1, B_KV_RANK), w_k, w_v, c, *tabs_b)


def _attn_b_kernel(q_ref, k_ref, v_ref, o_ref, s0_ref, s1_ref, m_ref, acc_ref, *, tq):
    qi = pl.program_id(2)
    q = q_ref[0]
    m_ref[...] = jnp.full_like(m_ref, NEG_INF)
    acc_ref[...] = jnp.zeros_like(acc_ref)

    def scores(ki):
        return _dot_t(q, k_ref[0, pl.ds(pl.multiple_of(ki * tq, tq), tq), :])

    def fold(s_ref, ki, diagonal):
        s = s_ref[...]
        if diagonal:
            row = lax.broadcasted_iota(jnp.int32, (tq, tq), 0)
            col = lax.broadcasted_iota(jnp.int32, (tq, tq), 1)
            s = jnp.where(row >= col, s, NEG_INF)
        m_prev = m_ref[...]
        m_new = jnp.maximum(m_prev, jnp.max(s, axis=-1, keepdims=True))
        alpha = jnp.exp(m_prev - m_new)
        p = jnp.exp((s - jnp.concatenate([m_new] * (tq // LANES), axis=1)).astype(BF16))
        v = v_ref[0, pl.ds(pl.multiple_of(ki * tq, tq), tq), :]
        acc_ref[...] = jnp.concatenate([alpha] * (B_VX // LANES), axis=1) * acc_ref[...] + _dot(p, v)
        m_ref[...] = m_new

    pairs = qi // 2
    s0_ref[...] = scores(0)

    def body(t, carry):
        s1_ref[...] = scores(2 * t + 1)
        fold(s0_ref, 2 * t, False)
        s0_ref[...] = scores(2 * t + 2)
        fold(s1_ref, 2 * t + 1, False)
        return carry

    lax.fori_loop(0, pairs, body, 0)

    @pl.when(qi % 2 == 1)
    def _():
        s1_ref[...] = scores(qi)
        fold(s0_ref, qi - 1, False)
        fold(s1_ref, qi, True)

    @pl.when(qi % 2 == 0)
    def _():
        fold(s0_ref, qi, True)

    acc = acc_ref[...]
    l = jnp.broadcast_to(acc[:, B_V_DIM:B_V_DIM + 1], (tq, B_V_DIM))
    o_ref[...] = (acc[:, :B_V_DIM] / l).astype(BF16)


def _attn_b(q3, k3, v3):
    tq = 512
    nq = SEQ // tq
    return pl.pallas_call(
        functools.partial(_attn_b_kernel, tq=tq),
        out_shape=jax.ShapeDtypeStruct((TOKENS, B_HEADS * B_V_DIM), BF16),
        grid=(BATCH, B_HEADS, nq),
        in_specs=[
            pl.BlockSpec((1, tq, B_SLOT), lambda b, h, qi: (h, b * nq + qi, 0)),
            pl.BlockSpec((1, SEQ, B_SLOT), lambda b, h, qi: (h, b, 0)),
            pl.BlockSpec((1, SEQ, B_VX), lambda b, h, qi: (h, b, 0)),
        ],
        out_specs=pl.BlockSpec((tq, B_V_DIM), lambda b, h, qi: (b * nq + qi, h)),
        scratch_shapes=[pltpu.VMEM((tq, tq), F32), pltpu.VMEM((tq, tq), F32),
                        pltpu.VMEM((tq, LANES), F32), pltpu.VMEM((tq, B_VX), F32)],
        compiler_params=_cparams("parallel", "parallel", "arbitrary"),
        name="attn_b",
    )(q3, k3, v3)


def _mixer_b(h, gain, w_in, q_norm, w_q, kv_norm, w_k, w_v, w_o, tabs_b):
    c = _b_in(h, gain, w_in)
    q3 = _b_q(c, q_norm, w_q, tabs_b)
    k3, v3 = _b_kv(c, kv_norm, w_k, w_v, tabs_b)
    return _out_proj(_attn_b(q3, k3, v3), w_o, h)


def _c_in_kernel(x_ref, g_ref, wb_ref, wc_ref, wu_ref, cw_ref, o_ref, xn_ref, carry_ref, *, tm):
    i = pl.program_id(0)
    j = pl.program_id(1)

    @pl.when(j == 0)
    def _():
        xn_ref[...] = _rms(x_ref[...], g_ref[...]).astype(BF16)

    xn = xn_ref[...]
    b_gate = _dot(xn, wb_ref[...])
    cu = _dot(xn, wc_ref[...]) * _dot(xn, wu_ref[...])
    w0, w1, w2 = cw_ref[0:1, :], cw_ref[1:2, :], cw_ref[2:3, :]
    z = w2 * cu + w1 * pltpu.roll(cu, 1, 0) + w0 * pltpu.roll(cu, 2, 0)
    o_ref[...] = (b_gate * z).astype(BF16)

    first = (i * tm) % SEQ == 0
    tail = jnp.where(first, 0.0, carry_ref[j])
    both = jnp.concatenate([tail, cu[0:8, :]], axis=0)
    z8 = w2 * both[8:16] + w1 * both[7:15] + w0 * both[6:14]
    o_ref[0:8, :] = (b_gate[0:8, :] * z8).astype(BF16)
    carry_ref[j] = cu[tm - 8:tm, :]


def _c_in(h, gain, w_in, conv_w):
    tm, tn = 1024, 512
    nj = D_MODEL // tn
    return pl.pallas_call(
        functools.partial(_c_in_kernel, tm=tm),
        out_shape=jax.ShapeDtypeStruct((TOKENS, D_MODEL), BF16),
        grid=(TOKENS // tm, nj),
        in_specs=[
            pl.BlockSpec((tm, D_MODEL), lambda i, j: (i, 0)),
            pl.BlockSpec((1, D_MODEL), lambda i, j: (0, 0)),
            pl.BlockSpec((D_MODEL, tn), lambda i, j: (0, j)),
            pl.BlockSpec((D_MODEL, tn), lambda i, j: (0, j + nj)),
            pl.BlockSpec((D_MODEL, tn), lambda i, j: (0, j + 2 * nj)),
            pl.BlockSpec((C_CONV, tn), lambda i, j: (0, j)),
        ],
        out_specs=pl.BlockSpec((tm, tn), lambda i, j: (i, j)),
        scratch_shapes=[pltpu.VMEM((tm, D_MODEL), BF16), pltpu.VMEM((nj, 8, tn), F32)],
        compiler_params=_cparams("arbitrary", "arbitrary"),
        name="c_in",
    )(h, gain.reshape(1, D_MODEL), w_in, w_in, w_in, conv_w)


def _mixer_c(h, gain, w_in, conv_w, w_out):
    return _out_proj(_c_in(h, gain, w_in, conv_w), w_out, h)


def kernel(x, positions, l0_ffn1_norm, l0_ffn1_w13, l0_ffn1_w2, l0_mix_norm, l0_a_w_qkv, l0_a_w_o, l0_ffn2_norm, l0_ffn2_w13, l0_ffn2_w2, l1_ffn1_norm, l1_ffn1_w13, l1_ffn1_w2, l1_mix_norm, l1_b_w_in, l1_b_q_norm, l1_b_w_uq, l1_b_kv_norm, l1_b_w_ukv, l1_b_w_o, l1_ffn2_norm, l1_ffn2_w13, l1_ffn2_w2, l2_ffn1_norm, l2_ffn1_w13, l2_ffn1_w2, l2_mix_norm, l2_c_w_in, l2_c_conv_w, l2_c_w_out, l2_ffn2_norm, l2_ffn2_w13, l2_ffn2_w2, l3_ffn1_norm, l3_ffn1_w13, l3_ffn1_w2, l3_mix_norm, l3_a_w_qkv, l3_a_w_o, l3_ffn2_norm, l3_ffn2_w13, l3_ffn2_w2, final_norm):
    bf = lambda w: w.astype(BF16)
    pos_tiles = positions.reshape(TOKENS // A_TILE, A_TILE)
    pos_a = [pos_tiles.reshape(-1, A_TILE // d, d).swapaxes(1, 2).reshape(TOKENS) for _, d in A_PATTERNS]
    tabs_a = _rope_tables(jnp.concatenate(pos_a + [jnp.zeros((A_TILE,), positions.dtype)]), A_ROT_HALF)
    tabs_b = _rope_tables(positions.reshape(TOKENS), B_ROT_HALF)

    def rope_group(w):
        return _rope_lanes(jnp.pad(w, [(0, 0)] * (w.ndim - 1) + [(0, LANES - B_ROPE_DIM)]), B_ROT_HALF)

    w_uq = l1_b_w_uq.reshape(B_Q_RANK, B_HEADS, B_NOPE_DIM + B_ROPE_DIM)
    w_q = jnp.concatenate([w_uq[..., :B_NOPE_DIM], rope_group(w_uq[..., B_NOPE_DIM:])], axis=-1)
    w_q = bf(w_q.reshape(B_Q_RANK, B_HEADS * B_SLOT))
    w_ukv = l1_b_w_ukv.reshape(B_KV_RANK, B_HEADS, B_NOPE_DIM + B_V_DIM)
    w_k = bf(w_ukv[:, :, :B_NOPE_DIM].reshape(B_KV_RANK, B_HEADS * B_NOPE_DIM))
    w_v = bf(w_ukv[:, :, B_NOPE_DIM:].reshape(B_KV_RANK, B_HEADS * B_V_DIM))
    n_c = B_Q_RANK + B_KV_RANK
    w_b_in = bf(jnp.concatenate([l1_b_w_in[:, :n_c], rope_group(l1_b_w_in[:, n_c:])], axis=-1))

    h = x.reshape(TOKENS, D_MODEL)

    h = _ffn(h, l0_ffn1_norm, bf(l0_ffn1_w13), bf(l0_ffn1_w2))
    h = _mixer_a(h, l0_mix_norm, _qkv_w(l0_a_w_qkv), bf(l0_a_w_o), tabs_a)
    h = _ffn(h, l0_ffn2_norm, bf(l0_ffn2_w13), bf(l0_ffn2_w2))

    h = _ffn(h, l1_ffn1_norm, bf(l1_ffn1_w13), bf(l1_ffn1_w2))
    h = _mixer_b(h, l1_mix_norm, w_b_in, l1_b_q_norm, w_q, l1_b_kv_norm, w_k, w_v, bf(l1_b_w_o), tabs_b)
    h = _ffn(h, l1_ffn2_norm, bf(l1_ffn2_w13), bf(l1_ffn2_w2))

    h = _ffn(h, l2_ffn1_norm, bf(l2_ffn1_w13), bf(l2_ffn1_w2))
    h = _mixer_c(h, l2_mix_norm, bf(l2_c_w_in), l2_c_conv_w, bf(l2_c_w_out))
    h = _ffn(h, l2_ffn2_norm, bf(l2_ffn2_w13), bf(l2_ffn2_w2))

    h = _ffn(h, l3_ffn1_norm, bf(l3_ffn1_w13), bf(l3_ffn1_w2))
    h = _mixer_a(h, l3_mix_norm, _qkv_w(l3_a_w_qkv), bf(l3_a_w_o), tabs_a)
    h = _ffn(h, l3_ffn2_norm, bf(l3_ffn2_w13), bf(l3_ffn2_w2), final_gain=final_norm)

    return h.reshape(BATCH, SEQ, D_MODEL)
```

```python
import functools

import jax
import jax.numpy as jnp
from jax import lax
from jax.experimental import pallas as pl
from jax.experimental.pallas import tpu as pltpu

F32 = jnp.float32
BF16 = jnp.bfloat16

D_MODEL = 2048
BATCH = 2
SEQ = 8192
TOKENS = BATCH * SEQ
D_FF = 5632
HALF_STEP = 0.5
NORM_EPS = 1e-6
NEG_INF = -1e30
ROPE_THETA = 500000.0

A_HEADS = 16
A_HEAD_DIM = 128
A_ROT_HALF = A_HEAD_DIM // 8
A_PATTERNS = ((128, 1), (512, 4), (2048, 16))
A_BLOCK = 128
A_TILE = 2048

B_HEADS = 16
B_Q_RANK = 1536
B_KV_RANK = 512
B_NOPE_DIM = 128
B_ROPE_DIM = 64
B_ROT_HALF = B_ROPE_DIM // 2
B_V_DIM = 128
B_SLOT = 256
B_VX = 256
B_IN_COLS = 2176

C_CONV = 3

LANES = 128
MXU_COLS = 256
ROPE_PAIR = LANES // 2
VMEM_LIMIT = 56 * 1024 * 1024


def _cparams(*sem):
    return pltpu.CompilerParams(dimension_semantics=sem, vmem_limit_bytes=VMEM_LIMIT)


def _rms(x, gain):
    ms = jnp.mean(x * x, axis=-1, keepdims=True)
    return x * lax.rsqrt(ms + NORM_EPS) * gain


def _dot(a, b):
    return jnp.dot(a, b, preferred_element_type=F32)


def _dot_t(a, b):
    return lax.dot_general(a, b, (((1,), (1,)), ((), ())), preferred_element_type=F32)


def _rope128(y, cos, sin):
    return y * cos + pltpu.roll(y, ROPE_PAIR, 1) * sin


def _rope_lanes(w, half):
    return jnp.concatenate([w[..., :half], w[..., 2 * half:ROPE_PAIR + half], w[..., half:2 * half],
                            w[..., ROPE_PAIR + half:]], axis=-1)


def _rope_tab_kernel(pos_ref, inv_ref, c_ref, s_ref):
    ang = pos_ref[...].astype(F32) * inv_ref[...]
    lane = lax.broadcasted_iota(jnp.int32, (1, LANES), 1)
    sn = jnp.sin(ang)
    c_ref[...] = jnp.cos(ang)
    s_ref[...] = jnp.where(lane < ROPE_PAIR, -sn, sn)


def _rope_tables(pos_flat, half):
    inv = ROPE_THETA ** (-jnp.arange(half, dtype=F32) / half)
    gap = jnp.zeros((ROPE_PAIR - half,), F32)
    inv = jnp.concatenate([inv, gap, inv, gap]).reshape(1, LANES)
    n = pos_flat.shape[0]
    rows = 2048
    tab = jax.ShapeDtypeStruct((n, LANES), F32)
    row_spec = pl.BlockSpec((rows, LANES), lambda i: (i, 0))
    return pl.pallas_call(
        _rope_tab_kernel,
        out_shape=(tab,) * 2,
        grid=(n // rows,),
        in_specs=[pl.BlockSpec((rows, 1), lambda i: (i, 0)), pl.BlockSpec((1, LANES), lambda i: (0, 0))],
        out_specs=(row_spec,) * 2,
        compiler_params=_cparams("parallel"),
        name="rope_tables",
    )(pos_flat.reshape(n, 1), inv)


def _ffn_kernel(x_ref, g_ref, w1_ref, w3_ref, w2_ref, fg_ref, o_ref, xn_ref, acc_ref, *, final):
    j = pl.program_id(1)

    @pl.when(j == 0)
    def _():
        xn_ref[...] = _rms(x_ref[...], g_ref[...]).astype(BF16)
        acc_ref[...] = jnp.zeros_like(acc_ref)

    xn = xn_ref[...]
    gate = _dot(xn, w1_ref[...])
    up = _dot(xn, w3_ref[...])
    act = (gate * (1.0 / (1.0 + jnp.exp(-gate))) * up).astype(BF16)
    acc_ref[...] += _dot(act, w2_ref[...])

    @pl.when(j == pl.num_programs(1) - 1)
    def _():
        y = x_ref[...] + HALF_STEP * acc_ref[...]
        if final:
            y = _rms(y, fg_ref[...])
        o_ref[...] = y


def _ffn(h, gain, w13, w2, final_gain=None):
    tm, tf = 512, 512
    nj = D_FF // tf
    ni = TOKENS // tm
    final = final_gain is not None
    fg = final_gain if final else gain
    return pl.pallas_call(
        functools.partial(_ffn_kernel, final=final),
        out_shape=jax.ShapeDtypeStruct((TOKENS, D_MODEL), F32),
        grid=(ni, nj),
        in_specs=[
            pl.BlockSpec((tm, D_MODEL), lambda i, j: (i, 0)),
            pl.BlockSpec((1, D_MODEL), lambda i, j: (0, 0)),
            pl.BlockSpec((D_MODEL, tf), lambda i, j: (0, j)),
            pl.BlockSpec((D_MODEL, tf), lambda i, j: (0, j + nj)),
            pl.BlockSpec((tf, D_MODEL), lambda i, j: (j, 0)),
            pl.BlockSpec((1, D_MODEL), lambda i, j: (0, 0)),
        ],
        out_specs=pl.BlockSpec((tm, D_MODEL), lambda i, j: (i, 0)),
        scratch_shapes=[pltpu.VMEM((tm, D_MODEL), BF16), pltpu.VMEM((tm, D_MODEL), F32)],
        compiler_params=_cparams("parallel", "arbitrary"),
        name="ffn",
    )(h, gain.reshape(1, D_MODEL), w13, w13, w2, fg.reshape(1, D_MODEL))


def _out_proj_kernel(x_ref, w_ref, r_ref, o_ref):
    o_ref[...] = r_ref[...] + _dot(x_ref[...], w_ref[...])


def _out_proj(x, w, res):
    k = x.shape[1]
    tm = 512
    return pl.pallas_call(
        _out_proj_kernel,
        out_shape=jax.ShapeDtypeStruct((TOKENS, D_MODEL), F32),
        grid=(TOKENS // tm,),
        in_specs=[
            pl.BlockSpec((tm, k), lambda i: (i, 0)),
            pl.BlockSpec((k, D_MODEL), lambda i: (0, 0)),
            pl.BlockSpec((tm, D_MODEL), lambda i: (i, 0)),
        ],
        out_specs=pl.BlockSpec((tm, D_MODEL), lambda i: (i, 0)),
        compiler_params=_cparams("parallel"),
        name="out_proj",
    )(x, w, res)


def _norm_perm_a_kernel(x_ref, g_ref, *refs, tm):
    o_refs, xs_ref = refs[:-1], refs[-1]
    xn = _rms(x_ref[...], g_ref[...])
    nchunk = D_MODEL // LANES
    for k in range(nchunk):
        xs_ref[k] = xn[:, k * LANES:(k + 1) * LANES]
    for (_, d), o_ref in zip(A_PATTERNS, o_refs):
        if d == 1:
            o_ref[...] = xn.astype(BF16)
        else:
            for c in range(d):
                for k in range(nchunk):
                    o_ref[0, c, 0, :, k * LANES:(k + 1) * LANES] = (
                        xs_ref[k, pl.ds(c, tm // d, stride=d), :].astype(BF16))


def _norm_perm_a(h, gain):
    tm = 512
    per_tile = A_TILE // tm
    shapes, specs = [], []
    for _, d in A_PATTERNS:
        if d == 1:
            shapes.append(jax.ShapeDtypeStruct((TOKENS, D_MODEL), BF16))
            specs.append(pl.BlockSpec((tm, D_MODEL), lambda i: (i, 0)))
        else:
            shapes.append(jax.ShapeDtypeStruct((TOKENS // A_TILE, d, per_tile, tm // d, D_MODEL), BF16))
            specs.append(pl.BlockSpec((1, d, 1, tm // d, D_MODEL),
                                      lambda i: (i // per_tile, 0, i % per_tile, 0, 0)))
    outs = pl.pallas_call(
        functools.partial(_norm_perm_a_kernel, tm=tm),
        out_shape=tuple(shapes),
        grid=(TOKENS // tm,),
        in_specs=[pl.BlockSpec((tm, D_MODEL), lambda i: (i, 0)), pl.BlockSpec((1, D_MODEL), lambda i: (0, 0))],
        out_specs=tuple(specs),
        scratch_shapes=[pltpu.VMEM((D_MODEL // LANES, tm, LANES), F32)],
        compiler_params=_cparams("parallel"),
        name="norm_perm_a",
    )(h, gain.reshape(1, D_MODEL))
    return [o.reshape(TOKENS, D_MODEL) for o in outs]


def _qkv_w_kernel(w_ref, o_ref):
    part = pl.program_id(1) % 3

    @pl.when(part == 2)
    def _():
        o_ref[...] = w_ref[...].astype(BF16)

    @pl.when(part != 2)
    def _():
        lane = lax.broadcasted_iota(jnp.int32, (1, LANES), 1)
        stay = (lane < A_ROT_HALF) | (lane >= ROPE_PAIR + A_ROT_HALF)
        for hh in range(A_HEADS):
            sl = slice(hh * A_HEAD_DIM, (hh + 1) * A_HEAD_DIM)
            y = w_ref[:, sl]
            moved = jnp.where(lane < ROPE_PAIR, pltpu.roll(y, LANES - A_ROT_HALF, 1),
                              pltpu.roll(y, ROPE_PAIR - A_ROT_HALF, 1))
            o_ref[:, sl] = jnp.where(stay, y, moved).astype(BF16)


def _qkv_w(w_qkv):
    rows = 256
    hd = A_HEADS * A_HEAD_DIM
    spec = pl.BlockSpec((rows, hd), lambda i, j: (i, j))
    return pl.pallas_call(
        _qkv_w_kernel,
        out_shape=jax.ShapeDtypeStruct(w_qkv.shape, BF16),
        grid=(D_MODEL // rows, w_qkv.shape[1] // hd),
        in_specs=[spec],
        out_specs=spec,
        compiler_params=_cparams("parallel", "parallel"),
        name="qkv_w",
    )(w_qkv)


def _qkv_a_kernel(x_ref, w_ref, cos_ref, sin_ref, o_ref, *, tn):
    part = pl.program_id(1) // ((A_HEADS * A_HEAD_DIM) // tn)
    scale = jnp.where(part == 0, A_HEAD_DIM ** -0.5, 1.0).astype(F32)
    cos, sin = cos_ref[...], sin_ref[...]
    x = x_ref[...]
    for cc in range(tn // MXU_COLS):
        y = _dot(x, w_ref[:, cc * MXU_COLS:(cc + 1) * MXU_COLS])
        for hh in range(MXU_COLS // LANES):
            sl = slice(cc * MXU_COLS + hh * LANES, cc * MXU_COLS + (hh + 1) * LANES)
            y_h = y[:, hh * LANES:(hh + 1) * LANES]
            o_ref[:, sl] = (_rope128(y_h, cos, sin) * scale).astype(BF16)


def _qkv_a(xn, w_qkv, tabs, group):
    tm, tn = 1024, 1024
    cols = 3 * A_HEADS * A_HEAD_DIM
    nj = cols // tn
    ni = TOKENS // tm
    v_first = 2 * (A_HEADS * A_HEAD_DIM) // tn
    tab_spec = pl.BlockSpec((tm, LANES),
                            lambda i, j: (jnp.where(j >= v_first, len(A_PATTERNS) * ni, group * ni + i), 0))
    return pl.pallas_call(
        functools.partial(_qkv_a_kernel, tn=tn),
        out_shape=jax.ShapeDtypeStruct((TOKENS, cols), BF16),
        grid=(ni, nj),
        in_specs=[
            pl.BlockSpec((tm, D_MODEL), lambda i, j: (i, 0)),
            pl.BlockSpec((D_MODEL, tn), lambda i, j: (0, group * nj + j)),
            tab_spec, tab_spec,
        ],
        out_specs=pl.BlockSpec((tm, tn), lambda i, j: (i, j)),
        compiler_params=_cparams("parallel", "parallel"),
        name="qkv_a",
    )(xn, w_qkv, *tabs)


def _attn_a_kernel(q_ref, kc_ref, kp_ref, vc_ref, vp_ref, o_ref, lse_ref, *, d, heads, per_trip):
    rows = A_TILE // d
    nblk = rows // A_BLOCK
    first_tile = pl.program_id(1) == 0
    r = lax.broadcasted_iota(jnp.int32, (A_BLOCK, 2 * A_BLOCK), 0)
    kk = lax.broadcasted_iota(jnp.int32, (A_BLOCK, 2 * A_BLOCK), 1)
    band = (kk >= r) & (kk <= r + A_BLOCK)
    lane = lax.broadcasted_iota(jnp.int32, (A_BLOCK, LANES), 1)
    hs = [slice(h * A_HEAD_DIM, (h + 1) * A_HEAD_DIM) for h in range(heads)]

    def trip(t, carry):
        loaded = []
        for i in range(per_trip):
            u = t * per_trip + i
            c = u // nblk
            blk = u % nblk
            start = pl.multiple_of(blk * A_BLOCK, A_BLOCK)
            before = pl.multiple_of(jnp.maximum(start - A_BLOCK, 0), A_BLOCK)
            q = q_ref[0, c, pl.ds(start, A_BLOCK), :]
            k = jnp.concatenate([jnp.where(blk == 0, kp_ref[0, c], kc_ref[0, c, pl.ds(before, A_BLOCK), :]),
                                 kc_ref[0, c, pl.ds(start, A_BLOCK), :]], axis=0)
            v = jnp.concatenate([jnp.where(blk == 0, vp_ref[0, c], vc_ref[0, c, pl.ds(before, A_BLOCK), :]),
                                 vc_ref[0, c, pl.ds(start, A_BLOCK), :]], axis=0)
            no_prev = jnp.logical_and(first_tile, blk == 0)
            mask = jnp.logical_and(band, jnp.logical_or(kk >= A_BLOCK, jnp.logical_not(no_prev)))
            out_rows = (pl.ds(blk * (A_BLOCK * d) + c, A_BLOCK, stride=d) if d > 1
                        else pl.ds(start, A_BLOCK))
            loaded.append(([_dot_t(q[:, sl], k[:, sl]) for sl in hs], v, mask, out_rows))
        for scores, v, mask, out_rows in loaded:
            lse_tile = jnp.zeros((A_BLOCK, LANES), F32)
            for h, sl in enumerate(hs):
                s = jnp.where(mask, scores[h], NEG_INF)
                m = jnp.max(s, axis=-1, keepdims=True)
                p = jnp.exp(s - m)
                l = jnp.sum(p, axis=-1, keepdims=True)
                o_ref[h, out_rows, :] = _dot(p.astype(BF16), v[:, sl]) / l
                lse_tile = jnp.where(lane == h, m + jnp.log(l), lse_tile)
            lse_ref[out_rows, :] = lse_tile
        return carry

    lax.fori_loop(0, d * nblk // per_trip, trip, 0)


def _attn_a(qkv, window, dilation):
    assert window // dilation == A_BLOCK, "band width must equal the query block"
    d = dilation
    heads = 4
    hw = heads * A_HEAD_DIM
    ngrp = A_HEADS // heads
    rows = A_TILE // d
    tiles = SEQ // A_TILE
    qkv_v = qkv.reshape(BATCH, tiles, d, rows, 3 * A_HEADS * A_HEAD_DIM)
    last_blk = rows // A_BLOCK - 1

    def cur(part):
        return pl.BlockSpec((None, 1, d, rows, hw), lambda b, n, g: (b, n, 0, 0, part * ngrp + g))

    def prev(part):
        return pl.BlockSpec((None, 1, d, A_BLOCK, hw),
                            lambda b, n, g: (b, jnp.maximum(n - 1, 0), 0, last_blk, part * ngrp + g))

    return pl.pallas_call(
        functools.partial(_attn_a_kernel, d=d, heads=heads, per_trip=4),
        out_shape=(jax.ShapeDtypeStruct((A_HEADS, TOKENS, A_HEAD_DIM), F32),
                   jax.ShapeDtypeStruct((TOKENS, ngrp * LANES), F32)),
        grid=(BATCH, tiles, ngrp),
        in_specs=[cur(0), cur(1), prev(1), cur(2), prev(2)],
        out_specs=(pl.BlockSpec((heads, A_TILE, A_HEAD_DIM), lambda b, n, g: (g, b * tiles + n, 0)),
                   pl.BlockSpec((A_TILE, LANES), lambda b, n, g: (b * tiles + n, g))),
        compiler_params=_cparams("parallel", "parallel", "parallel"),
        name="attn_a_d%d" % d,
    )(qkv_v, qkv_v, qkv_v, qkv_v, qkv_v)


def _comb_proj_a_kernel(o0, o1, o2, l0, l1, l2, w_ref, r_ref, out_ref, comb_ref, *, heads):
    s = pl.program_id(0)
    slot = s % 2

    @pl.when(s == 0)
    def _():
        comb_ref[1] = jnp.zeros(comb_ref.shape[1:], BF16)

    out_ref[...] = r_ref[...] + _dot(comb_ref[1 - slot], w_ref[...])
    a = [l0[...], l1[...], l2[...]]
    m = jnp.maximum(jnp.maximum(a[0], a[1]), a[2])
    e = [jnp.exp(t - m) for t in a]
    den = e[0] + e[1] + e[2]
    w = [t / den for t in e]
    rows = out_ref.shape[0]
    for h in range(A_HEADS):
        col = (h // heads) * LANES + h % heads
        acc = jnp.zeros((rows, A_HEAD_DIM), F32)
        for wg, og in zip(w, (o0, o1, o2)):
            acc = acc + jnp.broadcast_to(wg[:, col:col + 1], (rows, A_HEAD_DIM)) * og[h]
        comb_ref[slot, :, h * A_HEAD_DIM:(h + 1) * A_HEAD_DIM] = acc.astype(BF16)


def _comb_proj_a(outs, lses, w_o, res):
    tm = 256
    nt = TOKENS // tm
    hd = A_HEADS * A_HEAD_DIM
    lw = lses[0].shape[1]
    o_spec = pl.BlockSpec((A_HEADS, tm, A_HEAD_DIM), lambda s: (0, jnp.minimum(s, nt - 1), 0))
    l_spec = pl.BlockSpec((tm, lw), lambda s: (jnp.minimum(s, nt - 1), 0))
    row_spec = pl.BlockSpec((tm, D_MODEL), lambda s: (jnp.maximum(s - 1, 0), 0))
    return pl.pallas_call(
        functools.partial(_comb_proj_a_kernel, heads=A_HEADS * LANES // lw),
        out_shape=jax.ShapeDtypeStruct((TOKENS, D_MODEL), F32),
        grid=(nt + 1,),
        in_specs=[o_spec] * 3 + [l_spec] * 3 + [pl.BlockSpec((hd, D_MODEL), lambda s: (0, 0)), row_spec],
        out_specs=row_spec,
        scratch_shapes=[pltpu.VMEM((2, tm, hd), BF16)],
        compiler_params=_cparams("arbitrary"),
        name="comb_proj_a",
    )(*outs, *lses, w_o, res)


def _mixer_a(h, gain, w_qkv, w_o, tabs_a):
    outs, lses = [], []
    for g, ((window, dilation), xn) in enumerate(zip(A_PATTERNS, _norm_perm_a(h, gain))):
        o, lse = _attn_a(_qkv_a(xn, w_qkv, tabs_a, g), window, dilation)
        outs.append(o)
        lses.append(lse)
    return _comb_proj_a(outs, lses, w_o, h)


def _b_in_kernel(x_ref, g_ref, w_ref, o_ref):
    o_ref[...] = _dot(_rms(x_ref[...], g_ref[...]).astype(BF16), w_ref[...])


def _b_in(h, gain, w_in):
    tm = 512
    return pl.pallas_call(
        _b_in_kernel,
        out_shape=jax.ShapeDtypeStruct((TOKENS, B_IN_COLS), F32),
        grid=(TOKENS // tm,),
        in_specs=[
            pl.BlockSpec((tm, D_MODEL), lambda i: (i, 0)),
            pl.BlockSpec((1, D_MODEL), lambda i: (0, 0)),
            pl.BlockSpec((D_MODEL, B_IN_COLS), lambda i: (0, 0)),
        ],
        out_specs=pl.BlockSpec((tm, B_IN_COLS), lambda i: (i, 0)),
        compiler_params=_cparams("parallel"),
        name="b_in",
    )(h, gain.reshape(1, D_MODEL), w_in)


def _b_q_kernel(c_ref, g_ref, w_ref, cos_ref, sin_ref, o_ref, xn_ref, *, slots):
    @pl.when(pl.program_id(1) == 0)
    def _():
        xn_ref[...] = _rms(c_ref[...], g_ref[...]).astype(BF16)

    y = _dot(xn_ref[...], w_ref[...])
    scale = (B_NOPE_DIM + B_ROPE_DIM) ** -0.5
    cos, sin = cos_ref[...], sin_ref[...]
    for s in range(slots):
        nope = y[:, s * B_SLOT:s * B_SLOT + LANES]
        rot = _rope128(y[:, s * B_SLOT + LANES:(s + 1) * B_SLOT], cos, sin)
        o_ref[s, :, :LANES] = (nope * scale).astype(BF16)
        o_ref[s, :, LANES:] = (rot * scale).astype(BF16)


def _b_q(c, gain, w_q, tabs_b):
    tm, slots = 1024, 4
    tn = slots * B_SLOT
    tab_spec = pl.BlockSpec((tm, LANES), lambda i, j: (i, 0))
    return pl.pallas_call(
        functools.partial(_b_q_kernel, slots=slots),
        out_shape=jax.ShapeDtypeStruct((B_HEADS, TOKENS, B_SLOT), BF16),
        grid=(TOKENS // tm, B_HEADS // slots),
        in_specs=[
            pl.BlockSpec((tm, B_Q_RANK), lambda i, j: (i, 0)),
            pl.BlockSpec((1, B_Q_RANK), lambda i, j: (0, 0)),
            pl.BlockSpec((B_Q_RANK, tn), lambda i, j: (0, j)),
            tab_spec, tab_spec,
        ],
        out_specs=pl.BlockSpec((slots, tm, B_SLOT), lambda i, j: (j, i, 0)),
        scratch_shapes=[pltpu.VMEM((tm, B_Q_RANK), BF16)],
        compiler_params=_cparams("parallel", "arbitrary"),
        name="b_q",
    )(c, gain.reshape(1, B_Q_RANK), w_q, *tabs_b)


def _b_kv_kernel(c_ref, g_ref, wk_ref, wv_ref, cr_ref, cos_ref, sin_ref, k_ref, v_ref, *, slots):
    xn = _rms(c_ref[...], g_ref[...]).astype(BF16)
    yk = _dot(xn, wk_ref[...])
    yv = _dot(xn, wv_ref[...])
    kr = _rope128(cr_ref[...], cos_ref[...], sin_ref[...]).astype(BF16)
    lane = lax.broadcasted_iota(jnp.int32, (c_ref.shape[0], B_VX - B_V_DIM), 1)
    ones_col = jnp.where(lane == 0, 1.0, 0.0).astype(BF16)
    for s in range(slots):
        sl = slice(s * LANES, (s + 1) * LANES)
        k_ref[s, :, :LANES] = yk[:, sl].astype(BF16)
        k_ref[s, :, LANES:] = kr
        v_ref[s, :, :B_V_DIM] = yv[:, sl].astype(BF16)
        v_ref[s, :, B_V_DIM:] = ones_col


def _b_kv(c, gain, w_k, w_v, tabs_b):
    tm, slots = 1024, 4
    tn = slots * LANES
    kv_block = B_Q_RANK // B_KV_RANK
    rope_block = (B_Q_RANK + B_KV_RANK) // LANES
    tab_spec = pl.BlockSpec((tm, LANES), lambda i, j: (i, 0))
    return pl.pallas_call(
        functools.partial(_b_kv_kernel, slots=slots),
        out_shape=(jax.ShapeDtypeStruct((B_HEADS, TOKENS, B_SLOT), BF16),
                   jax.ShapeDtypeStruct((B_HEADS, TOKENS, B_VX), BF16)),
        grid=(TOKENS // tm, B_HEADS // slots),
        in_specs=[
            pl.BlockSpec((tm, B_KV_RANK), lambda i, j: (i, kv_block)),
            pl.BlockSpec((1, B_KV_RANK), lambda i, j: (0, 0)),
            pl.BlockSpec((B_KV_RANK, tn), lambda i, j: (0, j)),
            pl.BlockSpec((B_KV_RANK, tn), lambda i, j: (0, j)),
            pl.BlockSpec((tm, LANES), lambda i, j: (i, rope_block)),
            tab_spec, tab_spec,
        ],
        out_specs=(pl.BlockSpec((slots, tm, B_SLOT), lambda i, j: (j, i, 0)),
                   pl.BlockSpec((slots, tm, B_VX), lambda i, j: (j, i, 0))),
        compiler_params=_cparams("parallel", "parallel"),
        name="b_kv",
    )(c, gain.reshape(1, B_KV_RANK), w_k, w_v, c, *tabs_b)


def _attn_b_kernel(q_ref, k_ref, v_ref, o_ref, s0_ref, s1_ref, m_ref, acc_ref, *, tq):
    qi = pl.program_id(2)
    q = q_ref[0]
    m_ref[...] = jnp.full_like(m_ref, NEG_INF)
    acc_ref[...] = jnp.zeros_like(acc_ref)

    def scores(ki):
        return _dot_t(q, k_ref[0, pl.ds(pl.multiple_of(ki * tq, tq), tq), :])

    def fold(s_ref, ki, diagonal):
        s = s_ref[...]
        if diagonal:
            row = lax.broadcasted_iota(jnp.int32, (tq, tq), 0)
            col = lax.broadcasted_iota(jnp.int32, (tq, tq), 1)
            s = jnp.where(row >= col, s, NEG_INF)
        m_prev = m_ref[...]
        m_new = jnp.maximum(m_prev, jnp.max(s, axis=-1, keepdims=True))
        alpha = jnp.exp(m_prev - m_new)
        p = jnp.exp((s - jnp.concatenate([m_new] * (tq // LANES), axis=1)).astype(BF16))
        v = v_ref[0, pl.ds(pl.multiple_of(ki * tq, tq), tq), :]
        acc_ref[...] = jnp.concatenate([alpha] * (B_VX // LANES), axis=1) * acc_ref[...] + _dot(p, v)
        m_ref[...] = m_new

    pairs = qi // 2
    s0_ref[...] = scores(0)

    def body(t, carry):
        s1_ref[...] = scores(2 * t + 1)
        fold(s0_ref, 2 * t, False)
        s0_ref[...] = scores(2 * t + 2)
        fold(s1_ref, 2 * t + 1, False)
        return carry

    lax.fori_loop(0, pairs, body, 0)

    @pl.when(qi % 2 == 1)
    def _():
        s1_ref[...] = scores(qi)
        fold(s0_ref, qi - 1, False)
        fold(s1_ref, qi, True)

    @pl.when(qi % 2 == 0)
    def _():
        fold(s0_ref, qi, True)

    acc = acc_ref[...]
    l = jnp.broadcast_to(acc[:, B_V_DIM:B_V_DIM + 1], (tq, B_V_DIM))
    o_ref[...] = (acc[:, :B_V_DIM] / l).astype(BF16)


def _attn_b(q3, k3, v3):
    tq = 512
    nq = SEQ // tq
    return pl.pallas_call(
        functools.partial(_attn_b_kernel, tq=tq),
        out_shape=jax.ShapeDtypeStruct((TOKENS, B_HEADS * B_V_DIM), BF16),
        grid=(BATCH, B_HEADS, nq),
        in_specs=[
            pl.BlockSpec((1, tq, B_SLOT), lambda b, h, qi: (h, b * nq + qi, 0)),
            pl.BlockSpec((1, SEQ, B_SLOT), lambda b, h, qi: (h, b, 0)),
            pl.BlockSpec((1, SEQ, B_VX), lambda b, h, qi: (h, b, 0)),
        ],
        out_specs=pl.BlockSpec((tq, B_V_DIM), lambda b, h, qi: (b * nq + qi, h)),
        scratch_shapes=[pltpu.VMEM((tq, tq), F32), pltpu.VMEM((tq, tq), F32),
                        pltpu.VMEM((tq, LANES), F32), pltpu.VMEM((tq, B_VX), F32)],
        compiler_params=_cparams("parallel", "parallel", "arbitrary"),
        name="attn_b",
    )(q3, k3, v3)


def _mixer_b(h, gain, w_in, q_norm, w_q, kv_norm, w_k, w_v, w_o, tabs_b):
    c = _b_in(h, gain, w_in)
    q3 = _b_q(c, q_norm, w_q, tabs_b)
    k3, v3 = _b_kv(c, kv_norm, w_k, w_v, tabs_b)
    return _out_proj(_attn_b(q3, k3, v3), w_o, h)


def _c_in_kernel(x_ref, g_ref, wb_ref, wc_ref, wu_ref, cw_ref, o_ref, xn_ref, carry_ref, *, tm):
    i = pl.program_id(0)
    j = pl.program_id(1)

    @pl.when(j == 0)
    def _():
        xn_ref[...] = _rms(x_ref[...], g_ref[...]).astype(BF16)

    xn = xn_ref[...]
    b_gate = _dot(xn, wb_ref[...])
    cu = _dot(xn, wc_ref[...]) * _dot(xn, wu_ref[...])
    w0, w1, w2 = cw_ref[0:1, :], cw_ref[1:2, :], cw_ref[2:3, :]
    z = w2 * cu + w1 * pltpu.roll(cu, 1, 0) + w0 * pltpu.roll(cu, 2, 0)
    o_ref[...] = (b_gate * z).astype(BF16)

    first = (i * tm) % SEQ == 0
    tail = jnp.where(first, 0.0, carry_ref[j])
    both = jnp.concatenate([tail, cu[0:8, :]], axis=0)
    z8 = w2 * both[8:16] + w1 * both[7:15] + w0 * both[6:14]
    o_ref[0:8, :] = (b_gate[0:8, :] * z8).astype(BF16)
    carry_ref[j] = cu[tm - 8:tm, :]


def _c_in(h, gain, w_in, conv_w):
    tm, tn = 1024, 512
    nj = D_MODEL // tn
    return pl.pallas_call(
        functools.partial(_c_in_kernel, tm=tm),
        out_shape=jax.ShapeDtypeStruct((TOKENS, D_MODEL), BF16),
        grid=(TOKENS // tm, nj),
        in_specs=[
            pl.BlockSpec((tm, D_MODEL), lambda i, j: (i, 0)),
            pl.BlockSpec((1, D_MODEL), lambda i, j: (0, 0)),
            pl.BlockSpec((D_MODEL, tn), lambda i, j: (0, j)),
            pl.BlockSpec((D_MODEL, tn), lambda i, j: (0, j + nj)),
            pl.BlockSpec((D_MODEL, tn), lambda i, j: (0, j + 2 * nj)),
            pl.BlockSpec((C_CONV, tn), lambda i, j: (0, j)),
        ],
        out_specs=pl.BlockSpec((tm, tn), lambda i, j: (i, j)),
        scratch_shapes=[pltpu.VMEM((tm, D_MODEL), BF16), pltpu.VMEM((nj, 8, tn), F32)],
        compiler_params=_cparams("arbitrary", "arbitrary"),
        name="c_in",
    )(h, gain.reshape(1, D_MODEL), w_in, w_in, w_in, conv_w)


def _mixer_c(h, gain, w_in, conv_w, w_out):
    return _out_proj(_c_in(h, gain, w_in, conv_w), w_out, h)


def kernel(x, positions, l0_ffn1_norm, l0_ffn1_w13, l0_ffn1_w2, l0_mix_norm, l0_a_w_qkv, l0_a_w_o, l0_ffn2_norm, l0_ffn2_w13, l0_ffn2_w2, l1_ffn1_norm, l1_ffn1_w13, l1_ffn1_w2, l1_mix_norm, l1_b_w_in, l1_b_q_norm, l1_b_w_uq, l1_b_kv_norm, l1_b_w_ukv, l1_b_w_o, l1_ffn2_norm, l1_ffn2_w13, l1_ffn2_w2, l2_ffn1_norm, l2_ffn1_w13, l2_ffn1_w2, l2_mix_norm, l2_c_w_in, l2_c_conv_w, l2_c_w_out, l2_ffn2_norm, l2_ffn2_w13, l2_ffn2_w2, l3_ffn1_norm, l3_ffn1_w13, l3_ffn1_w2, l3_mix_norm, l3_a_w_qkv, l3_a_w_o, l3_ffn2_norm, l3_ffn2_w13, l3_ffn2_w2, final_norm):
    bf = lambda w: w.astype(BF16)
    pos_tiles = positions.reshape(TOKENS // A_TILE, A_TILE)
    pos_a = [pos_tiles.reshape(-1, A_TILE // d, d).swapaxes(1, 2).reshape(TOKENS) for _, d in A_PATTERNS]
    tabs_a = _rope_tables(jnp.concatenate(pos_a + [jnp.zeros((A_TILE,), positions.dtype)]), A_ROT_HALF)
    tabs_b = _rope_tables(positions.reshape(TOKENS), B_ROT_HALF)

    def rope_group(w):
        return _rope_lanes(jnp.pad(w, [(0, 0)] * (w.ndim - 1) + [(0, LANES - B_ROPE_DIM)]), B_ROT_HALF)

    w_uq = l1_b_w_uq.reshape(B_Q_RANK, B_HEADS, B_NOPE_DIM + B_ROPE_DIM)
    w_q = jnp.concatenate([w_uq[..., :B_NOPE_DIM], rope_group(w_uq[..., B_NOPE_DIM:])], axis=-1)
    w_q = bf(w_q.reshape(B_Q_RANK, B_HEADS * B_SLOT))
    w_ukv = l1_b_w_ukv.reshape(B_KV_RANK, B_HEADS, B_NOPE_DIM + B_V_DIM)
    w_k = bf(w_ukv[:, :, :B_NOPE_DIM].reshape(B_KV_RANK, B_HEADS * B_NOPE_DIM))
    w_v = bf(w_ukv[:, :, B_NOPE_DIM:].reshape(B_KV_RANK, B_HEADS * B_V_DIM))
    n_c = B_Q_RANK + B_KV_RANK
    w_b_in = bf(jnp.concatenate([l1_b_w_in[:, :n_c], rope_group(l1_b_w_in[:, n_c:])], axis=-1))

    h = x.reshape(TOKENS, D_MODEL)

    h = _ffn(h, l0_ffn1_norm, bf(l0_ffn1_w13), bf(l0_ffn1_w2))
    h = _mixer_a(h, l0_mix_norm, _qkv_w(l0_a_w_qkv), bf(l0_a_w_o), tabs_a)
    h = _ffn(h, l0_ffn2_norm, bf(l0_ffn2_w13), bf(l0_ffn2_w2))

    h = _ffn(h, l1_ffn1_norm, bf(l1_ffn1_w13), bf(l1_ffn1_w2))
    h = _mixer_b(h, l1_mix_norm, w_b_in, l1_b_q_norm, w_q, l1_b_kv_norm, w_k, w_v, bf(l1_b_w_o), tabs_b)
    h = _ffn(h, l1_ffn2_norm, bf(l1_ffn2_w13), bf(l1_ffn2_w2))

    h = _ffn(h, l2_ffn1_norm, bf(l2_ffn1_w13), bf(l2_ffn1_w2))
    h = _mixer_c(h, l2_mix_norm, bf(l2_c_w_in), l2_c_conv_w, bf(l2_c_w_out))
    h = _ffn(h, l2_ffn2_norm, bf(l2_ffn2_w13), bf(l2_ffn2_w2))

    h = _ffn(h, l3_ffn1_norm, bf(l3_ffn1_w13), bf(l3_ffn1_w2))
    h = _mixer_a(h, l3_mix_norm, _qkv_w(l3_a_w_qkv), bf(l3_a_w_o), tabs_a)
    h = _ffn(h, l3_ffn2_norm, bf(l3_ffn2_w13), bf(l3_ffn2_w2), final_gain=final_norm)

    return h.reshape(BATCH, SEQ, D_MODEL)
```

```python
import functools

import jax
import jax.numpy as jnp
from jax import lax
from jax.experimental import pallas as pl
from jax.experimental.pallas import tpu as pltpu

F32 = jnp.float32
BF16 = jnp.bfloat16

D_MODEL = 2048
BATCH = 2
SEQ = 8192
TOKENS = BATCH * SEQ
D_FF = 5632
HALF_STEP = 0.5
NORM_EPS = 1e-6
NEG_INF = -1e30
ROPE_THETA = 500000.0

A_HEADS = 16
A_HEAD_DIM = 128
A_ROT_HALF = A_HEAD_DIM // 8
A_PATTERNS = ((128, 1), (512, 4), (2048, 16))
A_BLOCK = 128
A_TILE = 2048

B_HEADS = 16
B_Q_RANK = 1536
B_KV_RANK = 512
B_NOPE_DIM = 128
B_ROPE_DIM = 64
B_ROT_HALF = B_ROPE_DIM // 2
B_V_DIM = 128
B_SLOT = 256
B_VX = 256
B_IN_COLS = 2176

C_CONV = 3

LANES = 128
MXU_COLS = 256
ROPE_PAIR = LANES // 2
VMEM_LIMIT = 56 * 1024 * 1024


def _cparams(*sem):
    return pltpu.CompilerParams(dimension_semantics=sem, vmem_limit_bytes=VMEM_LIMIT)


def _rms(x, gain):
    ms = jnp.mean(x * x, axis=-1, keepdims=True)
    return x * lax.rsqrt(ms + NORM_EPS) * gain


def _dot(a, b):
    return jnp.dot(a, b, preferred_element_type=F32)


def _dot_t(a, b):
    return lax.dot_general(a, b, (((1,), (1,)), ((), ())), preferred_element_type=F32)


def _rope128(y, cos, sin):
    return y * cos + pltpu.roll(y, ROPE_PAIR, 1) * sin


def _rope_lanes(w, half):
    return jnp.concatenate([w[..., :half], w[..., 2 * half:ROPE_PAIR + half], w[..., half:2 * half],
                            w[..., ROPE_PAIR + half:]], axis=-1)


def _rope_tab_kernel(pos_ref, inv_ref, c_ref, s_ref):
    ang = pos_ref[...].astype(F32) * inv_ref[...]
    lane = lax.broadcasted_iota(jnp.int32, (1, LANES), 1)
    sn = jnp.sin(ang)
    c_ref[...] = jnp.cos(ang)
    s_ref[...] = jnp.where(lane < ROPE_PAIR, -sn, sn)


def _rope_tables(pos_flat, half):
    inv = ROPE_THETA ** (-jnp.arange(half, dtype=F32) / half)
    gap = jnp.zeros((ROPE_PAIR - half,), F32)
    inv = jnp.concatenate([inv, gap, inv, gap]).reshape(1, LANES)
    n = pos_flat.shape[0]
    rows = 2048
    tab = jax.ShapeDtypeStruct((n, LANES), F32)
    row_spec = pl.BlockSpec((rows, LANES), lambda i: (i, 0))
    return pl.pallas_call(
        _rope_tab_kernel,
        out_shape=(tab,) * 2,
        grid=(n // rows,),
        in_specs=[pl.BlockSpec((rows, 1), lambda i: (i, 0)), pl.BlockSpec((1, LANES), lambda i: (0, 0))],
        out_specs=(row_spec,) * 2,
        compiler_params=_cparams("parallel"),
        name="rope_tables",
    )(pos_flat.reshape(n, 1), inv)


def _ffn_kernel(x_ref, g_ref, w1_ref, w3_ref, w2_ref, fg_ref, o_ref, xn_ref, acc_ref, *, final):
    j = pl.program_id(1)

    @pl.when(j == 0)
    def _():
        xn_ref[...] = _rms(x_ref[...], g_ref[...]).astype(BF16)
        acc_ref[...] = jnp.zeros_like(acc_ref)

    xn = xn_ref[...]
    gate = _dot(xn, w1_ref[...])
    up = _dot(xn, w3_ref[...])
    act = (gate * (1.0 / (1.0 + jnp.exp(-gate))) * up).astype(BF16)
    acc_ref[...] += _dot(act, w2_ref[...])

    @pl.when(j == pl.num_programs(1) - 1)
    def _():
        y = x_ref[...] + HALF_STEP * acc_ref[...]
        if final:
            y = _rms(y, fg_ref[...])
        o_ref[...] = y


def _ffn(h, gain, w13, w2, final_gain=None):
    tm, tf = 512, 512
    nj = D_FF // tf
    ni = TOKENS // tm
    final = final_gain is not None
    fg = final_gain if final else gain
    return pl.pallas_call(
        functools.partial(_ffn_kernel, final=final),
        out_shape=jax.ShapeDtypeStruct((TOKENS, D_MODEL), F32),
        grid=(ni, nj),
        in_specs=[
            pl.BlockSpec((tm, D_MODEL), lambda i, j: (i, 0)),
            pl.BlockSpec((1, D_MODEL), lambda i, j: (0, 0)),
            pl.BlockSpec((D_MODEL, tf), lambda i, j: (0, j)),
            pl.BlockSpec((D_MODEL, tf), lambda i, j: (0, j + nj)),
            pl.BlockSpec((tf, D_MODEL), lambda i, j: (j, 0)),
            pl.BlockSpec((1, D_MODEL), lambda i, j: (0, 0)),
        ],
        out_specs=pl.BlockSpec((tm, D_MODEL), lambda i, j: (i, 0)),
        scratch_shapes=[pltpu.VMEM((tm, D_MODEL), BF16), pltpu.VMEM((tm, D_MODEL), F32)],
        compiler_params=_cparams("parallel", "arbitrary"),
        name="ffn",
    )(h, gain.reshape(1, D_MODEL), w13, w13, w2, fg.reshape(1, D_MODEL))


def _out_proj_kernel(x_ref, w_ref, r_ref, o_ref):
    o_ref[...] = r_ref[...] + _dot(x_ref[...], w_ref[...])


def _out_proj(x, w, res):
    k = x.shape[1]
    tm = 512
    return pl.pallas_call(
        _out_proj_kernel,
        out_shape=jax.ShapeDtypeStruct((TOKENS, D_MODEL), F32),
        grid=(TOKENS // tm,),
        in_specs=[
            pl.BlockSpec((tm, k), lambda i: (i, 0)),
            pl.BlockSpec((k, D_MODEL), lambda i: (0, 0)),
            pl.BlockSpec((tm, D_MODEL), lambda i: (i, 0)),
        ],
        out_specs=pl.BlockSpec((tm, D_MODEL), lambda i: (i, 0)),
        compiler_params=_cparams("parallel"),
        name="out_proj",
    )(x, w, res)


def _norm_perm_a_kernel(x_ref, g_ref, *refs, tm):
    o_refs, xs_ref = refs[:-1], refs[-1]
    xn = _rms(x_ref[...], g_ref[...])
    nchunk = D_MODEL // LANES
    for k in range(nchunk):
        xs_ref[k] = xn[:, k * LANES:(k + 1) * LANES]
    for (_, d), o_ref in zip(A_PATTERNS, o_refs):
        if d == 1:
            o_ref[...] = xn.astype(BF16)
        else:
            for c in range(d):
                for k in range(nchunk):
                    o_ref[0, c, 0, :, k * LANES:(k + 1) * LANES] = (
                        xs_ref[k, pl.ds(c, tm // d, stride=d), :].astype(BF16))


def _norm_perm_a(h, gain):
    tm = 512
    per_tile = A_TILE // tm
    shapes, specs = [], []
    for _, d in A_PATTERNS:
        if d == 1:
            shapes.append(jax.ShapeDtypeStruct((TOKENS, D_MODEL), BF16))
            specs.append(pl.BlockSpec((tm, D_MODEL), lambda i: (i, 0)))
        else:
            shapes.append(jax.ShapeDtypeStruct((TOKENS // A_TILE, d, per_tile, tm // d, D_MODEL), BF16))
            specs.append(pl.BlockSpec((1, d, 1, tm // d, D_MODEL),
                                      lambda i: (i // per_tile, 0, i % per_tile, 0, 0)))
    outs = pl.pallas_call(
        functools.partial(_norm_perm_a_kernel, tm=tm),
        out_shape=tuple(shapes),
        grid=(TOKENS // tm,),
        in_specs=[pl.BlockSpec((tm, D_MODEL), lambda i: (i, 0)), pl.BlockSpec((1, D_MODEL), lambda i: (0, 0))],
        out_specs=tuple(specs),
        scratch_shapes=[pltpu.VMEM((D_MODEL // LANES, tm, LANES), F32)],
        compiler_params=_cparams("parallel"),
        name="norm_perm_a",
    )(h, gain.reshape(1, D_MODEL))
    return [o.reshape(TOKENS, D_MODEL) for o in outs]


def _qkv_w_kernel(w_ref, o_ref):
    part = pl.program_id(1) % 3

    @pl.when(part == 2)
    def _():
        o_ref[...] = w_ref[...].astype(BF16)

    @pl.when(part != 2)
    def _():
        lane = lax.broadcasted_iota(jnp.int32, (1, LANES), 1)
        stay = (lane < A_ROT_HALF) | (lane >= ROPE_PAIR + A_ROT_HALF)
        for hh in range(A_HEADS):
            sl = slice(hh * A_HEAD_DIM, (hh + 1) * A_HEAD_DIM)
            y = w_ref[:, sl]
            moved = jnp.where(lane < ROPE_PAIR, pltpu.roll(y, LANES - A_ROT_HALF, 1),
                              pltpu.roll(y, ROPE_PAIR - A_ROT_HALF, 1))
            o_ref[:, sl] = jnp.where(stay, y, moved).astype(BF16)


def _qkv_w(w_qkv):
    rows = 256
    hd = A_HEADS * A_HEAD_DIM
    spec = pl.BlockSpec((rows, hd), lambda i, j: (i, j))
    return pl.pallas_call(
        _qkv_w_kernel,
        out_shape=jax.ShapeDtypeStruct(w_qkv.shape, BF16),
        grid=(D_MODEL // rows, w_qkv.shape[1] // hd),
        in_specs=[spec],
        out_specs=spec,
        compiler_params=_cparams("parallel", "parallel"),
        name="qkv_w",
    )(w_qkv)


def _qkv_a_kernel(x_ref, w_ref, cos_ref, sin_ref, o_ref, *, tn):
    part = pl.program_id(1) // ((A_HEADS * A_HEAD_DIM) // tn)
    scale = jnp.where(part == 0, A_HEAD_DIM ** -0.5, 1.0).astype(F32)
    half = x_ref.shape[0] // 2
    for cc in range(tn // MXU_COLS):
        for rows in (slice(0, half), slice(half, 2 * half)):
            y = _dot(x_ref[rows, :], w_ref[:, cc * MXU_COLS:(cc + 1) * MXU_COLS])
            for hh in range(MXU_COLS // LANES):
                sl = slice(cc * MXU_COLS + hh * LANES, cc * MXU_COLS + (hh + 1) * LANES)
                y_h = y[:, hh * LANES:(hh + 1) * LANES]
                o_ref[rows, sl] = (_rope128(y_h, cos_ref[rows, :], sin_ref[rows, :]) * scale).astype(BF16)


def _qkv_a(xn, w_qkv, tabs, group):
    tm, tn = 1024, 1024
    cols = 3 * A_HEADS * A_HEAD_DIM
    nj = cols // tn
    ni = TOKENS // tm
    v_first = 2 * (A_HEADS * A_HEAD_DIM) // tn
    tab_spec = pl.BlockSpec((tm, LANES),
                            lambda i, j: (jnp.where(j >= v_first, len(A_PATTERNS) * ni, group * ni + i), 0))
    return pl.pallas_call(
        functools.partial(_qkv_a_kernel, tn=tn),
        out_shape=jax.ShapeDtypeStruct((TOKENS, cols), BF16),
        grid=(ni, nj),
        in_specs=[
            pl.BlockSpec((tm, D_MODEL), lambda i, j: (i, 0)),
            pl.BlockSpec((D_MODEL, tn), lambda i, j: (0, group * nj + j)),
            tab_spec, tab_spec,
        ],
        out_specs=pl.BlockSpec((tm, tn), lambda i, j: (i, j)),
        compiler_params=_cparams("parallel", "parallel"),
        name="qkv_a",
    )(xn, w_qkv, *tabs)


def _attn_a_kernel(q_ref, kc_ref, kp_ref, vc_ref, vp_ref, o_ref, lse_ref, os_ref, *, d, heads, per_trip):
    rows = A_TILE // d
    nblk = rows // A_BLOCK
    first_tile = pl.program_id(1) == 0
    r = lax.broadcasted_iota(jnp.int32, (A_BLOCK, 2 * A_BLOCK), 0)
    kk = lax.broadcasted_iota(jnp.int32, (A_BLOCK, 2 * A_BLOCK), 1)
    band = (kk >= r) & (kk <= r + A_BLOCK)
    lane = lax.broadcasted_iota(jnp.int32, (A_BLOCK, LANES), 1)
    hs = [slice(h * A_HEAD_DIM, (h + 1) * A_HEAD_DIM) for h in range(heads)]

    def trip(t, carry):
        loaded = []
        for i in range(per_trip):
            u = t * per_trip + i
            c = u // nblk
            blk = u % nblk
            start = pl.multiple_of(blk * A_BLOCK, A_BLOCK)
            before = pl.multiple_of(jnp.maximum(start - A_BLOCK, 0), A_BLOCK)
            q = q_ref[0, c, pl.ds(start, A_BLOCK), :]
            k = jnp.concatenate([jnp.where(blk == 0, kp_ref[0, c], kc_ref[0, c, pl.ds(before, A_BLOCK), :]),
                                 kc_ref[0, c, pl.ds(start, A_BLOCK), :]], axis=0)
            v = jnp.concatenate([jnp.where(blk == 0, vp_ref[0, c], vc_ref[0, c, pl.ds(before, A_BLOCK), :]),
                                 vc_ref[0, c, pl.ds(start, A_BLOCK), :]], axis=0)
            no_prev = jnp.logical_and(first_tile, blk == 0)
            mask = jnp.logical_and(band, jnp.logical_or(kk >= A_BLOCK, jnp.logical_not(no_prev)))
            out_rows = (pl.ds(blk * (A_BLOCK * d) + c, A_BLOCK, stride=d) if d > 1
                        else pl.ds(start, A_BLOCK))
            loaded.append(([_dot_t(q[:, sl], k[:, sl]) for sl in hs], v, mask, out_rows))
        for scores, v, mask, out_rows in loaded:
            lse_tile = jnp.zeros((A_BLOCK, LANES), F32)
            for h, sl in enumerate(hs):
                s = jnp.where(mask, scores[h], NEG_INF)
                m = jnp.max(s, axis=-1, keepdims=True)
                p = jnp.exp(s - m)
                l = jnp.sum(p, axis=-1, keepdims=True)
                os_ref[h, out_rows, :] = _dot(p.astype(BF16), v[:, sl]) / l
                lse_tile = jnp.where(lane == h, m + jnp.log(l), lse_tile)
            lse_ref[out_rows, :] = lse_tile
        return carry

    lax.fori_loop(0, d * nblk // per_trip, trip, 0)
    o_ref[...] = os_ref[...].astype(BF16)


def _attn_a(qkv, window, dilation):
    assert window // dilation == A_BLOCK, "band width must equal the query block"
    d = dilation
    heads = 4
    hw = heads * A_HEAD_DIM
    ngrp = A_HEADS // heads
    rows = A_TILE // d
    tiles = SEQ // A_TILE
    qkv_v = qkv.reshape(BATCH, tiles, d, rows, 3 * A_HEADS * A_HEAD_DIM)
    last_blk = rows // A_BLOCK - 1

    def cur(part):
        return pl.BlockSpec((None, 1, d, rows, hw), lambda b, n, g: (b, n, 0, 0, part * ngrp + g))

    def prev(part):
        return pl.BlockSpec((None, 1, d, A_BLOCK, hw),
                            lambda b, n, g: (b, jnp.maximum(n - 1, 0), 0, last_blk, part * ngrp + g))

    return pl.pallas_call(
        functools.partial(_attn_a_kernel, d=d, heads=heads, per_trip=4),
        out_shape=(jax.ShapeDtypeStruct((A_HEADS, TOKENS, A_HEAD_DIM), BF16),
                   jax.ShapeDtypeStruct((TOKENS, ngrp * LANES), F32)),
        grid=(BATCH, tiles, ngrp),
        in_specs=[cur(0), cur(1), prev(1), cur(2), prev(2)],
        out_specs=(pl.BlockSpec((heads, A_TILE, A_HEAD_DIM), lambda b, n, g: (g, b * tiles + n, 0)),
                   pl.BlockSpec((A_TILE, LANES), lambda b, n, g: (b * tiles + n, g))),
        scratch_shapes=[pltpu.VMEM((heads, A_TILE, A_HEAD_DIM), F32)],
        compiler_params=_cparams("parallel", "parallel", "parallel"),
        name="attn_a_d%d" % d,
    )(qkv_v, qkv_v, qkv_v, qkv_v, qkv_v)


def _comb_proj_a_kernel(o0, o1, o2, l0, l1, l2, w_ref, r_ref, out_ref, comb_ref, *, heads):
    s = pl.program_id(0)
    slot = s % 2

    @pl.when(s == 0)
    def _():
        comb_ref[1] = jnp.zeros(comb_ref.shape[1:], BF16)

    out_ref[...] = r_ref[...] + _dot(comb_ref[1 - slot], w_ref[...])
    a = [l0[...], l1[...], l2[...]]
    m = jnp.maximum(jnp.maximum(a[0], a[1]), a[2])
    e = [jnp.exp(t - m) for t in a]
    den = e[0] + e[1] + e[2]
    w = [t / den for t in e]
    rows = out_ref.shape[0]
    for h in range(A_HEADS):
        col = (h // heads) * LANES + h % heads
        acc = jnp.zeros((rows, A_HEAD_DIM), F32)
        for wg, og in zip(w, (o0, o1, o2)):
            acc = acc + jnp.broadcast_to(wg[:, col:col + 1], (rows, A_HEAD_DIM)) * og[h]
        comb_ref[slot, :, h * A_HEAD_DIM:(h + 1) * A_HEAD_DIM] = acc.astype(BF16)


def _comb_proj_a(outs, lses, w_o, res):
    tm = 256
    nt = TOKENS // tm
    hd = A_HEADS * A_HEAD_DIM
    lw = lses[0].shape[1]
    o_spec = pl.BlockSpec((A_HEADS, tm, A_HEAD_DIM), lambda s: (0, jnp.minimum(s, nt - 1), 0))
    l_spec = pl.BlockSpec((tm, lw), lambda s: (jnp.minimum(s, nt - 1), 0))
    row_spec = pl.BlockSpec((tm, D_MODEL), lambda s: (jnp.maximum(s - 1, 0), 0))
    return pl.pallas_call(
        functools.partial(_comb_proj_a_kernel, heads=A_HEADS * LANES // lw),
        out_shape=jax.ShapeDtypeStruct((TOKENS, D_MODEL), F32),
        grid=(nt + 1,),
        in_specs=[o_spec] * 3 + [l_spec] * 3 + [pl.BlockSpec((hd, D_MODEL), lambda s: (0, 0)), row_spec],
        out_specs=row_spec,
        scratch_shapes=[pltpu.VMEM((2, tm, hd), BF16)],
        compiler_params=_cparams("arbitrary"),
        name="comb_proj_a",
    )(*outs, *lses, w_o, res)


def _mixer_a(h, gain, w_qkv, w_o, tabs_a):
    outs, lses = [], []
    for g, ((window, dilation), xn) in enumerate(zip(A_PATTERNS, _norm_perm_a(h, gain))):
        o, lse = _attn_a(_qkv_a(xn, w_qkv, tabs_a, g), window, dilation)
        outs.append(o)
        lses.append(lse)
    return _comb_proj_a(outs, lses, w_o, h)


def _b_in_kernel(x_ref, g_ref, w_ref, o_ref):
    o_ref[...] = _dot(_rms(x_ref[...], g_ref[...]).astype(BF16), w_ref[...])


def _b_in(h, gain, w_in):
    tm = 512
    return pl.pallas_call(
        _b_in_kernel,
        out_shape=jax.ShapeDtypeStruct((TOKENS, B_IN_COLS), F32),
        grid=(TOKENS // tm,),
        in_specs=[
            pl.BlockSpec((tm, D_MODEL), lambda i: (i, 0)),
            pl.BlockSpec((1, D_MODEL), lambda i: (0, 0)),
            pl.BlockSpec((D_MODEL, B_IN_COLS), lambda i: (0, 0)),
        ],
        out_specs=pl.BlockSpec((tm, B_IN_COLS), lambda i: (i, 0)),
        compiler_params=_cparams("parallel"),
        name="b_in",
    )(h, gain.reshape(1, D_MODEL), w_in)


def _b_q_kernel(c_ref, g_ref, w_ref, cos_ref, sin_ref, o_ref, xn_ref, *, slots):
    @pl.when(pl.program_id(1) == 0)
    def _():
        xn_ref[...] = _rms(c_ref[...], g_ref[...]).astype(BF16)

    y = _dot(xn_ref[...], w_ref[...])
    scale = (B_NOPE_DIM + B_ROPE_DIM) ** -0.5
    cos, sin = cos_ref[...], sin_ref[...]
    for s in range(slots):
        nope = y[:, s * B_SLOT:s * B_SLOT + LANES]
        rot = _rope128(y[:, s * B_SLOT + LANES:(s + 1) * B_SLOT], cos, sin)
        o_ref[s, :, :LANES] = (nope * scale).astype(BF16)
        o_ref[s, :, LANES:] = (rot * scale).astype(BF16)


def _b_q(c, gain, w_q, tabs_b):
    tm, slots = 1024, 4
    tn = slots * B_SLOT
    tab_spec = pl.BlockSpec((tm, LANES), lambda i, j: (i, 0))
    return pl.pallas_call(
        functools.partial(_b_q_kernel, slots=slots),
        out_shape=jax.ShapeDtypeStruct((B_HEADS, TOKENS, B_SLOT), BF16),
        grid=(TOKENS // tm, B_HEADS // slots),
        in_specs=[
            pl.BlockSpec((tm, B_Q_RANK), lambda i, j: (i, 0)),
            pl.BlockSpec((1, B_Q_RANK), lambda i, j: (0, 0)),
            pl.BlockSpec((B_Q_RANK, tn), lambda i, j: (0, j)),
            tab_spec, tab_spec,
        ],
        out_specs=pl.BlockSpec((slots, tm, B_SLOT), lambda i, j: (j, i, 0)),
        scratch_shapes=[pltpu.VMEM((tm, B_Q_RANK), BF16)],
        compiler_params=_cparams("parallel", "arbitrary"),
        name="b_q",
    )(c, gain.reshape(1, B_Q_RANK), w_q, *tabs_b)


def _b_kv_kernel(c_ref, g_ref, wk_ref, wv_ref, cr_ref, cos_ref, sin_ref, k_ref, v_ref, *, slots):
    xn = _rms(c_ref[...], g_ref[...]).astype(BF16)
    yk = _dot(xn, wk_ref[...])
    yv = _dot(xn, wv_ref[...])
    kr = _rope128(cr_ref[...], cos_ref[...], sin_ref[...]).astype(BF16)
    lane = lax.broadcasted_iota(jnp.int32, (c_ref.shape[0], B_VX - B_V_DIM), 1)
    ones_col = jnp.where(lane == 0, 1.0, 0.0).astype(BF16)
    for s in range(slots):
        sl = slice(s * LANES, (s + 1) * LANES)
        k_ref[s, :, :LANES] = yk[:, sl].astype(BF16)
        k_ref[s, :, LANES:] = kr
        v_ref[s, :, :B_V_DIM] = yv[:, sl].astype(BF16)
        v_ref[s, :, B_V_DIM:] = ones_col


def _b_kv(c, gain, w_k, w_v, tabs_b):
    tm, slots = 1024, 4
    tn = slots * LANES
    kv_block = B_Q_RANK // B_KV_RANK
    rope_block = (B_Q_RANK + B_KV_RANK) // LANES
    tab_spec = pl.BlockSpec((tm, LANES), lambda i, j: (i, 0))
    return pl.pallas_call(
        functools.partial(_b_kv_kernel, slots=slots),
        out_shape=(jax.ShapeDtypeStruct((B_HEADS, TOKENS, B_SLOT), BF16),
                   jax.ShapeDtypeStruct((B_HEADS, TOKENS, B_VX), BF16)),
        grid=(TOKENS // tm, B_HEADS // slots),
        in_specs=[
            pl.BlockSpec((tm, B_KV_RANK), lambda i, j: (i, kv_block)),
            pl.BlockSpec((1, B_KV_RANK), lambda i, j: (0, 0)),
            pl.BlockSpec((B_KV_RANK, tn), lambda i, j: (0, j)),
            pl.BlockSpec((B_KV_RANK, tn), lambda i, j: (0, j)),
            pl.BlockSpec((tm, LANES), lambda i, j: (i, rope_block)),
            tab_spec, tab_spec,
        ],
        out_specs=(pl.BlockSpec((slots, tm, B_SLOT), lambda i, j: (j, i, 0)),
                   pl.BlockSpec((slots, tm, B_VX), lambda i, j: (j, i, 0))),
        compiler_params=_cparams("parallel", "parallel"),
        name="b_kv",
    )(c, gain.reshape(1, B_KV_RANK), w_k, w_v, c, *tabs_b)


def _attn_b_kernel(q_ref, k_ref, v_ref, o_ref, s0_ref, s1_ref, m_ref, acc_ref, *, tq):
    qi = pl.program_id(2)
    q = q_ref[0]
    m_ref[...] = jnp.full_like(m_ref, NEG_INF)
    acc_ref[...] = jnp.zeros_like(acc_ref)

    def scores(ki):
        return _dot_t(q, k_ref[0, pl.ds(pl.multiple_of(ki * tq, tq), tq), :])

    def fold(s_ref, ki, diagonal):
        s = s_ref[...]
        if diagonal:
            row = lax.broadcasted_iota(jnp.int32, (tq, tq), 0)
            col = lax.broadcasted_iota(jnp.int32, (tq, tq), 1)
            s = jnp.where(row >= col, s, NEG_INF)
        m_prev = m_ref[...]
        m_new = jnp.maximum(m_prev, jnp.max(s, axis=-1, keepdims=True))
        alpha = jnp.exp(m_prev - m_new)
        p = jnp.exp((s - jnp.concatenate([m_new] * (tq // LANES), axis=1)).astype(BF16))
        v = v_ref[0, pl.ds(pl.multiple_of(ki * tq, tq), tq), :]
        acc_ref[...] = jnp.concatenate([alpha] * (B_VX // LANES), axis=1) * acc_ref[...] + _dot(p, v)
        m_ref[...] = m_new

    pairs = qi // 2
    s0_ref[...] = scores(0)

    def body(t, carry):
        s1_ref[...] = scores(2 * t + 1)
        fold(s0_ref, 2 * t, False)
        s0_ref[...] = scores(2 * t + 2)
        fold(s1_ref, 2 * t + 1, False)
        return carry

    lax.fori_loop(0, pairs, body, 0)

    @pl.when(qi % 2 == 1)
    def _():
        s1_ref[...] = scores(qi)
        fold(s0_ref, qi - 1, False)
        fold(s1_ref, qi, True)

    @pl.when(qi % 2 == 0)
    def _():
        fold(s0_ref, qi, True)

    acc = acc_ref[...]
    l = jnp.broadcast_to(acc[:, B_V_DIM:B_V_DIM + 1], (tq, B_V_DIM))
    o_ref[...] = (acc[:, :B_V_DIM] / l).astype(BF16)


def _attn_b(q3, k3, v3):
    tq = 512
    nq = SEQ // tq
    return pl.pallas_call(
        functools.partial(_attn_b_kernel, tq=tq),
        out_shape=jax.ShapeDtypeStruct((TOKENS, B_HEADS * B_V_DIM), BF16),
        grid=(BATCH, B_HEADS, nq),
        in_specs=[
            pl.BlockSpec((1, tq, B_SLOT), lambda b, h, qi: (h, b * nq + qi, 0)),
            pl.BlockSpec((1, SEQ, B_SLOT), lambda b, h, qi: (h, b, 0)),
            pl.BlockSpec((1, SEQ, B_VX), lambda b, h, qi: (h, b, 0)),
        ],
        out_specs=pl.BlockSpec((tq, B_V_DIM), lambda b, h, qi: (b * nq + qi, h)),
        scratch_shapes=[pltpu.VMEM((tq, tq), F32), pltpu.VMEM((tq, tq), F32),
                        pltpu.VMEM((tq, LANES), F32), pltpu.VMEM((tq, B_VX), F32)],
        compiler_params=_cparams("parallel", "parallel", "arbitrary"),
        name="attn_b",
    )(q3, k3, v3)


def _mixer_b(h, gain, w_in, q_norm, w_q, kv_norm, w_k, w_v, w_o, tabs_b):
    c = _b_in(h, gain, w_in)
    q3 = _b_q(c, q_norm, w_q, tabs_b)
    k3, v3 = _b_kv(c, kv_norm, w_k, w_v, tabs_b)
    return _out_proj(_attn_b(q3, k3, v3), w_o, h)


def _c_in_kernel(x_ref, g_ref, wb_ref, wc_ref, wu_ref, cw_ref, o_ref, xn_ref, carry_ref, *, tm):
    i = pl.program_id(0)
    j = pl.program_id(1)

    @pl.when(j == 0)
    def _():
        xn_ref[...] = _rms(x_ref[...], g_ref[...]).astype(BF16)

    xn = xn_ref[...]
    b_gate = _dot(xn, wb_ref[...])
    cu = _dot(xn, wc_ref[...]) * _dot(xn, wu_ref[...])
    w0, w1, w2 = cw_ref[0:1, :], cw_ref[1:2, :], cw_ref[2:3, :]
    z = w2 * cu + w1 * pltpu.roll(cu, 1, 0) + w0 * pltpu.roll(cu, 2, 0)
    o_ref[...] = (b_gate * z).astype(BF16)

    first = (i * tm) % SEQ == 0
    tail = jnp.where(first, 0.0, carry_ref[j])
    both = jnp.concatenate([tail, cu[0:8, :]], axis=0)
    z8 = w2 * both[8:16] + w1 * both[7:15] + w0 * both[6:14]
    o_ref[0:8, :] = (b_gate[0:8, :] * z8).astype(BF16)
    carry_ref[j] = cu[tm - 8:tm, :]


def _c_in(h, gain, w_in, conv_w):
    tm, tn = 1024, 512
    nj = D_MODEL // tn
    return pl.pallas_call(
        functools.partial(_c_in_kernel, tm=tm),
        out_shape=jax.ShapeDtypeStruct((TOKENS, D_MODEL), BF16),
        grid=(TOKENS // tm, nj),
        in_specs=[
            pl.BlockSpec((tm, D_MODEL), lambda i, j: (i, 0)),
            pl.BlockSpec((1, D_MODEL), lambda i, j: (0, 0)),
            pl.BlockSpec((D_MODEL, tn), lambda i, j: (0, j)),
            pl.BlockSpec((D_MODEL, tn), lambda i, j: (0, j + nj)),
            pl.BlockSpec((D_MODEL, tn), lambda i, j: (0, j + 2 * nj)),
            pl.BlockSpec((C_CONV, tn), lambda i, j: (0, j)),
        ],
        out_specs=pl.BlockSpec((tm, tn), lambda i, j: (i, j)),
        scratch_shapes=[pltpu.VMEM((tm, D_MODEL), BF16), pltpu.VMEM((nj, 8, tn), F32)],
        compiler_params=_cparams("arbitrary", "arbitrary"),
        name="c_in",
    )(h, gain.reshape(1, D_MODEL), w_in, w_in, w_in, conv_w)


def _mixer_c(h, gain, w_in, conv_w, w_out):
    return _out_proj(_c_in(h, gain, w_in, conv_w), w_out, h)


def kernel(x, positions, l0_ffn1_norm, l0_ffn1_w13, l0_ffn1_w2, l0_mix_norm, l0_a_w_qkv, l0_a_w_o, l0_ffn2_norm, l0_ffn2_w13, l0_ffn2_w2, l1_ffn1_norm, l1_ffn1_w13, l1_ffn1_w2, l1_mix_norm, l1_b_w_in, l1_b_q_norm, l1_b_w_uq, l1_b_kv_norm, l1_b_w_ukv, l1_b_w_o, l1_ffn2_norm, l1_ffn2_w13, l1_ffn2_w2, l2_ffn1_norm, l2_ffn1_w13, l2_ffn1_w2, l2_mix_norm, l2_c_w_in, l2_c_conv_w, l2_c_w_out, l2_ffn2_norm, l2_ffn2_w13, l2_ffn2_w2, l3_ffn1_norm, l3_ffn1_w13, l3_ffn1_w2, l3_mix_norm, l3_a_w_qkv, l3_a_w_o, l3_ffn2_norm, l3_ffn2_w13, l3_ffn2_w2, final_norm):
    bf = lambda w: w.astype(BF16)
    pos_tiles = positions.reshape(TOKENS // A_TILE, A_TILE)
    pos_a = [pos_tiles.reshape(-1, A_TILE // d, d).swapaxes(1, 2).reshape(TOKENS) for _, d in A_PATTERNS]
    tabs_a = _rope_tables(jnp.concatenate(pos_a + [jnp.zeros((A_TILE,), positions.dtype)]), A_ROT_HALF)
    tabs_b = _rope_tables(positions.reshape(TOKENS), B_ROT_HALF)

    def rope_group(w):
        return _rope_lanes(jnp.pad(w, [(0, 0)] * (w.ndim - 1) + [(0, LANES - B_ROPE_DIM)]), B_ROT_HALF)

    w_uq = l1_b_w_uq.reshape(B_Q_RANK, B_HEADS, B_NOPE_DIM + B_ROPE_DIM)
    w_q = jnp.concatenate([w_uq[..., :B_NOPE_DIM], rope_group(w_uq[..., B_NOPE_DIM:])], axis=-1)
    w_q = bf(w_q.reshape(B_Q_RANK, B_HEADS * B_SLOT))
    w_ukv = l1_b_w_ukv.reshape(B_KV_RANK, B_HEADS, B_NOPE_DIM + B_V_DIM)
    w_k = bf(w_ukv[:, :, :B_NOPE_DIM].reshape(B_KV_RANK, B_HEADS * B_NOPE_DIM))
    w_v = bf(w_ukv[:, :, B_NOPE_DIM:].reshape(B_KV_RANK, B_HEADS * B_V_DIM))
    n_c = B_Q_RANK + B_KV_RANK
    w_b_in = bf(jnp.concatenate([l1_b_w_in[:, :n_c], rope_group(l1_b_w_in[:, n_c:])], axis=-1))

    h = x.reshape(TOKENS, D_MODEL)

    h = _ffn(h, l0_ffn1_norm, bf(l0_ffn1_w13), bf(l0_ffn1_w2))
    h = _mixer_a(h, l0_mix_norm, _qkv_w(l0_a_w_qkv), bf(l0_a_w_o), tabs_a)
    h = _ffn(h, l0_ffn2_norm, bf(l0_ffn2_w13), bf(l0_ffn2_w2))

    h = _ffn(h, l1_ffn1_norm, bf(l1_ffn1_w13), bf(l1_ffn1_w2))
    h = _mixer_b(h, l1_mix_norm, w_b_in, l1_b_q_norm, w_q, l1_b_kv_norm, w_k, w_v, bf(l1_b_w_o), tabs_b)
    h = _ffn(h, l1_ffn2_norm, bf(l1_ffn2_w13), bf(l1_ffn2_w2))

    h = _ffn(h, l2_ffn1_norm, bf(l2_ffn1_w13), bf(l2_ffn1_w2))
    h = _mixer_c(h, l2_mix_norm, bf(l2_c_w_in), l2_c_conv_w, bf(l2_c_w_out))
    h = _ffn(h, l2_ffn2_norm, bf(l2_ffn2_w13), bf(l2_ffn2_w2))

    h = _ffn(h, l3_ffn1_norm, bf(l3_ffn1_w13), bf(l3_ffn1_w2))
    h = _mixer_a(h, l3_mix_norm, _qkv_w(l3_a_w_qkv), bf(l3_a_w_o), tabs_a)
    h = _ffn(h, l3_ffn2_norm, bf(l3_ffn2_w13), bf(l3_ffn2_w2), final_gain=final_norm)

    return h.reshape(BATCH, SEQ, D_MODEL)
```

```python
import functools

import jax
import jax.numpy as jnp
from jax import lax
from jax.experimental import pallas as pl
from jax.experimental.pallas import tpu as pltpu

F32 = jnp.float32
BF16 = jnp.bfloat16

D_MODEL = 2048
BATCH = 2
SEQ = 8192
TOKENS = BATCH * SEQ
D_FF = 5632
HALF_STEP = 0.5
NORM_EPS = 1e-6
NEG_INF = -1e30
ROPE_THETA = 500000.0

A_HEADS = 16
A_HEAD_DIM = 128
A_ROT_HALF = A_HEAD_DIM // 8
A_PATTERNS = ((128, 1), (512, 4), (2048, 16))
A_BLOCK = 128
A_TILE = 2048

B_HEADS = 16
B_Q_RANK = 1536
B_KV_RANK = 512
B_NOPE_DIM = 128
B_ROPE_DIM = 64
B_ROT_HALF = B_ROPE_DIM // 2
B_V_DIM = 128
B_SLOT = 256
B_VX = 256
B_IN_COLS = 2176

C_CONV = 3

LANES = 128
MXU_COLS = 256
ROPE_PAIR = LANES // 2
VMEM_LIMIT = 56 * 1024 * 1024


def _cparams(*sem):
    return pltpu.CompilerParams(dimension_semantics=sem, vmem_limit_bytes=VMEM_LIMIT)


def _rms(x, gain):
    ms = jnp.mean(x * x, axis=-1, keepdims=True)
    return x * lax.rsqrt(ms + NORM_EPS) * gain


def _dot(a, b):
    return jnp.dot(a, b, preferred_element_type=F32)


def _dot_t(a, b):
    return lax.dot_general(a, b, (((1,), (1,)), ((), ())), preferred_element_type=F32)


def _rope128(y, cos, sin):
    return y * cos + pltpu.roll(y, ROPE_PAIR, 1) * sin


def _rope_lanes(w, half):
    return jnp.concatenate([w[..., :half], w[..., 2 * half:ROPE_PAIR + half], w[..., half:2 * half],
                            w[..., ROPE_PAIR + half:]], axis=-1)


def _rope_tab_kernel(pos_ref, inv_ref, c_ref, s_ref):
    ang = pos_ref[...].astype(F32) * inv_ref[...]
    lane = lax.broadcasted_iota(jnp.int32, (1, LANES), 1)
    sn = jnp.sin(ang)
    c_ref[...] = jnp.cos(ang)
    s_ref[...] = jnp.where(lane < ROPE_PAIR, -sn, sn)


def _rope_tables(pos_flat, half):
    inv = ROPE_THETA ** (-jnp.arange(half, dtype=F32) / half)
    gap = jnp.zeros((ROPE_PAIR - half,), F32)
    inv = jnp.concatenate([inv, gap, inv, gap]).reshape(1, LANES)
    n = pos_flat.shape[0]
    rows = 2048
    tab = jax.ShapeDtypeStruct((n, LANES), F32)
    row_spec = pl.BlockSpec((rows, LANES), lambda i: (i, 0))
    return pl.pallas_call(
        _rope_tab_kernel,
        out_shape=(tab,) * 2,
        grid=(n // rows,),
        in_specs=[pl.BlockSpec((rows, 1), lambda i: (i, 0)), pl.BlockSpec((1, LANES), lambda i: (0, 0))],
        out_specs=(row_spec,) * 2,
        compiler_params=_cparams("parallel"),
        name="rope_tables",
    )(pos_flat.reshape(n, 1), inv)


def _ffn_kernel(x_ref, g_ref, w1_ref, w3_ref, w2_ref, fg_ref, *refs, final, cast_next):
    j = pl.program_id(1)
    if cast_next:
        n13_ref, n2_ref, o_ref, c13_ref, c2_ref, xn_ref, acc_ref = refs
    else:
        o_ref, xn_ref, acc_ref = refs

    @pl.when(j == 0)
    def _():
        xn_ref[...] = _rms(x_ref[...], g_ref[...]).astype(BF16)
        acc_ref[...] = jnp.zeros_like(acc_ref)

    xn = xn_ref[...]
    gate = _dot(xn, w1_ref[...])
    up = _dot(xn, w3_ref[...])
    act = (gate * (1.0 / (1.0 + jnp.exp(-gate))) * up).astype(BF16)
    acc_ref[...] += _dot(act, w2_ref[...])
    if cast_next:
        c13_ref[...] = n13_ref[...].astype(BF16)
        c2_ref[...] = n2_ref[...].astype(BF16)

    @pl.when(j == pl.num_programs(1) - 1)
    def _():
        y = x_ref[...] + HALF_STEP * acc_ref[...]
        if final:
            y = _rms(y, fg_ref[...])
        o_ref[...] = y


def _ffn(h, gain, w13, w2, final_gain=None, next_f32=None):
    tm, tf = 512, 512
    nj = D_FF // tf
    ni = TOKENS // tm
    final = final_gain is not None
    fg = final_gain if final else gain
    in_specs = [
        pl.BlockSpec((tm, D_MODEL), lambda i, j: (i, 0)),
        pl.BlockSpec((1, D_MODEL), lambda i, j: (0, 0)),
        pl.BlockSpec((D_MODEL, tf), lambda i, j: (0, j)),
        pl.BlockSpec((D_MODEL, tf), lambda i, j: (0, j + nj)),
        pl.BlockSpec((tf, D_MODEL), lambda i, j: (j, 0)),
        pl.BlockSpec((1, D_MODEL), lambda i, j: (0, 0)),
    ]
    out_shape = [jax.ShapeDtypeStruct((TOKENS, D_MODEL), F32)]
    out_specs = [pl.BlockSpec((tm, D_MODEL), lambda i, j: (i, 0))]
    args = [h, gain.reshape(1, D_MODEL), w13, w13, w2, fg.reshape(1, D_MODEL)]
    if next_f32 is not None:
        slab13 = pl.BlockSpec((D_MODEL // ni, 2 * D_FF // nj), lambda i, j: (i, j))
        slab2 = pl.BlockSpec((D_FF // (ni * nj), D_MODEL), lambda i, j: (i * nj + j, 0))
        in_specs += [slab13, slab2]
        out_specs += [slab13, slab2]
        out_shape += [jax.ShapeDtypeStruct(w.shape, BF16) for w in next_f32]
        args += list(next_f32)
    outs = pl.pallas_call(
        functools.partial(_ffn_kernel, final=final, cast_next=next_f32 is not None),
        out_shape=tuple(out_shape),
        grid=(ni, nj),
        in_specs=in_specs,
        out_specs=tuple(out_specs),
        scratch_shapes=[pltpu.VMEM((tm, D_MODEL), BF16), pltpu.VMEM((tm, D_MODEL), F32)],
        compiler_params=_cparams("parallel", "arbitrary"),
        name="ffn",
    )(*args)
    return outs if next_f32 is not None else outs[0]


def _out_proj_kernel(x_ref, w_ref, r_ref, o_ref):
    o_ref[...] = r_ref[...] + _dot(x_ref[...], w_ref[...])


def _out_proj(x, w, res):
    k = x.shape[1]
    tm = 512
    return pl.pallas_call(
        _out_proj_kernel,
        out_shape=jax.ShapeDtypeStruct((TOKENS, D_MODEL), F32),
        grid=(TOKENS // tm,),
        in_specs=[
            pl.BlockSpec((tm, k), lambda i: (i, 0)),
            pl.BlockSpec((k, D_MODEL), lambda i: (0, 0)),
            pl.BlockSpec((tm, D_MODEL), lambda i: (i, 0)),
        ],
        out_specs=pl.BlockSpec((tm, D_MODEL), lambda i: (i, 0)),
        compiler_params=_cparams("parallel"),
        name="out_proj",
    )(x, w, res)


def _norm_perm_a_kernel(x_ref, g_ref, *refs, tm):
    o_refs, xs_ref = refs[:-1], refs[-1]
    xn = _rms(x_ref[...], g_ref[...])
    nchunk = D_MODEL // LANES
    for k in range(nchunk):
        xs_ref[k] = xn[:, k * LANES:(k + 1) * LANES]
    for (_, d), o_ref in zip(A_PATTERNS, o_refs):
        if d == 1:
            o_ref[...] = xn.astype(BF16)
        else:
            for c in range(d):
                for k in range(nchunk):
                    o_ref[0, c, 0, :, k * LANES:(k + 1) * LANES] = (
                        xs_ref[k, pl.ds(c, tm // d, stride=d), :].astype(BF16))


def _norm_perm_a(h, gain):
    tm = 512
    per_tile = A_TILE // tm
    shapes, specs = [], []
    for _, d in A_PATTERNS:
        if d == 1:
            shapes.append(jax.ShapeDtypeStruct((TOKENS, D_MODEL), BF16))
            specs.append(pl.BlockSpec((tm, D_MODEL), lambda i: (i, 0)))
        else:
            shapes.append(jax.ShapeDtypeStruct((TOKENS // A_TILE, d, per_tile, tm // d, D_MODEL), BF16))
            specs.append(pl.BlockSpec((1, d, 1, tm // d, D_MODEL),
                                      lambda i: (i // per_tile, 0, i % per_tile, 0, 0)))
    outs = pl.pallas_call(
        functools.partial(_norm_perm_a_kernel, tm=tm),
        out_shape=tuple(shapes),
        grid=(TOKENS // tm,),
        in_specs=[pl.BlockSpec((tm, D_MODEL), lambda i: (i, 0)), pl.BlockSpec((1, D_MODEL), lambda i: (0, 0))],
        out_specs=tuple(specs),
        scratch_shapes=[pltpu.VMEM((D_MODEL // LANES, tm, LANES), F32)],
        compiler_params=_cparams("parallel"),
        name="norm_perm_a",
    )(h, gain.reshape(1, D_MODEL))
    return [o.reshape(TOKENS, D_MODEL) for o in outs]


def _qkv_w_kernel(w_ref, o_ref):
    part = pl.program_id(1) % 3

    @pl.when(part == 2)
    def _():
        o_ref[...] = w_ref[...].astype(BF16)

    @pl.when(part != 2)
    def _():
        lane = lax.broadcasted_iota(jnp.int32, (1, LANES), 1)
        stay = (lane < A_ROT_HALF) | (lane >= ROPE_PAIR + A_ROT_HALF)
        for hh in range(A_HEADS):
            sl = slice(hh * A_HEAD_DIM, (hh + 1) * A_HEAD_DIM)
            y = w_ref[:, sl]
            moved = jnp.where(lane < ROPE_PAIR, pltpu.roll(y, LANES - A_ROT_HALF, 1),
                              pltpu.roll(y, ROPE_PAIR - A_ROT_HALF, 1))
            o_ref[:, sl] = jnp.where(stay, y, moved).astype(BF16)


def _qkv_w(w_qkv):
    rows = 256
    hd = A_HEADS * A_HEAD_DIM
    spec = pl.BlockSpec((rows, hd), lambda i, j: (i, j))
    return pl.pallas_call(
        _qkv_w_kernel,
        out_shape=jax.ShapeDtypeStruct(w_qkv.shape, BF16),
        grid=(D_MODEL // rows, w_qkv.shape[1] // hd),
        in_specs=[spec],
        out_specs=spec,
        compiler_params=_cparams("parallel", "parallel"),
        name="qkv_w",
    )(w_qkv)


def _qkv_a_kernel(x_ref, w_ref, cos_ref, sin_ref, o_ref, *, tn):
    part = pl.program_id(1) // ((A_HEADS * A_HEAD_DIM) // tn)
    scale = jnp.where(part == 0, A_HEAD_DIM ** -0.5, 1.0).astype(F32)
    half = x_ref.shape[0] // 2
    for cc in range(tn // MXU_COLS):
        for rows in (slice(0, half), slice(half, 2 * half)):
            y = _dot(x_ref[rows, :], w_ref[:, cc * MXU_COLS:(cc + 1) * MXU_COLS])
            for hh in range(MXU_COLS // LANES):
                sl = slice(cc * MXU_COLS + hh * LANES, cc * MXU_COLS + (hh + 1) * LANES)
                y_h = y[:, hh * LANES:(hh + 1) * LANES]
                o_ref[rows, sl] = (_rope128(y_h, cos_ref[rows, :], sin_ref[rows, :]) * scale).astype(BF16)


def _qkv_a(xn, w_qkv, tabs, group):
    tm, tn = 1024, 1024
    cols = 3 * A_HEADS * A_HEAD_DIM
    nj = cols // tn
    ni = TOKENS // tm
    v_first = 2 * (A_HEADS * A_HEAD_DIM) // tn
    tab_spec = pl.BlockSpec((tm, LANES),
                            lambda i, j: (jnp.where(j >= v_first, len(A_PATTERNS) * ni, group * ni + i), 0))
    return pl.pallas_call(
        functools.partial(_qkv_a_kernel, tn=tn),
        out_shape=jax.ShapeDtypeStruct((TOKENS, cols), BF16),
        grid=(ni, nj),
        in_specs=[
            pl.BlockSpec((tm, D_MODEL), lambda i, j: (i, 0)),
            pl.BlockSpec((D_MODEL, tn), lambda i, j: (0, group * nj + j)),
            tab_spec, tab_spec,
        ],
        out_specs=pl.BlockSpec((tm, tn), lambda i, j: (i, j)),
        compiler_params=_cparams("parallel", "parallel"),
        name="qkv_a",
    )(xn, w_qkv, *tabs)


def _attn_a_kernel(q_ref, kc_ref, kp_ref, vc_ref, vp_ref, o_ref, lse_ref, os_ref, *, d, heads, per_trip):
    rows = A_TILE // d
    nblk = rows // A_BLOCK
    first_tile = pl.program_id(1) == 0
    r = lax.broadcasted_iota(jnp.int32, (A_BLOCK, 2 * A_BLOCK), 0)
    kk = lax.broadcasted_iota(jnp.int32, (A_BLOCK, 2 * A_BLOCK), 1)
    band = (kk >= r) & (kk <= r + A_BLOCK)
    lane = lax.broadcasted_iota(jnp.int32, (A_BLOCK, LANES), 1)
    hs = [slice(h * A_HEAD_DIM, (h + 1) * A_HEAD_DIM) for h in range(heads)]

    def trip(t, carry):
        loaded = []
        for i in range(per_trip):
            u = t * per_trip + i
            c = u // nblk
            blk = u % nblk
            start = pl.multiple_of(blk * A_BLOCK, A_BLOCK)
            before = pl.multiple_of(jnp.maximum(start - A_BLOCK, 0), A_BLOCK)
            q = q_ref[0, c, pl.ds(start, A_BLOCK), :]
            k = jnp.concatenate([jnp.where(blk == 0, kp_ref[0, c], kc_ref[0, c, pl.ds(before, A_BLOCK), :]),
                                 kc_ref[0, c, pl.ds(start, A_BLOCK), :]], axis=0)
            v = jnp.concatenate([jnp.where(blk == 0, vp_ref[0, c], vc_ref[0, c, pl.ds(before, A_BLOCK), :]),
                                 vc_ref[0, c, pl.ds(start, A_BLOCK), :]], axis=0)
            no_prev = jnp.logical_and(first_tile, blk == 0)
            mask = jnp.logical_and(band, jnp.logical_or(kk >= A_BLOCK, jnp.logical_not(no_prev)))
            out_rows = (pl.ds(blk * (A_BLOCK * d) + c, A_BLOCK, stride=d) if d > 1
                        else pl.ds(start, A_BLOCK))
            loaded.append(([_dot_t(q[:, sl], k[:, sl]) for sl in hs], v, mask, out_rows))
        for scores, v, mask, out_rows in loaded:
            lse_tile = jnp.zeros((A_BLOCK, LANES), F32)
            for h, sl in enumerate(hs):
                s = jnp.where(mask, scores[h], NEG_INF)
                m = jnp.max(s, axis=-1, keepdims=True)
                p = jnp.exp(s - m)
                l = jnp.sum(p, axis=-1, keepdims=True)
                os_ref[h, out_rows, :] = _dot(p.astype(BF16), v[:, sl]) / l
                lse_tile = jnp.where(lane == h, m + jnp.log(l), lse_tile)
            lse_ref[out_rows, :] = lse_tile
        return carry

    lax.fori_loop(0, d * nblk // per_trip, trip, 0)
    o_ref[...] = os_ref[...].astype(BF16)


def _attn_a(qkv, window, dilation):
    assert window // dilation == A_BLOCK, "band width must equal the query block"
    d = dilation
    heads = 4
    hw = heads * A_HEAD_DIM
    ngrp = A_HEADS // heads
    rows = A_TILE // d
    tiles = SEQ // A_TILE
    qkv_v = qkv.reshape(BATCH, tiles, d, rows, 3 * A_HEADS * A_HEAD_DIM)
    last_blk = rows // A_BLOCK - 1

    def cur(part):
        return pl.BlockSpec((None, 1, d, rows, hw), lambda b, n, g: (b, n, 0, 0, part * ngrp + g))

    def prev(part):
        return pl.BlockSpec((None, 1, d, A_BLOCK, hw),
                            lambda b, n, g: (b, jnp.maximum(n - 1, 0), 0, last_blk, part * ngrp + g))

    return pl.pallas_call(
        functools.partial(_attn_a_kernel, d=d, heads=heads, per_trip=4),
        out_shape=(jax.ShapeDtypeStruct((A_HEADS, TOKENS, A_HEAD_DIM), BF16),
                   jax.ShapeDtypeStruct((TOKENS, ngrp * LANES), F32)),
        grid=(BATCH, tiles, ngrp),
        in_specs=[cur(0), cur(1), prev(1), cur(2), prev(2)],
        out_specs=(pl.BlockSpec((heads, A_TILE, A_HEAD_DIM), lambda b, n, g: (g, b * tiles + n, 0)),
                   pl.BlockSpec((A_TILE, LANES), lambda b, n, g: (b * tiles + n, g))),
        scratch_shapes=[pltpu.VMEM((heads, A_TILE, A_HEAD_DIM), F32)],
        compiler_params=_cparams("parallel", "parallel", "parallel"),
        name="attn_a_d%d" % d,
    )(qkv_v, qkv_v, qkv_v, qkv_v, qkv_v)


def _comb_proj_a_kernel(o0, o1, o2, l0, l1, l2, w_ref, r_ref, out_ref, comb_ref, *, heads):
    s = pl.program_id(0)
    slot = s % 2

    @pl.when(s == 0)
    def _():
        comb_ref[1] = jnp.zeros(comb_ref.shape[1:], BF16)

    out_ref[...] = r_ref[...] + _dot(comb_ref[1 - slot], w_ref[...])
    a = [l0[...], l1[...], l2[...]]
    m = jnp.maximum(jnp.maximum(a[0], a[1]), a[2])
    e = [jnp.exp(t - m) for t in a]
    den = e[0] + e[1] + e[2]
    w = [t / den for t in e]
    rows = out_ref.shape[0]
    for h in range(A_HEADS):
        col = (h // heads) * LANES + h % heads
        acc = jnp.zeros((rows, A_HEAD_DIM), F32)
        for wg, og in zip(w, (o0, o1, o2)):
            acc = acc + jnp.broadcast_to(wg[:, col:col + 1], (rows, A_HEAD_DIM)) * og[h]
        comb_ref[slot, :, h * A_HEAD_DIM:(h + 1) * A_HEAD_DIM] = acc.astype(BF16)


def _comb_proj_a(outs, lses, w_o, res):
    tm = 256
    nt = TOKENS // tm
    hd = A_HEADS * A_HEAD_DIM
    lw = lses[0].shape[1]
    o_spec = pl.BlockSpec((A_HEADS, tm, A_HEAD_DIM), lambda s: (0, jnp.minimum(s, nt - 1), 0))
    l_spec = pl.BlockSpec((tm, lw), lambda s: (jnp.minimum(s, nt - 1), 0))
    row_spec = pl.BlockSpec((tm, D_MODEL), lambda s: (jnp.maximum(s - 1, 0), 0))
    return pl.pallas_call(
        functools.partial(_comb_proj_a_kernel, heads=A_HEADS * LANES // lw),
        out_shape=jax.ShapeDtypeStruct((TOKENS, D_MODEL), F32),
        grid=(nt + 1,),
        in_specs=[o_spec] * 3 + [l_spec] * 3 + [pl.BlockSpec((hd, D_MODEL), lambda s: (0, 0)), row_spec],
        out_specs=row_spec,
        scratch_shapes=[pltpu.VMEM((2, tm, hd), BF16)],
        compiler_params=_cparams("arbitrary"),
        name="comb_proj_a",
    )(*outs, *lses, w_o, res)


def _mixer_a(h, gain, w_qkv, w_o, tabs_a):
    outs, lses = [], []
    for g, ((window, dilation), xn) in enumerate(zip(A_PATTERNS, _norm_perm_a(h, gain))):
        o, lse = _attn_a(_qkv_a(xn, w_qkv, tabs_a, g), window, dilation)
        outs.append(o)
        lses.append(lse)
    return _comb_proj_a(outs, lses, w_o, h)


def _b_in_kernel(x_ref, g_ref, w_ref, o_ref):
    o_ref[...] = _dot(_rms(x_ref[...], g_ref[...]).astype(BF16), w_ref[...])


def _b_in(h, gain, w_in):
    tm = 512
    return pl.pallas_call(
        _b_in_kernel,
        out_shape=jax.ShapeDtypeStruct((TOKENS, B_IN_COLS), F32),
        grid=(TOKENS // tm,),
        in_specs=[
            pl.BlockSpec((tm, D_MODEL), lambda i: (i, 0)),
            pl.BlockSpec((1, D_MODEL), lambda i: (0, 0)),
            pl.BlockSpec((D_MODEL, B_IN_COLS), lambda i: (0, 0)),
        ],
        out_specs=pl.BlockSpec((tm, B_IN_COLS), lambda i: (i, 0)),
        compiler_params=_cparams("parallel"),
        name="b_in",
    )(h, gain.reshape(1, D_MODEL), w_in)


def _b_q_kernel(c_ref, g_ref, w_ref, cos_ref, sin_ref, o_ref, xn_ref, *, slots):
    @pl.when(pl.program_id(1) == 0)
    def _():
        xn_ref[...] = _rms(c_ref[...], g_ref[...]).astype(BF16)

    y = _dot(xn_ref[...], w_ref[...])
    scale = (B_NOPE_DIM + B_ROPE_DIM) ** -0.5
    cos, sin = cos_ref[...], sin_ref[...]
    for s in range(slots):
        nope = y[:, s * B_SLOT:s * B_SLOT + LANES]
        rot = _rope128(y[:, s * B_SLOT + LANES:(s + 1) * B_SLOT], cos, sin)
        o_ref[s, :, :LANES] = (nope * scale).astype(BF16)
        o_ref[s, :, LANES:] = (rot * scale).astype(BF16)


def _b_q(c, gain, w_q, tabs_b):
    tm, slots = 1024, 4
    tn = slots * B_SLOT
    tab_spec = pl.BlockSpec((tm, LANES), lambda i, j: (i, 0))
    return pl.pallas_call(
        functools.partial(_b_q_kernel, slots=slots),
        out_shape=jax.ShapeDtypeStruct((B_HEADS, TOKENS, B_SLOT), BF16),
        grid=(TOKENS // tm, B_HEADS // slots),
        in_specs=[
            pl.BlockSpec((tm, B_Q_RANK), lambda i, j: (i, 0)),
            pl.BlockSpec((1, B_Q_RANK), lambda i, j: (0, 0)),
            pl.BlockSpec((B_Q_RANK, tn), lambda i, j: (0, j)),
            tab_spec, tab_spec,
        ],
        out_specs=pl.BlockSpec((slots, tm, B_SLOT), lambda i, j: (j, i, 0)),
        scratch_shapes=[pltpu.VMEM((tm, B_Q_RANK), BF16)],
        compiler_params=_cparams("parallel", "arbitrary"),
        name="b_q",
    )(c, gain.reshape(1, B_Q_RANK), w_q, *tabs_b)


def _b_kv_kernel(c_ref, g_ref, wk_ref, wv_ref, cr_ref, cos_ref, sin_ref, k_ref, v_ref, *, slots):
    xn = _rms(c_ref[...], g_ref[...]).astype(BF16)
    yk = _dot(xn, wk_ref[...])
    yv = _dot(xn, wv_ref[...])
    kr = _rope128(cr_ref[...], cos_ref[...], sin_ref[...]).astype(BF16)
    lane = lax.broadcasted_iota(jnp.int32, (c_ref.shape[0], B_VX - B_V_DIM), 1)
    ones_col = jnp.where(lane == 0, 1.0, 0.0).astype(BF16)
    for s in range(slots):
        sl = slice(s * LANES, (s + 1) * LANES)
        k_ref[s, :, :LANES] = yk[:, sl].astype(BF16)
        k_ref[s, :, LANES:] = kr
        v_ref[s, :, :B_V_DIM] = yv[:, sl].astype(BF16)
        v_ref[s, :, B_V_DIM:] = ones_col


def _b_kv(c, gain, w_k, w_v, tabs_b):
    tm, slots = 1024, 4
    tn = slots * LANES
    kv_block = B_Q_RANK // B_KV_RANK
    rope_block = (B_Q_RANK + B_KV_RANK) // LANES
    tab_spec = pl.BlockSpec((tm, LANES), lambda i, j: (i, 0))
    return pl.pallas_call(
        functools.partial(_b_kv_kernel, slots=slots),
        out_shape=(jax.ShapeDtypeStruct((B_HEADS, TOKENS, B_SLOT), BF16),
                   jax.ShapeDtypeStruct((B_HEADS, TOKENS, B_VX), BF16)),
        grid=(TOKENS // tm, B_HEADS // slots),
        in_specs=[
            pl.BlockSpec((tm, B_KV_RANK), lambda i, j: (i, kv_block)),
            pl.BlockSpec((1, B_KV_RANK), lambda i, j: (0, 0)),
            pl.BlockSpec((B_KV_RANK, tn), lambda i, j: (0, j)),
            pl.BlockSpec((B_KV_RANK, tn), lambda i, j: (0, j)),
            pl.BlockSpec((tm, LANES), lambda i, j: (i, rope_block)),
            tab_spec, tab_spec,
        ],
        out_specs=(pl.BlockSpec((slots, tm, B_SLOT), lambda i, j: (j, i, 0)),
                   pl.BlockSpec((slots, tm, B_VX), lambda i, j: (j, i, 0))),
        compiler_params=_cparams("parallel", "parallel"),
        name="b_kv",
    )(c, gain.reshape(1, B_KV_RANK), w_k, w_v, c, *tabs_b)


def _attn_b_kernel(q_ref, k_ref, v_ref, o_ref, s0_ref, s1_ref, m_ref, acc_ref, *, tq):
    qi = pl.program_id(2)
    q = q_ref[0]
    m_ref[...] = jnp.full_like(m_ref, NEG_INF)
    acc_ref[...] = jnp.zeros_like(acc_ref)

    def scores(ki):
        return _dot_t(q, k_ref[0, pl.ds(pl.multiple_of(ki * tq, tq), tq), :])

    def fold(s_ref, ki, diagonal):
        s = s_ref[...]
        if diagonal:
            row = lax.broadcasted_iota(jnp.int32, (tq, tq), 0)
            col = lax.broadcasted_iota(jnp.int32, (tq, tq), 1)
            s = jnp.where(row >= col, s, NEG_INF)
        m_prev = m_ref[...]
        m_new = jnp.maximum(m_prev, jnp.max(s, axis=-1, keepdims=True))
        alpha = jnp.exp(m_prev - m_new)
        p = jnp.exp((s - jnp.concatenate([m_new] * (tq // LANES), axis=1)).astype(BF16))
        v = v_ref[0, pl.ds(pl.multiple_of(ki * tq, tq), tq), :]
        acc_ref[...] = jnp.concatenate([alpha] * (B_VX // LANES), axis=1) * acc_ref[...] + _dot(p, v)
        m_ref[...] = m_new

    pairs = qi // 2
    s0_ref[...] = scores(0)

    def body(t, carry):
        s1_ref[...] = scores(2 * t + 1)
        fold(s0_ref, 2 * t, False)
        s0_ref[...] = scores(2 * t + 2)
        fold(s1_ref, 2 * t + 1, False)
        return carry

    lax.fori_loop(0, pairs, body, 0)

    @pl.when(qi % 2 == 1)
    def _():
        s1_ref[...] = scores(qi)
        fold(s0_ref, qi - 1, False)
        fold(s1_ref, qi, True)

    @pl.when(qi % 2 == 0)
    def _():
        fold(s0_ref, qi, True)

    acc = acc_ref[...]
    l = jnp.broadcast_to(acc[:, B_V_DIM:B_V_DIM + 1], (tq, B_V_DIM))
    o_ref[...] = (acc[:, :B_V_DIM] / l).astype(BF16)


def _attn_b(q3, k3, v3):
    tq = 512
    nq = SEQ // tq
    return pl.pallas_call(
        functools.partial(_attn_b_kernel, tq=tq),
        out_shape=jax.ShapeDtypeStruct((TOKENS, B_HEADS * B_V_DIM), BF16),
        grid=(BATCH, B_HEADS, nq),
        in_specs=[
            pl.BlockSpec((1, tq, B_SLOT), lambda b, h, qi: (h, b * nq + qi, 0)),
            pl.BlockSpec((1, SEQ, B_SLOT), lambda b, h, qi: (h, b, 0)),
            pl.BlockSpec((1, SEQ, B_VX), lambda b, h, qi: (h, b, 0)),
        ],
        out_specs=pl.BlockSpec((tq, B_V_DIM), lambda b, h, qi: (b * nq + qi, h)),
        scratch_shapes=[pltpu.VMEM((tq, tq), F32), pltpu.VMEM((tq, tq), F32),
                        pltpu.VMEM((tq, LANES), F32), pltpu.VMEM((tq, B_VX), F32)],
        compiler_params=_cparams("parallel", "parallel", "arbitrary"),
        name="attn_b",
    )(q3, k3, v3)


def _mixer_b(h, gain, w_in, q_norm, w_q, kv_norm, w_k, w_v, w_o, tabs_b):
    c = _b_in(h, gain, w_in)
    q3 = _b_q(c, q_norm, w_q, tabs_b)
    k3, v3 = _b_kv(c, kv_norm, w_k, w_v, tabs_b)
    return _out_proj(_attn_b(q3, k3, v3), w_o, h)


def _c_in_kernel(x_ref, g_ref, wb_ref, wc_ref, wu_ref, cw_ref, o_ref, xn_ref, carry_ref, *, tm):
    i = pl.program_id(0)
    j = pl.program_id(1)

    @pl.when(j == 0)
    def _():
        xn_ref[...] = _rms(x_ref[...], g_ref[...]).astype(BF16)

    xn = xn_ref[...]
    b_gate = _dot(xn, wb_ref[...])
    cu = _dot(xn, wc_ref[...]) * _dot(xn, wu_ref[...])
    w0, w1, w2 = cw_ref[0:1, :], cw_ref[1:2, :], cw_ref[2:3, :]
    z = w2 * cu + w1 * pltpu.roll(cu, 1, 0) + w0 * pltpu.roll(cu, 2, 0)
    o_ref[...] = (b_gate * z).astype(BF16)

    first = (i * tm) % SEQ == 0
    tail = jnp.where(first, 0.0, carry_ref[j])
    both = jnp.concatenate([tail, cu[0:8, :]], axis=0)
    z8 = w2 * both[8:16] + w1 * both[7:15] + w0 * both[6:14]
    o_ref[0:8, :] = (b_gate[0:8, :] * z8).astype(BF16)
    carry_ref[j] = cu[tm - 8:tm, :]


def _c_in(h, gain, w_in, conv_w):
    tm, tn = 1024, 512
    nj = D_MODEL // tn
    return pl.pallas_call(
        functools.partial(_c_in_kernel, tm=tm),
        out_shape=jax.ShapeDtypeStruct((TOKENS, D_MODEL), BF16),
        grid=(TOKENS // tm, nj),
        in_specs=[
            pl.BlockSpec((tm, D_MODEL), lambda i, j: (i, 0)),
            pl.BlockSpec((1, D_MODEL), lambda i, j: (0, 0)),
            pl.BlockSpec((D_MODEL, tn), lambda i, j: (0, j)),
            pl.BlockSpec((D_MODEL, tn), lambda i, j: (0, j + nj)),
            pl.BlockSpec((D_MODEL, tn), lambda i, j: (0, j + 2 * nj)),
            pl.BlockSpec((C_CONV, tn), lambda i, j: (0, j)),
        ],
        out_specs=pl.BlockSpec((tm, tn), lambda i, j: (i, j)),
        scratch_shapes=[pltpu.VMEM((tm, D_MODEL), BF16), pltpu.VMEM((nj, 8, tn), F32)],
        compiler_params=_cparams("arbitrary", "arbitrary"),
        name="c_in",
    )(h, gain.reshape(1, D_MODEL), w_in, w_in, w_in, conv_w)


def _mixer_c(h, gain, w_in, conv_w, w_out):
    return _out_proj(_c_in(h, gain, w_in, conv_w), w_out, h)


def kernel(x, positions, l0_ffn1_norm, l0_ffn1_w13, l0_ffn1_w2, l0_mix_norm, l0_a_w_qkv, l0_a_w_o, l0_ffn2_norm, l0_ffn2_w13, l0_ffn2_w2, l1_ffn1_norm, l1_ffn1_w13, l1_ffn1_w2, l1_mix_norm, l1_b_w_in, l1_b_q_norm, l1_b_w_uq, l1_b_kv_norm, l1_b_w_ukv, l1_b_w_o, l1_ffn2_norm, l1_ffn2_w13, l1_ffn2_w2, l2_ffn1_norm, l2_ffn1_w13, l2_ffn1_w2, l2_mix_norm, l2_c_w_in, l2_c_conv_w, l2_c_w_out, l2_ffn2_norm, l2_ffn2_w13, l2_ffn2_w2, l3_ffn1_norm, l3_ffn1_w13, l3_ffn1_w2, l3_mix_norm, l3_a_w_qkv, l3_a_w_o, l3_ffn2_norm, l3_ffn2_w13, l3_ffn2_w2, final_norm):
    bf = lambda w: w.astype(BF16)
    pos_tiles = positions.reshape(TOKENS // A_TILE, A_TILE)
    pos_a = [pos_tiles.reshape(-1, A_TILE // d, d).swapaxes(1, 2).reshape(TOKENS) for _, d in A_PATTERNS]
    tabs_a = _rope_tables(jnp.concatenate(pos_a + [jnp.zeros((A_TILE,), positions.dtype)]), A_ROT_HALF)
    tabs_b = _rope_tables(positions.reshape(TOKENS), B_ROT_HALF)

    def rope_group(w):
        return _rope_lanes(jnp.pad(w, [(0, 0)] * (w.ndim - 1) + [(0, LANES - B_ROPE_DIM)]), B_ROT_HALF)

    w_uq = l1_b_w_uq.reshape(B_Q_RANK, B_HEADS, B_NOPE_DIM + B_ROPE_DIM)
    w_q = jnp.concatenate([w_uq[..., :B_NOPE_DIM], rope_group(w_uq[..., B_NOPE_DIM:])], axis=-1)
    w_q = bf(w_q.reshape(B_Q_RANK, B_HEADS * B_SLOT))
    w_ukv = l1_b_w_ukv.reshape(B_KV_RANK, B_HEADS, B_NOPE_DIM + B_V_DIM)
    w_k = bf(w_ukv[:, :, :B_NOPE_DIM].reshape(B_KV_RANK, B_HEADS * B_NOPE_DIM))
    w_v = bf(w_ukv[:, :, B_NOPE_DIM:].reshape(B_KV_RANK, B_HEADS * B_V_DIM))
    n_c = B_Q_RANK + B_KV_RANK
    w_b_in = bf(jnp.concatenate([l1_b_w_in[:, :n_c], rope_group(l1_b_w_in[:, n_c:])], axis=-1))

    h = x.reshape(TOKENS, D_MODEL)

    h, w13, w2 = _ffn(h, l0_ffn1_norm, bf(l0_ffn1_w13), bf(l0_ffn1_w2), next_f32=(l0_ffn2_w13, l0_ffn2_w2))
    h = _mixer_a(h, l0_mix_norm, _qkv_w(l0_a_w_qkv), bf(l0_a_w_o), tabs_a)
    h, w13, w2 = _ffn(h, l0_ffn2_norm, w13, w2, next_f32=(l1_ffn1_w13, l1_ffn1_w2))

    h, w13, w2 = _ffn(h, l1_ffn1_norm, w13, w2, next_f32=(l1_ffn2_w13, l1_ffn2_w2))
    h = _mixer_b(h, l1_mix_norm, w_b_in, l1_b_q_norm, w_q, l1_b_kv_norm, w_k, w_v, bf(l1_b_w_o), tabs_b)
    h, w13, w2 = _ffn(h, l1_ffn2_norm, w13, w2, next_f32=(l2_ffn1_w13, l2_ffn1_w2))

    h, w13, w2 = _ffn(h, l2_ffn1_norm, w13, w2, next_f32=(l2_ffn2_w13, l2_ffn2_w2))
    h = _mixer_c(h, l2_mix_norm, bf(l2_c_w_in), l2_c_conv_w, bf(l2_c_w_out))
    h, w13, w2 = _ffn(h, l2_ffn2_norm, w13, w2, next_f32=(l3_ffn1_w13, l3_ffn1_w2))

    h, w13, w2 = _ffn(h, l3_ffn1_norm, w13, w2, next_f32=(l3_ffn2_w13, l3_ffn2_w2))
    h = _mixer_a(h, l3_mix_norm, _qkv_w(l3_a_w_qkv), bf(l3_a_w_o), tabs_a)
    h = _ffn(h, l3_ffn2_norm, w13, w2, final_gain=final_norm)

    return h.reshape(BATCH, SEQ, D_MODEL)
```

```python
import functools

import jax
import jax.numpy as jnp
from jax import lax
from jax.experimental import pallas as pl
from jax.experimental.pallas import tpu as pltpu

F32 = jnp.float32
BF16 = jnp.bfloat16

D_MODEL = 2048
BATCH = 2
SEQ = 8192
TOKENS = BATCH * SEQ
D_FF = 5632
HALF_STEP = 0.5
NORM_EPS = 1e-6
NEG_INF = -1e30
ROPE_THETA = 500000.0

A_HEADS = 16
A_HEAD_DIM = 128
A_ROT_HALF = A_HEAD_DIM // 8
A_PATTERNS = ((128, 1), (512, 4), (2048, 16))
A_BLOCK = 128
A_TILE = 2048

B_HEADS = 16
B_Q_RANK = 1536
B_KV_RANK = 512
B_NOPE_DIM = 128
B_ROPE_DIM = 64
B_ROT_HALF = B_ROPE_DIM // 2
B_V_DIM = 128
B_SLOT = 256
B_VX = 256
B_IN_COLS = 2176

C_CONV = 3

LANES = 128
MXU_COLS = 256
ROPE_PAIR = LANES // 2
VMEM_LIMIT = 56 * 1024 * 1024


def _cparams(*sem):
    return pltpu.CompilerParams(dimension_semantics=sem, vmem_limit_bytes=VMEM_LIMIT)


def _rms(x, gain):
    ms = jnp.mean(x * x, axis=-1, keepdims=True)
    return x * lax.rsqrt(ms + NORM_EPS) * gain


def _dot(a, b):
    return jnp.dot(a, b, preferred_element_type=F32)


def _dot_t(a, b):
    return lax.dot_general(a, b, (((1,), (1,)), ((), ())), preferred_element_type=F32)


def _rope128(y, cos, sin):
    return y * cos + pltpu.roll(y, ROPE_PAIR, 1) * sin


def _rope_lanes(w, half):
    return jnp.concatenate([w[..., :half], w[..., 2 * half:ROPE_PAIR + half], w[..., half:2 * half],
                            w[..., ROPE_PAIR + half:]], axis=-1)


def _rope_tab_kernel(pos_ref, inv_ref, c_ref, s_ref):
    ang = pos_ref[...].astype(F32) * inv_ref[...]
    lane = lax.broadcasted_iota(jnp.int32, (1, LANES), 1)
    sn = jnp.sin(ang)
    c_ref[...] = jnp.cos(ang)
    s_ref[...] = jnp.where(lane < ROPE_PAIR, -sn, sn)


def _rope_tables(pos_flat, half):
    inv = ROPE_THETA ** (-jnp.arange(half, dtype=F32) / half)
    gap = jnp.zeros((ROPE_PAIR - half,), F32)
    inv = jnp.concatenate([inv, gap, inv, gap]).reshape(1, LANES)
    n = pos_flat.shape[0]
    rows = 2048
    tab = jax.ShapeDtypeStruct((n, LANES), F32)
    row_spec = pl.BlockSpec((rows, LANES), lambda i: (i, 0))
    return pl.pallas_call(
        _rope_tab_kernel,
        out_shape=(tab,) * 2,
        grid=(n // rows,),
        in_specs=[pl.BlockSpec((rows, 1), lambda i: (i, 0)), pl.BlockSpec((1, LANES), lambda i: (0, 0))],
        out_specs=(row_spec,) * 2,
        compiler_params=_cparams("parallel"),
        name="rope_tables",
    )(pos_flat.reshape(n, 1), inv)


def _ffn_kernel(x_ref, g_ref, w1_ref, w3_ref, w2_ref, fg_ref, *refs, final, cast_next):
    j = pl.program_id(1)
    if cast_next:
        n13_ref, n2_ref, o_ref, c13_ref, c2_ref, xn_ref, acc_ref = refs
    else:
        o_ref, xn_ref, acc_ref = refs

    @pl.when(j == 0)
    def _():
        xn_ref[...] = _rms(x_ref[...], g_ref[...]).astype(BF16)
        acc_ref[...] = jnp.zeros_like(acc_ref)

    xn = xn_ref[...]
    gate = _dot(xn, w1_ref[...])
    up = _dot(xn, w3_ref[...])
    act = (gate * (1.0 / (1.0 + jnp.exp(-gate))) * up).astype(BF16)
    acc_ref[...] += _dot(act, w2_ref[...])
    if cast_next:
        c13_ref[...] = n13_ref[...].astype(BF16)
        c2_ref[...] = n2_ref[...].astype(BF16)

    @pl.when(j == pl.num_programs(1) - 1)
    def _():
        y = x_ref[...] + HALF_STEP * acc_ref[...]
        if final:
            y = _rms(y, fg_ref[...])
        o_ref[...] = y


def _ffn(h, gain, w13, w2, final_gain=None, next_f32=None):
    tm, tf = 512, 512
    nj = D_FF // tf
    ni = TOKENS // tm
    final = final_gain is not None
    fg = final_gain if final else gain
    in_specs = [
        pl.BlockSpec((tm, D_MODEL), lambda i, j: (i, 0)),
        pl.BlockSpec((1, D_MODEL), lambda i, j: (0, 0)),
        pl.BlockSpec((D_MODEL, tf), lambda i, j: (0, j)),
        pl.BlockSpec((D_MODEL, tf), lambda i, j: (0, j + nj)),
        pl.BlockSpec((tf, D_MODEL), lambda i, j: (j, 0)),
        pl.BlockSpec((1, D_MODEL), lambda i, j: (0, 0)),
    ]
    out_shape = [jax.ShapeDtypeStruct((TOKENS, D_MODEL), F32)]
    out_specs = [pl.BlockSpec((tm, D_MODEL), lambda i, j: (i, 0))]
    args = [h, gain.reshape(1, D_MODEL), w13, w13, w2, fg.reshape(1, D_MODEL)]
    if next_f32 is not None:
        slab13 = pl.BlockSpec((D_MODEL // ni, 2 * D_FF // nj), lambda i, j: (i, j))
        slab2 = pl.BlockSpec((D_FF // (ni * nj), D_MODEL), lambda i, j: (i * nj + j, 0))
        in_specs += [slab13, slab2]
        out_specs += [slab13, slab2]
        out_shape += [jax.ShapeDtypeStruct(w.shape, BF16) for w in next_f32]
        args += list(next_f32)
    outs = pl.pallas_call(
        functools.partial(_ffn_kernel, final=final, cast_next=next_f32 is not None),
        out_shape=tuple(out_shape),
        grid=(ni, nj),
        in_specs=in_specs,
        out_specs=tuple(out_specs),
        scratch_shapes=[pltpu.VMEM((tm, D_MODEL), BF16), pltpu.VMEM((tm, D_MODEL), F32)],
        compiler_params=_cparams("parallel", "arbitrary"),
        name="ffn",
    )(*args)
    return outs if next_f32 is not None else outs[0]


def _out_proj_kernel(x_ref, w_ref, r_ref, o_ref):
    o_ref[...] = r_ref[...] + _dot(x_ref[...], w_ref[...])


def _out_proj(x, w, res):
    k = x.shape[1]
    tm = 512
    return pl.pallas_call(
        _out_proj_kernel,
        out_shape=jax.ShapeDtypeStruct((TOKENS, D_MODEL), F32),
        grid=(TOKENS // tm,),
        in_specs=[
            pl.BlockSpec((tm, k), lambda i: (i, 0)),
            pl.BlockSpec((k, D_MODEL), lambda i: (0, 0)),
            pl.BlockSpec((tm, D_MODEL), lambda i: (i, 0)),
        ],
        out_specs=pl.BlockSpec((tm, D_MODEL), lambda i: (i, 0)),
        compiler_params=_cparams("parallel"),
        name="out_proj",
    )(x, w, res)


def _norm_perm_a_kernel(x_ref, g_ref, *refs, tm):
    o_refs, xs_ref = refs[:-1], refs[-1]
    xn = _rms(x_ref[...], g_ref[...])
    nchunk = D_MODEL // LANES
    for k in range(nchunk):
        xs_ref[k] = xn[:, k * LANES:(k + 1) * LANES]
    for (_, d), o_ref in zip(A_PATTERNS, o_refs):
        if d == 1:
            o_ref[...] = xn.astype(BF16)
        else:
            for c in range(d):
                for k in range(nchunk):
                    o_ref[0, c, 0, :, k * LANES:(k + 1) * LANES] = (
                        xs_ref[k, pl.ds(c, tm // d, stride=d), :].astype(BF16))


def _norm_perm_a(h, gain):
    tm = 512
    per_tile = A_TILE // tm
    shapes, specs = [], []
    for _, d in A_PATTERNS:
        if d == 1:
            shapes.append(jax.ShapeDtypeStruct((TOKENS, D_MODEL), BF16))
            specs.append(pl.BlockSpec((tm, D_MODEL), lambda i: (i, 0)))
        else:
            shapes.append(jax.ShapeDtypeStruct((TOKENS // A_TILE, d, per_tile, tm // d, D_MODEL), BF16))
            specs.append(pl.BlockSpec((1, d, 1, tm // d, D_MODEL),
                                      lambda i: (i // per_tile, 0, i % per_tile, 0, 0)))
    outs = pl.pallas_call(
        functools.partial(_norm_perm_a_kernel, tm=tm),
        out_shape=tuple(shapes),
        grid=(TOKENS // tm,),
        in_specs=[pl.BlockSpec((tm, D_MODEL), lambda i: (i, 0)), pl.BlockSpec((1, D_MODEL), lambda i: (0, 0))],
        out_specs=tuple(specs),
        scratch_shapes=[pltpu.VMEM((D_MODEL // LANES, tm, LANES), F32)],
        compiler_params=_cparams("parallel"),
        name="norm_perm_a",
    )(h, gain.reshape(1, D_MODEL))
    return [o.reshape(TOKENS, D_MODEL) for o in outs]


def _qkv_w_kernel(w_ref, o_ref):
    part = pl.program_id(1) % 3

    @pl.when(part == 2)
    def _():
        o_ref[...] = w_ref[...].astype(BF16)

    @pl.when(part != 2)
    def _():
        lane = lax.broadcasted_iota(jnp.int32, (1, LANES), 1)
        stay = (lane < A_ROT_HALF) | (lane >= ROPE_PAIR + A_ROT_HALF)
        for hh in range(A_HEADS):
            sl = slice(hh * A_HEAD_DIM, (hh + 1) * A_HEAD_DIM)
            y = w_ref[:, sl]
            moved = jnp.where(lane < ROPE_PAIR, pltpu.roll(y, LANES - A_ROT_HALF, 1),
                              pltpu.roll(y, ROPE_PAIR - A_ROT_HALF, 1))
            o_ref[:, sl] = jnp.where(stay, y, moved).astype(BF16)


def _qkv_w(w_qkv):
    rows = 256
    hd = A_HEADS * A_HEAD_DIM
    spec = pl.BlockSpec((rows, hd), lambda i, j: (i, j))
    return pl.pallas_call(
        _qkv_w_kernel,
        out_shape=jax.ShapeDtypeStruct(w_qkv.shape, BF16),
        grid=(D_MODEL // rows, w_qkv.shape[1] // hd),
        in_specs=[spec],
        out_specs=spec,
        compiler_params=_cparams("parallel", "parallel"),
        name="qkv_w",
    )(w_qkv)


def _qkv_a_kernel(x_ref, w_ref, cos_ref, sin_ref, o_ref, *, tn):
    part = pl.program_id(1) // ((A_HEADS * A_HEAD_DIM) // tn)
    scale = jnp.where(part == 0, A_HEAD_DIM ** -0.5, 1.0).astype(F32)
    half = x_ref.shape[0] // 2
    for cc in range(tn // MXU_COLS):
        for rows in (slice(0, half), slice(half, 2 * half)):
            y = _dot(x_ref[rows, :], w_ref[:, cc * MXU_COLS:(cc + 1) * MXU_COLS])
            for hh in range(MXU_COLS // LANES):
                sl = slice(cc * MXU_COLS + hh * LANES, cc * MXU_COLS + (hh + 1) * LANES)
                y_h = y[:, hh * LANES:(hh + 1) * LANES]
                o_ref[rows, sl] = (_rope128(y_h, cos_ref[rows, :], sin_ref[rows, :]) * scale).astype(BF16)


def _qkv_a(xn, w_qkv, tabs, group):
    tm, tn = 1024, 1024
    cols = 3 * A_HEADS * A_HEAD_DIM
    nj = cols // tn
    ni = TOKENS // tm
    v_first = 2 * (A_HEADS * A_HEAD_DIM) // tn
    tab_spec = pl.BlockSpec((tm, LANES),
                            lambda i, j: (jnp.where(j >= v_first, len(A_PATTERNS) * ni, group * ni + i), 0))
    return pl.pallas_call(
        functools.partial(_qkv_a_kernel, tn=tn),
        out_shape=jax.ShapeDtypeStruct((TOKENS, cols), BF16),
        grid=(ni, nj),
        in_specs=[
            pl.BlockSpec((tm, D_MODEL), lambda i, j: (i, 0)),
            pl.BlockSpec((D_MODEL, tn), lambda i, j: (0, group * nj + j)),
            tab_spec, tab_spec,
        ],
        out_specs=pl.BlockSpec((tm, tn), lambda i, j: (i, j)),
        compiler_params=_cparams("parallel", "parallel"),
        name="qkv_a",
    )(xn, w_qkv, *tabs)


def _attn_a_kernel(q_ref, kc_ref, kp_ref, vc_ref, vp_ref, o_ref, lse_ref, os_ref, *, d, heads, per_trip):
    rows = A_TILE // d
    nblk = rows // A_BLOCK
    first_tile = pl.program_id(1) == 0
    r = lax.broadcasted_iota(jnp.int32, (A_BLOCK, 2 * A_BLOCK), 0)
    kk = lax.broadcasted_iota(jnp.int32, (A_BLOCK, 2 * A_BLOCK), 1)
    band = (kk >= r) & (kk <= r + A_BLOCK)
    lane = lax.broadcasted_iota(jnp.int32, (A_BLOCK, LANES), 1)
    hs = [slice(h * A_HEAD_DIM, (h + 1) * A_HEAD_DIM) for h in range(heads)]
    first_head = pl.program_id(2) * heads

    @pl.when(first_head == 0)
    def _():
        lse_ref[...] = jnp.zeros_like(lse_ref)

    def trip(t, carry):
        loaded = []
        for i in range(per_trip):
            u = t * per_trip + i
            c = u // nblk
            blk = u % nblk
            start = pl.multiple_of(blk * A_BLOCK, A_BLOCK)
            before = pl.multiple_of(jnp.maximum(start - A_BLOCK, 0), A_BLOCK)
            q = q_ref[0, c, pl.ds(start, A_BLOCK), :]
            k = jnp.concatenate([jnp.where(blk == 0, kp_ref[0, c], kc_ref[0, c, pl.ds(before, A_BLOCK), :]),
                                 kc_ref[0, c, pl.ds(start, A_BLOCK), :]], axis=0)
            v = jnp.concatenate([jnp.where(blk == 0, vp_ref[0, c], vc_ref[0, c, pl.ds(before, A_BLOCK), :]),
                                 vc_ref[0, c, pl.ds(start, A_BLOCK), :]], axis=0)
            no_prev = jnp.logical_and(first_tile, blk == 0)
            mask = jnp.logical_and(band, jnp.logical_or(kk >= A_BLOCK, jnp.logical_not(no_prev)))
            out_rows = (pl.ds(blk * (A_BLOCK * d) + c, A_BLOCK, stride=d) if d > 1
                        else pl.ds(start, A_BLOCK))
            loaded.append(([_dot_t(q[:, sl], k[:, sl]) for sl in hs], v, mask, out_rows))
        for scores, v, mask, out_rows in loaded:
            lse_tile = lse_ref[out_rows, :]
            for h, sl in enumerate(hs):
                s = jnp.where(mask, scores[h], NEG_INF)
                m = jnp.max(s, axis=-1, keepdims=True)
                p = jnp.exp(s - m)
                l = jnp.sum(p, axis=-1, keepdims=True)
                os_ref[h, out_rows, :] = _dot(p.astype(BF16), v[:, sl]) / l
                lse_tile = jnp.where(lane == first_head + h, m + jnp.log(l), lse_tile)
            lse_ref[out_rows, :] = lse_tile
        return carry

    lax.fori_loop(0, d * nblk // per_trip, trip, 0)
    o_ref[...] = os_ref[...].astype(BF16)


def _attn_a(qkv, window, dilation):
    assert window // dilation == A_BLOCK, "band width must equal the query block"
    d = dilation
    heads = 4
    hw = heads * A_HEAD_DIM
    ngrp = A_HEADS // heads
    rows = A_TILE // d
    tiles = SEQ // A_TILE
    qkv_v = qkv.reshape(BATCH, tiles, d, rows, 3 * A_HEADS * A_HEAD_DIM)
    last_blk = rows // A_BLOCK - 1

    def cur(part):
        return pl.BlockSpec((None, 1, d, rows, hw), lambda b, n, g: (b, n, 0, 0, part * ngrp + g))

    def prev(part):
        return pl.BlockSpec((None, 1, d, A_BLOCK, hw),
                            lambda b, n, g: (b, jnp.maximum(n - 1, 0), 0, last_blk, part * ngrp + g))

    return pl.pallas_call(
        functools.partial(_attn_a_kernel, d=d, heads=heads, per_trip=4),
        out_shape=(jax.ShapeDtypeStruct((A_HEADS, TOKENS, A_HEAD_DIM), BF16),
                   jax.ShapeDtypeStruct((TOKENS, LANES), F32)),
        grid=(BATCH, tiles, ngrp),
        in_specs=[cur(0), cur(1), prev(1), cur(2), prev(2)],
        out_specs=(pl.BlockSpec((heads, A_TILE, A_HEAD_DIM), lambda b, n, g: (g, b * tiles + n, 0)),
                   pl.BlockSpec((A_TILE, LANES), lambda b, n, g: (b * tiles + n, 0))),
        scratch_shapes=[pltpu.VMEM((heads, A_TILE, A_HEAD_DIM), F32)],
        compiler_params=_cparams("parallel", "parallel", "arbitrary"),
        name="attn_a_d%d" % d,
    )(qkv_v, qkv_v, qkv_v, qkv_v, qkv_v)


def _comb_proj_a_kernel(o0, o1, o2, l0, l1, l2, w_ref, r_ref, out_ref, comb_ref):
    s = pl.program_id(0)
    slot = s % 2

    @pl.when(s == 0)
    def _():
        comb_ref[1] = jnp.zeros(comb_ref.shape[1:], BF16)

    out_ref[...] = r_ref[...] + _dot(comb_ref[1 - slot], w_ref[...])
    a = [l0[...], l1[...], l2[...]]
    m = jnp.maximum(jnp.maximum(a[0], a[1]), a[2])
    e = [jnp.exp(t - m) for t in a]
    den = e[0] + e[1] + e[2]
    w = [t / den for t in e]
    rows = out_ref.shape[0]
    for h in range(A_HEADS):
        acc = jnp.zeros((rows, A_HEAD_DIM), F32)
        for wg, og in zip(w, (o0, o1, o2)):
            acc = acc + jnp.broadcast_to(wg[:, h:h + 1], (rows, A_HEAD_DIM)) * og[h]
        comb_ref[slot, :, h * A_HEAD_DIM:(h + 1) * A_HEAD_DIM] = acc.astype(BF16)


def _comb_proj_a(outs, lses, w_o, res):
    tm = 256
    nt = TOKENS // tm
    hd = A_HEADS * A_HEAD_DIM
    o_spec = pl.BlockSpec((A_HEADS, tm, A_HEAD_DIM), lambda s: (0, jnp.minimum(s, nt - 1), 0))
    l_spec = pl.BlockSpec((tm, LANES), lambda s: (jnp.minimum(s, nt - 1), 0))
    row_spec = pl.BlockSpec((tm, D_MODEL), lambda s: (jnp.maximum(s - 1, 0), 0))
    return pl.pallas_call(
        _comb_proj_a_kernel,
        out_shape=jax.ShapeDtypeStruct((TOKENS, D_MODEL), F32),
        grid=(nt + 1,),
        in_specs=[o_spec] * 3 + [l_spec] * 3 + [pl.BlockSpec((hd, D_MODEL), lambda s: (0, 0)), row_spec],
        out_specs=row_spec,
        scratch_shapes=[pltpu.VMEM((2, tm, hd), BF16)],
        compiler_params=_cparams("arbitrary"),
        name="comb_proj_a",
    )(*outs, *lses, w_o, res)


def _mixer_a(h, gain, w_qkv, w_o, tabs_a):
    outs, lses = [], []
    for g, ((window, dilation), xn) in enumerate(zip(A_PATTERNS, _norm_perm_a(h, gain))):
        o, lse = _attn_a(_qkv_a(xn, w_qkv, tabs_a, g), window, dilation)
        outs.append(o)
        lses.append(lse)
    return _comb_proj_a(outs, lses, w_o, h)


def _b_in_kernel(x_ref, g_ref, w_ref, o_ref):
    o_ref[...] = _dot(_rms(x_ref[...], g_ref[...]).astype(BF16), w_ref[...])


def _b_in(h, gain, w_in):
    tm = 512
    return pl.pallas_call(
        _b_in_kernel,
        out_shape=jax.ShapeDtypeStruct((TOKENS, B_IN_COLS), F32),
        grid=(TOKENS // tm,),
        in_specs=[
            pl.BlockSpec((tm, D_MODEL), lambda i: (i, 0)),
            pl.BlockSpec((1, D_MODEL), lambda i: (0, 0)),
            pl.BlockSpec((D_MODEL, B_IN_COLS), lambda i: (0, 0)),
        ],
        out_specs=pl.BlockSpec((tm, B_IN_COLS), lambda i: (i, 0)),
        compiler_params=_cparams("parallel"),
        name="b_in",
    )(h, gain.reshape(1, D_MODEL), w_in)


def _b_q_kernel(c_ref, g_ref, w_ref, cos_ref, sin_ref, o_ref, xn_ref, *, slots):
    @pl.when(pl.program_id(1) == 0)
    def _():
        xn_ref[...] = _rms(c_ref[...], g_ref[...]).astype(BF16)

    y = _dot(xn_ref[...], w_ref[...])
    scale = (B_NOPE_DIM + B_ROPE_DIM) ** -0.5
    cos, sin = cos_ref[...], sin_ref[...]
    for s in range(slots):
        nope = y[:, s * B_SLOT:s * B_SLOT + LANES]
        rot = _rope128(y[:, s * B_SLOT + LANES:(s + 1) * B_SLOT], cos, sin)
        o_ref[s, :, :LANES] = (nope * scale).astype(BF16)
        o_ref[s, :, LANES:] = (rot * scale).astype(BF16)


def _b_q(c, gain, w_q, tabs_b):
    tm, slots = 1024, 4
    tn = slots * B_SLOT
    tab_spec = pl.BlockSpec((tm, LANES), lambda i, j: (i, 0))
    return pl.pallas_call(
        functools.partial(_b_q_kernel, slots=slots),
        out_shape=jax.ShapeDtypeStruct((B_HEADS, TOKENS, B_SLOT), BF16),
        grid=(TOKENS // tm, B_HEADS // slots),
        in_specs=[
            pl.BlockSpec((tm, B_Q_RANK), lambda i, j: (i, 0)),
            pl.BlockSpec((1, B_Q_RANK), lambda i, j: (0, 0)),
            pl.BlockSpec((B_Q_RANK, tn), lambda i, j: (0, j)),
            tab_spec, tab_spec,
        ],
        out_specs=pl.BlockSpec((slots, tm, B_SLOT), lambda i, j: (j, i, 0)),
        scratch_shapes=[pltpu.VMEM((tm, B_Q_RANK), BF16)],
        compiler_params=_cparams("parallel", "arbitrary"),
        name="b_q",
    )(c, gain.reshape(1, B_Q_RANK), w_q, *tabs_b)


def _b_kv_kernel(c_ref, g_ref, wk_ref, wv_ref, cr_ref, cos_ref, sin_ref, k_ref, v_ref, *, slots):
    xn = _rms(c_ref[...], g_ref[...]).astype(BF16)
    yk = _dot(xn, wk_ref[...])
    yv = _dot(xn, wv_ref[...])
    kr = _rope128(cr_ref[...], cos_ref[...], sin_ref[...]).astype(BF16)
    lane = lax.broadcasted_iota(jnp.int32, (c_ref.shape[0], B_VX - B_V_DIM), 1)
    ones_col = jnp.where(lane == 0, 1.0, 0.0).astype(BF16)
    for s in range(slots):
        sl = slice(s * LANES, (s + 1) * LANES)
        k_ref[s, :, :LANES] = yk[:, sl].astype(BF16)
        k_ref[s, :, LANES:] = kr
        v_ref[s, :, :B_V_DIM] = yv[:, sl].astype(BF16)
        v_ref[s, :, B_V_DIM:] = ones_col


def _b_kv(c, gain, w_k, w_v, tabs_b):
    tm, slots = 1024, 4
    tn = slots * LANES
    kv_block = B_Q_RANK // B_KV_RANK
    rope_block = (B_Q_RANK + B_KV_RANK) // LANES
    tab_spec = pl.BlockSpec((tm, LANES), lambda i, j: (i, 0))
    return pl.pallas_call(
        functools.partial(_b_kv_kernel, slots=slots),
        out_shape=(jax.ShapeDtypeStruct((B_HEADS, TOKENS, B_SLOT), BF16),
                   jax.ShapeDtypeStruct((B_HEADS, TOKENS, B_VX), BF16)),
        grid=(TOKENS // tm, B_HEADS // slots),
        in_specs=[
            pl.BlockSpec((tm, B_KV_RANK), lambda i, j: (i, kv_block)),
            pl.BlockSpec((1, B_KV_RANK), lambda i, j: (0, 0)),
            pl.BlockSpec((B_KV_RANK, tn), lambda i, j: (0, j)),
            pl.BlockSpec((B_KV_RANK, tn), lambda i, j: (0, j)),
            pl.BlockSpec((tm, LANES), lambda i, j: (i, rope_block)),
            tab_spec, tab_spec,
        ],
        out_specs=(pl.BlockSpec((slots, tm, B_SLOT), lambda i, j: (j, i, 0)),
                   pl.BlockSpec((slots, tm, B_VX), lambda i, j: (j, i, 0))),
        compiler_params=_cparams("parallel", "parallel"),
        name="b_kv",
    )(c, gain.reshape(1, B_KV_RANK), w_k, w_v, c, *tabs_b)


def _attn_b_kernel(q_ref, k_ref, v_ref, o_ref, s0_ref, s1_ref, m_ref, acc_ref, *, tq, heads):
    qi = pl.program_id(2)
    m_ref[...] = jnp.full_like(m_ref, NEG_INF)
    acc_ref[...] = jnp.zeros_like(acc_ref)

    def scores(s_ref, ki):
        for h in range(heads):
            k = k_ref[h, pl.ds(pl.multiple_of(ki * tq, tq), tq), :]
            s_ref[h] = _dot_t(q_ref[h], k)

    def fold(s_ref, ki, diagonal):
        for h in range(heads):
            s = s_ref[h]
            if diagonal:
                row = lax.broadcasted_iota(jnp.int32, (tq, tq), 0)
                col = lax.broadcasted_iota(jnp.int32, (tq, tq), 1)
                s = jnp.where(row >= col, s, NEG_INF)
            m_prev = m_ref[h]
            m_new = jnp.maximum(m_prev, jnp.max(s, axis=-1, keepdims=True))
            alpha = jnp.exp(m_prev - m_new)
            p = jnp.exp((s - jnp.concatenate([m_new] * (tq // LANES), axis=1)).astype(BF16))
            v = v_ref[h, pl.ds(pl.multiple_of(ki * tq, tq), tq), :]
            acc_ref[h] = jnp.concatenate([alpha] * (B_VX // LANES), axis=1) * acc_ref[h] + _dot(p, v)
            m_ref[h] = m_new

    pairs = qi // 2
    scores(s0_ref, 0)

    def body(t, carry):
        scores(s1_ref, 2 * t + 1)
        fold(s0_ref, 2 * t, False)
        scores(s0_ref, 2 * t + 2)
        fold(s1_ref, 2 * t + 1, False)
        return carry

    lax.fori_loop(0, pairs, body, 0)

    @pl.when(qi % 2 == 1)
    def _():
        scores(s1_ref, qi)
        fold(s0_ref, qi - 1, False)
        fold(s1_ref, qi, True)

    @pl.when(qi % 2 == 0)
    def _():
        fold(s0_ref, qi, True)

    for h in range(heads):
        acc = acc_ref[h]
        l = jnp.broadcast_to(acc[:, B_V_DIM:B_V_DIM + 1], (tq, B_V_DIM))
        o_ref[:, h * B_V_DIM:(h + 1) * B_V_DIM] = (acc[:, :B_V_DIM] / l).astype(BF16)


def _attn_b(q3, k3, v3):
    tq, heads = 512, 2
    nq = SEQ // tq
    return pl.pallas_call(
        functools.partial(_attn_b_kernel, tq=tq, heads=heads),
        out_shape=jax.ShapeDtypeStruct((TOKENS, B_HEADS * B_V_DIM), BF16),
        grid=(BATCH, B_HEADS // heads, nq),
        in_specs=[
            pl.BlockSpec((heads, tq, B_SLOT), lambda b, g, qi: (g, b * nq + qi, 0)),
            pl.BlockSpec((heads, SEQ, B_SLOT), lambda b, g, qi: (g, b, 0)),
            pl.BlockSpec((heads, SEQ, B_VX), lambda b, g, qi: (g, b, 0)),
        ],
        out_specs=pl.BlockSpec((tq, heads * B_V_DIM), lambda b, g, qi: (b * nq + qi, g)),
        scratch_shapes=[pltpu.VMEM((heads, tq, tq), F32), pltpu.VMEM((heads, tq, tq), F32),
                        pltpu.VMEM((heads, tq, LANES), F32), pltpu.VMEM((heads, tq, B_VX), F32)],
        compiler_params=_cparams("parallel", "parallel", "arbitrary"),
        name="attn_b",
    )(q3, k3, v3)


def _mixer_b(h, gain, w_in, q_norm, w_q, kv_norm, w_k, w_v, w_o, tabs_b):
    c = _b_in(h, gain, w_in)
    q3 = _b_q(c, q_norm, w_q, tabs_b)
    k3, v3 = _b_kv(c, kv_norm, w_k, w_v, tabs_b)
    return _out_proj(_attn_b(q3, k3, v3), w_o, h)


def _c_in_kernel(x_ref, g_ref, wb_ref, wc_ref, wu_ref, cw_ref, o_ref, xn_ref, carry_ref, *, tm):
    i = pl.program_id(0)
    j = pl.program_id(1)

    @pl.when(j == 0)
    def _():
        xn_ref[...] = _rms(x_ref[...], g_ref[...]).astype(BF16)

    xn = xn_ref[...]
    b_gate = _dot(xn, wb_ref[...])
    cu = _dot(xn, wc_ref[...]) * _dot(xn, wu_ref[...])
    w0, w1, w2 = cw_ref[0:1, :], cw_ref[1:2, :], cw_ref[2:3, :]
    z = w2 * cu + w1 * pltpu.roll(cu, 1, 0) + w0 * pltpu.roll(cu, 2, 0)
    o_ref[...] = (b_gate * z).astype(BF16)

    first = (i * tm) % SEQ == 0
    tail = jnp.where(first, 0.0, carry_ref[j])
    both = jnp.concatenate([tail, cu[0:8, :]], axis=0)
    z8 = w2 * both[8:16] + w1 * both[7:15] + w0 * both[6:14]
    o_ref[0:8, :] = (b_gate[0:8, :] * z8).astype(BF16)
    carry_ref[j] = cu[tm - 8:tm, :]


def _c_in(h, gain, w_in, conv_w):
    tm, tn = 1024, 512
    nj = D_MODEL // tn
    return pl.pallas_call(
        functools.partial(_c_in_kernel, tm=tm),
        out_shape=jax.ShapeDtypeStruct((TOKENS, D_MODEL), BF16),
        grid=(TOKENS // tm, nj),
        in_specs=[
            pl.BlockSpec((tm, D_MODEL), lambda i, j: (i, 0)),
            pl.BlockSpec((1, D_MODEL), lambda i, j: (0, 0)),
            pl.BlockSpec((D_MODEL, tn), lambda i, j: (0, j)),
            pl.BlockSpec((D_MODEL, tn), lambda i, j: (0, j + nj)),
            pl.BlockSpec((D_MODEL, tn), lambda i, j: (0, j + 2 * nj)),
            pl.BlockSpec((C_CONV, tn), lambda i, j: (0, j)),
        ],
        out_specs=pl.BlockSpec((tm, tn), lambda i, j: (i, j)),
        scratch_shapes=[pltpu.VMEM((tm, D_MODEL), BF16), pltpu.VMEM((nj, 8, tn), F32)],
        compiler_params=_cparams("arbitrary", "arbitrary"),
        name="c_in",
    )(h, gain.reshape(1, D_MODEL), w_in, w_in, w_in, conv_w)


def _mixer_c(h, gain, w_in, conv_w, w_out):
    return _out_proj(_c_in(h, gain, w_in, conv_w), w_out, h)


def kernel(x, positions, l0_ffn1_norm, l0_ffn1_w13, l0_ffn1_w2, l0_mix_norm, l0_a_w_qkv, l0_a_w_o, l0_ffn2_norm, l0_ffn2_w13, l0_ffn2_w2, l1_ffn1_norm, l1_ffn1_w13, l1_ffn1_w2, l1_mix_norm, l1_b_w_in, l1_b_q_norm, l1_b_w_uq, l1_b_kv_norm, l1_b_w_ukv, l1_b_w_o, l1_ffn2_norm, l1_ffn2_w13, l1_ffn2_w2, l2_ffn1_norm, l2_ffn1_w13, l2_ffn1_w2, l2_mix_norm, l2_c_w_in, l2_c_conv_w, l2_c_w_out, l2_ffn2_norm, l2_ffn2_w13, l2_ffn2_w2, l3_ffn1_norm, l3_ffn1_w13, l3_ffn1_w2, l3_mix_norm, l3_a_w_qkv, l3_a_w_o, l3_ffn2_norm, l3_ffn2_w13, l3_ffn2_w2, final_norm):
    bf = lambda w: w.astype(BF16)
    pos_tiles = positions.reshape(TOKENS // A_TILE, A_TILE)
    pos_a = [pos_tiles.reshape(-1, A_TILE // d, d).swapaxes(1, 2).reshape(TOKENS) for _, d in A_PATTERNS]
    tabs_a = _rope_tables(jnp.concatenate(pos_a + [jnp.zeros((A_TILE,), positions.dtype)]), A_ROT_HALF)
    tabs_b = _rope_tables(positions.reshape(TOKENS), B_ROT_HALF)

    def rope_group(w):
        return _rope_lanes(jnp.pad(w, [(0, 0)] * (w.ndim - 1) + [(0, LANES - B_ROPE_DIM)]), B_ROT_HALF)

    w_uq = l1_b_w_uq.reshape(B_Q_RANK, B_HEADS, B_NOPE_DIM + B_ROPE_DIM)
    w_q = jnp.concatenate([w_uq[..., :B_NOPE_DIM], rope_group(w_uq[..., B_NOPE_DIM:])], axis=-1)
    w_q = bf(w_q.reshape(B_Q_RANK, B_HEADS * B_SLOT))
    w_ukv = l1_b_w_ukv.reshape(B_KV_RANK, B_HEADS, B_NOPE_DIM + B_V_DIM)
    w_k = bf(w_ukv[:, :, :B_NOPE_DIM].reshape(B_KV_RANK, B_HEADS * B_NOPE_DIM))
    w_v = bf(w_ukv[:, :, B_NOPE_DIM:].reshape(B_KV_RANK, B_HEADS * B_V_DIM))
    n_c = B_Q_RANK + B_KV_RANK
    w_b_in = bf(jnp.concatenate([l1_b_w_in[:, :n_c], rope_group(l1_b_w_in[:, n_c:])], axis=-1))

    h = x.reshape(TOKENS, D_MODEL)

    h, w13, w2 = _ffn(h, l0_ffn1_norm, bf(l0_ffn1_w13), bf(l0_ffn1_w2), next_f32=(l0_ffn2_w13, l0_ffn2_w2))
    h = _mixer_a(h, l0_mix_norm, _qkv_w(l0_a_w_qkv), bf(l0_a_w_o), tabs_a)
    h, w13, w2 = _ffn(h, l0_ffn2_norm, w13, w2, next_f32=(l1_ffn1_w13, l1_ffn1_w2))

    h, w13, w2 = _ffn(h, l1_ffn1_norm, w13, w2, next_f32=(l1_ffn2_w13, l1_ffn2_w2))
    h = _mixer_b(h, l1_mix_norm, w_b_in, l1_b_q_norm, w_q, l1_b_kv_norm, w_k, w_v, bf(l1_b_w_o), tabs_b)
    h, w13, w2 = _ffn(h, l1_ffn2_norm, w13, w2, next_f32=(l2_ffn1_w13, l2_ffn1_w2))

    h, w13, w2 = _ffn(h, l2_ffn1_norm, w13, w2, next_f32=(l2_ffn2_w13, l2_ffn2_w2))
    h = _mixer_c(h, l2_mix_norm, bf(l2_c_w_in), l2_c_conv_w, bf(l2_c_w_out))
    h, w13, w2 = _ffn(h, l2_ffn2_norm, w13, w2, next_f32=(l3_ffn1_w13, l3_ffn1_w2))

    h, w13, w2 = _ffn(h, l3_ffn1_norm, w13, w2, next_f32=(l3_ffn2_w13, l3_ffn2_w2))
    h = _mixer_a(h, l3_mix_norm, _qkv_w(l3_a_w_qkv), bf(l3_a_w_o), tabs_a)
    h = _ffn(h, l3_ffn2_norm, w13, w2, final_gain=final_norm)

    return h.reshape(BATCH, SEQ, D_MODEL)
```

```python
import functools

import jax
import jax.numpy as jnp
from jax import lax
from jax.experimental import pallas as pl
from jax.experimental.pallas import tpu as pltpu

F32 = jnp.float32
BF16 = jnp.bfloat16

D_MODEL = 2048
BATCH = 2
SEQ = 8192
TOKENS = BATCH * SEQ
D_FF = 5632
HALF_STEP = 0.5
NORM_EPS = 1e-6
NEG_INF = -1e30
ROPE_THETA = 500000.0

A_HEADS = 16
A_HEAD_DIM = 128
A_ROT_HALF = A_HEAD_DIM // 8
A_PATTERNS = ((128, 1), (512, 4), (2048, 16))
A_BLOCK = 128
A_TILE = 2048
A_MID_DILATION = 4
assert [d for _, d in A_PATTERNS] == [1, A_MID_DILATION, A_MID_DILATION ** 2]

B_HEADS = 16
B_Q_RANK = 1536
B_KV_RANK = 512
B_NOPE_DIM = 128
B_ROPE_DIM = 64
B_ROT_HALF = B_ROPE_DIM // 2
B_V_DIM = 128
B_SLOT = 256
B_VX = 256
B_IN_COLS = 2176

C_CONV = 3

LANES = 128
MXU_COLS = 256
ROPE_PAIR = LANES // 2
VMEM_LIMIT = 56 * 1024 * 1024


def _cparams(*sem):
    return pltpu.CompilerParams(dimension_semantics=sem, vmem_limit_bytes=VMEM_LIMIT)


def _rms(x, gain):
    ms = jnp.mean(x * x, axis=-1, keepdims=True)
    return x * lax.rsqrt(ms + NORM_EPS) * gain


def _dot(a, b):
    return jnp.dot(a, b, preferred_element_type=F32)


def _dot_t(a, b):
    return lax.dot_general(a, b, (((1,), (1,)), ((), ())), preferred_element_type=F32)


def _rope128(y, cos, sin):
    return y * cos + pltpu.roll(y, ROPE_PAIR, 1) * sin


def _rope_lanes(w, half):
    return jnp.concatenate([w[..., :half], w[..., 2 * half:ROPE_PAIR + half], w[..., half:2 * half],
                            w[..., ROPE_PAIR + half:]], axis=-1)


def _rope_tab_kernel(pos_ref, inv_ref, c_ref, s_ref):
    ang = pos_ref[...].astype(F32) * inv_ref[...]
    lane = lax.broadcasted_iota(jnp.int32, (1, LANES), 1)
    sn = jnp.sin(ang)
    c_ref[...] = jnp.cos(ang)
    s_ref[...] = jnp.where(lane < ROPE_PAIR, -sn, sn)


def _rope_tables(pos_flat, half):
    inv = ROPE_THETA ** (-jnp.arange(half, dtype=F32) / half)
    gap = jnp.zeros((ROPE_PAIR - half,), F32)
    inv = jnp.concatenate([inv, gap, inv, gap]).reshape(1, LANES)
    n = pos_flat.shape[0]
    rows = 2048
    tab = jax.ShapeDtypeStruct((n, LANES), F32)
    row_spec = pl.BlockSpec((rows, LANES), lambda i: (i, 0))
    return pl.pallas_call(
        _rope_tab_kernel,
        out_shape=(tab,) * 2,
        grid=(n // rows,),
        in_specs=[pl.BlockSpec((rows, 1), lambda i: (i, 0)), pl.BlockSpec((1, LANES), lambda i: (0, 0))],
        out_specs=(row_spec,) * 2,
        compiler_params=_cparams("parallel"),
        name="rope_tables",
    )(pos_flat.reshape(n, 1), inv)


def _ffn_kernel(x_ref, g_ref, w1_ref, w3_ref, w2_ref, fg_ref, *refs, final, cast_next):
    j = pl.program_id(1)
    if cast_next:
        n13_ref, n2_ref, o_ref, c13_ref, c2_ref, xn_ref, acc_ref = refs
    else:
        o_ref, xn_ref, acc_ref = refs

    @pl.when(j == 0)
    def _():
        xn_ref[...] = _rms(x_ref[...], g_ref[...]).astype(BF16)
        acc_ref[...] = jnp.zeros_like(acc_ref)

    xn = xn_ref[...]
    gate = _dot(xn, w1_ref[...])
    up = _dot(xn, w3_ref[...])
    act = (gate * (1.0 / (1.0 + jnp.exp(-gate))) * up).astype(BF16)
    acc_ref[...] += _dot(act, w2_ref[...])
    if cast_next:
        c13_ref[...] = n13_ref[...].astype(BF16)
        c2_ref[...] = n2_ref[...].astype(BF16)

    @pl.when(j == pl.num_programs(1) - 1)
    def _():
        y = x_ref[...] + HALF_STEP * acc_ref[...]
        if final:
            y = _rms(y, fg_ref[...])
        o_ref[...] = y


def _ffn(h, gain, w13, w2, final_gain=None, next_f32=None):
    tm, tf = 512, 512
    nj = D_FF // tf
    ni = TOKENS // tm
    final = final_gain is not None
    fg = final_gain if final else gain
    in_specs = [
        pl.BlockSpec((tm, D_MODEL), lambda i, j: (i, 0)),
        pl.BlockSpec((1, D_MODEL), lambda i, j: (0, 0)),
        pl.BlockSpec((D_MODEL, tf), lambda i, j: (0, j)),
        pl.BlockSpec((D_MODEL, tf), lambda i, j: (0, j + nj)),
        pl.BlockSpec((tf, D_MODEL), lambda i, j: (j, 0)),
        pl.BlockSpec((1, D_MODEL), lambda i, j: (0, 0)),
    ]
    out_shape = [jax.ShapeDtypeStruct((TOKENS, D_MODEL), F32)]
    out_specs = [pl.BlockSpec((tm, D_MODEL), lambda i, j: (i, 0))]
    args = [h, gain.reshape(1, D_MODEL), w13, w13, w2, fg.reshape(1, D_MODEL)]
    if next_f32 is not None:
        slab13 = pl.BlockSpec((D_MODEL // ni, 2 * D_FF // nj), lambda i, j: (i, j))
        slab2 = pl.BlockSpec((D_FF // (ni * nj), D_MODEL), lambda i, j: (i * nj + j, 0))
        in_specs += [slab13, slab2]
        out_specs += [slab13, slab2]
        out_shape += [jax.ShapeDtypeStruct(w.shape, BF16) for w in next_f32]
        args += list(next_f32)
    outs = pl.pallas_call(
        functools.partial(_ffn_kernel, final=final, cast_next=next_f32 is not None),
        out_shape=tuple(out_shape),
        grid=(ni, nj),
        in_specs=in_specs,
        out_specs=tuple(out_specs),
        scratch_shapes=[pltpu.VMEM((tm, D_MODEL), BF16), pltpu.VMEM((tm, D_MODEL), F32)],
        compiler_params=_cparams("parallel", "arbitrary"),
        name="ffn",
    )(*args)
    return outs if next_f32 is not None else outs[0]


def _out_proj_kernel(x_ref, w_ref, r_ref, o_ref):
    o_ref[...] = r_ref[...] + _dot(x_ref[...], w_ref[...])


def _out_proj(x, w, res):
    k = x.shape[1]
    tm = 512
    return pl.pallas_call(
        _out_proj_kernel,
        out_shape=jax.ShapeDtypeStruct((TOKENS, D_MODEL), F32),
        grid=(TOKENS // tm,),
        in_specs=[
            pl.BlockSpec((tm, k), lambda i: (i, 0)),
            pl.BlockSpec((k, D_MODEL), lambda i: (0, 0)),
            pl.BlockSpec((tm, D_MODEL), lambda i: (i, 0)),
        ],
        out_specs=pl.BlockSpec((tm, D_MODEL), lambda i: (i, 0)),
        compiler_params=_cparams("parallel"),
        name="out_proj",
    )(x, w, res)


def _norm_perm_a_kernel(x_ref, g_ref, *refs, tm):
    o_refs, xs_ref, mid_ref = refs[:-2], refs[-2], refs[-1]
    xn = _rms(x_ref[...], g_ref[...])
    nchunk = D_MODEL // LANES
    for k in range(nchunk):
        xs_ref[k] = xn[:, k * LANES:(k + 1) * LANES]
    dp = 1
    for (_, d), o_ref in zip(A_PATTERNS, o_refs):
        if d == 1:
            o_ref[...] = xn.astype(BF16)
            continue
        step = d // dp
        for c in range(d):
            for k in range(nchunk):
                rows = pl.ds(c // dp, tm // d, stride=step)
                piece = xs_ref[k, rows, :] if dp == 1 else mid_ref[k, c % dp, rows, :]
                o_ref[0, c, 0, :, k * LANES:(k + 1) * LANES] = piece.astype(BF16)
                if d == A_MID_DILATION:
                    mid_ref[k, c] = piece
        dp = d


def _norm_perm_a(h, gain):
    tm = 512
    per_tile = A_TILE // tm
    shapes, specs = [], []
    for _, d in A_PATTERNS:
        if d == 1:
            shapes.append(jax.ShapeDtypeStruct((TOKENS, D_MODEL), BF16))
            specs.append(pl.BlockSpec((tm, D_MODEL), lambda i: (i, 0)))
        else:
            shapes.append(jax.ShapeDtypeStruct((TOKENS // A_TILE, d, per_tile, tm // d, D_MODEL), BF16))
            specs.append(pl.BlockSpec((1, d, 1, tm // d, D_MODEL),
                                      lambda i: (i // per_tile, 0, i % per_tile, 0, 0)))
    outs = pl.pallas_call(
        functools.partial(_norm_perm_a_kernel, tm=tm),
        out_shape=tuple(shapes),
        grid=(TOKENS // tm,),
        in_specs=[pl.BlockSpec((tm, D_MODEL), lambda i: (i, 0)), pl.BlockSpec((1, D_MODEL), lambda i: (0, 0))],
        out_specs=tuple(specs),
        scratch_shapes=[pltpu.VMEM((D_MODEL // LANES, tm, LANES), F32),
                        pltpu.VMEM((D_MODEL // LANES, A_MID_DILATION, tm // A_MID_DILATION, LANES), F32)],
        compiler_params=_cparams("parallel"),
        name="norm_perm_a",
    )(h, gain.reshape(1, D_MODEL))
    return [o.reshape(TOKENS, D_MODEL) for o in outs]


def _qkv_w_kernel(w_ref, o_ref):
    part = pl.program_id(1) % 3

    @pl.when(part == 2)
    def _():
        o_ref[...] = w_ref[...].astype(BF16)

    @pl.when(part != 2)
    def _():
        lane = lax.broadcasted_iota(jnp.int32, (1, LANES), 1)
        stay = (lane < A_ROT_HALF) | (lane >= ROPE_PAIR + A_ROT_HALF)
        for hh in range(A_HEADS):
            sl = slice(hh * A_HEAD_DIM, (hh + 1) * A_HEAD_DIM)
            y = w_ref[:, sl]
            moved = jnp.where(lane < ROPE_PAIR, pltpu.roll(y, LANES - A_ROT_HALF, 1),
                              pltpu.roll(y, ROPE_PAIR - A_ROT_HALF, 1))
            o_ref[:, sl] = jnp.where(stay, y, moved).astype(BF16)


def _qkv_w(w_qkv):
    rows = 256
    hd = A_HEADS * A_HEAD_DIM
    spec = pl.BlockSpec((rows, hd), lambda i, j: (i, j))
    return pl.pallas_call(
        _qkv_w_kernel,
        out_shape=jax.ShapeDtypeStruct(w_qkv.shape, BF16),
        grid=(D_MODEL // rows, w_qkv.shape[1] // hd),
        in_specs=[spec],
        out_specs=spec,
        compiler_params=_cparams("parallel", "parallel"),
        name="qkv_w",
    )(w_qkv)


def _qkv_a_kernel(x_ref, w_ref, cos_ref, sin_ref, o_ref, *, tn):
    part = pl.program_id(1) // ((A_HEADS * A_HEAD_DIM) // tn)
    scale = jnp.where(part == 0, A_HEAD_DIM ** -0.5, 1.0).astype(F32)
    half = x_ref.shape[0] // 2
    for cc in range(tn // MXU_COLS):
        for rows in (slice(0, half), slice(half, 2 * half)):
            y = _dot(x_ref[rows, :], w_ref[:, cc * MXU_COLS:(cc + 1) * MXU_COLS])
            for hh in range(MXU_COLS // LANES):
                sl = slice(cc * MXU_COLS + hh * LANES, cc * MXU_COLS + (hh + 1) * LANES)
                y_h = y[:, hh * LANES:(hh + 1) * LANES]
                o_ref[rows, sl] = (_rope128(y_h, cos_ref[rows, :], sin_ref[rows, :]) * scale).astype(BF16)


def _qkv_a(xn, w_qkv, tabs, group):
    tm, tn = 1024, 1024
    cols = 3 * A_HEADS * A_HEAD_DIM
    nj = cols // tn
    ni = TOKENS // tm
    v_first = 2 * (A_HEADS * A_HEAD_DIM) // tn
    tab_spec = pl.BlockSpec((tm, LANES),
                            lambda i, j: (jnp.where(j >= v_first, len(A_PATTERNS) * ni, group * ni + i), 0))
    return pl.pallas_call(
        functools.partial(_qkv_a_kernel, tn=tn),
        out_shape=jax.ShapeDtypeStruct((TOKENS, cols), BF16),
        grid=(ni, nj),
        in_specs=[
            pl.BlockSpec((tm, D_MODEL), lambda i, j: (i, 0)),
            pl.BlockSpec((D_MODEL, tn), lambda i, j: (0, group * nj + j)),
            tab_spec, tab_spec,
        ],
        out_specs=pl.BlockSpec((tm, tn), lambda i, j: (i, j)),
        compiler_params=_cparams("parallel", "parallel"),
        name="qkv_a",
    )(xn, w_qkv, *tabs)


def _attn_a_kernel(q_ref, kc_ref, kp_ref, vc_ref, vp_ref, o_ref, lse_ref, *scratch, d, heads, per_trip):
    rows = A_TILE // d
    nblk = rows // A_BLOCK
    two_stage = d > A_MID_DILATION
    if two_stage:
        os_ref, mid_ref = scratch
        stage_ref, mid_step = mid_ref, d // A_MID_DILATION
    else:
        (os_ref,) = scratch
        stage_ref = os_ref
    first_tile = pl.program_id(1) == 0
    r = lax.broadcasted_iota(jnp.int32, (A_BLOCK, 2 * A_BLOCK), 0)
    kk = lax.broadcasted_iota(jnp.int32, (A_BLOCK, 2 * A_BLOCK), 1)
    band = (kk >= r) & (kk <= r + A_BLOCK)
    lane = lax.broadcasted_iota(jnp.int32, (A_BLOCK, LANES), 1)
    hs = [slice(h * A_HEAD_DIM, (h + 1) * A_HEAD_DIM) for h in range(heads)]
    first_head = pl.program_id(2) * heads

    @pl.when(first_head == 0)
    def _():
        lse_ref[...] = jnp.zeros_like(lse_ref)

    def trip(t, carry):
        loaded = []
        for i in range(per_trip):
            u = t * per_trip + i
            c = u // nblk
            blk = u % nblk
            start = pl.multiple_of(blk * A_BLOCK, A_BLOCK)
            before = pl.multiple_of(jnp.maximum(start - A_BLOCK, 0), A_BLOCK)
            q = q_ref[0, c, pl.ds(start, A_BLOCK), :]
            k = jnp.concatenate([jnp.where(blk == 0, kp_ref[0, c], kc_ref[0, c, pl.ds(before, A_BLOCK), :]),
                                 kc_ref[0, c, pl.ds(start, A_BLOCK), :]], axis=0)
            v = jnp.concatenate([jnp.where(blk == 0, vp_ref[0, c], vc_ref[0, c, pl.ds(before, A_BLOCK), :]),
                                 vc_ref[0, c, pl.ds(start, A_BLOCK), :]], axis=0)
            no_prev = jnp.logical_and(first_tile, blk == 0)
            mask = jnp.logical_and(band, jnp.logical_or(kk >= A_BLOCK, jnp.logical_not(no_prev)))
            out_rows = (pl.ds(blk * (A_BLOCK * d) + c, A_BLOCK, stride=d) if d > 1
                        else pl.ds(start, A_BLOCK))
            if two_stage:
                o_at = (c % A_MID_DILATION,
                        pl.ds(blk * (A_BLOCK * mid_step) + c // A_MID_DILATION, A_BLOCK, stride=mid_step))
            else:
                o_at = (out_rows,)
            loaded.append(([_dot_t(q[:, sl], k[:, sl]) for sl in hs], v, mask, out_rows, o_at))
        for scores, v, mask, out_rows, o_at in loaded:
            lse_tile = lse_ref[out_rows, :]
            for h, sl in enumerate(hs):
                s = jnp.where(mask, scores[h], NEG_INF)
                m = jnp.max(s, axis=-1, keepdims=True)
                p = jnp.exp(s - m)
                l = jnp.sum(p, axis=-1, keepdims=True)
                stage_ref[(h,) + o_at + (slice(None),)] = _dot(p.astype(BF16), v[:, sl]) / l
                lse_tile = jnp.where(lane == first_head + h, m + jnp.log(l), lse_tile)
            lse_ref[out_rows, :] = lse_tile
        return carry

    lax.fori_loop(0, d * nblk // per_trip, trip, 0)
    if two_stage:
        for h in range(heads):
            for c4 in range(A_MID_DILATION):
                os_ref[h, pl.ds(c4, A_TILE // A_MID_DILATION, stride=A_MID_DILATION), :] = mid_ref[h, c4]
    o_ref[...] = os_ref[...].astype(BF16)


def _attn_a(qkv, window, dilation):
    assert window // dilation == A_BLOCK, "band width must equal the query block"
    d = dilation
    heads = 4
    hw = heads * A_HEAD_DIM
    ngrp = A_HEADS // heads
    rows = A_TILE // d
    tiles = SEQ // A_TILE
    qkv_v = qkv.reshape(BATCH, tiles, d, rows, 3 * A_HEADS * A_HEAD_DIM)
    last_blk = rows // A_BLOCK - 1

    def cur(part):
        return pl.BlockSpec((None, 1, d, rows, hw), lambda b, n, g: (b, n, 0, 0, part * ngrp + g))

    def prev(part):
        return pl.BlockSpec((None, 1, d, A_BLOCK, hw),
                            lambda b, n, g: (b, jnp.maximum(n - 1, 0), 0, last_blk, part * ngrp + g))

    return pl.pallas_call(
        functools.partial(_attn_a_kernel, d=d, heads=heads, per_trip=4),
        out_shape=(jax.ShapeDtypeStruct((A_HEADS, TOKENS, A_HEAD_DIM), BF16),
                   jax.ShapeDtypeStruct((TOKENS, LANES), F32)),
        grid=(BATCH, tiles, ngrp),
        in_specs=[cur(0), cur(1), prev(1), cur(2), prev(2)],
        out_specs=(pl.BlockSpec((heads, A_TILE, A_HEAD_DIM), lambda b, n, g: (g, b * tiles + n, 0)),
                   pl.BlockSpec((A_TILE, LANES), lambda b, n, g: (b * tiles + n, 0))),
        scratch_shapes=[pltpu.VMEM((heads, A_TILE, A_HEAD_DIM), F32)] + (
            [pltpu.VMEM((heads, A_MID_DILATION, A_TILE // A_MID_DILATION, A_HEAD_DIM), F32)]
            if d > A_MID_DILATION else []),
        compiler_params=_cparams("parallel", "parallel", "arbitrary"),
        name="attn_a_d%d" % d,
    )(qkv_v, qkv_v, qkv_v, qkv_v, qkv_v)


def _comb_proj_a_kernel(o0, o1, o2, l0, l1, l2, w_ref, r_ref, out_ref, comb_ref):
    s = pl.program_id(0)
    slot = s % 2

    @pl.when(s == 0)
    def _():
        comb_ref[1] = jnp.zeros(comb_ref.shape[1:], BF16)

    out_ref[...] = r_ref[...] + _dot(comb_ref[1 - slot], w_ref[...])
    a = [l0[...], l1[...], l2[...]]
    m = jnp.maximum(jnp.maximum(a[0], a[1]), a[2])
    e = [jnp.exp(t - m) for t in a]
    den = e[0] + e[1] + e[2]
    w = [t / den for t in e]
    rows = out_ref.shape[0]
    for h in range(A_HEADS):
        acc = jnp.zeros((rows, A_HEAD_DIM), F32)
        for wg, og in zip(w, (o0, o1, o2)):
            acc = acc + jnp.broadcast_to(wg[:, h:h + 1], (rows, A_HEAD_DIM)) * og[h]
        comb_ref[slot, :, h * A_HEAD_DIM:(h + 1) * A_HEAD_DIM] = acc.astype(BF16)


def _comb_proj_a(outs, lses, w_o, res):
    tm = 256
    nt = TOKENS // tm
    hd = A_HEADS * A_HEAD_DIM
    o_spec = pl.BlockSpec((A_HEADS, tm, A_HEAD_DIM), lambda s: (0, jnp.minimum(s, nt - 1), 0))
    l_spec = pl.BlockSpec((tm, LANES), lambda s: (jnp.minimum(s, nt - 1), 0))
    row_spec = pl.BlockSpec((tm, D_MODEL), lambda s: (jnp.maximum(s - 1, 0), 0))
    return pl.pallas_call(
        _comb_proj_a_kernel,
        out_shape=jax.ShapeDtypeStruct((TOKENS, D_MODEL), F32),
        grid=(nt + 1,),
        in_specs=[o_spec] * 3 + [l_spec] * 3 + [pl.BlockSpec((hd, D_MODEL), lambda s: (0, 0)), row_spec],
        out_specs=row_spec,
        scratch_shapes=[pltpu.VMEM((2, tm, hd), BF16)],
        compiler_params=_cparams("arbitrary"),
        name="comb_proj_a",
    )(*outs, *lses, w_o, res)


def _mixer_a(h, gain, w_qkv, w_o, tabs_a):
    outs, lses = [], []
    for g, ((window, dilation), xn) in enumerate(zip(A_PATTERNS, _norm_perm_a(h, gain))):
        o, lse = _attn_a(_qkv_a(xn, w_qkv, tabs_a, g), window, dilation)
        outs.append(o)
        lses.append(lse)
    return _comb_proj_a(outs, lses, w_o, h)


def _b_in_kernel(x_ref, g_ref, w_ref, o_ref):
    o_ref[...] = _dot(_rms(x_ref[...], g_ref[...]).astype(BF16), w_ref[...])


def _b_in(h, gain, w_in):
    tm = 512
    return pl.pallas_call(
        _b_in_kernel,
        out_shape=jax.ShapeDtypeStruct((TOKENS, B_IN_COLS), F32),
        grid=(TOKENS // tm,),
        in_specs=[
            pl.BlockSpec((tm, D_MODEL), lambda i: (i, 0)),
            pl.BlockSpec((1, D_MODEL), lambda i: (0, 0)),
            pl.BlockSpec((D_MODEL, B_IN_COLS), lambda i: (0, 0)),
        ],
        out_specs=pl.BlockSpec((tm, B_IN_COLS), lambda i: (i, 0)),
        compiler_params=_cparams("parallel"),
        name="b_in",
    )(h, gain.reshape(1, D_MODEL), w_in)


def _b_q_kernel(c_ref, g_ref, w_ref, cos_ref, sin_ref, o_ref, xn_ref, *, slots):
    @pl.when(pl.program_id(1) == 0)
    def _():
        xn_ref[...] = _rms(c_ref[...], g_ref[...]).astype(BF16)

    y = _dot(xn_ref[...], w_ref[...])
    scale = (B_NOPE_DIM + B_ROPE_DIM) ** -0.5
    cos, sin = cos_ref[...], sin_ref[...]
    for s in range(slots):
        nope = y[:, s * B_SLOT:s * B_SLOT + LANES]
        rot = _rope128(y[:, s * B_SLOT + LANES:(s + 1) * B_SLOT], cos, sin)
        o_ref[s, :, :LANES] = (nope * scale).astype(BF16)
        o_ref[s, :, LANES:] = (rot * scale).astype(BF16)


def _b_q(c, gain, w_q, tabs_b):
    tm, slots = 1024, 4
    tn = slots * B_SLOT
    tab_spec = pl.BlockSpec((tm, LANES), lambda i, j: (i, 0))
    return pl.pallas_call(
        functools.partial(_b_q_kernel, slots=slots),
        out_shape=jax.ShapeDtypeStruct((B_HEADS, TOKENS, B_SLOT), BF16),
        grid=(TOKENS // tm, B_HEADS // slots),
        in_specs=[
            pl.BlockSpec((tm, B_Q_RANK), lambda i, j: (i, 0)),
            pl.BlockSpec((1, B_Q_RANK), lambda i, j: (0, 0)),
            pl.BlockSpec((B_Q_RANK, tn), lambda i, j: (0, j)),
            tab_spec, tab_spec,
        ],
        out_specs=pl.BlockSpec((slots, tm, B_SLOT), lambda i, j: (j, i, 0)),
        scratch_shapes=[pltpu.VMEM((tm, B_Q_RANK), BF16)],
        compiler_params=_cparams("parallel", "arbitrary"),
        name="b_q",
    )(c, gain.reshape(1, B_Q_RANK), w_q, *tabs_b)


def _b_kv_kernel(c_ref, g_ref, wk_ref, wv_ref, cr_ref, cos_ref, sin_ref, k_ref, v_ref, *, slots):
    xn = _rms(c_ref[...], g_ref[...]).astype(BF16)
    yk = _dot(xn, wk_ref[...])
    yv = _dot(xn, wv_ref[...])
    kr = _rope128(cr_ref[...], cos_ref[...], sin_ref[...]).astype(BF16)
    lane = lax.broadcasted_iota(jnp.int32, (c_ref.shape[0], B_VX - B_V_DIM), 1)
    ones_col = jnp.where(lane == 0, 1.0, 0.0).astype(BF16)
    for s in range(slots):
        sl = slice(s * LANES, (s + 1) * LANES)
        k_ref[s, :, :LANES] = yk[:, sl].astype(BF16)
        k_ref[s, :, LANES:] = kr
        v_ref[s, :, :B_V_DIM] = yv[:, sl].astype(BF16)
        v_ref[s, :, B_V_DIM:] = ones_col


def _b_kv(c, gain, w_k, w_v, tabs_b):
    tm, slots = 1024, 4
    tn = slots * LANES
    kv_block = B_Q_RANK // B_KV_RANK
    rope_block = (B_Q_RANK + B_KV_RANK) // LANES
    tab_spec = pl.BlockSpec((tm, LANES), lambda i, j: (i, 0))
    return pl.pallas_call(
        functools.partial(_b_kv_kernel, slots=slots),
        out_shape=(jax.ShapeDtypeStruct((B_HEADS, TOKENS, B_SLOT), BF16),
                   jax.ShapeDtypeStruct((B_HEADS, TOKENS, B_VX), BF16)),
        grid=(TOKENS // tm, B_HEADS // slots),
        in_specs=[
            pl.BlockSpec((tm, B_KV_RANK), lambda i, j: (i, kv_block)),
            pl.BlockSpec((1, B_KV_RANK), lambda i, j: (0, 0)),
            pl.BlockSpec((B_KV_RANK, tn), lambda i, j: (0, j)),
            pl.BlockSpec((B_KV_RANK, tn), lambda i, j: (0, j)),
            pl.BlockSpec((tm, LANES), lambda i, j: (i, rope_block)),
            tab_spec, tab_spec,
        ],
        out_specs=(pl.BlockSpec((slots, tm, B_SLOT), lambda i, j: (j, i, 0)),
                   pl.BlockSpec((slots, tm, B_VX), lambda i, j: (j, i, 0))),
        compiler_params=_cparams("parallel", "parallel"),
        name="b_kv",
    )(c, gain.reshape(1, B_KV_RANK), w_k, w_v, c, *tabs_b)


def _attn_b_kernel(q_ref, k_ref, v_ref, o_ref, s0_ref, s1_ref, m_ref, acc_ref, *, tq, heads):
    qi = pl.program_id(2)
    m_ref[...] = jnp.full_like(m_ref, NEG_INF)
    acc_ref[...] = jnp.zeros_like(acc_ref)

    def scores(s_ref, ki):
        for h in range(heads):
            k = k_ref[h, pl.ds(pl.multiple_of(ki * tq, tq), tq), :]
            s_ref[h] = _dot_t(q_ref[h], k)

    def fold(s_ref, ki, diagonal):
        for h in range(heads):
            s = s_ref[h]
            if diagonal:
                row = lax.broadcasted_iota(jnp.int32, (tq, tq), 0)
                col = lax.broadcasted_iota(jnp.int32, (tq, tq), 1)
                s = jnp.where(row >= col, s, NEG_INF)
            m_prev = m_ref[h]
            m_new = jnp.maximum(m_prev, jnp.max(s, axis=-1, keepdims=True))
            alpha = jnp.exp(m_prev - m_new)
            p = jnp.exp((s - jnp.concatenate([m_new] * (tq // LANES), axis=1)).astype(BF16))
            v = v_ref[h, pl.ds(pl.multiple_of(ki * tq, tq), tq), :]
            acc_ref[h] = jnp.concatenate([alpha] * (B_VX // LANES), axis=1) * acc_ref[h] + _dot(p, v)
            m_ref[h] = m_new

    pairs = qi // 2
    scores(s0_ref, 0)

    def body(t, carry):
        scores(s1_ref, 2 * t + 1)
        fold(s0_ref, 2 * t, False)
        scores(s0_ref, 2 * t + 2)
        fold(s1_ref, 2 * t + 1, False)
        return carry

    lax.fori_loop(0, pairs, body, 0)

    @pl.when(qi % 2 == 1)
    def _():
        scores(s1_ref, qi)
        fold(s0_ref, qi - 1, False)
        fold(s1_ref, qi, True)

    @pl.when(qi % 2 == 0)
    def _():
        fold(s0_ref, qi, True)

    for h in range(heads):
        acc = acc_ref[h]
        l = jnp.broadcast_to(acc[:, B_V_DIM:B_V_DIM + 1], (tq, B_V_DIM))
        o_ref[:, h * B_V_DIM:(h + 1) * B_V_DIM] = (acc[:, :B_V_DIM] / l).astype(BF16)


def _attn_b(q3, k3, v3):
    tq, heads = 512, 2
    nq = SEQ // tq
    return pl.pallas_call(
        functools.partial(_attn_b_kernel, tq=tq, heads=heads),
        out_shape=jax.ShapeDtypeStruct((TOKENS, B_HEADS * B_V_DIM), BF16),
        grid=(BATCH, B_HEADS // heads, nq),
        in_specs=[
            pl.BlockSpec((heads, tq, B_SLOT), lambda b, g, qi: (g, b * nq + qi, 0)),
            pl.BlockSpec((heads, SEQ, B_SLOT), lambda b, g, qi: (g, b, 0)),
            pl.BlockSpec((heads, SEQ, B_VX), lambda b, g, qi: (g, b, 0)),
        ],
        out_specs=pl.BlockSpec((tq, heads * B_V_DIM), lambda b, g, qi: (b * nq + qi, g)),
        scratch_shapes=[pltpu.VMEM((heads, tq, tq), F32), pltpu.VMEM((heads, tq, tq), F32),
                        pltpu.VMEM((heads, tq, LANES), F32), pltpu.VMEM((heads, tq, B_VX), F32)],
        compiler_params=_cparams("parallel", "parallel", "arbitrary"),
        name="attn_b",
    )(q3, k3, v3)


def _mixer_b(h, gain, w_in, q_norm, w_q, kv_norm, w_k, w_v, w_o, tabs_b):
    c = _b_in(h, gain, w_in)
    q3 = _b_q(c, q_norm, w_q, tabs_b)
    k3, v3 = _b_kv(c, kv_norm, w_k, w_v, tabs_b)
    return _out_proj(_attn_b(q3, k3, v3), w_o, h)


def _c_in_kernel(x_ref, g_ref, wb_ref, wc_ref, wu_ref, cw_ref, o_ref, xn_ref, carry_ref, *, tm):
    i = pl.program_id(0)
    j = pl.program_id(1)

    @pl.when(j == 0)
    def _():
        xn_ref[...] = _rms(x_ref[...], g_ref[...]).astype(BF16)

    xn = xn_ref[...]
    b_gate = _dot(xn, wb_ref[...])
    cu = _dot(xn, wc_ref[...]) * _dot(xn, wu_ref[...])
    w0, w1, w2 = cw_ref[0:1, :], cw_ref[1:2, :], cw_ref[2:3, :]
    z = w2 * cu + w1 * pltpu.roll(cu, 1, 0) + w0 * pltpu.roll(cu, 2, 0)
    o_ref[...] = (b_gate * z).astype(BF16)

    first = (i * tm) % SEQ == 0
    tail = jnp.where(first, 0.0, carry_ref[j])
    both = jnp.concatenate([tail, cu[0:8, :]], axis=0)
    z8 = w2 * both[8:16] + w1 * both[7:15] + w0 * both[6:14]
    o_ref[0:8, :] = (b_gate[0:8, :] * z8).astype(BF16)
    carry_ref[j] = cu[tm - 8:tm, :]


def _c_in(h, gain, w_in, conv_w):
    tm, tn = 1024, 512
    nj = D_MODEL // tn
    return pl.pallas_call(
        functools.partial(_c_in_kernel, tm=tm),
        out_shape=jax.ShapeDtypeStruct((TOKENS, D_MODEL), BF16),
        grid=(TOKENS // tm, nj),
        in_specs=[
            pl.BlockSpec((tm, D_MODEL), lambda i, j: (i, 0)),
            pl.BlockSpec((1, D_MODEL), lambda i, j: (0, 0)),
            pl.BlockSpec((D_MODEL, tn), lambda i, j: (0, j)),
            pl.BlockSpec((D_MODEL, tn), lambda i, j: (0, j + nj)),
            pl.BlockSpec((D_MODEL, tn), lambda i, j: (0, j + 2 * nj)),
            pl.BlockSpec((C_CONV, tn), lambda i, j: (0, j)),
        ],
        out_specs=pl.BlockSpec((tm, tn), lambda i, j: (i, j)),
        scratch_shapes=[pltpu.VMEM((tm, D_MODEL), BF16), pltpu.VMEM((nj, 8, tn), F32)],
        compiler_params=_cparams("arbitrary", "arbitrary"),
        name="c_in",
    )(h, gain.reshape(1, D_MODEL), w_in, w_in, w_in, conv_w)


def _mixer_c(h, gain, w_in, conv_w, w_out):
    return _out_proj(_c_in(h, gain, w_in, conv_w), w_out, h)


def kernel(x, positions, l0_ffn1_norm, l0_ffn1_w13, l0_ffn1_w2, l0_mix_norm, l0_a_w_qkv, l0_a_w_o, l0_ffn2_norm, l0_ffn2_w13, l0_ffn2_w2, l1_ffn1_norm, l1_ffn1_w13, l1_ffn1_w2, l1_mix_norm, l1_b_w_in, l1_b_q_norm, l1_b_w_uq, l1_b_kv_norm, l1_b_w_ukv, l1_b_w_o, l1_ffn2_norm, l1_ffn2_w13, l1_ffn2_w2, l2_ffn1_norm, l2_ffn1_w13, l2_ffn1_w2, l2_mix_norm, l2_c_w_in, l2_c_conv_w, l2_c_w_out, l2_ffn2_norm, l2_ffn2_w13, l2_ffn2_w2, l3_ffn1_norm, l3_ffn1_w13, l3_ffn1_w2, l3_mix_norm, l3_a_w_qkv, l3_a_w_o, l3_ffn2_norm, l3_ffn2_w13, l3_ffn2_w2, final_norm):
    bf = lambda w: w.astype(BF16)
    pos_tiles = positions.reshape(TOKENS // A_TILE, A_TILE)
    pos_a = [pos_tiles.reshape(-1, A_TILE // d, d).swapaxes(1, 2).reshape(TOKENS) for _, d in A_PATTERNS]
    tabs_a = _rope_tables(jnp.concatenate(pos_a + [jnp.zeros((A_TILE,), positions.dtype)]), A_ROT_HALF)
    tabs_b = _rope_tables(positions.reshape(TOKENS), B_ROT_HALF)

    def rope_group(w):
        return _rope_lanes(jnp.pad(w, [(0, 0)] * (w.ndim - 1) + [(0, LANES - B_ROPE_DIM)]), B_ROT_HALF)

    w_uq = l1_b_w_uq.reshape(B_Q_RANK, B_HEADS, B_NOPE_DIM + B_ROPE_DIM)
    w_q = jnp.concatenate([w_uq[..., :B_NOPE_DIM], rope_group(w_uq[..., B_NOPE_DIM:])], axis=-1)
    w_q = bf(w_q.reshape(B_Q_RANK, B_HEADS * B_SLOT))
    w_ukv = l1_b_w_ukv.reshape(B_KV_RANK, B_HEADS, B_NOPE_DIM + B_V_DIM)
    w_k = bf(w_ukv[:, :, :B_NOPE_DIM].reshape(B_KV_RANK, B_HEADS * B_NOPE_DIM))
    w_v = bf(w_ukv[:, :, B_NOPE_DIM:].reshape(B_KV_RANK, B_HEADS * B_V_DIM))
    n_c = B_Q_RANK + B_KV_RANK
    w_b_in = bf(jnp.concatenate([l1_b_w_in[:, :n_c], rope_group(l1_b_w_in[:, n_c:])], axis=-1))

    h = x.reshape(TOKENS, D_MODEL)

    h, w13, w2 = _ffn(h, l0_ffn1_norm, bf(l0_ffn1_w13), bf(l0_ffn1_w2), next_f32=(l0_ffn2_w13, l0_ffn2_w2))
    h = _mixer_a(h, l0_mix_norm, _qkv_w(l0_a_w_qkv), bf(l0_a_w_o), tabs_a)
    h, w13, w2 = _ffn(h, l0_ffn2_norm, w13, w2, next_f32=(l1_ffn1_w13, l1_ffn1_w2))

    h, w13, w2 = _ffn(h, l1_ffn1_norm, w13, w2, next_f32=(l1_ffn2_w13, l1_ffn2_w2))
    h = _mixer_b(h, l1_mix_norm, w_b_in, l1_b_q_norm, w_q, l1_b_kv_norm, w_k, w_v, bf(l1_b_w_o), tabs_b)
    h, w13, w2 = _ffn(h, l1_ffn2_norm, w13, w2, next_f32=(l2_ffn1_w13, l2_ffn1_w2))

    h, w13, w2 = _ffn(h, l2_ffn1_norm, w13, w2, next_f32=(l2_ffn2_w13, l2_ffn2_w2))
    h = _mixer_c(h, l2_mix_norm, bf(l2_c_w_in), l2_c_conv_w, bf(l2_c_w_out))
    h, w13, w2 = _ffn(h, l2_ffn2_norm, w13, w2, next_f32=(l3_ffn1_w13, l3_ffn1_w2))

    h, w13, w2 = _ffn(h, l3_ffn1_norm, w13, w2, next_f32=(l3_ffn2_w13, l3_ffn2_w2))
    h = _mixer_a(h, l3_mix_norm, _qkv_w(l3_a_w_qkv), bf(l3_a_w_o), tabs_a)
    h = _ffn(h, l3_ffn2_norm, w13, w2, final_gain=final_norm)

    return h.reshape(BATCH, SEQ, D_MODEL)
```

```python
import functools

import jax
import jax.numpy as jnp
from jax import lax
from jax.experimental import pallas as pl
from jax.experimental.pallas import tpu as pltpu

F32 = jnp.float32
BF16 = jnp.bfloat16

D_MODEL = 2048
BATCH = 2
SEQ = 8192
TOKENS = BATCH * SEQ
D_FF = 5632
FFN_TF = 512
HALF_STEP = 0.5
NORM_EPS = 1e-6
NEG_INF = -1e30
ROPE_THETA = 500000.0

A_HEADS = 16
A_HEAD_DIM = 128
A_ROT_HALF = A_HEAD_DIM // 8
A_PATTERNS = ((128, 1), (512, 4), (2048, 16))
A_BLOCK = 128
A_TILE = 2048
A_MID_DILATION = 4
assert [d for _, d in A_PATTERNS] == [1, A_MID_DILATION, A_MID_DILATION ** 2]

B_HEADS = 16
B_Q_RANK = 1536
B_KV_RANK = 512
B_NOPE_DIM = 128
B_ROPE_DIM = 64
B_ROT_HALF = B_ROPE_DIM // 2
B_V_DIM = 128
B_SLOT = 256
B_VX = 256
B_IN_COLS = 2176

C_CONV = 3

LANES = 128
MXU_COLS = 256
ROPE_PAIR = LANES // 2
VMEM_LIMIT = 56 * 1024 * 1024


def _cparams(*sem):
    return pltpu.CompilerParams(dimension_semantics=sem, vmem_limit_bytes=VMEM_LIMIT)


def _rms(x, gain):
    ms = jnp.mean(x * x, axis=-1, keepdims=True)
    return x * lax.rsqrt(ms + NORM_EPS) * gain


def _dot(a, b):
    return jnp.dot(a, b, preferred_element_type=F32)


def _dot_t(a, b):
    return lax.dot_general(a, b, (((1,), (1,)), ((), ())), preferred_element_type=F32)


def _rope128(y, cos, sin):
    return y * cos + pltpu.roll(y, ROPE_PAIR, 1) * sin


def _rope_lanes(w, half):
    return jnp.concatenate([w[..., :half], w[..., 2 * half:ROPE_PAIR + half], w[..., half:2 * half],
                            w[..., ROPE_PAIR + half:]], axis=-1)


def _rope_tab_kernel(pos_ref, inv_ref, c_ref, s_ref):
    ang = pos_ref[...].astype(F32) * inv_ref[...]
    lane = lax.broadcasted_iota(jnp.int32, (1, LANES), 1)
    sn = jnp.sin(ang)
    c_ref[...] = jnp.cos(ang)
    s_ref[...] = jnp.where(lane < ROPE_PAIR, -sn, sn)


def _rope_tables(pos_flat, half):
    inv = ROPE_THETA ** (-jnp.arange(half, dtype=F32) / half)
    gap = jnp.zeros((ROPE_PAIR - half,), F32)
    inv = jnp.concatenate([inv, gap, inv, gap]).reshape(1, LANES)
    n = pos_flat.shape[0]
    rows = 2048
    tab = jax.ShapeDtypeStruct((n, LANES), F32)
    row_spec = pl.BlockSpec((rows, LANES), lambda i: (i, 0))
    return pl.pallas_call(
        _rope_tab_kernel,
        out_shape=(tab,) * 2,
        grid=(n // rows,),
        in_specs=[pl.BlockSpec((rows, 1), lambda i: (i, 0)), pl.BlockSpec((1, LANES), lambda i: (0, 0))],
        out_specs=(row_spec,) * 2,
        compiler_params=_cparams("parallel"),
        name="rope_tables",
    )(pos_flat.reshape(n, 1), inv)


def _ffn_kernel(x_ref, g_ref, w1_ref, w3_ref, w2_ref, fg_ref, *refs, final, cast_next):
    j = pl.program_id(1)
    if cast_next:
        n13_ref, n2_ref, o_ref, c13_ref, c2_ref, xn_ref, acc_ref = refs
    else:
        o_ref, xn_ref, acc_ref = refs

    @pl.when(j == 0)
    def _():
        xn_ref[...] = _rms(x_ref[...], g_ref[...]).astype(BF16)
        acc_ref[...] = jnp.zeros_like(acc_ref)

    xn = xn_ref[...]
    gate = _dot(xn, w1_ref[...])
    up = _dot(xn, w3_ref[...])
    act = (gate * (1.0 / (1.0 + jnp.exp(-gate))) * up).astype(BF16)
    acc_ref[...] += _dot(act, w2_ref[...])
    if cast_next:
        for t in range(c13_ref.shape[0]):
            c13_ref[t] = n13_ref[:, t * FFN_TF:(t + 1) * FFN_TF].astype(BF16)
        c2_ref[...] = n2_ref[...].astype(BF16)

    @pl.when(j == pl.num_programs(1) - 1)
    def _():
        y = x_ref[...] + HALF_STEP * acc_ref[...]
        if final:
            y = _rms(y, fg_ref[...])
        o_ref[...] = y


def _ffn(h, gain, w13, w2, final_gain=None, next_f32=None):
    tm, tf = 512, FFN_TF
    nj = D_FF // tf
    ni = TOKENS // tm
    final = final_gain is not None
    fg = final_gain if final else gain
    in_specs = [
        pl.BlockSpec((tm, D_MODEL), lambda i, j: (i, 0)),
        pl.BlockSpec((1, D_MODEL), lambda i, j: (0, 0)),
        pl.BlockSpec((None, D_MODEL, tf), lambda i, j: (j, 0, 0)),
        pl.BlockSpec((None, D_MODEL, tf), lambda i, j: (j + nj, 0, 0)),
        pl.BlockSpec((tf, D_MODEL), lambda i, j: (j, 0)),
        pl.BlockSpec((1, D_MODEL), lambda i, j: (0, 0)),
    ]
    out_shape = [jax.ShapeDtypeStruct((TOKENS, D_MODEL), F32)]
    out_specs = [pl.BlockSpec((tm, D_MODEL), lambda i, j: (i, 0))]
    args = [h, gain.reshape(1, D_MODEL), w13, w13, w2, fg.reshape(1, D_MODEL)]
    if next_f32 is not None:
        rows13, tiles13 = D_MODEL // ni, 2 * D_FF // (nj * tf)
        slab2 = pl.BlockSpec((D_FF // (ni * nj), D_MODEL), lambda i, j: (i * nj + j, 0))
        in_specs += [pl.BlockSpec((rows13, tiles13 * tf), lambda i, j: (i, j)), slab2]
        out_specs += [pl.BlockSpec((tiles13, rows13, tf), lambda i, j: (j, i, 0)), slab2]
        out_shape += [jax.ShapeDtypeStruct(w13.shape, BF16), jax.ShapeDtypeStruct(w2.shape, BF16)]
        args += list(next_f32)
    outs = pl.pallas_call(
        functools.partial(_ffn_kernel, final=final, cast_next=next_f32 is not None),
        out_shape=tuple(out_shape),
        grid=(ni, nj),
        in_specs=in_specs,
        out_specs=tuple(out_specs),
        scratch_shapes=[pltpu.VMEM((tm, D_MODEL), BF16), pltpu.VMEM((tm, D_MODEL), F32)],
        compiler_params=_cparams("parallel", "arbitrary"),
        name="ffn",
    )(*args)
    return outs if next_f32 is not None else outs[0]


def _out_proj_kernel(x_ref, w_ref, r_ref, o_ref):
    o_ref[...] = r_ref[...] + _dot(x_ref[...], w_ref[...])


def _out_proj(x, w, res):
    k = x.shape[1]
    tm = 512
    return pl.pallas_call(
        _out_proj_kernel,
        out_shape=jax.ShapeDtypeStruct((TOKENS, D_MODEL), F32),
        grid=(TOKENS // tm,),
        in_specs=[
            pl.BlockSpec((tm, k), lambda i: (i, 0)),
            pl.BlockSpec((k, D_MODEL), lambda i: (0, 0)),
            pl.BlockSpec((tm, D_MODEL), lambda i: (i, 0)),
        ],
        out_specs=pl.BlockSpec((tm, D_MODEL), lambda i: (i, 0)),
        compiler_params=_cparams("parallel"),
        name="out_proj",
    )(x, w, res)


def _norm_perm_a_kernel(x_ref, g_ref, *refs, tm):
    o_refs, xs_ref, mid_ref = refs[:-2], refs[-2], refs[-1]
    xn = _rms(x_ref[...], g_ref[...])
    nchunk = D_MODEL // LANES
    for k in range(nchunk):
        xs_ref[k] = xn[:, k * LANES:(k + 1) * LANES]
    dp = 1
    for (_, d), o_ref in zip(A_PATTERNS, o_refs):
        if d == 1:
            o_ref[...] = xn.astype(BF16)
            continue
        step = d // dp
        for c in range(d):
            for k in range(nchunk):
                rows = pl.ds(c // dp, tm // d, stride=step)
                piece = xs_ref[k, rows, :] if dp == 1 else mid_ref[k, c % dp, rows, :]
                o_ref[0, c, 0, :, k * LANES:(k + 1) * LANES] = piece.astype(BF16)
                if d == A_MID_DILATION:
                    mid_ref[k, c] = piece
        dp = d


def _norm_perm_a(h, gain):
    tm = 512
    per_tile = A_TILE // tm
    shapes, specs = [], []
    for _, d in A_PATTERNS:
        if d == 1:
            shapes.append(jax.ShapeDtypeStruct((TOKENS, D_MODEL), BF16))
            specs.append(pl.BlockSpec((tm, D_MODEL), lambda i: (i, 0)))
        else:
            shapes.append(jax.ShapeDtypeStruct((TOKENS // A_TILE, d, per_tile, tm // d, D_MODEL), BF16))
            specs.append(pl.BlockSpec((1, d, 1, tm // d, D_MODEL),
                                      lambda i: (i // per_tile, 0, i % per_tile, 0, 0)))
    outs = pl.pallas_call(
        functools.partial(_norm_perm_a_kernel, tm=tm),
        out_shape=tuple(shapes),
        grid=(TOKENS // tm,),
        in_specs=[pl.BlockSpec((tm, D_MODEL), lambda i: (i, 0)), pl.BlockSpec((1, D_MODEL), lambda i: (0, 0))],
        out_specs=tuple(specs),
        scratch_shapes=[pltpu.VMEM((D_MODEL // LANES, tm, LANES), F32),
                        pltpu.VMEM((D_MODEL // LANES, A_MID_DILATION, tm // A_MID_DILATION, LANES), F32)],
        compiler_params=_cparams("parallel"),
        name="norm_perm_a",
    )(h, gain.reshape(1, D_MODEL))
    return [o.reshape(TOKENS, D_MODEL) for o in outs]


def _qkv_w_kernel(w_ref, o_ref):
    part = pl.program_id(1) % 3

    @pl.when(part == 2)
    def _():
        o_ref[...] = w_ref[...].astype(BF16)

    @pl.when(part != 2)
    def _():
        lane = lax.broadcasted_iota(jnp.int32, (1, LANES), 1)
        stay = (lane < A_ROT_HALF) | (lane >= ROPE_PAIR + A_ROT_HALF)
        for hh in range(A_HEADS):
            sl = slice(hh * A_HEAD_DIM, (hh + 1) * A_HEAD_DIM)
            y = w_ref[:, sl]
            moved = jnp.where(lane < ROPE_PAIR, pltpu.roll(y, LANES - A_ROT_HALF, 1),
                              pltpu.roll(y, ROPE_PAIR - A_ROT_HALF, 1))
            o_ref[:, sl] = jnp.where(stay, y, moved).astype(BF16)


def _qkv_w(w_qkv):
    rows = 256
    hd = A_HEADS * A_HEAD_DIM
    spec = pl.BlockSpec((rows, hd), lambda i, j: (i, j))
    return pl.pallas_call(
        _qkv_w_kernel,
        out_shape=jax.ShapeDtypeStruct(w_qkv.shape, BF16),
        grid=(D_MODEL // rows, w_qkv.shape[1] // hd),
        in_specs=[spec],
        out_specs=spec,
        compiler_params=_cparams("parallel", "parallel"),
        name="qkv_w",
    )(w_qkv)


def _qkv_a_kernel(x_ref, w_ref, cos_ref, sin_ref, o_ref, *, tn):
    part = pl.program_id(1) // ((A_HEADS * A_HEAD_DIM) // tn)
    scale = jnp.where(part == 0, A_HEAD_DIM ** -0.5, 1.0).astype(F32)
    half = x_ref.shape[0] // 2
    for cc in range(tn // MXU_COLS):
        for rows in (slice(0, half), slice(half, 2 * half)):
            y = _dot(x_ref[rows, :], w_ref[:, cc * MXU_COLS:(cc + 1) * MXU_COLS])
            for hh in range(MXU_COLS // LANES):
                sl = slice(cc * MXU_COLS + hh * LANES, cc * MXU_COLS + (hh + 1) * LANES)
                y_h = y[:, hh * LANES:(hh + 1) * LANES]
                o_ref[rows, sl] = (_rope128(y_h, cos_ref[rows, :], sin_ref[rows, :]) * scale).astype(BF16)


def _qkv_a(xn, w_qkv, tabs, group):
    tm, tn = 1024, 1024
    cols = 3 * A_HEADS * A_HEAD_DIM
    nj = cols // tn
    ni = TOKENS // tm
    v_first = 2 * (A_HEADS * A_HEAD_DIM) // tn
    tab_spec = pl.BlockSpec((tm, LANES),
                            lambda i, j: (jnp.where(j >= v_first, len(A_PATTERNS) * ni, group * ni + i), 0))
    return pl.pallas_call(
        functools.partial(_qkv_a_kernel, tn=tn),
        out_shape=jax.ShapeDtypeStruct((TOKENS, cols), BF16),
        grid=(ni, nj),
        in_specs=[
            pl.BlockSpec((tm, D_MODEL), lambda i, j: (i, 0)),
            pl.BlockSpec((D_MODEL, tn), lambda i, j: (0, group * nj + j)),
            tab_spec, tab_spec,
        ],
        out_specs=pl.BlockSpec((tm, tn), lambda i, j: (i, j)),
        compiler_params=_cparams("parallel", "parallel"),
        name="qkv_a",
    )(xn, w_qkv, *tabs)


def _attn_a_kernel(q_ref, kc_ref, kp_ref, vc_ref, vp_ref, o_ref, lse_ref, *scratch, d, heads, per_trip):
    rows = A_TILE // d
    nblk = rows // A_BLOCK
    two_stage = d > A_MID_DILATION
    if two_stage:
        os_ref, mid_ref = scratch
        stage_ref, mid_step = mid_ref, d // A_MID_DILATION
    else:
        (os_ref,) = scratch
        stage_ref = os_ref
    first_tile = pl.program_id(1) == 0
    r = lax.broadcasted_iota(jnp.int32, (A_BLOCK, 2 * A_BLOCK), 0)
    kk = lax.broadcasted_iota(jnp.int32, (A_BLOCK, 2 * A_BLOCK), 1)
    band = (kk >= r) & (kk <= r + A_BLOCK)
    lane = lax.broadcasted_iota(jnp.int32, (A_BLOCK, LANES), 1)
    hs = [slice(h * A_HEAD_DIM, (h + 1) * A_HEAD_DIM) for h in range(heads)]
    first_head = pl.program_id(2) * heads

    @pl.when(first_head == 0)
    def _():
        lse_ref[...] = jnp.zeros_like(lse_ref)

    def trip(t, carry):
        loaded = []
        for i in range(per_trip):
            u = t * per_trip + i
            c = u // nblk
            blk = u % nblk
            start = pl.multiple_of(blk * A_BLOCK, A_BLOCK)
            before = pl.multiple_of(jnp.maximum(start - A_BLOCK, 0), A_BLOCK)
            q = q_ref[0, c, pl.ds(start, A_BLOCK), :]
            k = jnp.concatenate([jnp.where(blk == 0, kp_ref[0, c], kc_ref[0, c, pl.ds(before, A_BLOCK), :]),
                                 kc_ref[0, c, pl.ds(start, A_BLOCK), :]], axis=0)
            v = jnp.concatenate([jnp.where(blk == 0, vp_ref[0, c], vc_ref[0, c, pl.ds(before, A_BLOCK), :]),
                                 vc_ref[0, c, pl.ds(start, A_BLOCK), :]], axis=0)
            no_prev = jnp.logical_and(first_tile, blk == 0)
            mask = jnp.logical_and(band, jnp.logical_or(kk >= A_BLOCK, jnp.logical_not(no_prev)))
            out_rows = (pl.ds(blk * (A_BLOCK * d) + c, A_BLOCK, stride=d) if d > 1
                        else pl.ds(start, A_BLOCK))
            if two_stage:
                o_at = (c % A_MID_DILATION,
                        pl.ds(blk * (A_BLOCK * mid_step) + c // A_MID_DILATION, A_BLOCK, stride=mid_step))
            else:
                o_at = (out_rows,)
            loaded.append(([_dot_t(q[:, sl], k[:, sl]) for sl in hs], v, mask, out_rows, o_at))
        for scores, v, mask, out_rows, o_at in loaded:
            lse_tile = lse_ref[out_rows, :]
            for h, sl in enumerate(hs):
                s = jnp.where(mask, scores[h], NEG_INF)
                m = jnp.max(s, axis=-1, keepdims=True)
                p = jnp.exp(s - m)
                l = jnp.sum(p, axis=-1, keepdims=True)
                stage_ref[(h,) + o_at + (slice(None),)] = _dot(p.astype(BF16), v[:, sl]) / l
                lse_tile = jnp.where(lane == first_head + h, m + jnp.log(l), lse_tile)
            lse_ref[out_rows, :] = lse_tile
        return carry

    lax.fori_loop(0, d * nblk // per_trip, trip, 0)
    if two_stage:
        for h in range(heads):
            for c4 in range(A_MID_DILATION):
                os_ref[h, pl.ds(c4, A_TILE // A_MID_DILATION, stride=A_MID_DILATION), :] = mid_ref[h, c4]
    o_ref[...] = os_ref[...].astype(BF16)


def _attn_a(qkv, window, dilation):
    assert window // dilation == A_BLOCK, "band width must equal the query block"
    d = dilation
    heads = 4
    hw = heads * A_HEAD_DIM
    ngrp = A_HEADS // heads
    rows = A_TILE // d
    tiles = SEQ // A_TILE
    qkv_v = qkv.reshape(BATCH, tiles, d, rows, 3 * A_HEADS * A_HEAD_DIM)
    last_blk = rows // A_BLOCK - 1

    def cur(part):
        return pl.BlockSpec((None, 1, d, rows, hw), lambda b, n, g: (b, n, 0, 0, part * ngrp + g))

    def prev(part):
        return pl.BlockSpec((None, 1, d, A_BLOCK, hw),
                            lambda b, n, g: (b, jnp.maximum(n - 1, 0), 0, last_blk, part * ngrp + g))

    return pl.pallas_call(
        functools.partial(_attn_a_kernel, d=d, heads=heads, per_trip=4),
        out_shape=(jax.ShapeDtypeStruct((A_HEADS, TOKENS, A_HEAD_DIM), BF16),
                   jax.ShapeDtypeStruct((TOKENS, LANES), F32)),
        grid=(BATCH, tiles, ngrp),
        in_specs=[cur(0), cur(1), prev(1), cur(2), prev(2)],
        out_specs=(pl.BlockSpec((heads, A_TILE, A_HEAD_DIM), lambda b, n, g: (g, b * tiles + n, 0)),
                   pl.BlockSpec((A_TILE, LANES), lambda b, n, g: (b * tiles + n, 0))),
        scratch_shapes=[pltpu.VMEM((heads, A_TILE, A_HEAD_DIM), F32)] + (
            [pltpu.VMEM((heads, A_MID_DILATION, A_TILE // A_MID_DILATION, A_HEAD_DIM), F32)]
            if d > A_MID_DILATION else []),
        compiler_params=_cparams("parallel", "parallel", "arbitrary"),
        name="attn_a_d%d" % d,
    )(qkv_v, qkv_v, qkv_v, qkv_v, qkv_v)


def _comb_proj_a_kernel(o0, o1, o2, l0, l1, l2, w_ref, r_ref, out_ref, comb_ref):
    s = pl.program_id(0)
    slot = s % 2

    @pl.when(s == 0)
    def _():
        comb_ref[1] = jnp.zeros(comb_ref.shape[1:], BF16)

    out_ref[...] = r_ref[...] + _dot(comb_ref[1 - slot], w_ref[...])
    a = [l0[...], l1[...], l2[...]]
    m = jnp.maximum(jnp.maximum(a[0], a[1]), a[2])
    e = [jnp.exp(t - m) for t in a]
    den = e[0] + e[1] + e[2]
    w = [t / den for t in e]
    rows = out_ref.shape[0]
    for h in range(A_HEADS):
        acc = jnp.zeros((rows, A_HEAD_DIM), F32)
        for wg, og in zip(w, (o0, o1, o2)):
            acc = acc + jnp.broadcast_to(wg[:, h:h + 1], (rows, A_HEAD_DIM)) * og[h]
        comb_ref[slot, :, h * A_HEAD_DIM:(h + 1) * A_HEAD_DIM] = acc.astype(BF16)


def _comb_proj_a(outs, lses, w_o, res):
    tm = 256
    nt = TOKENS // tm
    hd = A_HEADS * A_HEAD_DIM
    o_spec = pl.BlockSpec((A_HEADS, tm, A_HEAD_DIM), lambda s: (0, jnp.minimum(s, nt - 1), 0))
    l_spec = pl.BlockSpec((tm, LANES), lambda s: (jnp.minimum(s, nt - 1), 0))
    row_spec = pl.BlockSpec((tm, D_MODEL), lambda s: (jnp.maximum(s - 1, 0), 0))
    return pl.pallas_call(
        _comb_proj_a_kernel,
        out_shape=jax.ShapeDtypeStruct((TOKENS, D_MODEL), F32),
        grid=(nt + 1,),
        in_specs=[o_spec] * 3 + [l_spec] * 3 + [pl.BlockSpec((hd, D_MODEL), lambda s: (0, 0)), row_spec],
        out_specs=row_spec,
        scratch_shapes=[pltpu.VMEM((2, tm, hd), BF16)],
        compiler_params=_cparams("arbitrary"),
        name="comb_proj_a",
    )(*outs, *lses, w_o, res)


def _mixer_a(h, gain, w_qkv, w_o, tabs_a):
    outs, lses = [], []
    for g, ((window, dilation), xn) in enumerate(zip(A_PATTERNS, _norm_perm_a(h, gain))):
        o, lse = _attn_a(_qkv_a(xn, w_qkv, tabs_a, g), window, dilation)
        outs.append(o)
        lses.append(lse)
    return _comb_proj_a(outs, lses, w_o, h)


def _b_in_kernel(x_ref, g_ref, w_ref, o_ref):
    o_ref[...] = _dot(_rms(x_ref[...], g_ref[...]).astype(BF16), w_ref[...])


def _b_in(h, gain, w_in):
    tm = 512
    return pl.pallas_call(
        _b_in_kernel,
        out_shape=jax.ShapeDtypeStruct((TOKENS, B_IN_COLS), F32),
        grid=(TOKENS // tm,),
        in_specs=[
            pl.BlockSpec((tm, D_MODEL), lambda i: (i, 0)),
            pl.BlockSpec((1, D_MODEL), lambda i: (0, 0)),
            pl.BlockSpec((D_MODEL, B_IN_COLS), lambda i: (0, 0)),
        ],
        out_specs=pl.BlockSpec((tm, B_IN_COLS), lambda i: (i, 0)),
        compiler_params=_cparams("parallel"),
        name="b_in",
    )(h, gain.reshape(1, D_MODEL), w_in)


def _b_q_kernel(c_ref, g_ref, w_ref, cos_ref, sin_ref, o_ref, xn_ref, *, slots):
    @pl.when(pl.program_id(1) == 0)
    def _():
        xn_ref[...] = _rms(c_ref[...], g_ref[...]).astype(BF16)

    y = _dot(xn_ref[...], w_ref[...])
    scale = (B_NOPE_DIM + B_ROPE_DIM) ** -0.5
    cos, sin = cos_ref[...], sin_ref[...]
    for s in range(slots):
        nope = y[:, s * B_SLOT:s * B_SLOT + LANES]
        rot = _rope128(y[:, s * B_SLOT + LANES:(s + 1) * B_SLOT], cos, sin)
        o_ref[s, :, :LANES] = (nope * scale).astype(BF16)
        o_ref[s, :, LANES:] = (rot * scale).astype(BF16)


def _b_q(c, gain, w_q, tabs_b):
    tm, slots = 1024, 4
    tn = slots * B_SLOT
    tab_spec = pl.BlockSpec((tm, LANES), lambda i, j: (i, 0))
    return pl.pallas_call(
        functools.partial(_b_q_kernel, slots=slots),
        out_shape=jax.ShapeDtypeStruct((B_HEADS, TOKENS, B_SLOT), BF16),
        grid=(TOKENS // tm, B_HEADS // slots),
        in_specs=[
            pl.BlockSpec((tm, B_Q_RANK), lambda i, j: (i, 0)),
            pl.BlockSpec((1, B_Q_RANK), lambda i, j: (0, 0)),
            pl.BlockSpec((B_Q_RANK, tn), lambda i, j: (0, j)),
            tab_spec, tab_spec,
        ],
        out_specs=pl.BlockSpec((slots, tm, B_SLOT), lambda i, j: (j, i, 0)),
        scratch_shapes=[pltpu.VMEM((tm, B_Q_RANK), BF16)],
        compiler_params=_cparams("parallel", "arbitrary"),
        name="b_q",
    )(c, gain.reshape(1, B_Q_RANK), w_q, *tabs_b)


def _b_kv_kernel(c_ref, g_ref, wk_ref, wv_ref, cr_ref, cos_ref, sin_ref, k_ref, v_ref, *, slots):
    xn = _rms(c_ref[...], g_ref[...]).astype(BF16)
    yk = _dot(xn, wk_ref[...])
    yv = _dot(xn, wv_ref[...])
    kr = _rope128(cr_ref[...], cos_ref[...], sin_ref[...]).astype(BF16)
    lane = lax.broadcasted_iota(jnp.int32, (c_ref.shape[0], B_VX - B_V_DIM), 1)
    ones_col = jnp.where(lane == 0, 1.0, 0.0).astype(BF16)
    for s in range(slots):
        sl = slice(s * LANES, (s + 1) * LANES)
        k_ref[s, :, :LANES] = yk[:, sl].astype(BF16)
        k_ref[s, :, LANES:] = kr
        v_ref[s, :, :B_V_DIM] = yv[:, sl].astype(BF16)
        v_ref[s, :, B_V_DIM:] = ones_col


def _b_kv(c, gain, w_k, w_v, tabs_b):
    tm, slots = 1024, 4
    tn = slots * LANES
    kv_block = B_Q_RANK // B_KV_RANK
    rope_block = (B_Q_RANK + B_KV_RANK) // LANES
    tab_spec = pl.BlockSpec((tm, LANES), lambda i, j: (i, 0))
    return pl.pallas_call(
        functools.partial(_b_kv_kernel, slots=slots),
        out_shape=(jax.ShapeDtypeStruct((B_HEADS, TOKENS, B_SLOT), BF16),
                   jax.ShapeDtypeStruct((B_HEADS, TOKENS, B_VX), BF16)),
        grid=(TOKENS // tm, B_HEADS // slots),
        in_specs=[
            pl.BlockSpec((tm, B_KV_RANK), lambda i, j: (i, kv_block)),
            pl.BlockSpec((1, B_KV_RANK), lambda i, j: (0, 0)),
            pl.BlockSpec((B_KV_RANK, tn), lambda i, j: (0, j)),
            pl.BlockSpec((B_KV_RANK, tn), lambda i, j: (0, j)),
            pl.BlockSpec((tm, LANES), lambda i, j: (i, rope_block)),
            tab_spec, tab_spec,
        ],
        out_specs=(pl.BlockSpec((slots, tm, B_SLOT), lambda i, j: (j, i, 0)),
                   pl.BlockSpec((slots, tm, B_VX), lambda i, j: (j, i, 0))),
        compiler_params=_cparams("parallel", "parallel"),
        name="b_kv",
    )(c, gain.reshape(1, B_KV_RANK), w_k, w_v, c, *tabs_b)


def _attn_b_kernel(q_ref, k_ref, v_ref, o_ref, s0_ref, s1_ref, m_ref, acc_ref, *, tq, heads):
    qi = pl.program_id(2)
    m_ref[...] = jnp.full_like(m_ref, NEG_INF)
    acc_ref[...] = jnp.zeros_like(acc_ref)

    def scores(s_ref, ki):
        for h in range(heads):
            k = k_ref[h, pl.ds(pl.multiple_of(ki * tq, tq), tq), :]
            s_ref[h] = _dot_t(q_ref[h], k)

    def fold(s_ref, ki, diagonal):
        for h in range(heads):
            s = s_ref[h]
            if diagonal:
                row = lax.broadcasted_iota(jnp.int32, (tq, tq), 0)
                col = lax.broadcasted_iota(jnp.int32, (tq, tq), 1)
                s = jnp.where(row >= col, s, NEG_INF)
            m_prev = m_ref[h]
            m_new = jnp.maximum(m_prev, jnp.max(s, axis=-1, keepdims=True))
            alpha = jnp.exp(m_prev - m_new)
            p = jnp.exp((s - jnp.concatenate([m_new] * (tq // LANES), axis=1)).astype(BF16))
            v = v_ref[h, pl.ds(pl.multiple_of(ki * tq, tq), tq), :]
            acc_ref[h] = jnp.concatenate([alpha] * (B_VX // LANES), axis=1) * acc_ref[h] + _dot(p, v)
            m_ref[h] = m_new

    pairs = qi // 2
    scores(s0_ref, 0)

    def body(t, carry):
        scores(s1_ref, 2 * t + 1)
        fold(s0_ref, 2 * t, False)
        scores(s0_ref, 2 * t + 2)
        fold(s1_ref, 2 * t + 1, False)
        return carry

    lax.fori_loop(0, pairs, body, 0)

    @pl.when(qi % 2 == 1)
    def _():
        scores(s1_ref, qi)
        fold(s0_ref, qi - 1, False)
        fold(s1_ref, qi, True)

    @pl.when(qi % 2 == 0)
    def _():
        fold(s0_ref, qi, True)

    for h in range(heads):
        acc = acc_ref[h]
        l = jnp.broadcast_to(acc[:, B_V_DIM:B_V_DIM + 1], (tq, B_V_DIM))
        o_ref[:, h * B_V_DIM:(h + 1) * B_V_DIM] = (acc[:, :B_V_DIM] / l).astype(BF16)


def _attn_b(q3, k3, v3):
    tq, heads = 512, 2
    nq = SEQ // tq
    return pl.pallas_call(
        functools.partial(_attn_b_kernel, tq=tq, heads=heads),
        out_shape=jax.ShapeDtypeStruct((TOKENS, B_HEADS * B_V_DIM), BF16),
        grid=(BATCH, B_HEADS // heads, nq),
        in_specs=[
            pl.BlockSpec((heads, tq, B_SLOT), lambda b, g, qi: (g, b * nq + qi, 0)),
            pl.BlockSpec((heads, SEQ, B_SLOT), lambda b, g, qi: (g, b, 0)),
            pl.BlockSpec((heads, SEQ, B_VX), lambda b, g, qi: (g, b, 0)),
        ],
        out_specs=pl.BlockSpec((tq, heads * B_V_DIM), lambda b, g, qi: (b * nq + qi, g)),
        scratch_shapes=[pltpu.VMEM((heads, tq, tq), F32), pltpu.VMEM((heads, tq, tq), F32),
                        pltpu.VMEM((heads, tq, LANES), F32), pltpu.VMEM((heads, tq, B_VX), F32)],
        compiler_params=_cparams("parallel", "parallel", "arbitrary"),
        name="attn_b",
    )(q3, k3, v3)


def _mixer_b(h, gain, w_in, q_norm, w_q, kv_norm, w_k, w_v, w_o, tabs_b):
    c = _b_in(h, gain, w_in)
    q3 = _b_q(c, q_norm, w_q, tabs_b)
    k3, v3 = _b_kv(c, kv_norm, w_k, w_v, tabs_b)
    return _out_proj(_attn_b(q3, k3, v3), w_o, h)


def _c_in_kernel(x_ref, g_ref, wb_ref, wc_ref, wu_ref, cw_ref, o_ref, xn_ref, carry_ref, *, tm):
    i = pl.program_id(0)
    j = pl.program_id(1)

    @pl.when(j == 0)
    def _():
        xn_ref[...] = _rms(x_ref[...], g_ref[...]).astype(BF16)

    xn = xn_ref[...]
    b_gate = _dot(xn, wb_ref[...])
    cu = _dot(xn, wc_ref[...]) * _dot(xn, wu_ref[...])
    w0, w1, w2 = cw_ref[0:1, :], cw_ref[1:2, :], cw_ref[2:3, :]
    z = w2 * cu + w1 * pltpu.roll(cu, 1, 0) + w0 * pltpu.roll(cu, 2, 0)
    o_ref[...] = (b_gate * z).astype(BF16)

    first = (i * tm) % SEQ == 0
    tail = jnp.where(first, 0.0, carry_ref[j])
    both = jnp.concatenate([tail, cu[0:8, :]], axis=0)
    z8 = w2 * both[8:16] + w1 * both[7:15] + w0 * both[6:14]
    o_ref[0:8, :] = (b_gate[0:8, :] * z8).astype(BF16)
    carry_ref[j] = cu[tm - 8:tm, :]


def _c_in(h, gain, w_in, conv_w):
    tm, tn = 1024, 512
    nj = D_MODEL // tn
    return pl.pallas_call(
        functools.partial(_c_in_kernel, tm=tm),
        out_shape=jax.ShapeDtypeStruct((TOKENS, D_MODEL), BF16),
        grid=(TOKENS // tm, nj),
        in_specs=[
            pl.BlockSpec((tm, D_MODEL), lambda i, j: (i, 0)),
            pl.BlockSpec((1, D_MODEL), lambda i, j: (0, 0)),
            pl.BlockSpec((D_MODEL, tn), lambda i, j: (0, j)),
            pl.BlockSpec((D_MODEL, tn), lambda i, j: (0, j + nj)),
            pl.BlockSpec((D_MODEL, tn), lambda i, j: (0, j + 2 * nj)),
            pl.BlockSpec((C_CONV, tn), lambda i, j: (0, j)),
        ],
        out_specs=pl.BlockSpec((tm, tn), lambda i, j: (i, j)),
        scratch_shapes=[pltpu.VMEM((tm, D_MODEL), BF16), pltpu.VMEM((nj, 8, tn), F32)],
        compiler_params=_cparams("arbitrary", "arbitrary"),
        name="c_in",
    )(h, gain.reshape(1, D_MODEL), w_in, w_in, w_in, conv_w)


def _mixer_c(h, gain, w_in, conv_w, w_out):
    return _out_proj(_c_in(h, gain, w_in, conv_w), w_out, h)


def kernel(x, positions, l0_ffn1_norm, l0_ffn1_w13, l0_ffn1_w2, l0_mix_norm, l0_a_w_qkv, l0_a_w_o, l0_ffn2_norm, l0_ffn2_w13, l0_ffn2_w2, l1_ffn1_norm, l1_ffn1_w13, l1_ffn1_w2, l1_mix_norm, l1_b_w_in, l1_b_q_norm, l1_b_w_uq, l1_b_kv_norm, l1_b_w_ukv, l1_b_w_o, l1_ffn2_norm, l1_ffn2_w13, l1_ffn2_w2, l2_ffn1_norm, l2_ffn1_w13, l2_ffn1_w2, l2_mix_norm, l2_c_w_in, l2_c_conv_w, l2_c_w_out, l2_ffn2_norm, l2_ffn2_w13, l2_ffn2_w2, l3_ffn1_norm, l3_ffn1_w13, l3_ffn1_w2, l3_mix_norm, l3_a_w_qkv, l3_a_w_o, l3_ffn2_norm, l3_ffn2_w13, l3_ffn2_w2, final_norm):
    bf = lambda w: w.astype(BF16)
    pos_tiles = positions.reshape(TOKENS // A_TILE, A_TILE)
    pos_a = [pos_tiles.reshape(-1, A_TILE // d, d).swapaxes(1, 2).reshape(TOKENS) for _, d in A_PATTERNS]
    tabs_a = _rope_tables(jnp.concatenate(pos_a + [jnp.zeros((A_TILE,), positions.dtype)]), A_ROT_HALF)
    tabs_b = _rope_tables(positions.reshape(TOKENS), B_ROT_HALF)

    def rope_group(w):
        return _rope_lanes(jnp.pad(w, [(0, 0)] * (w.ndim - 1) + [(0, LANES - B_ROPE_DIM)]), B_ROT_HALF)

    w_uq = l1_b_w_uq.reshape(B_Q_RANK, B_HEADS, B_NOPE_DIM + B_ROPE_DIM)
    w_q = jnp.concatenate([w_uq[..., :B_NOPE_DIM], rope_group(w_uq[..., B_NOPE_DIM:])], axis=-1)
    w_q = bf(w_q.reshape(B_Q_RANK, B_HEADS * B_SLOT))
    w_ukv = l1_b_w_ukv.reshape(B_KV_RANK, B_HEADS, B_NOPE_DIM + B_V_DIM)
    w_k = bf(w_ukv[:, :, :B_NOPE_DIM].reshape(B_KV_RANK, B_HEADS * B_NOPE_DIM))
    w_v = bf(w_ukv[:, :, B_NOPE_DIM:].reshape(B_KV_RANK, B_HEADS * B_V_DIM))
    n_c = B_Q_RANK + B_KV_RANK
    w_b_in = bf(jnp.concatenate([l1_b_w_in[:, :n_c], rope_group(l1_b_w_in[:, n_c:])], axis=-1))

    h = x.reshape(TOKENS, D_MODEL)

    w13 = bf(l0_ffn1_w13).reshape(D_MODEL, 2 * D_FF // FFN_TF, FFN_TF).swapaxes(0, 1)
    h, w13, w2 = _ffn(h, l0_ffn1_norm, w13, bf(l0_ffn1_w2), next_f32=(l0_ffn2_w13, l0_ffn2_w2))
    h = _mixer_a(h, l0_mix_norm, _qkv_w(l0_a_w_qkv), bf(l0_a_w_o), tabs_a)
    h, w13, w2 = _ffn(h, l0_ffn2_norm, w13, w2, next_f32=(l1_ffn1_w13, l1_ffn1_w2))

    h, w13, w2 = _ffn(h, l1_ffn1_norm, w13, w2, next_f32=(l1_ffn2_w13, l1_ffn2_w2))
    h = _mixer_b(h, l1_mix_norm, w_b_in, l1_b_q_norm, w_q, l1_b_kv_norm, w_k, w_v, bf(l1_b_w_o), tabs_b)
    h, w13, w2 = _ffn(h, l1_ffn2_norm, w13, w2, next_f32=(l2_ffn1_w13, l2_ffn1_w2))

    h, w13, w2 = _ffn(h, l2_ffn1_norm, w13, w2, next_f32=(l2_ffn2_w13, l2_ffn2_w2))
    h = _mixer_c(h, l2_mix_norm, bf(l2_c_w_in), l2_c_conv_w, bf(l2_c_w_out))
    h, w13, w2 = _ffn(h, l2_ffn2_norm, w13, w2, next_f32=(l3_ffn1_w13, l3_ffn1_w2))

    h, w13, w2 = _ffn(h, l3_ffn1_norm, w13, w2, next_f32=(l3_ffn2_w13, l3_ffn2_w2))
    h = _mixer_a(h, l3_mix_norm, _qkv_w(l3_a_w_qkv), bf(l3_a_w_o), tabs_a)
    h = _ffn(h, l3_ffn2_norm, w13, w2, final_gain=final_norm)

    return h.reshape(BATCH, SEQ, D_MODEL)
```

```python
import functools

import jax
import jax.numpy as jnp
from jax import lax
from jax.experimental import pallas as pl
from jax.experimental.pallas import tpu as pltpu

F32 = jnp.float32
BF16 = jnp.bfloat16

D_MODEL = 2048
BATCH = 2
SEQ = 8192
TOKENS = BATCH * SEQ
D_FF = 5632
HALF_STEP = 0.5
NORM_EPS = 1e-6
NEG_INF = -1e30
ROPE_THETA = 500000.0

A_HEADS = 16
A_HEAD_DIM = 128
A_ROT_HALF = A_HEAD_DIM // 8
A_PATTERNS = ((128, 1), (512, 4), (2048, 16))
A_BLOCK = 128
A_TILE = 2048
A_MID_DILATION = 4
assert [d for _, d in A_PATTERNS] == [1, A_MID_DILATION, A_MID_DILATION ** 2]

B_HEADS = 16
B_Q_RANK = 1536
B_KV_RANK = 512
B_NOPE_DIM = 128
B_ROPE_DIM = 64
B_ROT_HALF = B_ROPE_DIM // 2
B_V_DIM = 128
B_SLOT = 256
B_VX = 256
B_IN_COLS = 2176

C_CONV = 3

LANES = 128
MXU_COLS = 256
ROPE_PAIR = LANES // 2
VMEM_LIMIT = 56 * 1024 * 1024


def _cparams(*sem):
    return pltpu.CompilerParams(dimension_semantics=sem, vmem_limit_bytes=VMEM_LIMIT)


def _rms(x, gain):
    ms = jnp.mean(x * x, axis=-1, keepdims=True)
    return x * lax.rsqrt(ms + NORM_EPS) * gain


def _dot(a, b):
    return jnp.dot(a, b, preferred_element_type=F32)


def _dot_t(a, b):
    return lax.dot_general(a, b, (((1,), (1,)), ((), ())), preferred_element_type=F32)


def _rope128(y, cos, sin):
    return y * cos + pltpu.roll(y, ROPE_PAIR, 1) * sin


def _rope_lanes(w, half):
    return jnp.concatenate([w[..., :half], w[..., 2 * half:ROPE_PAIR + half], w[..., half:2 * half],
                            w[..., ROPE_PAIR + half:]], axis=-1)


def _rope_tab_kernel(pos_ref, inv_ref, c_ref, s_ref):
    ang = pos_ref[...].astype(F32) * inv_ref[...]
    lane = lax.broadcasted_iota(jnp.int32, (1, LANES), 1)
    sn = jnp.sin(ang)
    c_ref[...] = jnp.cos(ang)
    s_ref[...] = jnp.where(lane < ROPE_PAIR, -sn, sn)


def _rope_tables(pos_flat, half):
    inv = ROPE_THETA ** (-jnp.arange(half, dtype=F32) / half)
    gap = jnp.zeros((ROPE_PAIR - half,), F32)
    inv = jnp.concatenate([inv, gap, inv, gap]).reshape(1, LANES)
    n = pos_flat.shape[0]
    rows = 2048
    tab = jax.ShapeDtypeStruct((n, LANES), F32)
    row_spec = pl.BlockSpec((rows, LANES), lambda i: (i, 0))
    return pl.pallas_call(
        _rope_tab_kernel,
        out_shape=(tab,) * 2,
        grid=(n // rows,),
        in_specs=[pl.BlockSpec((rows, 1), lambda i: (i, 0)), pl.BlockSpec((1, LANES), lambda i: (0, 0))],
        out_specs=(row_spec,) * 2,
        compiler_params=_cparams("parallel"),
        name="rope_tables",
    )(pos_flat.reshape(n, 1), inv)


def _ffn_kernel(x_ref, g_ref, w1_ref, w3_ref, w2_ref, fg_ref, *refs, final, cast_next):
    j = pl.program_id(1)
    if cast_next:
        n13_ref, n2_ref, o_ref, c13_ref, c2_ref, xn_ref, acc_ref = refs
    else:
        o_ref, xn_ref, acc_ref = refs

    @pl.when(j == 0)
    def _():
        xn_ref[...] = _rms(x_ref[...], g_ref[...]).astype(BF16)
        acc_ref[...] = jnp.zeros_like(acc_ref)

    xn = xn_ref[...]
    gate = _dot(xn, w1_ref[...])
    up = _dot(xn, w3_ref[...])
    act = (gate * (1.0 / (1.0 + jnp.exp(-gate))) * up).astype(BF16)
    acc_ref[...] += _dot(act, w2_ref[...])
    if cast_next:
        c13_ref[...] = n13_ref[...].astype(BF16)
        c2_ref[...] = n2_ref[...].astype(BF16)

    @pl.when(j == pl.num_programs(1) - 1)
    def _():
        y = x_ref[...] + HALF_STEP * acc_ref[...]
        if final:
            y = _rms(y, fg_ref[...])
        o_ref[...] = y


def _ffn(h, gain, w13, w2, final_gain=None, next_f32=None):
    tm, tf = 512, 512
    nj = D_FF // tf
    ni = TOKENS // tm
    final = final_gain is not None
    fg = final_gain if final else gain
    in_specs = [
        pl.BlockSpec((tm, D_MODEL), lambda i, j: (i, 0)),
        pl.BlockSpec((1, D_MODEL), lambda i, j: (0, 0)),
        pl.BlockSpec((D_MODEL, tf), lambda i, j: (0, j)),
        pl.BlockSpec((D_MODEL, tf), lambda i, j: (0, j + nj)),
        pl.BlockSpec((tf, D_MODEL), lambda i, j: (j, 0)),
        pl.BlockSpec((1, D_MODEL), lambda i, j: (0, 0)),
    ]
    out_shape = [jax.ShapeDtypeStruct((TOKENS, D_MODEL), F32)]
    out_specs = [pl.BlockSpec((tm, D_MODEL), lambda i, j: (i, 0))]
    args = [h, gain.reshape(1, D_MODEL), w13, w13, w2, fg.reshape(1, D_MODEL)]
    if next_f32 is not None:
        slab13 = pl.BlockSpec((D_MODEL // ni, 2 * D_FF // nj), lambda i, j: (i, j))
        slab2 = pl.BlockSpec((D_FF // (ni * nj), D_MODEL), lambda i, j: (i * nj + j, 0))
        in_specs += [slab13, slab2]
        out_specs += [slab13, slab2]
        out_shape += [jax.ShapeDtypeStruct(w.shape, BF16) for w in next_f32]
        args += list(next_f32)
    outs = pl.pallas_call(
        functools.partial(_ffn_kernel, final=final, cast_next=next_f32 is not None),
        out_shape=tuple(out_shape),
        grid=(ni, nj),
        in_specs=in_specs,
        out_specs=tuple(out_specs),
        scratch_shapes=[pltpu.VMEM((tm, D_MODEL), BF16), pltpu.VMEM((tm, D_MODEL), F32)],
        compiler_params=_cparams("parallel", "arbitrary"),
        name="ffn",
    )(*args)
    return outs if next_f32 is not None else outs[0]


def _out_proj_kernel(x_ref, w_ref, r_ref, o_ref):
    o_ref[...] = r_ref[...] + _dot(x_ref[...], w_ref[...])


def _out_proj(x, w, res):
    k = x.shape[1]
    tm = 512
    return pl.pallas_call(
        _out_proj_kernel,
        out_shape=jax.ShapeDtypeStruct((TOKENS, D_MODEL), F32),
        grid=(TOKENS // tm,),
        in_specs=[
            pl.BlockSpec((tm, k), lambda i: (i, 0)),
            pl.BlockSpec((k, D_MODEL), lambda i: (0, 0)),
            pl.BlockSpec((tm, D_MODEL), lambda i: (i, 0)),
        ],
        out_specs=pl.BlockSpec((tm, D_MODEL), lambda i: (i, 0)),
        compiler_params=_cparams("parallel"),
        name="out_proj",
    )(x, w, res)


def _norm_perm_a_kernel(x_ref, g_ref, *refs, tm):
    o_refs, xs_ref, mid_ref = refs[:-2], refs[-2], refs[-1]
    xn = _rms(x_ref[...], g_ref[...])
    nchunk = D_MODEL // LANES
    for k in range(nchunk):
        xs_ref[k] = xn[:, k * LANES:(k + 1) * LANES]
    dp = 1
    for (_, d), o_ref in zip(A_PATTERNS, o_refs):
        if d == 1:
            o_ref[...] = xn.astype(BF16)
            continue
        step = d // dp
        for c in range(d):
            for k in range(nchunk):
                rows = pl.ds(c // dp, tm // d, stride=step)
                piece = xs_ref[k, rows, :] if dp == 1 else mid_ref[k, c % dp, rows, :]
                o_ref[0, c, 0, :, k * LANES:(k + 1) * LANES] = piece.astype(BF16)
                if d == A_MID_DILATION:
                    mid_ref[k, c] = piece
        dp = d


def _norm_perm_a(h, gain):
    tm = 512
    per_tile = A_TILE // tm
    shapes, specs = [], []
    for _, d in A_PATTERNS:
        if d == 1:
            shapes.append(jax.ShapeDtypeStruct((TOKENS, D_MODEL), BF16))
            specs.append(pl.BlockSpec((tm, D_MODEL), lambda i: (i, 0)))
        else:
            shapes.append(jax.ShapeDtypeStruct((TOKENS // A_TILE, d, per_tile, tm // d, D_MODEL), BF16))
            specs.append(pl.BlockSpec((1, d, 1, tm // d, D_MODEL),
                                      lambda i: (i // per_tile, 0, i % per_tile, 0, 0)))
    outs = pl.pallas_call(
        functools.partial(_norm_perm_a_kernel, tm=tm),
        out_shape=tuple(shapes),
        grid=(TOKENS // tm,),
        in_specs=[pl.BlockSpec((tm, D_MODEL), lambda i: (i, 0)), pl.BlockSpec((1, D_MODEL), lambda i: (0, 0))],
        out_specs=tuple(specs),
        scratch_shapes=[pltpu.VMEM((D_MODEL // LANES, tm, LANES), F32),
                        pltpu.VMEM((D_MODEL // LANES, A_MID_DILATION, tm // A_MID_DILATION, LANES), F32)],
        compiler_params=_cparams("parallel"),
        name="norm_perm_a",
    )(h, gain.reshape(1, D_MODEL))
    return [o.reshape(TOKENS, D_MODEL) for o in outs]


def _qkv_w_kernel(w_ref, o_ref):
    part = pl.program_id(1) % 3

    @pl.when(part == 2)
    def _():
        o_ref[...] = w_ref[...].astype(BF16)

    @pl.when(part != 2)
    def _():
        lane = lax.broadcasted_iota(jnp.int32, (1, LANES), 1)
        stay = (lane < A_ROT_HALF) | (lane >= ROPE_PAIR + A_ROT_HALF)
        for hh in range(A_HEADS):
            sl = slice(hh * A_HEAD_DIM, (hh + 1) * A_HEAD_DIM)
            y = w_ref[:, sl]
            moved = jnp.where(lane < ROPE_PAIR, pltpu.roll(y, LANES - A_ROT_HALF, 1),
                              pltpu.roll(y, ROPE_PAIR - A_ROT_HALF, 1))
            o_ref[:, sl] = jnp.where(stay, y, moved).astype(BF16)


def _qkv_w(w_qkv):
    rows = 256
    hd = A_HEADS * A_HEAD_DIM
    spec = pl.BlockSpec((rows, hd), lambda i, j: (i, j))
    return pl.pallas_call(
        _qkv_w_kernel,
        out_shape=jax.ShapeDtypeStruct(w_qkv.shape, BF16),
        grid=(D_MODEL // rows, w_qkv.shape[1] // hd),
        in_specs=[spec],
        out_specs=spec,
        compiler_params=_cparams("parallel", "parallel"),
        name="qkv_w",
    )(w_qkv)


def _qkv_a_kernel(x_ref, w_ref, cos_ref, sin_ref, o_ref, *, tn):
    part = pl.program_id(1) // ((A_HEADS * A_HEAD_DIM) // tn)
    scale = jnp.where(part == 0, A_HEAD_DIM ** -0.5, 1.0).astype(F32)
    half = x_ref.shape[0] // 2
    for cc in range(tn // MXU_COLS):
        for rows in (slice(0, half), slice(half, 2 * half)):
            y = _dot(x_ref[rows, :], w_ref[:, cc * MXU_COLS:(cc + 1) * MXU_COLS])
            for hh in range(MXU_COLS // LANES):
                sl = slice(cc * MXU_COLS + hh * LANES, cc * MXU_COLS + (hh + 1) * LANES)
                y_h = y[:, hh * LANES:(hh + 1) * LANES]
                o_ref[rows, sl] = (_rope128(y_h, cos_ref[rows, :], sin_ref[rows, :]) * scale).astype(BF16)


def _qkv_a(xn, w_qkv, tabs, group):
    tm, tn = 1024, 2048
    cols = 3 * A_HEADS * A_HEAD_DIM
    nj = cols // tn
    ni = TOKENS // tm
    v_first = 2 * (A_HEADS * A_HEAD_DIM) // tn
    tab_spec = pl.BlockSpec((tm, LANES),
                            lambda i, j: (jnp.where(j >= v_first, len(A_PATTERNS) * ni, group * ni + i), 0))
    return pl.pallas_call(
        functools.partial(_qkv_a_kernel, tn=tn),
        out_shape=jax.ShapeDtypeStruct((TOKENS, cols), BF16),
        grid=(ni, nj),
        in_specs=[
            pl.BlockSpec((tm, D_MODEL), lambda i, j: (i, 0)),
            pl.BlockSpec((D_MODEL, tn), lambda i, j: (0, group * nj + j)),
            tab_spec, tab_spec,
        ],
        out_specs=pl.BlockSpec((tm, tn), lambda i, j: (i, j)),
        compiler_params=_cparams("parallel", "parallel"),
        name="qkv_a",
    )(xn, w_qkv, *tabs)


def _attn_a_kernel(q_ref, kc_ref, kp_ref, vc_ref, vp_ref, o_ref, lse_ref, *scratch, d, heads, per_trip):
    rows = A_TILE // d
    nblk = rows // A_BLOCK
    two_stage = d > A_MID_DILATION
    if two_stage:
        os_ref, mid_ref = scratch
        stage_ref, mid_step = mid_ref, d // A_MID_DILATION
    else:
        (os_ref,) = scratch
        stage_ref = os_ref
    first_tile = pl.program_id(1) == 0
    r = lax.broadcasted_iota(jnp.int32, (A_BLOCK, 2 * A_BLOCK), 0)
    kk = lax.broadcasted_iota(jnp.int32, (A_BLOCK, 2 * A_BLOCK), 1)
    band = (kk >= r) & (kk <= r + A_BLOCK)
    lane = lax.broadcasted_iota(jnp.int32, (A_BLOCK, LANES), 1)
    hs = [slice(h * A_HEAD_DIM, (h + 1) * A_HEAD_DIM) for h in range(heads)]
    first_head = pl.program_id(2) * heads

    @pl.when(first_head == 0)
    def _():
        lse_ref[...] = jnp.zeros_like(lse_ref)

    def trip(t, carry):
        loaded = []
        for i in range(per_trip):
            u = t * per_trip + i
            c = u // nblk
            blk = u % nblk
            start = pl.multiple_of(blk * A_BLOCK, A_BLOCK)
            before = pl.multiple_of(jnp.maximum(start - A_BLOCK, 0), A_BLOCK)
            q = q_ref[0, c, pl.ds(start, A_BLOCK), :]
            k = jnp.concatenate([jnp.where(blk == 0, kp_ref[0, c], kc_ref[0, c, pl.ds(before, A_BLOCK), :]),
                                 kc_ref[0, c, pl.ds(start, A_BLOCK), :]], axis=0)
            v = jnp.concatenate([jnp.where(blk == 0, vp_ref[0, c], vc_ref[0, c, pl.ds(before, A_BLOCK), :]),
                                 vc_ref[0, c, pl.ds(start, A_BLOCK), :]], axis=0)
            no_prev = jnp.logical_and(first_tile, blk == 0)
            mask = jnp.logical_and(band, jnp.logical_or(kk >= A_BLOCK, jnp.logical_not(no_prev)))
            out_rows = (pl.ds(blk * (A_BLOCK * d) + c, A_BLOCK, stride=d) if d > 1
                        else pl.ds(start, A_BLOCK))
            if two_stage:
                o_at = (c % A_MID_DILATION,
                        pl.ds(blk * (A_BLOCK * mid_step) + c // A_MID_DILATION, A_BLOCK, stride=mid_step))
            else:
                o_at = (out_rows,)
            loaded.append(([_dot_t(q[:, sl], k[:, sl]) for sl in hs], v, mask, out_rows, o_at))
        for scores, v, mask, out_rows, o_at in loaded:
            lse_tile = lse_ref[out_rows, :]
            for h, sl in enumerate(hs):
                s = jnp.where(mask, scores[h], NEG_INF)
                m = jnp.max(s, axis=-1, keepdims=True)
                p = jnp.exp(s - m)
                l = jnp.sum(p, axis=-1, keepdims=True)
                stage_ref[(h,) + o_at + (slice(None),)] = _dot(p.astype(BF16), v[:, sl]) / l
                lse_tile = jnp.where(lane == first_head + h, m + jnp.log(l), lse_tile)
            lse_ref[out_rows, :] = lse_tile
        return carry

    lax.fori_loop(0, d * nblk // per_trip, trip, 0)
    if two_stage:
        for h in range(heads):
            for c4 in range(A_MID_DILATION):
                os_ref[h, pl.ds(c4, A_TILE // A_MID_DILATION, stride=A_MID_DILATION), :] = mid_ref[h, c4]
    o_ref[...] = os_ref[...].astype(BF16)


def _attn_a(qkv, window, dilation):
    assert window // dilation == A_BLOCK, "band width must equal the query block"
    d = dilation
    heads = 4
    hw = heads * A_HEAD_DIM
    ngrp = A_HEADS // heads
    rows = A_TILE // d
    tiles = SEQ // A_TILE
    qkv_v = qkv.reshape(BATCH, tiles, d, rows, 3 * A_HEADS * A_HEAD_DIM)
    last_blk = rows // A_BLOCK - 1

    def cur(part):
        return pl.BlockSpec((None, 1, d, rows, hw), lambda b, n, g: (b, n, 0, 0, part * ngrp + g))

    def prev(part):
        return pl.BlockSpec((None, 1, d, A_BLOCK, hw),
                            lambda b, n, g: (b, jnp.maximum(n - 1, 0), 0, last_blk, part * ngrp + g))

    return pl.pallas_call(
        functools.partial(_attn_a_kernel, d=d, heads=heads, per_trip=4),
        out_shape=(jax.ShapeDtypeStruct((A_HEADS, TOKENS, A_HEAD_DIM), BF16),
                   jax.ShapeDtypeStruct((TOKENS, LANES), F32)),
        grid=(BATCH, tiles, ngrp),
        in_specs=[cur(0), cur(1), prev(1), cur(2), prev(2)],
        out_specs=(pl.BlockSpec((heads, A_TILE, A_HEAD_DIM), lambda b, n, g: (g, b * tiles + n, 0)),
                   pl.BlockSpec((A_TILE, LANES), lambda b, n, g: (b * tiles + n, 0))),
        scratch_shapes=[pltpu.VMEM((heads, A_TILE, A_HEAD_DIM), F32)] + (
            [pltpu.VMEM((heads, A_MID_DILATION, A_TILE // A_MID_DILATION, A_HEAD_DIM), F32)]
            if d > A_MID_DILATION else []),
        compiler_params=_cparams("parallel", "parallel", "arbitrary"),
        name="attn_a_d%d" % d,
    )(qkv_v, qkv_v, qkv_v, qkv_v, qkv_v)


def _comb_proj_a_kernel(o0, o1, o2, l0, l1, l2, w_ref, r_ref, out_ref, comb_ref):
    s = pl.program_id(0)
    slot = s % 2

    @pl.when(s == 0)
    def _():
        comb_ref[1] = jnp.zeros(comb_ref.shape[1:], BF16)

    out_ref[...] = r_ref[...] + _dot(comb_ref[1 - slot], w_ref[...])
    a = [l0[...], l1[...], l2[...]]
    m = jnp.maximum(jnp.maximum(a[0], a[1]), a[2])
    e = [jnp.exp(t - m) for t in a]
    den = e[0] + e[1] + e[2]
    w = [t / den for t in e]
    rows = out_ref.shape[0]
    for h in range(A_HEADS):
        acc = jnp.zeros((rows, A_HEAD_DIM), F32)
        for wg, og in zip(w, (o0, o1, o2)):
            acc = acc + jnp.broadcast_to(wg[:, h:h + 1], (rows, A_HEAD_DIM)) * og[h]
        comb_ref[slot, :, h * A_HEAD_DIM:(h + 1) * A_HEAD_DIM] = acc.astype(BF16)


def _comb_proj_a(outs, lses, w_o, res):
    tm = 256
    nt = TOKENS // tm
    hd = A_HEADS * A_HEAD_DIM
    o_spec = pl.BlockSpec((A_HEADS, tm, A_HEAD_DIM), lambda s: (0, jnp.minimum(s, nt - 1), 0))
    l_spec = pl.BlockSpec((tm, LANES), lambda s: (jnp.minimum(s, nt - 1), 0))
    row_spec = pl.BlockSpec((tm, D_MODEL), lambda s: (jnp.maximum(s - 1, 0), 0))
    return pl.pallas_call(
        _comb_proj_a_kernel,
        out_shape=jax.ShapeDtypeStruct((TOKENS, D_MODEL), F32),
        grid=(nt + 1,),
        in_specs=[o_spec] * 3 + [l_spec] * 3 + [pl.BlockSpec((hd, D_MODEL), lambda s: (0, 0)), row_spec],
        out_specs=row_spec,
        scratch_shapes=[pltpu.VMEM((2, tm, hd), BF16)],
        compiler_params=_cparams("arbitrary"),
        name="comb_proj_a",
    )(*outs, *lses, w_o, res)


def _mixer_a(h, gain, w_qkv, w_o, tabs_a):
    outs, lses = [], []
    for g, ((window, dilation), xn) in enumerate(zip(A_PATTERNS, _norm_perm_a(h, gain))):
        o, lse = _attn_a(_qkv_a(xn, w_qkv, tabs_a, g), window, dilation)
        outs.append(o)
        lses.append(lse)
    return _comb_proj_a(outs, lses, w_o, h)


def _b_in_kernel(x_ref, g_ref, w_ref, o_ref):
    o_ref[...] = _dot(_rms(x_ref[...], g_ref[...]).astype(BF16), w_ref[...])


def _b_in(h, gain, w_in):
    tm = 512
    return pl.pallas_call(
        _b_in_kernel,
        out_shape=jax.ShapeDtypeStruct((TOKENS, B_IN_COLS), F32),
        grid=(TOKENS // tm,),
        in_specs=[
            pl.BlockSpec((tm, D_MODEL), lambda i: (i, 0)),
            pl.BlockSpec((1, D_MODEL), lambda i: (0, 0)),
            pl.BlockSpec((D_MODEL, B_IN_COLS), lambda i: (0, 0)),
        ],
        out_specs=pl.BlockSpec((tm, B_IN_COLS), lambda i: (i, 0)),
        compiler_params=_cparams("parallel"),
        name="b_in",
    )(h, gain.reshape(1, D_MODEL), w_in)


def _b_q_kernel(c_ref, g_ref, w_ref, cos_ref, sin_ref, o_ref, xn_ref, *, slots):
    @pl.when(pl.program_id(1) == 0)
    def _():
        xn_ref[...] = _rms(c_ref[...], g_ref[...]).astype(BF16)

    y = _dot(xn_ref[...], w_ref[...])
    scale = (B_NOPE_DIM + B_ROPE_DIM) ** -0.5
    cos, sin = cos_ref[...], sin_ref[...]
    for s in range(slots):
        nope = y[:, s * B_SLOT:s * B_SLOT + LANES]
        rot = _rope128(y[:, s * B_SLOT + LANES:(s + 1) * B_SLOT], cos, sin)
        o_ref[s, :, :LANES] = (nope * scale).astype(BF16)
        o_ref[s, :, LANES:] = (rot * scale).astype(BF16)


def _b_q(c, gain, w_q, tabs_b):
    tm, slots = 1024, 4
    tn = slots * B_SLOT
    tab_spec = pl.BlockSpec((tm, LANES), lambda i, j: (i, 0))
    return pl.pallas_call(
        functools.partial(_b_q_kernel, slots=slots),
        out_shape=jax.ShapeDtypeStruct((B_HEADS, TOKENS, B_SLOT), BF16),
        grid=(TOKENS // tm, B_HEADS // slots),
        in_specs=[
            pl.BlockSpec((tm, B_Q_RANK), lambda i, j: (i, 0)),
            pl.BlockSpec((1, B_Q_RANK), lambda i, j: (0, 0)),
            pl.BlockSpec((B_Q_RANK, tn), lambda i, j: (0, j)),
            tab_spec, tab_spec,
        ],
        out_specs=pl.BlockSpec((slots, tm, B_SLOT), lambda i, j: (j, i, 0)),
        scratch_shapes=[pltpu.VMEM((tm, B_Q_RANK), BF16)],
        compiler_params=_cparams("parallel", "arbitrary"),
        name="b_q",
    )(c, gain.reshape(1, B_Q_RANK), w_q, *tabs_b)


def _b_kv_kernel(c_ref, g_ref, wk_ref, wv_ref, cr_ref, cos_ref, sin_ref, k_ref, v_ref, *, slots):
    xn = _rms(c_ref[...], g_ref[...]).astype(BF16)
    yk = _dot(xn, wk_ref[...])
    yv = _dot(xn, wv_ref[...])
    kr = _rope128(cr_ref[...], cos_ref[...], sin_ref[...]).astype(BF16)
    lane = lax.broadcasted_iota(jnp.int32, (c_ref.shape[0], B_VX - B_V_DIM), 1)
    ones_col = jnp.where(lane == 0, 1.0, 0.0).astype(BF16)
    for s in range(slots):
        sl = slice(s * LANES, (s + 1) * LANES)
        k_ref[s, :, :LANES] = yk[:, sl].astype(BF16)
        k_ref[s, :, LANES:] = kr
        v_ref[s, :, :B_V_DIM] = yv[:, sl].astype(BF16)
        v_ref[s, :, B_V_DIM:] = ones_col


def _b_kv(c, gain, w_k, w_v, tabs_b):
    tm, slots = 1024, 4
    tn = slots * LANES
    kv_block = B_Q_RANK // B_KV_RANK
    rope_block = (B_Q_RANK + B_KV_RANK) // LANES
    tab_spec = pl.BlockSpec((tm, LANES), lambda i, j: (i, 0))
    return pl.pallas_call(
        functools.partial(_b_kv_kernel, slots=slots),
        out_shape=(jax.ShapeDtypeStruct((B_HEADS, TOKENS, B_SLOT), BF16),
                   jax.ShapeDtypeStruct((B_HEADS, TOKENS, B_VX), BF16)),
        grid=(TOKENS // tm, B_HEADS // slots),
        in_specs=[
            pl.BlockSpec((tm, B_KV_RANK), lambda i, j: (i, kv_block)),
            pl.BlockSpec((1, B_KV_RANK), lambda i, j: (0, 0)),
            pl.BlockSpec((B_KV_RANK, tn), lambda i, j: (0, j)),
            pl.BlockSpec((B_KV_RANK, tn), lambda i, j: (0, j)),
            pl.BlockSpec((tm, LANES), lambda i, j: (i, rope_block)),
            tab_spec, tab_spec,
        ],
        out_specs=(pl.BlockSpec((slots, tm, B_SLOT), lambda i, j: (j, i, 0)),
                   pl.BlockSpec((slots, tm, B_VX), lambda i, j: (j, i, 0))),
        compiler_params=_cparams("parallel", "parallel"),
        name="b_kv",
    )(c, gain.reshape(1, B_KV_RANK), w_k, w_v, c, *tabs_b)


def _attn_b_kernel(q_ref, k_ref, v_ref, o_ref, s0_ref, s1_ref, m_ref, acc_ref, *, tq, heads):
    qi = pl.program_id(2)
    m_ref[...] = jnp.full_like(m_ref, NEG_INF)
    acc_ref[...] = jnp.zeros_like(acc_ref)

    def scores(s_ref, ki):
        for h in range(heads):
            k = k_ref[h, pl.ds(pl.multiple_of(ki * tq, tq), tq), :]
            s_ref[h] = _dot_t(q_ref[h], k)

    def fold(s_ref, ki, diagonal):
        for h in range(heads):
            s = s_ref[h]
            if diagonal:
                row = lax.broadcasted_iota(jnp.int32, (tq, tq), 0)
                col = lax.broadcasted_iota(jnp.int32, (tq, tq), 1)
                s = jnp.where(row >= col, s, NEG_INF)
            m_prev = m_ref[h]
            m_new = jnp.maximum(m_prev, jnp.max(s, axis=-1, keepdims=True))
            alpha = jnp.exp(m_prev - m_new)
            p = jnp.exp((s - jnp.concatenate([m_new] * (tq // LANES), axis=1)).astype(BF16))
            v = v_ref[h, pl.ds(pl.multiple_of(ki * tq, tq), tq), :]
            acc_ref[h] = jnp.concatenate([alpha] * (B_VX // LANES), axis=1) * acc_ref[h] + _dot(p, v)
            m_ref[h] = m_new

    pairs = qi // 2
    scores(s0_ref, 0)

    def body(t, carry):
        scores(s1_ref, 2 * t + 1)
        fold(s0_ref, 2 * t, False)
        scores(s0_ref, 2 * t + 2)
        fold(s1_ref, 2 * t + 1, False)
        return carry

    lax.fori_loop(0, pairs, body, 0)

    @pl.when(qi % 2 == 1)
    def _():
        scores(s1_ref, qi)
        fold(s0_ref, qi - 1, False)
        fold(s1_ref, qi, True)

    @pl.when(qi % 2 == 0)
    def _():
        fold(s0_ref, qi, True)

    for h in range(heads):
        acc = acc_ref[h]
        l = jnp.broadcast_to(acc[:, B_V_DIM:B_V_DIM + 1], (tq, B_V_DIM))
        o_ref[:, h * B_V_DIM:(h + 1) * B_V_DIM] = (acc[:, :B_V_DIM] / l).astype(BF16)


def _attn_b(q3, k3, v3):
    tq, heads = 512, 2
    nq = SEQ // tq
    return pl.pallas_call(
        functools.partial(_attn_b_kernel, tq=tq, heads=heads),
        out_shape=jax.ShapeDtypeStruct((TOKENS, B_HEADS * B_V_DIM), BF16),
        grid=(BATCH, B_HEADS // heads, nq),
        in_specs=[
            pl.BlockSpec((heads, tq, B_SLOT), lambda b, g, qi: (g, b * nq + qi, 0)),
            pl.BlockSpec((heads, SEQ, B_SLOT), lambda b, g, qi: (g, b, 0)),
            pl.BlockSpec((heads, SEQ, B_VX), lambda b, g, qi: (g, b, 0)),
        ],
        out_specs=pl.BlockSpec((tq, heads * B_V_DIM), lambda b, g, qi: (b * nq + qi, g)),
        scratch_shapes=[pltpu.VMEM((heads, tq, tq), F32), pltpu.VMEM((heads, tq, tq), F32),
                        pltpu.VMEM((heads, tq, LANES), F32), pltpu.VMEM((heads, tq, B_VX), F32)],
        compiler_params=_cparams("parallel", "parallel", "arbitrary"),
        name="attn_b",
    )(q3, k3, v3)


def _mixer_b(h, gain, w_in, q_norm, w_q, kv_norm, w_k, w_v, w_o, tabs_b):
    c = _b_in(h, gain, w_in)
    q3 = _b_q(c, q_norm, w_q, tabs_b)
    k3, v3 = _b_kv(c, kv_norm, w_k, w_v, tabs_b)
    return _out_proj(_attn_b(q3, k3, v3), w_o, h)


def _c_in_kernel(x_ref, g_ref, wb_ref, wc_ref, wu_ref, cw_ref, o_ref, xn_ref, carry_ref, *, tm):
    i = pl.program_id(0)
    j = pl.program_id(1)

    @pl.when(j == 0)
    def _():
        xn_ref[...] = _rms(x_ref[...], g_ref[...]).astype(BF16)

    xn = xn_ref[...]
    b_gate = _dot(xn, wb_ref[...])
    cu = _dot(xn, wc_ref[...]) * _dot(xn, wu_ref[...])
    w0, w1, w2 = cw_ref[0:1, :], cw_ref[1:2, :], cw_ref[2:3, :]
    z = w2 * cu + w1 * pltpu.roll(cu, 1, 0) + w0 * pltpu.roll(cu, 2, 0)
    o_ref[...] = (b_gate * z).astype(BF16)

    first = (i * tm) % SEQ == 0
    tail = jnp.where(first, 0.0, carry_ref[j])
    both = jnp.concatenate([tail, cu[0:8, :]], axis=0)
    z8 = w2 * both[8:16] + w1 * both[7:15] + w0 * both[6:14]
    o_ref[0:8, :] = (b_gate[0:8, :] * z8).astype(BF16)
    carry_ref[j] = cu[tm - 8:tm, :]


def _c_in(h, gain, w_in, conv_w):
    tm, tn = 1024, 512
    nj = D_MODEL // tn
    return pl.pallas_call(
        functools.partial(_c_in_kernel, tm=tm),
        out_shape=jax.ShapeDtypeStruct((TOKENS, D_MODEL), BF16),
        grid=(TOKENS // tm, nj),
        in_specs=[
            pl.BlockSpec((tm, D_MODEL), lambda i, j: (i, 0)),
            pl.BlockSpec((1, D_MODEL), lambda i, j: (0, 0)),
            pl.BlockSpec((D_MODEL, tn), lambda i, j: (0, j)),
            pl.BlockSpec((D_MODEL, tn), lambda i, j: (0, j + nj)),
            pl.BlockSpec((D_MODEL, tn), lambda i, j: (0, j + 2 * nj)),
            pl.BlockSpec((C_CONV, tn), lambda i, j: (0, j)),
        ],
        out_specs=pl.BlockSpec((tm, tn), lambda i, j: (i, j)),
        scratch_shapes=[pltpu.VMEM((tm, D_MODEL), BF16), pltpu.VMEM((nj, 8, tn), F32)],
        compiler_params=_cparams("arbitrary", "arbitrary"),
        name="c_in",
    )(h, gain.reshape(1, D_MODEL), w_in, w_in, w_in, conv_w)


def _mixer_c(h, gain, w_in, conv_w, w_out):
    return _out_proj(_c_in(h, gain, w_in, conv_w), w_out, h)


def kernel(x, positions, l0_ffn1_norm, l0_ffn1_w13, l0_ffn1_w2, l0_mix_norm, l0_a_w_qkv, l0_a_w_o, l0_ffn2_norm, l0_ffn2_w13, l0_ffn2_w2, l1_ffn1_norm, l1_ffn1_w13, l1_ffn1_w2, l1_mix_norm, l1_b_w_in, l1_b_q_norm, l1_b_w_uq, l1_b_kv_norm, l1_b_w_ukv, l1_b_w_o, l1_ffn2_norm, l1_ffn2_w13, l1_ffn2_w2, l2_ffn1_norm, l2_ffn1_w13, l2_ffn1_w2, l2_mix_norm, l2_c_w_in, l2_c_conv_w, l2_c_w_out, l2_ffn2_norm, l2_ffn2_w13, l2_ffn2_w2, l3_ffn1_norm, l3_ffn1_w13, l3_ffn1_w2, l3_mix_norm, l3_a_w_qkv, l3_a_w_o, l3_ffn2_norm, l3_ffn2_w13, l3_ffn2_w2, final_norm):
    bf = lambda w: w.astype(BF16)
    pos_tiles = positions.reshape(TOKENS // A_TILE, A_TILE)
    pos_a = [pos_tiles.reshape(-1, A_TILE // d, d).swapaxes(1, 2).reshape(TOKENS) for _, d in A_PATTERNS]
    tabs_a = _rope_tables(jnp.concatenate(pos_a + [jnp.zeros((A_TILE,), positions.dtype)]), A_ROT_HALF)
    tabs_b = _rope_tables(positions.reshape(TOKENS), B_ROT_HALF)

    def rope_group(w):
        return _rope_lanes(jnp.pad(w, [(0, 0)] * (w.ndim - 1) + [(0, LANES - B_ROPE_DIM)]), B_ROT_HALF)

    w_uq = l1_b_w_uq.reshape(B_Q_RANK, B_HEADS, B_NOPE_DIM + B_ROPE_DIM)
    w_q = jnp.concatenate([w_uq[..., :B_NOPE_DIM], rope_group(w_uq[..., B_NOPE_DIM:])], axis=-1)
    w_q = bf(w_q.reshape(B_Q_RANK, B_HEADS * B_SLOT))
    w_ukv = l1_b_w_ukv.reshape(B_KV_RANK, B_HEADS, B_NOPE_DIM + B_V_DIM)
    w_k = bf(w_ukv[:, :, :B_NOPE_DIM].reshape(B_KV_RANK, B_HEADS * B_NOPE_DIM))
    w_v = bf(w_ukv[:, :, B_NOPE_DIM:].reshape(B_KV_RANK, B_HEADS * B_V_DIM))
    n_c = B_Q_RANK + B_KV_RANK
    w_b_in = bf(jnp.concatenate([l1_b_w_in[:, :n_c], rope_group(l1_b_w_in[:, n_c:])], axis=-1))

    h = x.reshape(TOKENS, D_MODEL)

    h, w13, w2 = _ffn(h, l0_ffn1_norm, bf(l0_ffn1_w13), bf(l0_ffn1_w2), next_f32=(l0_ffn2_w13, l0_ffn2_w2))
    h = _mixer_a(h, l0_mix_norm, _qkv_w(l0_a_w_qkv), bf(l0_a_w_o), tabs_a)
    h, w13, w2 = _ffn(h, l0_ffn2_norm, w13, w2, next_f32=(l1_ffn1_w13, l1_ffn1_w2))

    h, w13, w2 = _ffn(h, l1_ffn1_norm, w13, w2, next_f32=(l1_ffn2_w13, l1_ffn2_w2))
    h = _mixer_b(h, l1_mix_norm, w_b_in, l1_b_q_norm, w_q, l1_b_kv_norm, w_k, w_v, bf(l1_b_w_o), tabs_b)
    h, w13, w2 = _ffn(h, l1_ffn2_norm, w13, w2, next_f32=(l2_ffn1_w13, l2_ffn1_w2))

    h, w13, w2 = _ffn(h, l2_ffn1_norm, w13, w2, next_f32=(l2_ffn2_w13, l2_ffn2_w2))
    h = _mixer_c(h, l2_mix_norm, bf(l2_c_w_in), l2_c_conv_w, bf(l2_c_w_out))
    h, w13, w2 = _ffn(h, l2_ffn2_norm, w13, w2, next_f32=(l3_ffn1_w13, l3_ffn1_w2))

    h, w13, w2 = _ffn(h, l3_ffn1_norm, w13, w2, next_f32=(l3_ffn2_w13, l3_ffn2_w2))
    h = _mixer_a(h, l3_mix_norm, _qkv_w(l3_a_w_qkv), bf(l3_a_w_o), tabs_a)
    h = _ffn(h, l3_ffn2_norm, w13, w2, final_gain=final_norm)

    return h.reshape(BATCH, SEQ, D_MODEL)
```

```python
import functools

import jax
import jax.numpy as jnp
from jax import lax
from jax.experimental import pallas as pl
from jax.experimental.pallas import tpu as pltpu

F32 = jnp.float32
BF16 = jnp.bfloat16

D_MODEL = 2048
BATCH = 2
SEQ = 8192
TOKENS = BATCH * SEQ
D_FF = 5632
HALF_STEP = 0.5
NORM_EPS = 1e-6
NEG_INF = -1e30
ROPE_THETA = 500000.0

A_HEADS = 16
A_HEAD_DIM = 128
A_ROT_HALF = A_HEAD_DIM // 8
A_PATTERNS = ((128, 1), (512, 4), (2048, 16))
A_BLOCK = 128
A_TILE = 2048
A_MID_DILATION = 4
assert [d for _, d in A_PATTERNS] == [1, A_MID_DILATION, A_MID_DILATION ** 2]

B_HEADS = 16
B_Q_RANK = 1536
B_KV_RANK = 512
B_NOPE_DIM = 128
B_ROPE_DIM = 64
B_ROT_HALF = B_ROPE_DIM // 2
B_V_DIM = 128
B_SLOT = 256
B_VX = 256
B_IN_COLS = 2176

C_CONV = 3

LANES = 128
MXU_COLS = 256
ROPE_PAIR = LANES // 2
VMEM_LIMIT = 56 * 1024 * 1024


def _cparams(*sem):
    return pltpu.CompilerParams(dimension_semantics=sem, vmem_limit_bytes=VMEM_LIMIT)


def _rms(x, gain):
    ms = jnp.mean(x * x, axis=-1, keepdims=True)
    return x * lax.rsqrt(ms + NORM_EPS) * gain


def _dot(a, b):
    return jnp.dot(a, b, preferred_element_type=F32)


def _dot_t(a, b):
    return lax.dot_general(a, b, (((1,), (1,)), ((), ())), preferred_element_type=F32)


def _rope128(y, cos, sin):
    return y * cos + pltpu.roll(y, ROPE_PAIR, 1) * sin


def _rope_lanes(w, half):
    return jnp.concatenate([w[..., :half], w[..., 2 * half:ROPE_PAIR + half], w[..., half:2 * half],
                            w[..., ROPE_PAIR + half:]], axis=-1)


def _rope_tab_kernel(pos_ref, inv_ref, c_ref, s_ref):
    ang = pos_ref[...].astype(F32) * inv_ref[...]
    lane = lax.broadcasted_iota(jnp.int32, (1, LANES), 1)
    sn = jnp.sin(ang)
    c_ref[...] = jnp.cos(ang)
    s_ref[...] = jnp.where(lane < ROPE_PAIR, -sn, sn)


def _rope_tables(pos_flat, half):
    inv = ROPE_THETA ** (-jnp.arange(half, dtype=F32) / half)
    gap = jnp.zeros((ROPE_PAIR - half,), F32)
    inv = jnp.concatenate([inv, gap, inv, gap]).reshape(1, LANES)
    n = pos_flat.shape[0]
    rows = 2048
    tab = jax.ShapeDtypeStruct((n, LANES), F32)
    row_spec = pl.BlockSpec((rows, LANES), lambda i: (i, 0))
    return pl.pallas_call(
        _rope_tab_kernel,
        out_shape=(tab,) * 2,
        grid=(n // rows,),
        in_specs=[pl.BlockSpec((rows, 1), lambda i: (i, 0)), pl.BlockSpec((1, LANES), lambda i: (0, 0))],
        out_specs=(row_spec,) * 2,
        compiler_params=_cparams("parallel"),
        name="rope_tables",
    )(pos_flat.reshape(n, 1), inv)


def _ffn_kernel(x_ref, g_ref, w1_ref, w3_ref, w2_ref, *refs, final, cast_next):
    j = pl.program_id(1)
    if cast_next:
        n13_ref, n2_ref, o_ref, c13_ref, c2_ref, xn_ref, acc_ref = refs
    else:
        o_ref, xn_ref, acc_ref = refs

    @pl.when(j == 0)
    def _():
        xn_ref[...] = _rms(x_ref[...], g_ref[0:1, :]).astype(BF16)
        acc_ref[...] = jnp.zeros_like(acc_ref)

    xn = xn_ref[...]
    gate = _dot(xn, w1_ref[...])
    up = _dot(xn, w3_ref[...])
    act = (gate * (1.0 / (1.0 + jnp.exp(-gate))) * up).astype(BF16)
    acc_ref[...] += _dot(act, w2_ref[...])
    if cast_next:
        c13_ref[...] = n13_ref[...].astype(BF16)
        c2_ref[...] = n2_ref[...].astype(BF16)

    @pl.when(j == pl.num_programs(1) - 1)
    def _():
        y = x_ref[...] + HALF_STEP * acc_ref[...]
        if final:
            y = _rms(y, g_ref[1:2, :])
        o_ref[...] = y


def _ffn(h, gain, w13, w2, final_gain=None, next_f32=None):
    tm, tf = 512, 512
    nj = D_FF // tf
    ni = TOKENS // tm
    final = final_gain is not None
    gains = jnp.stack([gain, final_gain if final else gain])
    in_specs = [
        pl.BlockSpec((tm, D_MODEL), lambda i, j: (i, 0)),
        pl.BlockSpec((2, D_MODEL), lambda i, j: (0, 0)),
        pl.BlockSpec((D_MODEL, tf), lambda i, j: (0, j)),
        pl.BlockSpec((D_MODEL, tf), lambda i, j: (0, j + nj)),
        pl.BlockSpec((tf, D_MODEL), lambda i, j: (j, 0)),
    ]
    out_shape = [jax.ShapeDtypeStruct((TOKENS, D_MODEL), F32)]
    out_specs = [pl.BlockSpec((tm, D_MODEL), lambda i, j: (i, 0))]
    args = [h, gains, w13, w13, w2]
    if next_f32 is not None:
        slab13 = pl.BlockSpec((D_MODEL // ni, 2 * D_FF // nj), lambda i, j: (i, j))
        slab2 = pl.BlockSpec((D_FF // (ni * nj), D_MODEL), lambda i, j: (i * nj + j, 0))
        in_specs += [slab13, slab2]
        out_specs += [slab13, slab2]
        out_shape += [jax.ShapeDtypeStruct(w.shape, BF16) for w in next_f32]
        args += list(next_f32)
    outs = pl.pallas_call(
        functools.partial(_ffn_kernel, final=final, cast_next=next_f32 is not None),
        out_shape=tuple(out_shape),
        grid=(ni, nj),
        in_specs=in_specs,
        out_specs=tuple(out_specs),
        scratch_shapes=[pltpu.VMEM((tm, D_MODEL), BF16), pltpu.VMEM((tm, D_MODEL), F32)],
        compiler_params=_cparams("parallel", "arbitrary"),
        name="ffn",
    )(*args)
    return outs if next_f32 is not None else outs[0]


def _out_proj_kernel(x_ref, w_ref, r_ref, o_ref):
    o_ref[...] = r_ref[...] + _dot(x_ref[...], w_ref[...])


def _out_proj(x, w, res):
    k = x.shape[1]
    tm = 512
    return pl.pallas_call(
        _out_proj_kernel,
        out_shape=jax.ShapeDtypeStruct((TOKENS, D_MODEL), F32),
        grid=(TOKENS // tm,),
        in_specs=[
            pl.BlockSpec((tm, k), lambda i: (i, 0)),
            pl.BlockSpec((k, D_MODEL), lambda i: (0, 0)),
            pl.BlockSpec((tm, D_MODEL), lambda i: (i, 0)),
        ],
        out_specs=pl.BlockSpec((tm, D_MODEL), lambda i: (i, 0)),
        compiler_params=_cparams("parallel"),
        name="out_proj",
    )(x, w, res)


def _norm_perm_a_kernel(x_ref, g_ref, *refs, tm):
    o_refs, xs_ref, mid_ref = refs[:-2], refs[-2], refs[-1]
    xn = _rms(x_ref[...], g_ref[...])
    nchunk = D_MODEL // LANES
    for k in range(nchunk):
        xs_ref[k] = xn[:, k * LANES:(k + 1) * LANES]
    dp = 1
    for (_, d), o_ref in zip(A_PATTERNS, o_refs):
        if d == 1:
            o_ref[...] = xn.astype(BF16)
            continue
        step = d // dp
        for c in range(d):
            for k in range(nchunk):
                rows = pl.ds(c // dp, tm // d, stride=step)
                piece = xs_ref[k, rows, :] if dp == 1 else mid_ref[k, c % dp, rows, :]
                o_ref[0, c, 0, :, k * LANES:(k + 1) * LANES] = piece.astype(BF16)
                if d == A_MID_DILATION:
                    mid_ref[k, c] = piece
        dp = d


def _norm_perm_a(h, gain):
    tm = 512
    per_tile = A_TILE // tm
    shapes, specs = [], []
    for _, d in A_PATTERNS:
        if d == 1:
            shapes.append(jax.ShapeDtypeStruct((TOKENS, D_MODEL), BF16))
            specs.append(pl.BlockSpec((tm, D_MODEL), lambda i: (i, 0)))
        else:
            shapes.append(jax.ShapeDtypeStruct((TOKENS // A_TILE, d, per_tile, tm // d, D_MODEL), BF16))
            specs.append(pl.BlockSpec((1, d, 1, tm // d, D_MODEL),
                                      lambda i: (i // per_tile, 0, i % per_tile, 0, 0)))
    outs = pl.pallas_call(
        functools.partial(_norm_perm_a_kernel, tm=tm),
        out_shape=tuple(shapes),
        grid=(TOKENS // tm,),
        in_specs=[pl.BlockSpec((tm, D_MODEL), lambda i: (i, 0)), pl.BlockSpec((1, D_MODEL), lambda i: (0, 0))],
        out_specs=tuple(specs),
        scratch_shapes=[pltpu.VMEM((D_MODEL // LANES, tm, LANES), F32),
                        pltpu.VMEM((D_MODEL // LANES, A_MID_DILATION, tm // A_MID_DILATION, LANES), F32)],
        compiler_params=_cparams("parallel"),
        name="norm_perm_a",
    )(h, gain.reshape(1, D_MODEL))
    return [o.reshape(TOKENS, D_MODEL) for o in outs]


def _qkv_w_kernel(w_ref, o_ref):
    part = pl.program_id(1) % 3

    @pl.when(part == 2)
    def _():
        o_ref[...] = w_ref[...].astype(BF16)

    @pl.when(part != 2)
    def _():
        lane = lax.broadcasted_iota(jnp.int32, (1, LANES), 1)
        stay = (lane < A_ROT_HALF) | (lane >= ROPE_PAIR + A_ROT_HALF)
        for hh in range(A_HEADS):
            sl = slice(hh * A_HEAD_DIM, (hh + 1) * A_HEAD_DIM)
            y = w_ref[:, sl]
            moved = jnp.where(lane < ROPE_PAIR, pltpu.roll(y, LANES - A_ROT_HALF, 1),
                              pltpu.roll(y, ROPE_PAIR - A_ROT_HALF, 1))
            o_ref[:, sl] = jnp.where(stay, y, moved).astype(BF16)


def _qkv_w(w_qkv):
    rows = 256
    hd = A_HEADS * A_HEAD_DIM
    spec = pl.BlockSpec((rows, hd), lambda i, j: (i, j))
    return pl.pallas_call(
        _qkv_w_kernel,
        out_shape=jax.ShapeDtypeStruct(w_qkv.shape, BF16),
        grid=(D_MODEL // rows, w_qkv.shape[1] // hd),
        in_specs=[spec],
        out_specs=spec,
        compiler_params=_cparams("parallel", "parallel"),
        name="qkv_w",
    )(w_qkv)


def _qkv_a_kernel(x_ref, w_ref, cos_ref, sin_ref, o_ref, *, tn):
    part = pl.program_id(1) // ((A_HEADS * A_HEAD_DIM) // tn)
    scale = jnp.where(part == 0, A_HEAD_DIM ** -0.5, 1.0).astype(F32)
    half = x_ref.shape[0] // 2
    for cc in range(tn // MXU_COLS):
        for rows in (slice(0, half), slice(half, 2 * half)):
            y = _dot(x_ref[rows, :], w_ref[:, cc * MXU_COLS:(cc + 1) * MXU_COLS])
            for hh in range(MXU_COLS // LANES):
                sl = slice(cc * MXU_COLS + hh * LANES, cc * MXU_COLS + (hh + 1) * LANES)
                y_h = y[:, hh * LANES:(hh + 1) * LANES]
                o_ref[rows, sl] = (_rope128(y_h, cos_ref[rows, :], sin_ref[rows, :]) * scale).astype(BF16)


def _qkv_a(xn, w_qkv, tabs, group):
    tm, tn = 1024, 2048
    cols = 3 * A_HEADS * A_HEAD_DIM
    nj = cols // tn
    ni = TOKENS // tm
    v_first = 2 * (A_HEADS * A_HEAD_DIM) // tn
    tab_spec = pl.BlockSpec((tm, LANES),
                            lambda i, j: (jnp.where(j >= v_first, len(A_PATTERNS) * ni, group * ni + i), 0))
    return pl.pallas_call(
        functools.partial(_qkv_a_kernel, tn=tn),
        out_shape=jax.ShapeDtypeStruct((TOKENS, cols), BF16),
        grid=(ni, nj),
        in_specs=[
            pl.BlockSpec((tm, D_MODEL), lambda i, j: (i, 0)),
            pl.BlockSpec((D_MODEL, tn), lambda i, j: (0, group * nj + j)),
            tab_spec, tab_spec,
        ],
        out_specs=pl.BlockSpec((tm, tn), lambda i, j: (i, j)),
        compiler_params=_cparams("parallel", "parallel"),
        name="qkv_a",
    )(xn, w_qkv, *tabs)


def _attn_a_kernel(q_ref, kc_ref, kp_ref, vc_ref, vp_ref, o_ref, lse_ref, *scratch, d, heads, per_trip):
    rows = A_TILE // d
    nblk = rows // A_BLOCK
    two_stage = d > A_MID_DILATION
    if two_stage:
        os_ref, mid_ref = scratch
        stage_ref, mid_step = mid_ref, d // A_MID_DILATION
    else:
        (os_ref,) = scratch
        stage_ref = os_ref
    first_tile = pl.program_id(1) == 0
    r = lax.broadcasted_iota(jnp.int32, (A_BLOCK, 2 * A_BLOCK), 0)
    kk = lax.broadcasted_iota(jnp.int32, (A_BLOCK, 2 * A_BLOCK), 1)
    band = (kk >= r) & (kk <= r + A_BLOCK)
    lane = lax.broadcasted_iota(jnp.int32, (A_BLOCK, LANES), 1)
    hs = [slice(h * A_HEAD_DIM, (h + 1) * A_HEAD_DIM) for h in range(heads)]
    first_head = pl.program_id(2) * heads

    @pl.when(first_head == 0)
    def _():
        lse_ref[...] = jnp.zeros_like(lse_ref)

    def trip(t, carry):
        loaded = []
        for i in range(per_trip):
            u = t * per_trip + i
            c = u // nblk
            blk = u % nblk
            start = pl.multiple_of(blk * A_BLOCK, A_BLOCK)
            before = pl.multiple_of(jnp.maximum(start - A_BLOCK, 0), A_BLOCK)
            q = q_ref[0, c, pl.ds(start, A_BLOCK), :]
            k = jnp.concatenate([jnp.where(blk == 0, kp_ref[0, c], kc_ref[0, c, pl.ds(before, A_BLOCK), :]),
                                 kc_ref[0, c, pl.ds(start, A_BLOCK), :]], axis=0)
            v = jnp.concatenate([jnp.where(blk == 0, vp_ref[0, c], vc_ref[0, c, pl.ds(before, A_BLOCK), :]),
                                 vc_ref[0, c, pl.ds(start, A_BLOCK), :]], axis=0)
            no_prev = jnp.logical_and(first_tile, blk == 0)
            mask = jnp.logical_and(band, jnp.logical_or(kk >= A_BLOCK, jnp.logical_not(no_prev)))
            out_rows = (pl.ds(blk * (A_BLOCK * d) + c, A_BLOCK, stride=d) if d > 1
                        else pl.ds(start, A_BLOCK))
            if two_stage:
                o_at = (c % A_MID_DILATION,
                        pl.ds(blk * (A_BLOCK * mid_step) + c // A_MID_DILATION, A_BLOCK, stride=mid_step))
            else:
                o_at = (out_rows,)
            loaded.append(([_dot_t(q[:, sl], k[:, sl]) for sl in hs], v, mask, out_rows, o_at))
        for scores, v, mask, out_rows, o_at in loaded:
            lse_tile = lse_ref[out_rows, :]
            for h, sl in enumerate(hs):
                s = jnp.where(mask, scores[h], NEG_INF)
                m = jnp.max(s, axis=-1, keepdims=True)
                p = jnp.exp(s - m)
                l = jnp.sum(p, axis=-1, keepdims=True)
                stage_ref[(h,) + o_at + (slice(None),)] = _dot(p.astype(BF16), v[:, sl]) / l
                lse_tile = jnp.where(lane == first_head + h, m + jnp.log(l), lse_tile)
            lse_ref[out_rows, :] = lse_tile
        return carry

    lax.fori_loop(0, d * nblk // per_trip, trip, 0)
    if two_stage:
        for h in range(heads):
            for c4 in range(A_MID_DILATION):
                os_ref[h, pl.ds(c4, A_TILE // A_MID_DILATION, stride=A_MID_DILATION), :] = mid_ref[h, c4]
    o_ref[...] = os_ref[...].astype(BF16)


def _attn_a(qkv, window, dilation):
    assert window // dilation == A_BLOCK, "band width must equal the query block"
    d = dilation
    heads = 4
    hw = heads * A_HEAD_DIM
    ngrp = A_HEADS // heads
    rows = A_TILE // d
    tiles = SEQ // A_TILE
    qkv_v = qkv.reshape(BATCH, tiles, d, rows, 3 * A_HEADS * A_HEAD_DIM)
    last_blk = rows // A_BLOCK - 1

    def cur(part):
        return pl.BlockSpec((None, 1, d, rows, hw), lambda b, n, g: (b, n, 0, 0, part * ngrp + g))

    def prev(part):
        return pl.BlockSpec((None, 1, d, A_BLOCK, hw),
                            lambda b, n, g: (b, jnp.maximum(n - 1, 0), 0, last_blk, part * ngrp + g))

    return pl.pallas_call(
        functools.partial(_attn_a_kernel, d=d, heads=heads, per_trip=4),
        out_shape=(jax.ShapeDtypeStruct((A_HEADS, TOKENS, A_HEAD_DIM), BF16),
                   jax.ShapeDtypeStruct((TOKENS, LANES), F32)),
        grid=(BATCH, tiles, ngrp),
        in_specs=[cur(0), cur(1), prev(1), cur(2), prev(2)],
        out_specs=(pl.BlockSpec((heads, A_TILE, A_HEAD_DIM), lambda b, n, g: (g, b * tiles + n, 0)),
                   pl.BlockSpec((A_TILE, LANES), lambda b, n, g: (b * tiles + n, 0))),
        scratch_shapes=[pltpu.VMEM((heads, A_TILE, A_HEAD_DIM), F32)] + (
            [pltpu.VMEM((heads, A_MID_DILATION, A_TILE // A_MID_DILATION, A_HEAD_DIM), F32)]
            if d > A_MID_DILATION else []),
        compiler_params=_cparams("parallel", "parallel", "arbitrary"),
        name="attn_a_d%d" % d,
    )(qkv_v, qkv_v, qkv_v, qkv_v, qkv_v)


def _comb_proj_a_kernel(o0, o1, o2, l0, l1, l2, w_ref, r_ref, out_ref, comb_ref):
    s = pl.program_id(0)
    slot = s % 2

    @pl.when(s == 0)
    def _():
        comb_ref[1] = jnp.zeros(comb_ref.shape[1:], BF16)

    out_ref[...] = r_ref[...] + _dot(comb_ref[1 - slot], w_ref[...])
    a = [l0[...], l1[...], l2[...]]
    m = jnp.maximum(jnp.maximum(a[0], a[1]), a[2])
    e = [jnp.exp(t - m) for t in a]
    den = e[0] + e[1] + e[2]
    w = [t / den for t in e]
    rows = out_ref.shape[0]
    for h in range(A_HEADS):
        acc = jnp.zeros((rows, A_HEAD_DIM), F32)
        for wg, og in zip(w, (o0, o1, o2)):
            acc = acc + jnp.broadcast_to(wg[:, h:h + 1], (rows, A_HEAD_DIM)) * og[h]
        comb_ref[slot, :, h * A_HEAD_DIM:(h + 1) * A_HEAD_DIM] = acc.astype(BF16)


def _comb_proj_a(outs, lses, w_o, res):
    tm = 256
    nt = TOKENS // tm
    hd = A_HEADS * A_HEAD_DIM
    o_spec = pl.BlockSpec((A_HEADS, tm, A_HEAD_DIM), lambda s: (0, jnp.minimum(s, nt - 1), 0))
    l_spec = pl.BlockSpec((tm, LANES), lambda s: (jnp.minimum(s, nt - 1), 0))
    row_spec = pl.BlockSpec((tm, D_MODEL), lambda s: (jnp.maximum(s - 1, 0), 0))
    return pl.pallas_call(
        _comb_proj_a_kernel,
        out_shape=jax.ShapeDtypeStruct((TOKENS, D_MODEL), F32),
        grid=(nt + 1,),
        in_specs=[o_spec] * 3 + [l_spec] * 3 + [pl.BlockSpec((hd, D_MODEL), lambda s: (0, 0)), row_spec],
        out_specs=row_spec,
        scratch_shapes=[pltpu.VMEM((2, tm, hd), BF16)],
        compiler_params=_cparams("arbitrary"),
        name="comb_proj_a",
    )(*outs, *lses, w_o, res)


def _mixer_a(h, gain, w_qkv, w_o, tabs_a):
    outs, lses = [], []
    for g, ((window, dilation), xn) in enumerate(zip(A_PATTERNS, _norm_perm_a(h, gain))):
        o, lse = _attn_a(_qkv_a(xn, w_qkv, tabs_a, g), window, dilation)
        outs.append(o)
        lses.append(lse)
    return _comb_proj_a(outs, lses, w_o, h)


def _b_in_kernel(x_ref, g_ref, w_ref, o_ref):
    o_ref[...] = _dot(_rms(x_ref[...], g_ref[...]).astype(BF16), w_ref[...])


def _b_in(h, gain, w_in):
    tm = 512
    return pl.pallas_call(
        _b_in_kernel,
        out_shape=jax.ShapeDtypeStruct((TOKENS, B_IN_COLS), F32),
        grid=(TOKENS // tm,),
        in_specs=[
            pl.BlockSpec((tm, D_MODEL), lambda i: (i, 0)),
            pl.BlockSpec((1, D_MODEL), lambda i: (0, 0)),
            pl.BlockSpec((D_MODEL, B_IN_COLS), lambda i: (0, 0)),
        ],
        out_specs=pl.BlockSpec((tm, B_IN_COLS), lambda i: (i, 0)),
        compiler_params=_cparams("parallel"),
        name="b_in",
    )(h, gain.reshape(1, D_MODEL), w_in)


def _b_q_kernel(c_ref, g_ref, w_ref, cos_ref, sin_ref, o_ref, xn_ref, *, slots):
    @pl.when(pl.program_id(1) == 0)
    def _():
        xn_ref[...] = _rms(c_ref[...], g_ref[...]).astype(BF16)

    y = _dot(xn_ref[...], w_ref[...])
    scale = (B_NOPE_DIM + B_ROPE_DIM) ** -0.5
    cos, sin = cos_ref[...], sin_ref[...]
    for s in range(slots):
        nope = y[:, s * B_SLOT:s * B_SLOT + LANES]
        rot = _rope128(y[:, s * B_SLOT + LANES:(s + 1) * B_SLOT], cos, sin)
        o_ref[s, :, :LANES] = (nope * scale).astype(BF16)
        o_ref[s, :, LANES:] = (rot * scale).astype(BF16)


def _b_q(c, gain, w_q, tabs_b):
    tm, slots = 1024, 8
    tn = slots * B_SLOT
    tab_spec = pl.BlockSpec((tm, LANES), lambda i, j: (i, 0))
    return pl.pallas_call(
        functools.partial(_b_q_kernel, slots=slots),
        out_shape=jax.ShapeDtypeStruct((B_HEADS, TOKENS, B_SLOT), BF16),
        grid=(TOKENS // tm, B_HEADS // slots),
        in_specs=[
            pl.BlockSpec((tm, B_Q_RANK), lambda i, j: (i, 0)),
            pl.BlockSpec((1, B_Q_RANK), lambda i, j: (0, 0)),
            pl.BlockSpec((B_Q_RANK, tn), lambda i, j: (0, j)),
            tab_spec, tab_spec,
        ],
        out_specs=pl.BlockSpec((slots, tm, B_SLOT), lambda i, j: (j, i, 0)),
        scratch_shapes=[pltpu.VMEM((tm, B_Q_RANK), BF16)],
        compiler_params=_cparams("parallel", "arbitrary"),
        name="b_q",
    )(c, gain.reshape(1, B_Q_RANK), w_q, *tabs_b)


def _b_kv_kernel(c_ref, g_ref, wk_ref, wv_ref, cr_ref, cos_ref, sin_ref, k_ref, v_ref, *, slots):
    xn = _rms(c_ref[...], g_ref[...]).astype(BF16)
    yk = _dot(xn, wk_ref[...])
    yv = _dot(xn, wv_ref[...])
    kr = _rope128(cr_ref[...], cos_ref[...], sin_ref[...]).astype(BF16)
    lane = lax.broadcasted_iota(jnp.int32, (c_ref.shape[0], B_VX - B_V_DIM), 1)
    ones_col = jnp.where(lane == 0, 1.0, 0.0).astype(BF16)
    for s in range(slots):
        sl = slice(s * LANES, (s + 1) * LANES)
        k_ref[s, :, :LANES] = yk[:, sl].astype(BF16)
        k_ref[s, :, LANES:] = kr
        v_ref[s, :, :B_V_DIM] = yv[:, sl].astype(BF16)
        v_ref[s, :, B_V_DIM:] = ones_col


def _b_kv(c, gain, w_k, w_v, tabs_b):
    tm, slots = 1024, 8
    tn = slots * LANES
    kv_block = B_Q_RANK // B_KV_RANK
    rope_block = (B_Q_RANK + B_KV_RANK) // LANES
    tab_spec = pl.BlockSpec((tm, LANES), lambda i, j: (i, 0))
    return pl.pallas_call(
        functools.partial(_b_kv_kernel, slots=slots),
        out_shape=(jax.ShapeDtypeStruct((B_HEADS, TOKENS, B_SLOT), BF16),
                   jax.ShapeDtypeStruct((B_HEADS, TOKENS, B_VX), BF16)),
        grid=(TOKENS // tm, B_HEADS // slots),
        in_specs=[
            pl.BlockSpec((tm, B_KV_RANK), lambda i, j: (i, kv_block)),
            pl.BlockSpec((1, B_KV_RANK), lambda i, j: (0, 0)),
            pl.BlockSpec((B_KV_RANK, tn), lambda i, j: (0, j)),
            pl.BlockSpec((B_KV_RANK, tn), lambda i, j: (0, j)),
            pl.BlockSpec((tm, LANES), lambda i, j: (i, rope_block)),
            tab_spec, tab_spec,
        ],
        out_specs=(pl.BlockSpec((slots, tm, B_SLOT), lambda i, j: (j, i, 0)),
                   pl.BlockSpec((slots, tm, B_VX), lambda i, j: (j, i, 0))),
        compiler_params=_cparams("parallel", "parallel"),
        name="b_kv",
    )(c, gain.reshape(1, B_KV_RANK), w_k, w_v, c, *tabs_b)


def _attn_b_kernel(q_ref, k_ref, v_ref, o_ref, s0_ref, s1_ref, m_ref, acc_ref, *, tq, heads):
    qi = pl.program_id(2)
    m_ref[...] = jnp.full_like(m_ref, NEG_INF)
    acc_ref[...] = jnp.zeros_like(acc_ref)

    def scores(s_ref, ki):
        for h in range(heads):
            k = k_ref[h, pl.ds(pl.multiple_of(ki * tq, tq), tq), :]
            s_ref[h] = _dot_t(q_ref[h], k)

    def fold(s_ref, ki, diagonal):
        for h in range(heads):
            s = s_ref[h]
            if diagonal:
                row = lax.broadcasted_iota(jnp.int32, (tq, tq), 0)
                col = lax.broadcasted_iota(jnp.int32, (tq, tq), 1)
                s = jnp.where(row >= col, s, NEG_INF)
            m_prev = m_ref[h]
            m_new = jnp.maximum(m_prev, jnp.max(s, axis=-1, keepdims=True))
            alpha = jnp.exp(m_prev - m_new)
            p = jnp.exp((s - jnp.concatenate([m_new] * (tq // LANES), axis=1)).astype(BF16))
            v = v_ref[h, pl.ds(pl.multiple_of(ki * tq, tq), tq), :]
            acc_ref[h] = jnp.concatenate([alpha] * (B_VX // LANES), axis=1) * acc_ref[h] + _dot(p, v)
            m_ref[h] = m_new

    pairs = qi // 2
    scores(s0_ref, 0)

    def body(t, carry):
        scores(s1_ref, 2 * t + 1)
        fold(s0_ref, 2 * t, False)
        scores(s0_ref, 2 * t + 2)
        fold(s1_ref, 2 * t + 1, False)
        return carry

    lax.fori_loop(0, pairs, body, 0)

    @pl.when(qi % 2 == 1)
    def _():
        scores(s1_ref, qi)
        fold(s0_ref, qi - 1, False)
        fold(s1_ref, qi, True)

    @pl.when(qi % 2 == 0)
    def _():
        fold(s0_ref, qi, True)

    for h in range(heads):
        acc = acc_ref[h]
        l = jnp.broadcast_to(acc[:, B_V_DIM:B_V_DIM + 1], (tq, B_V_DIM))
        o_ref[:, h * B_V_DIM:(h + 1) * B_V_DIM] = (acc[:, :B_V_DIM] / l).astype(BF16)


def _attn_b(q3, k3, v3):
    tq, heads = 512, 2
    nq = SEQ // tq
    return pl.pallas_call(
        functools.partial(_attn_b_kernel, tq=tq, heads=heads),
        out_shape=jax.ShapeDtypeStruct((TOKENS, B_HEADS * B_V_DIM), BF16),
        grid=(BATCH, B_HEADS // heads, nq),
        in_specs=[
            pl.BlockSpec((heads, tq, B_SLOT), lambda b, g, qi: (g, b * nq + qi, 0)),
            pl.BlockSpec((heads, SEQ, B_SLOT), lambda b, g, qi: (g, b, 0)),
            pl.BlockSpec((heads, SEQ, B_VX), lambda b, g, qi: (g, b, 0)),
        ],
        out_specs=pl.BlockSpec((tq, heads * B_V_DIM), lambda b, g, qi: (b * nq + qi, g)),
        scratch_shapes=[pltpu.VMEM((heads, tq, tq), F32), pltpu.VMEM((heads, tq, tq), F32),
                        pltpu.VMEM((heads, tq, LANES), F32), pltpu.VMEM((heads, tq, B_VX), F32)],
        compiler_params=_cparams("parallel", "parallel", "arbitrary"),
        name="attn_b",
    )(q3, k3, v3)


def _mixer_b(h, gain, w_in, q_norm, w_q, kv_norm, w_k, w_v, w_o, tabs_b):
    c = _b_in(h, gain, w_in)
    q3 = _b_q(c, q_norm, w_q, tabs_b)
    k3, v3 = _b_kv(c, kv_norm, w_k, w_v, tabs_b)
    return _out_proj(_attn_b(q3, k3, v3), w_o, h)


def _c_in_kernel(x_ref, g_ref, wb_ref, wc_ref, wu_ref, cw_ref, o_ref, xn_ref, carry_ref, *, tm):
    i = pl.program_id(0)
    j = pl.program_id(1)

    @pl.when(j == 0)
    def _():
        xn_ref[...] = _rms(x_ref[...], g_ref[...]).astype(BF16)

    xn = xn_ref[...]
    b_gate = _dot(xn, wb_ref[...])
    cu = _dot(xn, wc_ref[...]) * _dot(xn, wu_ref[...])
    w0, w1, w2 = cw_ref[0:1, :], cw_ref[1:2, :], cw_ref[2:3, :]
    z = w2 * cu + w1 * pltpu.roll(cu, 1, 0) + w0 * pltpu.roll(cu, 2, 0)
    o_ref[...] = (b_gate * z).astype(BF16)

    first = (i * tm) % SEQ == 0
    tail = jnp.where(first, 0.0, carry_ref[j])
    both = jnp.concatenate([tail, cu[0:8, :]], axis=0)
    z8 = w2 * both[8:16] + w1 * both[7:15] + w0 * both[6:14]
    o_ref[0:8, :] = (b_gate[0:8, :] * z8).astype(BF16)
    carry_ref[j] = cu[tm - 8:tm, :]


def _c_in(h, gain, w_in, conv_w):
    tm, tn = 1024, 512
    nj = D_MODEL // tn
    return pl.pallas_call(
        functools.partial(_c_in_kernel, tm=tm),
        out_shape=jax.ShapeDtypeStruct((TOKENS, D_MODEL), BF16),
        grid=(TOKENS // tm, nj),
        in_specs=[
            pl.BlockSpec((tm, D_MODEL), lambda i, j: (i, 0)),
            pl.BlockSpec((1, D_MODEL), lambda i, j: (0, 0)),
            pl.BlockSpec((D_MODEL, tn), lambda i, j: (0, j)),
            pl.BlockSpec((D_MODEL, tn), lambda i, j: (0, j + nj)),
            pl.BlockSpec((D_MODEL, tn), lambda i, j: (0, j + 2 * nj)),
            pl.BlockSpec((C_CONV, tn), lambda i, j: (0, j)),
        ],
        out_specs=pl.BlockSpec((tm, tn), lambda i, j: (i, j)),
        scratch_shapes=[pltpu.VMEM((tm, D_MODEL), BF16), pltpu.VMEM((nj, 8, tn), F32)],
        compiler_params=_cparams("arbitrary", "arbitrary"),
        name="c_in",
    )(h, gain.reshape(1, D_MODEL), w_in, w_in, w_in, conv_w)


def _mixer_c(h, gain, w_in, conv_w, w_out):
    return _out_proj(_c_in(h, gain, w_in, conv_w), w_out, h)


def kernel(x, positions, l0_ffn1_norm, l0_ffn1_w13, l0_ffn1_w2, l0_mix_norm, l0_a_w_qkv, l0_a_w_o, l0_ffn2_norm, l0_ffn2_w13, l0_ffn2_w2, l1_ffn1_norm, l1_ffn1_w13, l1_ffn1_w2, l1_mix_norm, l1_b_w_in, l1_b_q_norm, l1_b_w_uq, l1_b_kv_norm, l1_b_w_ukv, l1_b_w_o, l1_ffn2_norm, l1_ffn2_w13, l1_ffn2_w2, l2_ffn1_norm, l2_ffn1_w13, l2_ffn1_w2, l2_mix_norm, l2_c_w_in, l2_c_conv_w, l2_c_w_out, l2_ffn2_norm, l2_ffn2_w13, l2_ffn2_w2, l3_ffn1_norm, l3_ffn1_w13, l3_ffn1_w2, l3_mix_norm, l3_a_w_qkv, l3_a_w_o, l3_ffn2_norm, l3_ffn2_w13, l3_ffn2_w2, final_norm):
    bf = lambda w: w.astype(BF16)
    pos_tiles = positions.reshape(TOKENS // A_TILE, A_TILE)
    pos_a = [pos_tiles.reshape(-1, A_TILE // d, d).swapaxes(1, 2).reshape(TOKENS) for _, d in A_PATTERNS]
    tabs_a = _rope_tables(jnp.concatenate(pos_a + [jnp.zeros((A_TILE,), positions.dtype)]), A_ROT_HALF)
    tabs_b = _rope_tables(positions.reshape(TOKENS), B_ROT_HALF)

    def rope_group(w):
        return _rope_lanes(jnp.pad(w, [(0, 0)] * (w.ndim - 1) + [(0, LANES - B_ROPE_DIM)]), B_ROT_HALF)

    w_uq = l1_b_w_uq.reshape(B_Q_RANK, B_HEADS, B_NOPE_DIM + B_ROPE_DIM)
    w_q = jnp.concatenate([w_uq[..., :B_NOPE_DIM], rope_group(w_uq[..., B_NOPE_DIM:])], axis=-1)
    w_q = bf(w_q.reshape(B_Q_RANK, B_HEADS * B_SLOT))
    w_ukv = l1_b_w_ukv.reshape(B_KV_RANK, B_HEADS, B_NOPE_DIM + B_V_DIM)
    w_k = bf(w_ukv[:, :, :B_NOPE_DIM].reshape(B_KV_RANK, B_HEADS * B_NOPE_DIM))
    w_v = bf(w_ukv[:, :, B_NOPE_DIM:].reshape(B_KV_RANK, B_HEADS * B_V_DIM))
    n_c = B_Q_RANK + B_KV_RANK
    w_b_in = bf(jnp.concatenate([l1_b_w_in[:, :n_c], rope_group(l1_b_w_in[:, n_c:])], axis=-1))

    h = x.reshape(TOKENS, D_MODEL)

    h, w13, w2 = _ffn(h, l0_ffn1_norm, bf(l0_ffn1_w13), bf(l0_ffn1_w2), next_f32=(l0_ffn2_w13, l0_ffn2_w2))
    h = _mixer_a(h, l0_mix_norm, _qkv_w(l0_a_w_qkv), bf(l0_a_w_o), tabs_a)
    h, w13, w2 = _ffn(h, l0_ffn2_norm, w13, w2, next_f32=(l1_ffn1_w13, l1_ffn1_w2))

    h, w13, w2 = _ffn(h, l1_ffn1_norm, w13, w2, next_f32=(l1_ffn2_w13, l1_ffn2_w2))
    h = _mixer_b(h, l1_mix_norm, w_b_in, l1_b_q_norm, w_q, l1_b_kv_norm, w_k, w_v, bf(l1_b_w_o), tabs_b)
    h, w13, w2 = _ffn(h, l1_ffn2_norm, w13, w2, next_f32=(l2_ffn1_w13, l2_ffn1_w2))

    h, w13, w2 = _ffn(h, l2_ffn1_norm, w13, w2, next_f32=(l2_ffn2_w13, l2_ffn2_w2))
    h = _mixer_c(h, l2_mix_norm, bf(l2_c_w_in), l2_c_conv_w, bf(l2_c_w_out))
    h, w13, w2 = _ffn(h, l2_ffn2_norm, w13, w2, next_f32=(l3_ffn1_w13, l3_ffn1_w2))

    h, w13, w2 = _ffn(h, l3_ffn1_norm, w13, w2, next_f32=(l3_ffn2_w13, l3_ffn2_w2))
    h = _mixer_a(h, l3_mix_norm, _qkv_w(l3_a_w_qkv), bf(l3_a_w_o), tabs_a)
    h = _ffn(h, l3_ffn2_norm, w13, w2, final_gain=final_norm)

    return h.reshape(BATCH, SEQ, D_MODEL)
```

```python
import functools

import jax
import jax.numpy as jnp
from jax import lax
from jax.experimental import pallas as pl
from jax.experimental.pallas import tpu as pltpu

F32 = jnp.float32
BF16 = jnp.bfloat16

D_MODEL = 2048
BATCH = 2
SEQ = 8192
TOKENS = BATCH * SEQ
D_FF = 5632
HALF_STEP = 0.5
NORM_EPS = 1e-6
NEG_INF = -1e30
ROPE_THETA = 500000.0

A_HEADS = 16
A_HEAD_DIM = 128
A_ROT_HALF = A_HEAD_DIM // 8
A_PATTERNS = ((128, 1), (512, 4), (2048, 16))
A_BLOCK = 128
A_TILE = 2048
A_MID_DILATION = 4
assert [d for _, d in A_PATTERNS] == [1, A_MID_DILATION, A_MID_DILATION ** 2]

B_HEADS = 16
B_Q_RANK = 1536
B_KV_RANK = 512
B_NOPE_DIM = 128
B_ROPE_DIM = 64
B_ROT_HALF = B_ROPE_DIM // 2
B_V_DIM = 128
B_SLOT = 256
B_VX = 256
B_IN_COLS = 2176

C_CONV = 3

LANES = 128
MXU_COLS = 256
ROPE_PAIR = LANES // 2
VMEM_LIMIT = 56 * 1024 * 1024


def _cparams(*sem):
    return pltpu.CompilerParams(dimension_semantics=sem, vmem_limit_bytes=VMEM_LIMIT)


def _rms(x, gain):
    ms = jnp.mean(x * x, axis=-1, keepdims=True)
    return x * lax.rsqrt(ms + NORM_EPS) * gain


def _dot(a, b):
    return jnp.dot(a, b, preferred_element_type=F32)


def _dot_t(a, b):
    return lax.dot_general(a, b, (((1,), (1,)), ((), ())), preferred_element_type=F32)


def _rope128(y, cos, sin):
    return y * cos + pltpu.roll(y, ROPE_PAIR, 1) * sin


def _rope_lanes(w, half):
    return jnp.concatenate([w[..., :half], w[..., 2 * half:ROPE_PAIR + half], w[..., half:2 * half],
                            w[..., ROPE_PAIR + half:]], axis=-1)


def _rope_tab_kernel(pos_ref, inv_ref, c_ref, s_ref):
    ang = pos_ref[...].astype(F32) * inv_ref[...]
    lane = lax.broadcasted_iota(jnp.int32, (1, LANES), 1)
    sn = jnp.sin(ang)
    c_ref[...] = jnp.cos(ang)
    s_ref[...] = jnp.where(lane < ROPE_PAIR, -sn, sn)


def _rope_tables(pos_flat, half):
    inv = ROPE_THETA ** (-jnp.arange(half, dtype=F32) / half)
    gap = jnp.zeros((ROPE_PAIR - half,), F32)
    inv = jnp.concatenate([inv, gap, inv, gap]).reshape(1, LANES)
    n = pos_flat.shape[0]
    rows = 2048
    tab = jax.ShapeDtypeStruct((n, LANES), F32)
    row_spec = pl.BlockSpec((rows, LANES), lambda i: (i, 0))
    return pl.pallas_call(
        _rope_tab_kernel,
        out_shape=(tab,) * 2,
        grid=(n // rows,),
        in_specs=[pl.BlockSpec((rows, 1), lambda i: (i, 0)), pl.BlockSpec((1, LANES), lambda i: (0, 0))],
        out_specs=(row_spec,) * 2,
        compiler_params=_cparams("parallel"),
        name="rope_tables",
    )(pos_flat.reshape(n, 1), inv)


def _ffn_kernel(x_ref, g_ref, w1_ref, w3_ref, w2_ref, *refs, final, cast_next):
    j = pl.program_id(1)
    if cast_next:
        n13_ref, n2_ref, o_ref, c13_ref, c2_ref, xn_ref, acc_ref = refs
    else:
        o_ref, xn_ref, acc_ref = refs

    @pl.when(j == 0)
    def _():
        xn_ref[...] = _rms(x_ref[...], g_ref[0:1, :]).astype(BF16)
        acc_ref[...] = jnp.zeros_like(acc_ref)

    xn = xn_ref[...]
    gate = _dot(xn, w1_ref[...])
    up = _dot(xn, w3_ref[...])
    act = (gate * (1.0 / (1.0 + jnp.exp(-gate))) * up).astype(BF16)
    acc_ref[...] += _dot(act, w2_ref[...])
    if cast_next:
        c13_ref[...] = n13_ref[...].astype(BF16)
        c2_ref[...] = n2_ref[...].astype(BF16)

    @pl.when(j == pl.num_programs(1) - 1)
    def _():
        y = x_ref[...] + HALF_STEP * acc_ref[...]
        if final:
            y = _rms(y, g_ref[1:2, :])
        o_ref[...] = y


def _ffn(h, gain, w13, w2, final_gain=None, next_f32=None):
    tm, tf = 512, 512
    nj = D_FF // tf
    ni = TOKENS // tm
    final = final_gain is not None
    gains = jnp.stack([gain, final_gain if final else gain])
    in_specs = [
        pl.BlockSpec((tm, D_MODEL), lambda i, j: (i, 0)),
        pl.BlockSpec((2, D_MODEL), lambda i, j: (0, 0)),
        pl.BlockSpec((D_MODEL, tf), lambda i, j: (0, j)),
        pl.BlockSpec((D_MODEL, tf), lambda i, j: (0, j + nj)),
        pl.BlockSpec((tf, D_MODEL), lambda i, j: (j, 0)),
    ]
    out_shape = [jax.ShapeDtypeStruct((TOKENS, D_MODEL), F32)]
    out_specs = [pl.BlockSpec((tm, D_MODEL), lambda i, j: (i, 0))]
    args = [h, gains, w13, w13, w2]
    if next_f32 is not None:
        slab13 = pl.BlockSpec((D_MODEL // ni, 2 * D_FF // nj), lambda i, j: (i, j))
        slab2 = pl.BlockSpec((D_FF // (ni * nj), D_MODEL), lambda i, j: (i * nj + j, 0))
        in_specs += [slab13, slab2]
        out_specs += [slab13, slab2]
        out_shape += [jax.ShapeDtypeStruct(w.shape, BF16) for w in next_f32]
        args += list(next_f32)
    outs = pl.pallas_call(
        functools.partial(_ffn_kernel, final=final, cast_next=next_f32 is not None),
        out_shape=tuple(out_shape),
        grid=(ni, nj),
        in_specs=in_specs,
        out_specs=tuple(out_specs),
        scratch_shapes=[pltpu.VMEM((tm, D_MODEL), BF16), pltpu.VMEM((tm, D_MODEL), F32)],
        compiler_params=_cparams("parallel", "arbitrary"),
        name="ffn",
    )(*args)
    return outs if next_f32 is not None else outs[0]


def _out_proj_kernel(x_ref, w_ref, r_ref, o_ref):
    o_ref[...] = r_ref[...] + _dot(x_ref[...], w_ref[...])


def _out_proj(x, w, res):
    k = x.shape[1]
    tm = 512
    return pl.pallas_call(
        _out_proj_kernel,
        out_shape=jax.ShapeDtypeStruct((TOKENS, D_MODEL), F32),
        grid=(TOKENS // tm,),
        in_specs=[
            pl.BlockSpec((tm, k), lambda i: (i, 0)),
            pl.BlockSpec((k, D_MODEL), lambda i: (0, 0)),
            pl.BlockSpec((tm, D_MODEL), lambda i: (i, 0)),
        ],
        out_specs=pl.BlockSpec((tm, D_MODEL), lambda i: (i, 0)),
        compiler_params=_cparams("parallel"),
        name="out_proj",
    )(x, w, res)


def _norm_perm_a_kernel(x_ref, g_ref, *refs, tm):
    o_refs, xs_ref, mid_ref = refs[:-2], refs[-2], refs[-1]
    xn = _rms(x_ref[...], g_ref[...])
    nchunk = D_MODEL // LANES
    for k in range(nchunk):
        xs_ref[k] = xn[:, k * LANES:(k + 1) * LANES]
    dp = 1
    for (_, d), o_ref in zip(A_PATTERNS, o_refs):
        if d == 1:
            o_ref[...] = xn.astype(BF16)
            continue
        step = d // dp
        for c in range(d):
            for k in range(nchunk):
                rows = pl.ds(c // dp, tm // d, stride=step)
                piece = xs_ref[k, rows, :] if dp == 1 else mid_ref[k, c % dp, rows, :]
                o_ref[0, c, 0, :, k * LANES:(k + 1) * LANES] = piece.astype(BF16)
                if d == A_MID_DILATION:
                    mid_ref[k, c] = piece
        dp = d


def _norm_perm_a(h, gain):
    tm = 512
    per_tile = A_TILE // tm
    shapes, specs = [], []
    for _, d in A_PATTERNS:
        if d == 1:
            shapes.append(jax.ShapeDtypeStruct((TOKENS, D_MODEL), BF16))
            specs.append(pl.BlockSpec((tm, D_MODEL), lambda i: (i, 0)))
        else:
            shapes.append(jax.ShapeDtypeStruct((TOKENS // A_TILE, d, per_tile, tm // d, D_MODEL), BF16))
            specs.append(pl.BlockSpec((1, d, 1, tm // d, D_MODEL),
                                      lambda i: (i // per_tile, 0, i % per_tile, 0, 0)))
    outs = pl.pallas_call(
        functools.partial(_norm_perm_a_kernel, tm=tm),
        out_shape=tuple(shapes),
        grid=(TOKENS // tm,),
        in_specs=[pl.BlockSpec((tm, D_MODEL), lambda i: (i, 0)), pl.BlockSpec((1, D_MODEL), lambda i: (0, 0))],
        out_specs=tuple(specs),
        scratch_shapes=[pltpu.VMEM((D_MODEL // LANES, tm, LANES), F32),
                        pltpu.VMEM((D_MODEL // LANES, A_MID_DILATION, tm // A_MID_DILATION, LANES), F32)],
        compiler_params=_cparams("parallel"),
        name="norm_perm_a",
    )(h, gain.reshape(1, D_MODEL))
    return [o.reshape(TOKENS, D_MODEL) for o in outs]


def _qkv_w_kernel(w_ref, o_ref):
    part = pl.program_id(1) % 3

    @pl.when(part == 2)
    def _():
        o_ref[...] = w_ref[...].astype(BF16)

    @pl.when(part != 2)
    def _():
        lane = lax.broadcasted_iota(jnp.int32, (1, LANES), 1)
        stay = (lane < A_ROT_HALF) | (lane >= ROPE_PAIR + A_ROT_HALF)
        for hh in range(A_HEADS):
            sl = slice(hh * A_HEAD_DIM, (hh + 1) * A_HEAD_DIM)
            y = w_ref[:, sl]
            moved = jnp.where(lane < ROPE_PAIR, pltpu.roll(y, LANES - A_ROT_HALF, 1),
                              pltpu.roll(y, ROPE_PAIR - A_ROT_HALF, 1))
            o_ref[:, sl] = jnp.where(stay, y, moved).astype(BF16)


def _qkv_w(w_qkv):
    rows = 512
    hd = A_HEADS * A_HEAD_DIM
    spec = pl.BlockSpec((rows, hd), lambda i, j: (i, j))
    return pl.pallas_call(
        _qkv_w_kernel,
        out_shape=jax.ShapeDtypeStruct(w_qkv.shape, BF16),
        grid=(D_MODEL // rows, w_qkv.shape[1] // hd),
        in_specs=[spec],
        out_specs=spec,
        compiler_params=_cparams("parallel", "parallel"),
        name="qkv_w",
    )(w_qkv)


def _qkv_a_kernel(x_ref, w_ref, cos_ref, sin_ref, o_ref, *, tn):
    part = pl.program_id(1) // ((A_HEADS * A_HEAD_DIM) // tn)
    scale = jnp.where(part == 0, A_HEAD_DIM ** -0.5, 1.0).astype(F32)
    half = x_ref.shape[0] // 2
    for cc in range(tn // MXU_COLS):
        for rows in (slice(0, half), slice(half, 2 * half)):
            y = _dot(x_ref[rows, :], w_ref[:, cc * MXU_COLS:(cc + 1) * MXU_COLS])
            for hh in range(MXU_COLS // LANES):
                sl = slice(cc * MXU_COLS + hh * LANES, cc * MXU_COLS + (hh + 1) * LANES)
                y_h = y[:, hh * LANES:(hh + 1) * LANES]
                o_ref[rows, sl] = (_rope128(y_h, cos_ref[rows, :], sin_ref[rows, :]) * scale).astype(BF16)


def _qkv_a(xn, w_qkv, tabs, group):
    tm, tn = 1024, 2048
    cols = 3 * A_HEADS * A_HEAD_DIM
    nj = cols // tn
    ni = TOKENS // tm
    v_first = 2 * (A_HEADS * A_HEAD_DIM) // tn
    tab_spec = pl.BlockSpec((tm, LANES),
                            lambda i, j: (jnp.where(j >= v_first, len(A_PATTERNS) * ni, group * ni + i), 0))
    return pl.pallas_call(
        functools.partial(_qkv_a_kernel, tn=tn),
        out_shape=jax.ShapeDtypeStruct((TOKENS, cols), BF16),
        grid=(ni, nj),
        in_specs=[
            pl.BlockSpec((tm, D_MODEL), lambda i, j: (i, 0)),
            pl.BlockSpec((D_MODEL, tn), lambda i, j: (0, group * nj + j)),
            tab_spec, tab_spec,
        ],
        out_specs=pl.BlockSpec((tm, tn), lambda i, j: (i, j)),
        compiler_params=_cparams("parallel", "parallel"),
        name="qkv_a",
    )(xn, w_qkv, *tabs)


def _attn_a_kernel(q_ref, kc_ref, kp_ref, vc_ref, vp_ref, o_ref, lse_ref, *scratch, d, heads, per_trip):
    rows = A_TILE // d
    nblk = rows // A_BLOCK
    two_stage = d > A_MID_DILATION
    if two_stage:
        os_ref, mid_ref = scratch
        stage_ref, mid_step = mid_ref, d // A_MID_DILATION
    else:
        (os_ref,) = scratch
        stage_ref = os_ref
    first_tile = pl.program_id(1) == 0
    r = lax.broadcasted_iota(jnp.int32, (A_BLOCK, 2 * A_BLOCK), 0)
    kk = lax.broadcasted_iota(jnp.int32, (A_BLOCK, 2 * A_BLOCK), 1)
    band = (kk >= r) & (kk <= r + A_BLOCK)
    lane = lax.broadcasted_iota(jnp.int32, (A_BLOCK, LANES), 1)
    hs = [slice(h * A_HEAD_DIM, (h + 1) * A_HEAD_DIM) for h in range(heads)]
    first_head = pl.program_id(2) * heads

    @pl.when(first_head == 0)
    def _():
        lse_ref[...] = jnp.zeros_like(lse_ref)

    def trip(t, carry):
        loaded = []
        for i in range(per_trip):
            u = t * per_trip + i
            c = u // nblk
            blk = u % nblk
            start = pl.multiple_of(blk * A_BLOCK, A_BLOCK)
            before = pl.multiple_of(jnp.maximum(start - A_BLOCK, 0), A_BLOCK)
            q = q_ref[0, c, pl.ds(start, A_BLOCK), :]
            k = jnp.concatenate([jnp.where(blk == 0, kp_ref[0, c], kc_ref[0, c, pl.ds(before, A_BLOCK), :]),
                                 kc_ref[0, c, pl.ds(start, A_BLOCK), :]], axis=0)
            v = jnp.concatenate([jnp.where(blk == 0, vp_ref[0, c], vc_ref[0, c, pl.ds(before, A_BLOCK), :]),
                                 vc_ref[0, c, pl.ds(start, A_BLOCK), :]], axis=0)
            no_prev = jnp.logical_and(first_tile, blk == 0)
            mask = jnp.logical_and(band, jnp.logical_or(kk >= A_BLOCK, jnp.logical_not(no_prev)))
            out_rows = (pl.ds(blk * (A_BLOCK * d) + c, A_BLOCK, stride=d) if d > 1
                        else pl.ds(start, A_BLOCK))
            if two_stage:
                o_at = (c % A_MID_DILATION,
                        pl.ds(blk * (A_BLOCK * mid_step) + c // A_MID_DILATION, A_BLOCK, stride=mid_step))
            else:
                o_at = (out_rows,)
            loaded.append(([_dot_t(q[:, sl], k[:, sl]) for sl in hs], v, mask, out_rows, o_at))
        for scores, v, mask, out_rows, o_at in loaded:
            lse_tile = lse_ref[out_rows, :]
            for h, sl in enumerate(hs):
                s = jnp.where(mask, scores[h], NEG_INF)
                m = jnp.max(s, axis=-1, keepdims=True)
                p = jnp.exp(s - m)
                l = jnp.sum(p, axis=-1, keepdims=True)
                stage_ref[(h,) + o_at + (slice(None),)] = _dot(p.astype(BF16), v[:, sl]) / l
                lse_tile = jnp.where(lane == first_head + h, m + jnp.log(l), lse_tile)
            lse_ref[out_rows, :] = lse_tile
        return carry

    lax.fori_loop(0, d * nblk // per_trip, trip, 0)
    if two_stage:
        for h in range(heads):
            for c4 in range(A_MID_DILATION):
                os_ref[h, pl.ds(c4, A_TILE // A_MID_DILATION, stride=A_MID_DILATION), :] = mid_ref[h, c4]
    o_ref[...] = os_ref[...].astype(BF16)


def _attn_a(qkv, window, dilation):
    assert window // dilation == A_BLOCK, "band width must equal the query block"
    d = dilation
    heads = 4
    hw = heads * A_HEAD_DIM
    ngrp = A_HEADS // heads
    rows = A_TILE // d
    tiles = SEQ // A_TILE
    qkv_v = qkv.reshape(BATCH, tiles, d, rows, 3 * A_HEADS * A_HEAD_DIM)
    last_blk = rows // A_BLOCK - 1

    def cur(part):
        return pl.BlockSpec((None, 1, d, rows, hw), lambda b, n, g: (b, n, 0, 0, part * ngrp + g))

    def prev(part):
        return pl.BlockSpec((None, 1, d, A_BLOCK, hw),
                            lambda b, n, g: (b, jnp.maximum(n - 1, 0), 0, last_blk, part * ngrp + g))

    return pl.pallas_call(
        functools.partial(_attn_a_kernel, d=d, heads=heads, per_trip=4),
        out_shape=(jax.ShapeDtypeStruct((A_HEADS, TOKENS, A_HEAD_DIM), BF16),
                   jax.ShapeDtypeStruct((TOKENS, LANES), F32)),
        grid=(BATCH, tiles, ngrp),
        in_specs=[cur(0), cur(1), prev(1), cur(2), prev(2)],
        out_specs=(pl.BlockSpec((heads, A_TILE, A_HEAD_DIM), lambda b, n, g: (g, b * tiles + n, 0)),
                   pl.BlockSpec((A_TILE, LANES), lambda b, n, g: (b * tiles + n, 0))),
        scratch_shapes=[pltpu.VMEM((heads, A_TILE, A_HEAD_DIM), F32)] + (
            [pltpu.VMEM((heads, A_MID_DILATION, A_TILE // A_MID_DILATION, A_HEAD_DIM), F32)]
            if d > A_MID_DILATION else []),
        compiler_params=_cparams("parallel", "parallel", "arbitrary"),
        name="attn_a_d%d" % d,
    )(qkv_v, qkv_v, qkv_v, qkv_v, qkv_v)


def _comb_proj_a_kernel(o0, o1, o2, l0, l1, l2, w_ref, r_ref, out_ref, comb_ref):
    s = pl.program_id(0)
    slot = s % 2

    @pl.when(s == 0)
    def _():
        comb_ref[1] = jnp.zeros(comb_ref.shape[1:], BF16)

    out_ref[...] = r_ref[...] + _dot(comb_ref[1 - slot], w_ref[...])
    a = [l0[...], l1[...], l2[...]]
    m = jnp.maximum(jnp.maximum(a[0], a[1]), a[2])
    e = [jnp.exp(t - m) for t in a]
    den = e[0] + e[1] + e[2]
    w = [t / den for t in e]
    rows = out_ref.shape[0]
    for h in range(A_HEADS):
        acc = jnp.zeros((rows, A_HEAD_DIM), F32)
        for wg, og in zip(w, (o0, o1, o2)):
            acc = acc + jnp.broadcast_to(wg[:, h:h + 1], (rows, A_HEAD_DIM)) * og[h]
        comb_ref[slot, :, h * A_HEAD_DIM:(h + 1) * A_HEAD_DIM] = acc.astype(BF16)


def _comb_proj_a(outs, lses, w_o, res):
    tm = 256
    nt = TOKENS // tm
    hd = A_HEADS * A_HEAD_DIM
    o_spec = pl.BlockSpec((A_HEADS, tm, A_HEAD_DIM), lambda s: (0, jnp.minimum(s, nt - 1), 0))
    l_spec = pl.BlockSpec((tm, LANES), lambda s: (jnp.minimum(s, nt - 1), 0))
    row_spec = pl.BlockSpec((tm, D_MODEL), lambda s: (jnp.maximum(s - 1, 0), 0))
    return pl.pallas_call(
        _comb_proj_a_kernel,
        out_shape=jax.ShapeDtypeStruct((TOKENS, D_MODEL), F32),
        grid=(nt + 1,),
        in_specs=[o_spec] * 3 + [l_spec] * 3 + [pl.BlockSpec((hd, D_MODEL), lambda s: (0, 0)), row_spec],
        out_specs=row_spec,
        scratch_shapes=[pltpu.VMEM((2, tm, hd), BF16)],
        compiler_params=_cparams("arbitrary"),
        name="comb_proj_a",
    )(*outs, *lses, w_o, res)


def _mixer_a(h, gain, w_qkv, w_o, tabs_a):
    outs, lses = [], []
    for g, ((window, dilation), xn) in enumerate(zip(A_PATTERNS, _norm_perm_a(h, gain))):
        o, lse = _attn_a(_qkv_a(xn, w_qkv, tabs_a, g), window, dilation)
        outs.append(o)
        lses.append(lse)
    return _comb_proj_a(outs, lses, w_o, h)


def _b_in_kernel(x_ref, g_ref, w_ref, o_ref):
    o_ref[...] = _dot(_rms(x_ref[...], g_ref[...]).astype(BF16), w_ref[...])


def _b_in(h, gain, w_in):
    tm = 512
    return pl.pallas_call(
        _b_in_kernel,
        out_shape=jax.ShapeDtypeStruct((TOKENS, B_IN_COLS), F32),
        grid=(TOKENS // tm,),
        in_specs=[
            pl.BlockSpec((tm, D_MODEL), lambda i: (i, 0)),
            pl.BlockSpec((1, D_MODEL), lambda i: (0, 0)),
            pl.BlockSpec((D_MODEL, B_IN_COLS), lambda i: (0, 0)),
        ],
        out_specs=pl.BlockSpec((tm, B_IN_COLS), lambda i: (i, 0)),
        compiler_params=_cparams("parallel"),
        name="b_in",
    )(h, gain.reshape(1, D_MODEL), w_in)


def _b_q_kernel(c_ref, g_ref, w_ref, cos_ref, sin_ref, o_ref, xn_ref, *, slots):
    @pl.when(pl.program_id(1) == 0)
    def _():
        xn_ref[...] = _rms(c_ref[...], g_ref[...]).astype(BF16)

    y = _dot(xn_ref[...], w_ref[...])
    scale = (B_NOPE_DIM + B_ROPE_DIM) ** -0.5
    cos, sin = cos_ref[...], sin_ref[...]
    for s in range(slots):
        nope = y[:, s * B_SLOT:s * B_SLOT + LANES]
        rot = _rope128(y[:, s * B_SLOT + LANES:(s + 1) * B_SLOT], cos, sin)
        o_ref[s, :, :LANES] = (nope * scale).astype(BF16)
        o_ref[s, :, LANES:] = (rot * scale).astype(BF16)


def _b_q(c, gain, w_q, tabs_b):
    tm, slots = 1024, 8
    tn = slots * B_SLOT
    tab_spec = pl.BlockSpec((tm, LANES), lambda i, j: (i, 0))
    return pl.pallas_call(
        functools.partial(_b_q_kernel, slots=slots),
        out_shape=jax.ShapeDtypeStruct((B_HEADS, TOKENS, B_SLOT), BF16),
        grid=(TOKENS // tm, B_HEADS // slots),
        in_specs=[
            pl.BlockSpec((tm, B_Q_RANK), lambda i, j: (i, 0)),
            pl.BlockSpec((1, B_Q_RANK), lambda i, j: (0, 0)),
            pl.BlockSpec((B_Q_RANK, tn), lambda i, j: (0, j)),
            tab_spec, tab_spec,
        ],
        out_specs=pl.BlockSpec((slots, tm, B_SLOT), lambda i, j: (j, i, 0)),
        scratch_shapes=[pltpu.VMEM((tm, B_Q_RANK), BF16)],
        compiler_params=_cparams("parallel", "arbitrary"),
        name="b_q",
    )(c, gain.reshape(1, B_Q_RANK), w_q, *tabs_b)


def _b_kv_kernel(c_ref, g_ref, wk_ref, wv_ref, cr_ref, cos_ref, sin_ref, k_ref, v_ref, *, slots):
    xn = _rms(c_ref[...], g_ref[...]).astype(BF16)
    yk = _dot(xn, wk_ref[...])
    yv = _dot(xn, wv_ref[...])
    kr = _rope128(cr_ref[...], cos_ref[...], sin_ref[...]).astype(BF16)
    lane = lax.broadcasted_iota(jnp.int32, (c_ref.shape[0], B_VX - B_V_DIM), 1)
    ones_col = jnp.where(lane == 0, 1.0, 0.0).astype(BF16)
    for s in range(slots):
        sl = slice(s * LANES, (s + 1) * LANES)
        k_ref[s, :, :LANES] = yk[:, sl].astype(BF16)
        k_ref[s, :, LANES:] = kr
        v_ref[s, :, :B_V_DIM] = yv[:, sl].astype(BF16)
        v_ref[s, :, B_V_DIM:] = ones_col


def _b_kv(c, gain, w_k, w_v, tabs_b):
    tm, slots = 1024, 8
    tn = slots * LANES
    kv_block = B_Q_RANK // B_KV_RANK
    rope_block = (B_Q_RANK + B_KV_RANK) // LANES
    tab_spec = pl.BlockSpec((tm, LANES), lambda i, j: (i, 0))
    return pl.pallas_call(
        functools.partial(_b_kv_kernel, slots=slots),
        out_shape=(jax.ShapeDtypeStruct((B_HEADS, TOKENS, B_SLOT), BF16),
                   jax.ShapeDtypeStruct((B_HEADS, TOKENS, B_VX), BF16)),
        grid=(TOKENS // tm, B_HEADS // slots),
        in_specs=[
            pl.BlockSpec((tm, B_KV_RANK), lambda i, j: (i, kv_block)),
            pl.BlockSpec((1, B_KV_RANK), lambda i, j: (0, 0)),
            pl.BlockSpec((B_KV_RANK, tn), lambda i, j: (0, j)),
            pl.BlockSpec((B_KV_RANK, tn), lambda i, j: (0, j)),
            pl.BlockSpec((tm, LANES), lambda i, j: (i, rope_block)),
            tab_spec, tab_spec,
        ],
        out_specs=(pl.BlockSpec((slots, tm, B_SLOT), lambda i, j: (j, i, 0)),
                   pl.BlockSpec((slots, tm, B_VX), lambda i, j: (j, i, 0))),
        compiler_params=_cparams("parallel", "parallel"),
        name="b_kv",
    )(c, gain.reshape(1, B_KV_RANK), w_k, w_v, c, *tabs_b)


def _attn_b_kernel(q_ref, k_ref, v_ref, o_ref, s0_ref, s1_ref, m_ref, acc_ref, *, tq, heads):
    qi = pl.program_id(2)
    m_ref[...] = jnp.full_like(m_ref, NEG_INF)
    acc_ref[...] = jnp.zeros_like(acc_ref)

    def scores(s_ref, ki):
        for h in range(heads):
            k = k_ref[h, pl.ds(pl.multiple_of(ki * tq, tq), tq), :]
            s_ref[h] = _dot_t(q_ref[h], k)

    def fold(s_ref, ki, diagonal):
        for h in range(heads):
            s = s_ref[h]
            if diagonal:
                row = lax.broadcasted_iota(jnp.int32, (tq, tq), 0)
                col = lax.broadcasted_iota(jnp.int32, (tq, tq), 1)
                s = jnp.where(row >= col, s, NEG_INF)
            m_prev = m_ref[h]
            m_new = jnp.maximum(m_prev, jnp.max(s, axis=-1, keepdims=True))
            alpha = jnp.exp(m_prev - m_new)
            p = jnp.exp((s - jnp.concatenate([m_new] * (tq // LANES), axis=1)).astype(BF16))
            v = v_ref[h, pl.ds(pl.multiple_of(ki * tq, tq), tq), :]
            acc_ref[h] = jnp.concatenate([alpha] * (B_VX // LANES), axis=1) * acc_ref[h] + _dot(p, v)
            m_ref[h] = m_new

    pairs = qi // 2
    scores(s0_ref, 0)

    def body(t, carry):
        scores(s1_ref, 2 * t + 1)
        fold(s0_ref, 2 * t, False)
        scores(s0_ref, 2 * t + 2)
        fold(s1_ref, 2 * t + 1, False)
        return carry

    lax.fori_loop(0, pairs, body, 0)

    @pl.when(qi % 2 == 1)
    def _():
        scores(s1_ref, qi)
        fold(s0_ref, qi - 1, False)
        fold(s1_ref, qi, True)

    @pl.when(qi % 2 == 0)
    def _():
        fold(s0_ref, qi, True)

    for h in range(heads):
        acc = acc_ref[h]
        l = jnp.broadcast_to(acc[:, B_V_DIM:B_V_DIM + 1], (tq, B_V_DIM))
        o_ref[:, h * B_V_DIM:(h + 1) * B_V_DIM] = (acc[:, :B_V_DIM] / l).astype(BF16)


def _attn_b(q3, k3, v3):
    tq, heads = 512, 2
    nq = SEQ // tq
    return pl.pallas_call(
        functools.partial(_attn_b_kernel, tq=tq, heads=heads),
        out_shape=jax.ShapeDtypeStruct((TOKENS, B_HEADS * B_V_DIM), BF16),
        grid=(BATCH, B_HEADS // heads, nq),
        in_specs=[
            pl.BlockSpec((heads, tq, B_SLOT), lambda b, g, qi: (g, b * nq + qi, 0)),
            pl.BlockSpec((heads, SEQ, B_SLOT), lambda b, g, qi: (g, b, 0)),
            pl.BlockSpec((heads, SEQ, B_VX), lambda b, g, qi: (g, b, 0)),
        ],
        out_specs=pl.BlockSpec((tq, heads * B_V_DIM), lambda b, g, qi: (b * nq + qi, g)),
        scratch_shapes=[pltpu.VMEM((heads, tq, tq), F32), pltpu.VMEM((heads, tq, tq), F32),
                        pltpu.VMEM((heads, tq, LANES), F32), pltpu.VMEM((heads, tq, B_VX), F32)],
        compiler_params=_cparams("parallel", "parallel", "arbitrary"),
        name="attn_b",
    )(q3, k3, v3)


def _mixer_b(h, gain, w_in, q_norm, w_q, kv_norm, w_k, w_v, w_o, tabs_b):
    c = _b_in(h, gain, w_in)
    q3 = _b_q(c, q_norm, w_q, tabs_b)
    k3, v3 = _b_kv(c, kv_norm, w_k, w_v, tabs_b)
    return _out_proj(_attn_b(q3, k3, v3), w_o, h)


def _c_in_kernel(x_ref, g_ref, wb_ref, wc_ref, wu_ref, cw_ref, o_ref, xn_ref, carry_ref, *, tm):
    i = pl.program_id(0)
    j = pl.program_id(1)

    @pl.when(j == 0)
    def _():
        xn_ref[...] = _rms(x_ref[...], g_ref[...]).astype(BF16)

    xn = xn_ref[...]
    first = (i * tm) % SEQ == 0
    for cc in range(o_ref.shape[1] // MXU_COLS):
        cs = slice(cc * MXU_COLS, (cc + 1) * MXU_COLS)
        b_gate = _dot(xn, wb_ref[:, cs])
        cu = _dot(xn, wc_ref[:, cs]) * _dot(xn, wu_ref[:, cs])
        w0, w1, w2 = cw_ref[0:1, cs], cw_ref[1:2, cs], cw_ref[2:3, cs]
        z = w2 * cu + w1 * pltpu.roll(cu, 1, 0) + w0 * pltpu.roll(cu, 2, 0)
        o_ref[:, cs] = (b_gate * z).astype(BF16)

        tail = jnp.where(first, 0.0, carry_ref[j, :, cs])
        both = jnp.concatenate([tail, cu[0:8, :]], axis=0)
        z8 = w2 * both[8:16] + w1 * both[7:15] + w0 * both[6:14]
        o_ref[0:8, cs] = (b_gate[0:8, :] * z8).astype(BF16)
        carry_ref[j, :, cs] = cu[tm - 8:tm, :]


def _c_in(h, gain, w_in, conv_w):
    tm, tn = 1024, 512
    nj = D_MODEL // tn
    return pl.pallas_call(
        functools.partial(_c_in_kernel, tm=tm),
        out_shape=jax.ShapeDtypeStruct((TOKENS, D_MODEL), BF16),
        grid=(TOKENS // tm, nj),
        in_specs=[
            pl.BlockSpec((tm, D_MODEL), lambda i, j: (i, 0)),
            pl.BlockSpec((1, D_MODEL), lambda i, j: (0, 0)),
            pl.BlockSpec((D_MODEL, tn), lambda i, j: (0, j)),
            pl.BlockSpec((D_MODEL, tn), lambda i, j: (0, j + nj)),
            pl.BlockSpec((D_MODEL, tn), lambda i, j: (0, j + 2 * nj)),
            pl.BlockSpec((C_CONV, tn), lambda i, j: (0, j)),
        ],
        out_specs=pl.BlockSpec((tm, tn), lambda i, j: (i, j)),
        scratch_shapes=[pltpu.VMEM((tm, D_MODEL), BF16), pltpu.VMEM((nj, 8, tn), F32)],
        compiler_params=_cparams("arbitrary", "arbitrary"),
        name="c_in",
    )(h, gain.reshape(1, D_MODEL), w_in, w_in, w_in, conv_w)


def _mixer_c(h, gain, w_in, conv_w, w_out):
    return _out_proj(_c_in(h, gain, w_in, conv_w), w_out, h)


def kernel(x, positions, l0_ffn1_norm, l0_ffn1_w13, l0_ffn1_w2, l0_mix_norm, l0_a_w_qkv, l0_a_w_o, l0_ffn2_norm, l0_ffn2_w13, l0_ffn2_w2, l1_ffn1_norm, l1_ffn1_w13, l1_ffn1_w2, l1_mix_norm, l1_b_w_in, l1_b_q_norm, l1_b_w_uq, l1_b_kv_norm, l1_b_w_ukv, l1_b_w_o, l1_ffn2_norm, l1_ffn2_w13, l1_ffn2_w2, l2_ffn1_norm, l2_ffn1_w13, l2_ffn1_w2, l2_mix_norm, l2_c_w_in, l2_c_conv_w, l2_c_w_out, l2_ffn2_norm, l2_ffn2_w13, l2_ffn2_w2, l3_ffn1_norm, l3_ffn1_w13, l3_ffn1_w2, l3_mix_norm, l3_a_w_qkv, l3_a_w_o, l3_ffn2_norm, l3_ffn2_w13, l3_ffn2_w2, final_norm):
    bf = lambda w: w.astype(BF16)
    pos_tiles = positions.reshape(TOKENS // A_TILE, A_TILE)
    pos_a = [pos_tiles.reshape(-1, A_TILE // d, d).swapaxes(1, 2).reshape(TOKENS) for _, d in A_PATTERNS]
    tabs_a = _rope_tables(jnp.concatenate(pos_a + [jnp.zeros((A_TILE,), positions.dtype)]), A_ROT_HALF)
    tabs_b = _rope_tables(positions.reshape(TOKENS), B_ROT_HALF)

    def rope_group(w):
        return _rope_lanes(jnp.pad(w, [(0, 0)] * (w.ndim - 1) + [(0, LANES - B_ROPE_DIM)]), B_ROT_HALF)

    w_uq = l1_b_w_uq.reshape(B_Q_RANK, B_HEADS, B_NOPE_DIM + B_ROPE_DIM)
    w_q = jnp.concatenate([w_uq[..., :B_NOPE_DIM], rope_group(w_uq[..., B_NOPE_DIM:])], axis=-1)
    w_q = bf(w_q.reshape(B_Q_RANK, B_HEADS * B_SLOT))
    w_ukv = l1_b_w_ukv.reshape(B_KV_RANK, B_HEADS, B_NOPE_DIM + B_V_DIM)
    w_k = bf(w_ukv[:, :, :B_NOPE_DIM].reshape(B_KV_RANK, B_HEADS * B_NOPE_DIM))
    w_v = bf(w_ukv[:, :, B_NOPE_DIM:].reshape(B_KV_RANK, B_HEADS * B_V_DIM))
    n_c = B_Q_RANK + B_KV_RANK
    w_b_in = bf(jnp.concatenate([l1_b_w_in[:, :n_c], rope_group(l1_b_w_in[:, n_c:])], axis=-1))

    h = x.reshape(TOKENS, D_MODEL)

    h, w13, w2 = _ffn(h, l0_ffn1_norm, bf(l0_ffn1_w13), bf(l0_ffn1_w2), next_f32=(l0_ffn2_w13, l0_ffn2_w2))
    h = _mixer_a(h, l0_mix_norm, _qkv_w(l0_a_w_qkv), bf(l0_a_w_o), tabs_a)
    h, w13, w2 = _ffn(h, l0_ffn2_norm, w13, w2, next_f32=(l1_ffn1_w13, l1_ffn1_w2))

    h, w13, w2 = _ffn(h, l1_ffn1_norm, w13, w2, next_f32=(l1_ffn2_w13, l1_ffn2_w2))
    h = _mixer_b(h, l1_mix_norm, w_b_in, l1_b_q_norm, w_q, l1_b_kv_norm, w_k, w_v, bf(l1_b_w_o), tabs_b)
    h, w13, w2 = _ffn(h, l1_ffn2_norm, w13, w2, next_f32=(l2_ffn1_w13, l2_ffn1_w2))

    h, w13, w2 = _ffn(h, l2_ffn1_norm, w13, w2, next_f32=(l2_ffn2_w13, l2_ffn2_w2))
    h = _mixer_c(h, l2_mix_norm, bf(l2_c_w_in), l2_c_conv_w, bf(l2_c_w_out))
    h, w13, w2 = _ffn(h, l2_ffn2_norm, w13, w2, next_f32=(l3_ffn1_w13, l3_ffn1_w2))

    h, w13, w2 = _ffn(h, l3_ffn1_norm, w13, w2, next_f32=(l3_ffn2_w13, l3_ffn2_w2))
    h = _mixer_a(h, l3_mix_norm, _qkv_w(l3_a_w_qkv), bf(l3_a_w_o), tabs_a)
    h = _ffn(h, l3_ffn2_norm, w13, w2, final_gain=final_norm)

    return h.reshape(BATCH, SEQ, D_MODEL)
```

```python
import functools

import jax
import jax.numpy as jnp
from jax import lax
from jax.experimental import pallas as pl
from jax.experimental.pallas import tpu as pltpu

F32 = jnp.float32
BF16 = jnp.bfloat16

D_MODEL = 2048
BATCH = 2
SEQ = 8192
TOKENS = BATCH * SEQ
D_FF = 5632
HALF_STEP = 0.5
NORM_EPS = 1e-6
NEG_INF = -1e30
ROPE_THETA = 500000.0

A_HEADS = 16
A_HEAD_DIM = 128
A_ROT_HALF = A_HEAD_DIM // 8
A_PATTERNS = ((128, 1), (512, 4), (2048, 16))
A_BLOCK = 128
A_TILE = 2048
A_MID_DILATION = 4
assert [d for _, d in A_PATTERNS] == [1, A_MID_DILATION, A_MID_DILATION ** 2]

B_HEADS = 16
B_Q_RANK = 1536
B_KV_RANK = 512
B_NOPE_DIM = 128
B_ROPE_DIM = 64
B_ROT_HALF = B_ROPE_DIM // 2
B_V_DIM = 128
B_SLOT = 256
B_VX = 256
B_IN_COLS = 2176

C_CONV = 3

LANES = 128
MXU_COLS = 256
ROPE_PAIR = LANES // 2
VMEM_LIMIT = 56 * 1024 * 1024


def _cparams(*sem):
    return pltpu.CompilerParams(dimension_semantics=sem, vmem_limit_bytes=VMEM_LIMIT)


def _rms(x, gain):
    ms = jnp.mean(x * x, axis=-1, keepdims=True)
    return x * lax.rsqrt(ms + NORM_EPS) * gain


def _dot(a, b):
    return jnp.dot(a, b, preferred_element_type=F32)


def _dot_t(a, b):
    return lax.dot_general(a, b, (((1,), (1,)), ((), ())), preferred_element_type=F32)


def _rope128(y, cos, sin):
    return y * cos + pltpu.roll(y, ROPE_PAIR, 1) * sin


def _rope_lanes(w, half):
    return jnp.concatenate([w[..., :half], w[..., 2 * half:ROPE_PAIR + half], w[..., half:2 * half],
                            w[..., ROPE_PAIR + half:]], axis=-1)


def _rope_tab_kernel(pos_ref, inv_ref, c_ref, s_ref):
    ang = pos_ref[...].astype(F32) * inv_ref[...]
    lane = lax.broadcasted_iota(jnp.int32, (1, LANES), 1)
    sn = jnp.sin(ang)
    c_ref[...] = jnp.cos(ang)
    s_ref[...] = jnp.where(lane < ROPE_PAIR, -sn, sn)


def _rope_tables(pos_flat, half):
    inv = ROPE_THETA ** (-jnp.arange(half, dtype=F32) / half)
    gap = jnp.zeros((ROPE_PAIR - half,), F32)
    inv = jnp.concatenate([inv, gap, inv, gap]).reshape(1, LANES)
    n = pos_flat.shape[0]
    rows = 2048
    tab = jax.ShapeDtypeStruct((n, LANES), F32)
    row_spec = pl.BlockSpec((rows, LANES), lambda i: (i, 0))
    return pl.pallas_call(
        _rope_tab_kernel,
        out_shape=(tab,) * 2,
        grid=(n // rows,),
        in_specs=[pl.BlockSpec((rows, 1), lambda i: (i, 0)), pl.BlockSpec((1, LANES), lambda i: (0, 0))],
        out_specs=(row_spec,) * 2,
        compiler_params=_cparams("parallel"),
        name="rope_tables",
    )(pos_flat.reshape(n, 1), inv)


def _ffn_kernel(x_ref, g_ref, w1_ref, w3_ref, w2_ref, *refs, final, cast_next):
    j = pl.program_id(1)
    if cast_next:
        n13_ref, n2_ref, o_ref, c13_ref, c2_ref, xn_ref, acc_ref = refs
    else:
        o_ref, xn_ref, acc_ref = refs

    @pl.when(j == 0)
    def _():
        xn_ref[...] = _rms(x_ref[...], g_ref[0:1, :]).astype(BF16)
        acc_ref[...] = jnp.zeros_like(acc_ref)

    xn = xn_ref[...]
    gate = _dot(xn, w1_ref[...])
    up = _dot(xn, w3_ref[...])
    act = (gate * (1.0 / (1.0 + jnp.exp(-gate))) * up).astype(BF16)
    acc_ref[...] += _dot(act, w2_ref[...])
    if cast_next:
        c13_ref[...] = n13_ref[...].astype(BF16)
        c2_ref[...] = n2_ref[...].astype(BF16)

    @pl.when(j == pl.num_programs(1) - 1)
    def _():
        y = x_ref[...] + HALF_STEP * acc_ref[...]
        if final:
            y = _rms(y, g_ref[1:2, :])
        o_ref[...] = y


def _ffn(h, gain, w13, w2, final_gain=None, next_f32=None):
    tm, tf = 512, 512
    nj = D_FF // tf
    ni = TOKENS // tm
    final = final_gain is not None
    gains = jnp.stack([gain, final_gain if final else gain])
    in_specs = [
        pl.BlockSpec((tm, D_MODEL), lambda i, j: (i, 0)),
        pl.BlockSpec((2, D_MODEL), lambda i, j: (0, 0)),
        pl.BlockSpec((D_MODEL, tf), lambda i, j: (0, j)),
        pl.BlockSpec((D_MODEL, tf), lambda i, j: (0, j + nj)),
        pl.BlockSpec((tf, D_MODEL), lambda i, j: (j, 0)),
    ]
    out_shape = [jax.ShapeDtypeStruct((TOKENS, D_MODEL), F32)]
    out_specs = [pl.BlockSpec((tm, D_MODEL), lambda i, j: (i, 0))]
    args = [h, gains, w13, w13, w2]
    if next_f32 is not None:
        slab13 = pl.BlockSpec((D_MODEL // ni, 2 * D_FF // nj), lambda i, j: (i, j))
        slab2 = pl.BlockSpec((D_FF // (ni * nj), D_MODEL), lambda i, j: (i * nj + j, 0))
        in_specs += [slab13, slab2]
        out_specs += [slab13, slab2]
        out_shape += [jax.ShapeDtypeStruct(w.shape, BF16) for w in next_f32]
        args += list(next_f32)
    outs = pl.pallas_call(
        functools.partial(_ffn_kernel, final=final, cast_next=next_f32 is not None),
        out_shape=tuple(out_shape),
        grid=(ni, nj),
        in_specs=in_specs,
        out_specs=tuple(out_specs),
        scratch_shapes=[pltpu.VMEM((tm, D_MODEL), BF16), pltpu.VMEM((tm, D_MODEL), F32)],
        compiler_params=_cparams("parallel", "arbitrary"),
        name="ffn",
    )(*args)
    return outs if next_f32 is not None else outs[0]


def _out_proj_kernel(x_ref, w_ref, r_ref, o_ref):
    o_ref[...] = r_ref[...] + _dot(x_ref[...], w_ref[...])


def _out_proj(x, w, res):
    k = x.shape[1]
    tm = 512
    return pl.pallas_call(
        _out_proj_kernel,
        out_shape=jax.ShapeDtypeStruct((TOKENS, D_MODEL), F32),
        grid=(TOKENS // tm,),
        in_specs=[
            pl.BlockSpec((tm, k), lambda i: (i, 0)),
            pl.BlockSpec((k, D_MODEL), lambda i: (0, 0)),
            pl.BlockSpec((tm, D_MODEL), lambda i: (i, 0)),
        ],
        out_specs=pl.BlockSpec((tm, D_MODEL), lambda i: (i, 0)),
        compiler_params=_cparams("parallel"),
        name="out_proj",
    )(x, w, res)


def _norm_perm_a_kernel(x_ref, g_ref, *refs, tm):
    o_refs, xs_ref, mid_ref = refs[:-2], refs[-2], refs[-1]
    xn = _rms(x_ref[...], g_ref[...])
    nchunk = D_MODEL // LANES
    for k in range(nchunk):
        xs_ref[k] = xn[:, k * LANES:(k + 1) * LANES]
    dp = 1
    for (_, d), o_ref in zip(A_PATTERNS, o_refs):
        if d == 1:
            o_ref[...] = xn.astype(BF16)
            continue
        step = d // dp
        for c in range(d):
            for k in range(nchunk):
                rows = pl.ds(c // dp, tm // d, stride=step)
                piece = xs_ref[k, rows, :] if dp == 1 else mid_ref[k, c % dp, rows, :]
                o_ref[0, c, 0, :, k * LANES:(k + 1) * LANES] = piece.astype(BF16)
                if d == A_MID_DILATION:
                    mid_ref[k, c] = piece
        dp = d


def _norm_perm_a(h, gain):
    tm = 512
    per_tile = A_TILE // tm
    shapes, specs = [], []
    for _, d in A_PATTERNS:
        if d == 1:
            shapes.append(jax.ShapeDtypeStruct((TOKENS, D_MODEL), BF16))
            specs.append(pl.BlockSpec((tm, D_MODEL), lambda i: (i, 0)))
        else:
            shapes.append(jax.ShapeDtypeStruct((TOKENS // A_TILE, d, per_tile, tm // d, D_MODEL), BF16))
            specs.append(pl.BlockSpec((1, d, 1, tm // d, D_MODEL),
                                      lambda i: (i // per_tile, 0, i % per_tile, 0, 0)))
    outs = pl.pallas_call(
        functools.partial(_norm_perm_a_kernel, tm=tm),
        out_shape=tuple(shapes),
        grid=(TOKENS // tm,),
        in_specs=[pl.BlockSpec((tm, D_MODEL), lambda i: (i, 0)), pl.BlockSpec((1, D_MODEL), lambda i: (0, 0))],
        out_specs=tuple(specs),
        scratch_shapes=[pltpu.VMEM((D_MODEL // LANES, tm, LANES), F32),
                        pltpu.VMEM((D_MODEL // LANES, A_MID_DILATION, tm // A_MID_DILATION, LANES), F32)],
        compiler_params=_cparams("parallel"),
        name="norm_perm_a",
    )(h, gain.reshape(1, D_MODEL))
    return [o.reshape(TOKENS, D_MODEL) for o in outs]


def _qkv_w_kernel(w_ref, o_ref):
    part = pl.program_id(1) % 3

    @pl.when(part == 2)
    def _():
        o_ref[...] = w_ref[...].astype(BF16)

    @pl.when(part != 2)
    def _():
        lane = lax.broadcasted_iota(jnp.int32, (1, LANES), 1)
        stay = (lane < A_ROT_HALF) | (lane >= ROPE_PAIR + A_ROT_HALF)
        for hh in range(A_HEADS):
            sl = slice(hh * A_HEAD_DIM, (hh + 1) * A_HEAD_DIM)
            y = w_ref[:, sl]
            moved = jnp.where(lane < ROPE_PAIR, pltpu.roll(y, LANES - A_ROT_HALF, 1),
                              pltpu.roll(y, ROPE_PAIR - A_ROT_HALF, 1))
            o_ref[:, sl] = jnp.where(stay, y, moved).astype(BF16)


def _qkv_w(w_qkv):
    rows = 512
    hd = A_HEADS * A_HEAD_DIM
    spec = pl.BlockSpec((rows, hd), lambda i, j: (i, j))
    return pl.pallas_call(
        _qkv_w_kernel,
        out_shape=jax.ShapeDtypeStruct(w_qkv.shape, BF16),
        grid=(D_MODEL // rows, w_qkv.shape[1] // hd),
        in_specs=[spec],
        out_specs=spec,
        compiler_params=_cparams("parallel", "parallel"),
        name="qkv_w",
    )(w_qkv)


def _qkv_a_kernel(x_ref, w_ref, cos_ref, sin_ref, o_ref, *, tn):
    part = pl.program_id(1) // ((A_HEADS * A_HEAD_DIM) // tn)
    scale = jnp.where(part == 0, A_HEAD_DIM ** -0.5, 1.0).astype(F32)
    half = x_ref.shape[0] // 2
    for cc in range(tn // MXU_COLS):
        for rows in (slice(0, half), slice(half, 2 * half)):
            y = _dot(x_ref[rows, :], w_ref[:, cc * MXU_COLS:(cc + 1) * MXU_COLS])
            for hh in range(MXU_COLS // LANES):
                sl = slice(cc * MXU_COLS + hh * LANES, cc * MXU_COLS + (hh + 1) * LANES)
                y_h = y[:, hh * LANES:(hh + 1) * LANES]
                o_ref[rows, sl] = (_rope128(y_h, cos_ref[rows, :], sin_ref[rows, :]) * scale).astype(BF16)


def _qkv_a(xn, w_qkv, tabs, group):
    tm, tn = 1024, 2048
    cols = 3 * A_HEADS * A_HEAD_DIM
    nj = cols // tn
    ni = TOKENS // tm
    v_first = 2 * (A_HEADS * A_HEAD_DIM) // tn
    tab_spec = pl.BlockSpec((tm, LANES),
                            lambda i, j: (jnp.where(j >= v_first, len(A_PATTERNS) * ni, group * ni + i), 0))
    return pl.pallas_call(
        functools.partial(_qkv_a_kernel, tn=tn),
        out_shape=jax.ShapeDtypeStruct((TOKENS, cols), BF16),
        grid=(ni, nj),
        in_specs=[
            pl.BlockSpec((tm, D_MODEL), lambda i, j: (i, 0)),
            pl.BlockSpec((D_MODEL, tn), lambda i, j: (0, group * nj + j)),
            tab_spec, tab_spec,
        ],
        out_specs=pl.BlockSpec((tm, tn), lambda i, j: (i, j)),
        compiler_params=_cparams("parallel", "parallel"),
        name="qkv_a",
    )(xn, w_qkv, *tabs)


def _attn_a_kernel(q_ref, kc_ref, kp_ref, vc_ref, vp_ref, o_ref, lse_ref, *scratch, d, heads, per_trip):
    rows = A_TILE // d
    nblk = rows // A_BLOCK
    two_stage = d > A_MID_DILATION
    if two_stage:
        os_ref, mid_ref, lse_mid_ref = scratch
        stage_ref, lse_stage_ref, mid_step = mid_ref, lse_mid_ref, d // A_MID_DILATION
    else:
        (os_ref,) = scratch
        stage_ref, lse_stage_ref = os_ref, lse_ref
    first_tile = pl.program_id(1) == 0
    r = lax.broadcasted_iota(jnp.int32, (A_BLOCK, 2 * A_BLOCK), 0)
    kk = lax.broadcasted_iota(jnp.int32, (A_BLOCK, 2 * A_BLOCK), 1)
    band = (kk >= r) & (kk <= r + A_BLOCK)
    lane = lax.broadcasted_iota(jnp.int32, (A_BLOCK, LANES), 1)
    hs = [slice(h * A_HEAD_DIM, (h + 1) * A_HEAD_DIM) for h in range(heads)]
    first_head = pl.program_id(2) * heads

    @pl.when(first_head == 0)
    def _():
        lse_stage_ref[...] = jnp.zeros_like(lse_stage_ref)

    def trip(t, carry):
        loaded = []
        for i in range(per_trip):
            u = t * per_trip + i
            c = u // nblk
            blk = u % nblk
            start = pl.multiple_of(blk * A_BLOCK, A_BLOCK)
            before = pl.multiple_of(jnp.maximum(start - A_BLOCK, 0), A_BLOCK)
            q = q_ref[0, c, pl.ds(start, A_BLOCK), :]
            k = jnp.concatenate([jnp.where(blk == 0, kp_ref[0, c], kc_ref[0, c, pl.ds(before, A_BLOCK), :]),
                                 kc_ref[0, c, pl.ds(start, A_BLOCK), :]], axis=0)
            v = jnp.concatenate([jnp.where(blk == 0, vp_ref[0, c], vc_ref[0, c, pl.ds(before, A_BLOCK), :]),
                                 vc_ref[0, c, pl.ds(start, A_BLOCK), :]], axis=0)
            no_prev = jnp.logical_and(first_tile, blk == 0)
            mask = jnp.logical_and(band, jnp.logical_or(kk >= A_BLOCK, jnp.logical_not(no_prev)))
            out_rows = (pl.ds(blk * (A_BLOCK * d) + c, A_BLOCK, stride=d) if d > 1
                        else pl.ds(start, A_BLOCK))
            if two_stage:
                o_at = (c % A_MID_DILATION,
                        pl.ds(blk * (A_BLOCK * mid_step) + c // A_MID_DILATION, A_BLOCK, stride=mid_step))
            else:
                o_at = (out_rows,)
            loaded.append(([_dot_t(q[:, sl], k[:, sl]) for sl in hs], v, mask, out_rows, o_at))
        for scores, v, mask, out_rows, o_at in loaded:
            lse_tile = lse_stage_ref[o_at + (slice(None),)]
            for h, sl in enumerate(hs):
                s = jnp.where(mask, scores[h], NEG_INF)
                m = jnp.max(s, axis=-1, keepdims=True)
                p = jnp.exp(s - m)
                l = jnp.sum(p, axis=-1, keepdims=True)
                stage_ref[(h,) + o_at + (slice(None),)] = _dot(p.astype(BF16), v[:, sl]) / l
                lse_tile = jnp.where(lane == first_head + h, m + jnp.log(l), lse_tile)
            lse_stage_ref[o_at + (slice(None),)] = lse_tile
        return carry

    lax.fori_loop(0, d * nblk // per_trip, trip, 0)
    if two_stage:
        for h in range(heads):
            for c4 in range(A_MID_DILATION):
                os_ref[h, pl.ds(c4, A_TILE // A_MID_DILATION, stride=A_MID_DILATION), :] = mid_ref[h, c4]
        for c4 in range(A_MID_DILATION):
            lse_ref[pl.ds(c4, A_TILE // A_MID_DILATION, stride=A_MID_DILATION), :] = lse_mid_ref[c4]
    o_ref[...] = os_ref[...].astype(BF16)


def _attn_a(qkv, window, dilation):
    assert window // dilation == A_BLOCK, "band width must equal the query block"
    d = dilation
    heads = 4
    hw = heads * A_HEAD_DIM
    ngrp = A_HEADS // heads
    rows = A_TILE // d
    tiles = SEQ // A_TILE
    qkv_v = qkv.reshape(BATCH, tiles, d, rows, 3 * A_HEADS * A_HEAD_DIM)
    last_blk = rows // A_BLOCK - 1

    def cur(part):
        return pl.BlockSpec((None, 1, d, rows, hw), lambda b, n, g: (b, n, 0, 0, part * ngrp + g))

    def prev(part):
        return pl.BlockSpec((None, 1, d, A_BLOCK, hw),
                            lambda b, n, g: (b, jnp.maximum(n - 1, 0), 0, last_blk, part * ngrp + g))

    return pl.pallas_call(
        functools.partial(_attn_a_kernel, d=d, heads=heads, per_trip=4),
        out_shape=(jax.ShapeDtypeStruct((A_HEADS, TOKENS, A_HEAD_DIM), BF16),
                   jax.ShapeDtypeStruct((TOKENS, LANES), F32)),
        grid=(BATCH, tiles, ngrp),
        in_specs=[cur(0), cur(1), prev(1), cur(2), prev(2)],
        out_specs=(pl.BlockSpec((heads, A_TILE, A_HEAD_DIM), lambda b, n, g: (g, b * tiles + n, 0)),
                   pl.BlockSpec((A_TILE, LANES), lambda b, n, g: (b * tiles + n, 0))),
        scratch_shapes=[pltpu.VMEM((heads, A_TILE, A_HEAD_DIM), F32)] + (
            [pltpu.VMEM((heads, A_MID_DILATION, A_TILE // A_MID_DILATION, A_HEAD_DIM), F32),
             pltpu.VMEM((A_MID_DILATION, A_TILE // A_MID_DILATION, LANES), F32)]
            if d > A_MID_DILATION else []),
        compiler_params=_cparams("parallel", "parallel", "arbitrary"),
        name="attn_a_d%d" % d,
    )(qkv_v, qkv_v, qkv_v, qkv_v, qkv_v)


def _comb_proj_a_kernel(o0, o1, o2, l0, l1, l2, w_ref, r_ref, out_ref, comb_ref):
    s = pl.program_id(0)
    slot = s % 2

    @pl.when(s == 0)
    def _():
        comb_ref[1] = jnp.zeros(comb_ref.shape[1:], BF16)

    out_ref[...] = r_ref[...] + _dot(comb_ref[1 - slot], w_ref[...])
    a = [l0[...], l1[...], l2[...]]
    m = jnp.maximum(jnp.maximum(a[0], a[1]), a[2])
    e = [jnp.exp(t - m) for t in a]
    den = e[0] + e[1] + e[2]
    w = [t / den for t in e]
    rows = out_ref.shape[0]
    for h in range(A_HEADS):
        acc = jnp.zeros((rows, A_HEAD_DIM), F32)
        for wg, og in zip(w, (o0, o1, o2)):
            acc = acc + jnp.broadcast_to(wg[:, h:h + 1], (rows, A_HEAD_DIM)) * og[h]
        comb_ref[slot, :, h * A_HEAD_DIM:(h + 1) * A_HEAD_DIM] = acc.astype(BF16)


def _comb_proj_a(outs, lses, w_o, res):
    tm = 256
    nt = TOKENS // tm
    hd = A_HEADS * A_HEAD_DIM
    o_spec = pl.BlockSpec((A_HEADS, tm, A_HEAD_DIM), lambda s: (0, jnp.minimum(s, nt - 1), 0))
    l_spec = pl.BlockSpec((tm, LANES), lambda s: (jnp.minimum(s, nt - 1), 0))
    row_spec = pl.BlockSpec((tm, D_MODEL), lambda s: (jnp.maximum(s - 1, 0), 0))
    return pl.pallas_call(
        _comb_proj_a_kernel,
        out_shape=jax.ShapeDtypeStruct((TOKENS, D_MODEL), F32),
        grid=(nt + 1,),
        in_specs=[o_spec] * 3 + [l_spec] * 3 + [pl.BlockSpec((hd, D_MODEL), lambda s: (0, 0)), row_spec],
        out_specs=row_spec,
        scratch_shapes=[pltpu.VMEM((2, tm, hd), BF16)],
        compiler_params=_cparams("arbitrary"),
        name="comb_proj_a",
    )(*outs, *lses, w_o, res)


def _mixer_a(h, gain, w_qkv, w_o, tabs_a):
    outs, lses = [], []
    for g, ((window, dilation), xn) in enumerate(zip(A_PATTERNS, _norm_perm_a(h, gain))):
        o, lse = _attn_a(_qkv_a(xn, w_qkv, tabs_a, g), window, dilation)
        outs.append(o)
        lses.append(lse)
    return _comb_proj_a(outs, lses, w_o, h)


def _b_in_kernel(x_ref, g_ref, w_ref, o_ref):
    o_ref[...] = _dot(_rms(x_ref[...], g_ref[...]).astype(BF16), w_ref[...])


def _b_in(h, gain, w_in):
    tm = 512
    return pl.pallas_call(
        _b_in_kernel,
        out_shape=jax.ShapeDtypeStruct((TOKENS, B_IN_COLS), F32),
        grid=(TOKENS // tm,),
        in_specs=[
            pl.BlockSpec((tm, D_MODEL), lambda i: (i, 0)),
            pl.BlockSpec((1, D_MODEL), lambda i: (0, 0)),
            pl.BlockSpec((D_MODEL, B_IN_COLS), lambda i: (0, 0)),
        ],
        out_specs=pl.BlockSpec((tm, B_IN_COLS), lambda i: (i, 0)),
        compiler_params=_cparams("parallel"),
        name="b_in",
    )(h, gain.reshape(1, D_MODEL), w_in)


def _b_q_kernel(c_ref, g_ref, w_ref, cos_ref, sin_ref, o_ref, xn_ref, *, slots):
    @pl.when(pl.program_id(1) == 0)
    def _():
        xn_ref[...] = _rms(c_ref[...], g_ref[...]).astype(BF16)

    y = _dot(xn_ref[...], w_ref[...])
    scale = (B_NOPE_DIM + B_ROPE_DIM) ** -0.5
    cos, sin = cos_ref[...], sin_ref[...]
    for s in range(slots):
        nope = y[:, s * B_SLOT:s * B_SLOT + LANES]
        rot = _rope128(y[:, s * B_SLOT + LANES:(s + 1) * B_SLOT], cos, sin)
        o_ref[s, :, :LANES] = (nope * scale).astype(BF16)
        o_ref[s, :, LANES:] = (rot * scale).astype(BF16)


def _b_q(c, gain, w_q, tabs_b):
    tm, slots = 1024, 8
    tn = slots * B_SLOT
    tab_spec = pl.BlockSpec((tm, LANES), lambda i, j: (i, 0))
    return pl.pallas_call(
        functools.partial(_b_q_kernel, slots=slots),
        out_shape=jax.ShapeDtypeStruct((B_HEADS, TOKENS, B_SLOT), BF16),
        grid=(TOKENS // tm, B_HEADS // slots),
        in_specs=[
            pl.BlockSpec((tm, B_Q_RANK), lambda i, j: (i, 0)),
            pl.BlockSpec((1, B_Q_RANK), lambda i, j: (0, 0)),
            pl.BlockSpec((B_Q_RANK, tn), lambda i, j: (0, j)),
            tab_spec, tab_spec,
        ],
        out_specs=pl.BlockSpec((slots, tm, B_SLOT), lambda i, j: (j, i, 0)),
        scratch_shapes=[pltpu.VMEM((tm, B_Q_RANK), BF16)],
        compiler_params=_cparams("parallel", "arbitrary"),
        name="b_q",
    )(c, gain.reshape(1, B_Q_RANK), w_q, *tabs_b)


def _b_kv_kernel(c_ref, g_ref, wk_ref, wv_ref, cr_ref, cos_ref, sin_ref, k_ref, v_ref, *, slots):
    xn = _rms(c_ref[...], g_ref[...]).astype(BF16)
    yk = _dot(xn, wk_ref[...])
    yv = _dot(xn, wv_ref[...])
    kr = _rope128(cr_ref[...], cos_ref[...], sin_ref[...]).astype(BF16)
    lane = lax.broadcasted_iota(jnp.int32, (c_ref.shape[0], B_VX - B_V_DIM), 1)
    ones_col = jnp.where(lane == 0, 1.0, 0.0).astype(BF16)
    for s in range(slots):
        sl = slice(s * LANES, (s + 1) * LANES)
        k_ref[s, :, :LANES] = yk[:, sl].astype(BF16)
        k_ref[s, :, LANES:] = kr
        v_ref[s, :, :B_V_DIM] = yv[:, sl].astype(BF16)
        v_ref[s, :, B_V_DIM:] = ones_col


def _b_kv(c, gain, w_k, w_v, tabs_b):
    tm, slots = 1024, 8
    tn = slots * LANES
    kv_block = B_Q_RANK // B_KV_RANK
    rope_block = (B_Q_RANK + B_KV_RANK) // LANES
    tab_spec = pl.BlockSpec((tm, LANES), lambda i, j: (i, 0))
    return pl.pallas_call(
        functools.partial(_b_kv_kernel, slots=slots),
        out_shape=(jax.ShapeDtypeStruct((B_HEADS, TOKENS, B_SLOT), BF16),
                   jax.ShapeDtypeStruct((B_HEADS, TOKENS, B_VX), BF16)),
        grid=(TOKENS // tm, B_HEADS // slots),
        in_specs=[
            pl.BlockSpec((tm, B_KV_RANK), lambda i, j: (i, kv_block)),
            pl.BlockSpec((1, B_KV_RANK), lambda i, j: (0, 0)),
            pl.BlockSpec((B_KV_RANK, tn), lambda i, j: (0, j)),
            pl.BlockSpec((B_KV_RANK, tn), lambda i, j: (0, j)),
            pl.BlockSpec((tm, LANES), lambda i, j: (i, rope_block)),
            tab_spec, tab_spec,
        ],
        out_specs=(pl.BlockSpec((slots, tm, B_SLOT), lambda i, j: (j, i, 0)),
                   pl.BlockSpec((slots, tm, B_VX), lambda i, j: (j, i, 0))),
        compiler_params=_cparams("parallel", "parallel"),
        name="b_kv",
    )(c, gain.reshape(1, B_KV_RANK), w_k, w_v, c, *tabs_b)


def _attn_b_kernel(q_ref, k_ref, v_ref, o_ref, s0_ref, s1_ref, m_ref, acc_ref, *, tq, heads):
    qi = pl.program_id(2)
    m_ref[...] = jnp.full_like(m_ref, NEG_INF)
    acc_ref[...] = jnp.zeros_like(acc_ref)

    def scores(s_ref, ki):
        for h in range(heads):
            k = k_ref[h, pl.ds(pl.multiple_of(ki * tq, tq), tq), :]
            s_ref[h] = _dot_t(q_ref[h], k)

    def fold(s_ref, ki, diagonal):
        for h in range(heads):
            s = s_ref[h]
            if diagonal:
                row = lax.broadcasted_iota(jnp.int32, (tq, tq), 0)
                col = lax.broadcasted_iota(jnp.int32, (tq, tq), 1)
                s = jnp.where(row >= col, s, NEG_INF)
            m_prev = m_ref[h]
            m_new = jnp.maximum(m_prev, jnp.max(s, axis=-1, keepdims=True))
            alpha = jnp.exp(m_prev - m_new)
            p = jnp.exp((s - jnp.concatenate([m_new] * (tq // LANES), axis=1)).astype(BF16))
            v = v_ref[h, pl.ds(pl.multiple_of(ki * tq, tq), tq), :]
            acc_ref[h] = jnp.concatenate([alpha] * (B_VX // LANES), axis=1) * acc_ref[h] + _dot(p, v)
            m_ref[h] = m_new

    pairs = qi // 2
    scores(s0_ref, 0)

    def body(t, carry):
        scores(s1_ref, 2 * t + 1)
        fold(s0_ref, 2 * t, False)
        scores(s0_ref, 2 * t + 2)
        fold(s1_ref, 2 * t + 1, False)
        return carry

    lax.fori_loop(0, pairs, body, 0)

    @pl.when(qi % 2 == 1)
    def _():
        scores(s1_ref, qi)
        fold(s0_ref, qi - 1, False)
        fold(s1_ref, qi, True)

    @pl.when(qi % 2 == 0)
    def _():
        fold(s0_ref, qi, True)

    for h in range(heads):
        acc = acc_ref[h]
        l = jnp.broadcast_to(acc[:, B_V_DIM:B_V_DIM + 1], (tq, B_V_DIM))
        o_ref[:, h * B_V_DIM:(h + 1) * B_V_DIM] = (acc[:, :B_V_DIM] / l).astype(BF16)


def _attn_b(q3, k3, v3):
    tq, heads = 512, 2
    nq = SEQ // tq
    return pl.pallas_call(
        functools.partial(_attn_b_kernel, tq=tq, heads=heads),
        out_shape=jax.ShapeDtypeStruct((TOKENS, B_HEADS * B_V_DIM), BF16),
        grid=(BATCH, B_HEADS // heads, nq),
        in_specs=[
            pl.BlockSpec((heads, tq, B_SLOT), lambda b, g, qi: (g, b * nq + qi, 0)),
            pl.BlockSpec((heads, SEQ, B_SLOT), lambda b, g, qi: (g, b, 0)),
            pl.BlockSpec((heads, SEQ, B_VX), lambda b, g, qi: (g, b, 0)),
        ],
        out_specs=pl.BlockSpec((tq, heads * B_V_DIM), lambda b, g, qi: (b * nq + qi, g)),
        scratch_shapes=[pltpu.VMEM((heads, tq, tq), F32), pltpu.VMEM((heads, tq, tq), F32),
                        pltpu.VMEM((heads, tq, LANES), F32), pltpu.VMEM((heads, tq, B_VX), F32)],
        compiler_params=_cparams("parallel", "parallel", "arbitrary"),
        name="attn_b",
    )(q3, k3, v3)


def _mixer_b(h, gain, w_in, q_norm, w_q, kv_norm, w_k, w_v, w_o, tabs_b):
    c = _b_in(h, gain, w_in)
    q3 = _b_q(c, q_norm, w_q, tabs_b)
    k3, v3 = _b_kv(c, kv_norm, w_k, w_v, tabs_b)
    return _out_proj(_attn_b(q3, k3, v3), w_o, h)


def _c_in_kernel(x_ref, g_ref, wb_ref, wc_ref, wu_ref, cw_ref, o_ref, xn_ref, carry_ref, *, tm):
    i = pl.program_id(0)
    j = pl.program_id(1)

    @pl.when(j == 0)
    def _():
        xn_ref[...] = _rms(x_ref[...], g_ref[...]).astype(BF16)

    xn = xn_ref[...]
    first = (i * tm) % SEQ == 0
    for cc in range(o_ref.shape[1] // MXU_COLS):
        cs = slice(cc * MXU_COLS, (cc + 1) * MXU_COLS)
        b_gate = _dot(xn, wb_ref[:, cs])
        cu = _dot(xn, wc_ref[:, cs]) * _dot(xn, wu_ref[:, cs])
        w0, w1, w2 = cw_ref[0:1, cs], cw_ref[1:2, cs], cw_ref[2:3, cs]
        z = w2 * cu + w1 * pltpu.roll(cu, 1, 0) + w0 * pltpu.roll(cu, 2, 0)
        o_ref[:, cs] = (b_gate * z).astype(BF16)

        tail = jnp.where(first, 0.0, carry_ref[j, :, cs])
        both = jnp.concatenate([tail, cu[0:8, :]], axis=0)
        z8 = w2 * both[8:16] + w1 * both[7:15] + w0 * both[6:14]
        o_ref[0:8, cs] = (b_gate[0:8, :] * z8).astype(BF16)
        carry_ref[j, :, cs] = cu[tm - 8:tm, :]


def _c_in(h, gain, w_in, conv_w):
    tm, tn = 1024, 512
    nj = D_MODEL // tn
    return pl.pallas_call(
        functools.partial(_c_in_kernel, tm=tm),
        out_shape=jax.ShapeDtypeStruct((TOKENS, D_MODEL), BF16),
        grid=(TOKENS // tm, nj),
        in_specs=[
            pl.BlockSpec((tm, D_MODEL), lambda i, j: (i, 0)),
            pl.BlockSpec((1, D_MODEL), lambda i, j: (0, 0)),
            pl.BlockSpec((D_MODEL, tn), lambda i, j: (0, j)),
            pl.BlockSpec((D_MODEL, tn), lambda i, j: (0, j + nj)),
            pl.BlockSpec((D_MODEL, tn), lambda i, j: (0, j + 2 * nj)),
            pl.BlockSpec((C_CONV, tn), lambda i, j: (0, j)),
        ],
        out_specs=pl.BlockSpec((tm, tn), lambda i, j: (i, j)),
        scratch_shapes=[pltpu.VMEM((tm, D_MODEL), BF16), pltpu.VMEM((nj, 8, tn), F32)],
        compiler_params=_cparams("arbitrary", "arbitrary"),
        name="c_in",
    )(h, gain.reshape(1, D_MODEL), w_in, w_in, w_in, conv_w)


def _mixer_c(h, gain, w_in, conv_w, w_out):
    return _out_proj(_c_in(h, gain, w_in, conv_w), w_out, h)


def kernel(x, positions, l0_ffn1_norm, l0_ffn1_w13, l0_ffn1_w2, l0_mix_norm, l0_a_w_qkv, l0_a_w_o, l0_ffn2_norm, l0_ffn2_w13, l0_ffn2_w2, l1_ffn1_norm, l1_ffn1_w13, l1_ffn1_w2, l1_mix_norm, l1_b_w_in, l1_b_q_norm, l1_b_w_uq, l1_b_kv_norm, l1_b_w_ukv, l1_b_w_o, l1_ffn2_norm, l1_ffn2_w13, l1_ffn2_w2, l2_ffn1_norm, l2_ffn1_w13, l2_ffn1_w2, l2_mix_norm, l2_c_w_in, l2_c_conv_w, l2_c_w_out, l2_ffn2_norm, l2_ffn2_w13, l2_ffn2_w2, l3_ffn1_norm, l3_ffn1_w13, l3_ffn1_w2, l3_mix_norm, l3_a_w_qkv, l3_a_w_o, l3_ffn2_norm, l3_ffn2_w13, l3_ffn2_w2, final_norm):
    bf = lambda w: w.astype(BF16)
    pos_tiles = positions.reshape(TOKENS // A_TILE, A_TILE)
    pos_a = [pos_tiles.reshape(-1, A_TILE // d, d).swapaxes(1, 2).reshape(TOKENS) for _, d in A_PATTERNS]
    tabs_a = _rope_tables(jnp.concatenate(pos_a + [jnp.zeros((A_TILE,), positions.dtype)]), A_ROT_HALF)
    tabs_b = _rope_tables(positions.reshape(TOKENS), B_ROT_HALF)

    def rope_group(w):
        return _rope_lanes(jnp.pad(w, [(0, 0)] * (w.ndim - 1) + [(0, LANES - B_ROPE_DIM)]), B_ROT_HALF)

    w_uq = l1_b_w_uq.reshape(B_Q_RANK, B_HEADS, B_NOPE_DIM + B_ROPE_DIM)
    w_q = jnp.concatenate([w_uq[..., :B_NOPE_DIM], rope_group(w_uq[..., B_NOPE_DIM:])], axis=-1)
    w_q = bf(w_q.reshape(B_Q_RANK, B_HEADS * B_SLOT))
    w_ukv = l1_b_w_ukv.reshape(B_KV_RANK, B_HEADS, B_NOPE_DIM + B_V_DIM)
    w_k = bf(w_ukv[:, :, :B_NOPE_DIM].reshape(B_KV_RANK, B_HEADS * B_NOPE_DIM))
    w_v = bf(w_ukv[:, :, B_NOPE_DIM:].reshape(B_KV_RANK, B_HEADS * B_V_DIM))
    n_c = B_Q_RANK + B_KV_RANK
    w_b_in = bf(jnp.concatenate([l1_b_w_in[:, :n_c], rope_group(l1_b_w_in[:, n_c:])], axis=-1))

    h = x.reshape(TOKENS, D_MODEL)

    h, w13, w2 = _ffn(h, l0_ffn1_norm, bf(l0_ffn1_w13), bf(l0_ffn1_w2), next_f32=(l0_ffn2_w13, l0_ffn2_w2))
    h = _mixer_a(h, l0_mix_norm, _qkv_w(l0_a_w_qkv), bf(l0_a_w_o), tabs_a)
    h, w13, w2 = _ffn(h, l0_ffn2_norm, w13, w2, next_f32=(l1_ffn1_w13, l1_ffn1_w2))

    h, w13, w2 = _ffn(h, l1_ffn1_norm, w13, w2, next_f32=(l1_ffn2_w13, l1_ffn2_w2))
    h = _mixer_b(h, l1_mix_norm, w_b_in, l1_b_q_norm, w_q, l1_b_kv_norm, w_k, w_v, bf(l1_b_w_o), tabs_b)
    h, w13, w2 = _ffn(h, l1_ffn2_norm, w13, w2, next_f32=(l2_ffn1_w13, l2_ffn1_w2))

    h, w13, w2 = _ffn(h, l2_ffn1_norm, w13, w2, next_f32=(l2_ffn2_w13, l2_ffn2_w2))
    h = _mixer_c(h, l2_mix_norm, bf(l2_c_w_in), l2_c_conv_w, bf(l2_c_w_out))
    h, w13, w2 = _ffn(h, l2_ffn2_norm, w13, w2, next_f32=(l3_ffn1_w13, l3_ffn1_w2))

    h, w13, w2 = _ffn(h, l3_ffn1_norm, w13, w2, next_f32=(l3_ffn2_w13, l3_ffn2_w2))
    h = _mixer_a(h, l3_mix_norm, _qkv_w(l3_a_w_qkv), bf(l3_a_w_o), tabs_a)
    h = _ffn(h, l3_ffn2_norm, w13, w2, final_gain=final_norm)

    return h.reshape(BATCH, SEQ, D_MODEL)
```

```python
import functools

import jax
import jax.numpy as jnp
from jax import lax
from jax.experimental import pallas as pl
from jax.experimental.pallas import tpu as pltpu

F32 = jnp.float32
BF16 = jnp.bfloat16

D_MODEL = 2048
BATCH = 2
SEQ = 8192
TOKENS = BATCH * SEQ
D_FF = 5632
HALF_STEP = 0.5
NORM_EPS = 1e-6
NEG_INF = -1e30
ROPE_THETA = 500000.0

A_HEADS = 16
A_HEAD_DIM = 128
A_ROT_HALF = A_HEAD_DIM // 8
A_PATTERNS = ((128, 1), (512, 4), (2048, 16))
A_BLOCK = 128
A_TILE = 2048
A_MID_DILATION = 4
assert [d for _, d in A_PATTERNS] == [1, A_MID_DILATION, A_MID_DILATION ** 2]

B_HEADS = 16
B_Q_RANK = 1536
B_KV_RANK = 512
B_NOPE_DIM = 128
B_ROPE_DIM = 64
B_ROT_HALF = B_ROPE_DIM // 2
B_V_DIM = 128
B_SLOT = 256
B_VX = 256
B_IN_COLS = 2176

C_CONV = 3

LANES = 128
MXU_COLS = 256
ROPE_PAIR = LANES // 2
VMEM_LIMIT = 56 * 1024 * 1024


def _cparams(*sem):
    return pltpu.CompilerParams(dimension_semantics=sem, vmem_limit_bytes=VMEM_LIMIT)


def _rms(x, gain):
    ms = jnp.mean(x * x, axis=-1, keepdims=True)
    return x * lax.rsqrt(ms + NORM_EPS) * gain


def _dot(a, b):
    return jnp.dot(a, b, preferred_element_type=F32)


def _dot_t(a, b):
    return lax.dot_general(a, b, (((1,), (1,)), ((), ())), preferred_element_type=F32)


def _rope128(y, cos, sin):
    return y * cos + pltpu.roll(y, ROPE_PAIR, 1) * sin


def _rope_lanes(w, half):
    return jnp.concatenate([w[..., :half], w[..., 2 * half:ROPE_PAIR + half], w[..., half:2 * half],
                            w[..., ROPE_PAIR + half:]], axis=-1)


def _rope_tab_kernel(pos_ref, inv_ref, c_ref, s_ref):
    ang = pos_ref[...].astype(F32) * inv_ref[...]
    lane = lax.broadcasted_iota(jnp.int32, (1, LANES), 1)
    sn = jnp.sin(ang)
    c_ref[...] = jnp.cos(ang)
    s_ref[...] = jnp.where(lane < ROPE_PAIR, -sn, sn)


def _rope_tables(pos_flat, half):
    inv = ROPE_THETA ** (-jnp.arange(half, dtype=F32) / half)
    gap = jnp.zeros((ROPE_PAIR - half,), F32)
    inv = jnp.concatenate([inv, gap, inv, gap]).reshape(1, LANES)
    n = pos_flat.shape[0]
    rows = 2048
    tab = jax.ShapeDtypeStruct((n, LANES), F32)
    row_spec = pl.BlockSpec((rows, LANES), lambda i: (i, 0))
    return pl.pallas_call(
        _rope_tab_kernel,
        out_shape=(tab,) * 2,
        grid=(n // rows,),
        in_specs=[pl.BlockSpec((rows, 1), lambda i: (i, 0)), pl.BlockSpec((1, LANES), lambda i: (0, 0))],
        out_specs=(row_spec,) * 2,
        compiler_params=_cparams("parallel"),
        name="rope_tables",
    )(pos_flat.reshape(n, 1), inv)


FFN_SLOTS = 3


def _ffn_kernel(x_ref, g_ref, w13_hbm, w2_hbm, o_ref, xn_ref, acc_ref, w1_buf, w3_buf, w2_buf, sems,
                *, final, nj, tf):
    i = pl.program_id(0)

    def copies(j):
        slot = j % FFN_SLOTS
        return (pltpu.make_async_copy(w13_hbm.at[:, pl.ds(j * tf, tf)], w1_buf.at[slot], sems.at[0, slot]),
                pltpu.make_async_copy(w13_hbm.at[:, pl.ds((j + nj) * tf, tf)], w3_buf.at[slot], sems.at[1, slot]),
                pltpu.make_async_copy(w2_hbm.at[pl.ds(j * tf, tf), :], w2_buf.at[slot], sems.at[2, slot]))

    def start(j):
        for c in copies(j):
            c.start()

    @pl.when(i == 0)
    def _():
        start(0)

    xn_ref[...] = _rms(x_ref[...], g_ref[0:1, :]).astype(BF16)
    xn = xn_ref[...]
    for j in range(nj):
        slot = j % FFN_SLOTS
        if j + 1 < nj:
            start(j + 1)
        else:
            pl.when(i < pl.num_programs(0) - 1)(functools.partial(start, 0))
        for c in copies(j):
            c.wait()
        gate = _dot(xn, w1_buf[slot])
        up = _dot(xn, w3_buf[slot])
        act = (gate * (1.0 / (1.0 + jnp.exp(-gate))) * up).astype(BF16)
        part = _dot(act, w2_buf[slot])
        acc_ref[...] = part if j == 0 else acc_ref[...] + part
    y = x_ref[...] + HALF_STEP * acc_ref[...]
    if final:
        y = _rms(y, g_ref[1:2, :])
    o_ref[...] = y


def _ffn(h, gain, w13, w2, final_gain=None, next_f32=None):
    tm, tf = 512, 512
    nj = D_FF // tf
    assert nj % FFN_SLOTS != 1, "tile 0 of the next row tile must not land in the last tile's slot"
    final = final_gain is not None
    gains = jnp.stack([gain, final_gain if final else gain])
    out = pl.pallas_call(
        functools.partial(_ffn_kernel, final=final, nj=nj, tf=tf),
        out_shape=jax.ShapeDtypeStruct((TOKENS, D_MODEL), F32),
        grid=(TOKENS // tm,),
        in_specs=[
            pl.BlockSpec((tm, D_MODEL), lambda i: (i, 0)),
            pl.BlockSpec((2, D_MODEL), lambda i: (0, 0)),
            pl.BlockSpec(memory_space=pl.ANY),
            pl.BlockSpec(memory_space=pl.ANY),
        ],
        out_specs=pl.BlockSpec((tm, D_MODEL), lambda i: (i, 0)),
        scratch_shapes=[pltpu.VMEM((tm, D_MODEL), BF16), pltpu.VMEM((tm, D_MODEL), F32),
                        pltpu.VMEM((FFN_SLOTS, D_MODEL, tf), BF16), pltpu.VMEM((FFN_SLOTS, D_MODEL, tf), BF16),
                        pltpu.VMEM((FFN_SLOTS, tf, D_MODEL), BF16), pltpu.SemaphoreType.DMA((3, FFN_SLOTS))],
        compiler_params=_cparams("arbitrary"),
        name="ffn",
    )(h, gains, w13, w2)
    if next_f32 is not None:
        return out, next_f32[0].astype(BF16), next_f32[1].astype(BF16)
    return out


def _out_proj_kernel(x_ref, w_ref, r_ref, o_ref):
    o_ref[...] = r_ref[...] + _dot(x_ref[...], w_ref[...])


def _out_proj(x, w, res):
    k = x.shape[1]
    tm = 512
    return pl.pallas_call(
        _out_proj_kernel,
        out_shape=jax.ShapeDtypeStruct((TOKENS, D_MODEL), F32),
        grid=(TOKENS // tm,),
        in_specs=[
            pl.BlockSpec((tm, k), lambda i: (i, 0)),
            pl.BlockSpec((k, D_MODEL), lambda i: (0, 0)),
            pl.BlockSpec((tm, D_MODEL), lambda i: (i, 0)),
        ],
        out_specs=pl.BlockSpec((tm, D_MODEL), lambda i: (i, 0)),
        compiler_params=_cparams("parallel"),
        name="out_proj",
    )(x, w, res)


def _norm_perm_a_kernel(x_ref, g_ref, *refs, tm):
    o_refs, xs_ref, mid_ref = refs[:-2], refs[-2], refs[-1]
    xn = _rms(x_ref[...], g_ref[...])
    nchunk = D_MODEL // LANES
    for k in range(nchunk):
        xs_ref[k] = xn[:, k * LANES:(k + 1) * LANES]
    dp = 1
    for (_, d), o_ref in zip(A_PATTERNS, o_refs):
        if d == 1:
            o_ref[...] = xn.astype(BF16)
            continue
        step = d // dp
        for c in range(d):
            for k in range(nchunk):
                rows = pl.ds(c // dp, tm // d, stride=step)
                piece = xs_ref[k, rows, :] if dp == 1 else mid_ref[k, c % dp, rows, :]
                o_ref[0, c, 0, :, k * LANES:(k + 1) * LANES] = piece.astype(BF16)
                if d == A_MID_DILATION:
                    mid_ref[k, c] = piece
        dp = d


def _norm_perm_a(h, gain):
    tm = 512
    per_tile = A_TILE // tm
    shapes, specs = [], []
    for _, d in A_PATTERNS:
        if d == 1:
            shapes.append(jax.ShapeDtypeStruct((TOKENS, D_MODEL), BF16))
            specs.append(pl.BlockSpec((tm, D_MODEL), lambda i: (i, 0)))
        else:
            shapes.append(jax.ShapeDtypeStruct((TOKENS // A_TILE, d, per_tile, tm // d, D_MODEL), BF16))
            specs.append(pl.BlockSpec((1, d, 1, tm // d, D_MODEL),
                                      lambda i: (i // per_tile, 0, i % per_tile, 0, 0)))
    outs = pl.pallas_call(
        functools.partial(_norm_perm_a_kernel, tm=tm),
        out_shape=tuple(shapes),
        grid=(TOKENS // tm,),
        in_specs=[pl.BlockSpec((tm, D_MODEL), lambda i: (i, 0)), pl.BlockSpec((1, D_MODEL), lambda i: (0, 0))],
        out_specs=tuple(specs),
        scratch_shapes=[pltpu.VMEM((D_MODEL // LANES, tm, LANES), F32),
                        pltpu.VMEM((D_MODEL // LANES, A_MID_DILATION, tm // A_MID_DILATION, LANES), F32)],
        compiler_params=_cparams("parallel"),
        name="norm_perm_a",
    )(h, gain.reshape(1, D_MODEL))
    return [o.reshape(TOKENS, D_MODEL) for o in outs]


def _qkv_w_kernel(w_ref, o_ref):
    part = pl.program_id(1) % 3

    @pl.when(part == 2)
    def _():
        o_ref[...] = w_ref[...].astype(BF16)

    @pl.when(part != 2)
    def _():
        lane = lax.broadcasted_iota(jnp.int32, (1, LANES), 1)
        stay = (lane < A_ROT_HALF) | (lane >= ROPE_PAIR + A_ROT_HALF)
        for hh in range(A_HEADS):
            sl = slice(hh * A_HEAD_DIM, (hh + 1) * A_HEAD_DIM)
            y = w_ref[:, sl]
            moved = jnp.where(lane < ROPE_PAIR, pltpu.roll(y, LANES - A_ROT_HALF, 1),
                              pltpu.roll(y, ROPE_PAIR - A_ROT_HALF, 1))
            o_ref[:, sl] = jnp.where(stay, y, moved).astype(BF16)


def _qkv_w(w_qkv):
    rows = 512
    hd = A_HEADS * A_HEAD_DIM
    spec = pl.BlockSpec((rows, hd), lambda i, j: (i, j))
    return pl.pallas_call(
        _qkv_w_kernel,
        out_shape=jax.ShapeDtypeStruct(w_qkv.shape, BF16),
        grid=(D_MODEL // rows, w_qkv.shape[1] // hd),
        in_specs=[spec],
        out_specs=spec,
        compiler_params=_cparams("parallel", "parallel"),
        name="qkv_w",
    )(w_qkv)


def _qkv_a_kernel(x_ref, w_ref, cos_ref, sin_ref, o_ref, *, tn):
    part = pl.program_id(1) // ((A_HEADS * A_HEAD_DIM) // tn)
    scale = jnp.where(part == 0, A_HEAD_DIM ** -0.5, 1.0).astype(F32)
    half = x_ref.shape[0] // 2
    for cc in range(tn // MXU_COLS):
        for rows in (slice(0, half), slice(half, 2 * half)):
            y = _dot(x_ref[rows, :], w_ref[:, cc * MXU_COLS:(cc + 1) * MXU_COLS])
            for hh in range(MXU_COLS // LANES):
                sl = slice(cc * MXU_COLS + hh * LANES, cc * MXU_COLS + (hh + 1) * LANES)
                y_h = y[:, hh * LANES:(hh + 1) * LANES]
                o_ref[rows, sl] = (_rope128(y_h, cos_ref[rows, :], sin_ref[rows, :]) * scale).astype(BF16)


def _qkv_a(xn, w_qkv, tabs, group):
    tm, tn = 1024, 2048
    cols = 3 * A_HEADS * A_HEAD_DIM
    nj = cols // tn
    ni = TOKENS // tm
    v_first = 2 * (A_HEADS * A_HEAD_DIM) // tn
    tab_spec = pl.BlockSpec((tm, LANES),
                            lambda i, j: (jnp.where(j >= v_first, len(A_PATTERNS) * ni, group * ni + i), 0))
    return pl.pallas_call(
        functools.partial(_qkv_a_kernel, tn=tn),
        out_shape=jax.ShapeDtypeStruct((TOKENS, cols), BF16),
        grid=(ni, nj),
        in_specs=[
            pl.BlockSpec((tm, D_MODEL), lambda i, j: (i, 0)),
            pl.BlockSpec((D_MODEL, tn), lambda i, j: (0, group * nj + j)),
            tab_spec, tab_spec,
        ],
        out_specs=pl.BlockSpec((tm, tn), lambda i, j: (i, j)),
        compiler_params=_cparams("parallel", "parallel"),
        name="qkv_a",
    )(xn, w_qkv, *tabs)


def _attn_a_kernel(q_ref, kc_ref, kp_ref, vc_ref, vp_ref, o_ref, lse_ref, *scratch, d, heads, per_trip):
    rows = A_TILE // d
    nblk = rows // A_BLOCK
    two_stage = d > A_MID_DILATION
    if two_stage:
        os_ref, mid_ref, lse_mid_ref = scratch
        stage_ref, lse_stage_ref, mid_step = mid_ref, lse_mid_ref, d // A_MID_DILATION
    else:
        (os_ref,) = scratch
        stage_ref, lse_stage_ref = os_ref, lse_ref
    first_tile = pl.program_id(1) == 0
    r = lax.broadcasted_iota(jnp.int32, (A_BLOCK, 2 * A_BLOCK), 0)
    kk = lax.broadcasted_iota(jnp.int32, (A_BLOCK, 2 * A_BLOCK), 1)
    band = (kk >= r) & (kk <= r + A_BLOCK)
    lane = lax.broadcasted_iota(jnp.int32, (A_BLOCK, LANES), 1)
    hs = [slice(h * A_HEAD_DIM, (h + 1) * A_HEAD_DIM) for h in range(heads)]
    first_head = pl.program_id(2) * heads

    @pl.when(first_head == 0)
    def _():
        lse_stage_ref[...] = jnp.zeros_like(lse_stage_ref)

    def trip(t, carry):
        loaded = []
        for i in range(per_trip):
            u = t * per_trip + i
            c = u // nblk
            blk = u % nblk
            start = pl.multiple_of(blk * A_BLOCK, A_BLOCK)
            before = pl.multiple_of(jnp.maximum(start - A_BLOCK, 0), A_BLOCK)
            q = q_ref[0, c, pl.ds(start, A_BLOCK), :]
            k = jnp.concatenate([jnp.where(blk == 0, kp_ref[0, c], kc_ref[0, c, pl.ds(before, A_BLOCK), :]),
                                 kc_ref[0, c, pl.ds(start, A_BLOCK), :]], axis=0)
            v = jnp.concatenate([jnp.where(blk == 0, vp_ref[0, c], vc_ref[0, c, pl.ds(before, A_BLOCK), :]),
                                 vc_ref[0, c, pl.ds(start, A_BLOCK), :]], axis=0)
            no_prev = jnp.logical_and(first_tile, blk == 0)
            mask = jnp.logical_and(band, jnp.logical_or(kk >= A_BLOCK, jnp.logical_not(no_prev)))
            out_rows = (pl.ds(blk * (A_BLOCK * d) + c, A_BLOCK, stride=d) if d > 1
                        else pl.ds(start, A_BLOCK))
            if two_stage:
                o_at = (c % A_MID_DILATION,
                        pl.ds(blk * (A_BLOCK * mid_step) + c // A_MID_DILATION, A_BLOCK, stride=mid_step))
            else:
                o_at = (out_rows,)
            loaded.append(([_dot_t(q[:, sl], k[:, sl]) for sl in hs], v, mask, out_rows, o_at))
        for scores, v, mask, out_rows, o_at in loaded:
            lse_tile = lse_stage_ref[o_at + (slice(None),)]
            for h, sl in enumerate(hs):
                s = jnp.where(mask, scores[h], NEG_INF)
                m = jnp.max(s, axis=-1, keepdims=True)
                p = jnp.exp(s - m)
                l = jnp.sum(p, axis=-1, keepdims=True)
                stage_ref[(h,) + o_at + (slice(None),)] = _dot(p.astype(BF16), v[:, sl]) / l
                lse_tile = jnp.where(lane == first_head + h, m + jnp.log(l), lse_tile)
            lse_stage_ref[o_at + (slice(None),)] = lse_tile
        return carry

    lax.fori_loop(0, d * nblk // per_trip, trip, 0)
    if two_stage:
        for h in range(heads):
            for c4 in range(A_MID_DILATION):
                os_ref[h, pl.ds(c4, A_TILE // A_MID_DILATION, stride=A_MID_DILATION), :] = mid_ref[h, c4]
        for c4 in range(A_MID_DILATION):
            lse_ref[pl.ds(c4, A_TILE // A_MID_DILATION, stride=A_MID_DILATION), :] = lse_mid_ref[c4]
    o_ref[...] = os_ref[...].astype(BF16)


def _attn_a(qkv, window, dilation):
    assert window // dilation == A_BLOCK, "band width must equal the query block"
    d = dilation
    heads = 4
    hw = heads * A_HEAD_DIM
    ngrp = A_HEADS // heads
    rows = A_TILE // d
    tiles = SEQ // A_TILE
    qkv_v = qkv.reshape(BATCH, tiles, d, rows, 3 * A_HEADS * A_HEAD_DIM)
    last_blk = rows // A_BLOCK - 1

    def cur(part):
        return pl.BlockSpec((None, 1, d, rows, hw), lambda b, n, g: (b, n, 0, 0, part * ngrp + g))

    def prev(part):
        return pl.BlockSpec((None, 1, d, A_BLOCK, hw),
                            lambda b, n, g: (b, jnp.maximum(n - 1, 0), 0, last_blk, part * ngrp + g))

    return pl.pallas_call(
        functools.partial(_attn_a_kernel, d=d, heads=heads, per_trip=4),
        out_shape=(jax.ShapeDtypeStruct((A_HEADS, TOKENS, A_HEAD_DIM), BF16),
                   jax.ShapeDtypeStruct((TOKENS, LANES), F32)),
        grid=(BATCH, tiles, ngrp),
        in_specs=[cur(0), cur(1), prev(1), cur(2), prev(2)],
        out_specs=(pl.BlockSpec((heads, A_TILE, A_HEAD_DIM), lambda b, n, g: (g, b * tiles + n, 0)),
                   pl.BlockSpec((A_TILE, LANES), lambda b, n, g: (b * tiles + n, 0))),
        scratch_shapes=[pltpu.VMEM((heads, A_TILE, A_HEAD_DIM), F32)] + (
            [pltpu.VMEM((heads, A_MID_DILATION, A_TILE // A_MID_DILATION, A_HEAD_DIM), F32),
             pltpu.VMEM((A_MID_DILATION, A_TILE // A_MID_DILATION, LANES), F32)]
            if d > A_MID_DILATION else []),
        compiler_params=_cparams("parallel", "parallel", "arbitrary"),
        name="attn_a_d%d" % d,
    )(qkv_v, qkv_v, qkv_v, qkv_v, qkv_v)


def _comb_proj_a_kernel(o0, o1, o2, l0, l1, l2, w_ref, r_ref, out_ref, comb_ref):
    s = pl.program_id(0)
    slot = s % 2

    @pl.when(s == 0)
    def _():
        comb_ref[1] = jnp.zeros(comb_ref.shape[1:], BF16)

    out_ref[...] = r_ref[...] + _dot(comb_ref[1 - slot], w_ref[...])
    a = [l0[...], l1[...], l2[...]]
    m = jnp.maximum(jnp.maximum(a[0], a[1]), a[2])
    e = [jnp.exp(t - m) for t in a]
    den = e[0] + e[1] + e[2]
    w = [t / den for t in e]
    rows = out_ref.shape[0]
    for h in range(A_HEADS):
        acc = jnp.zeros((rows, A_HEAD_DIM), F32)
        for wg, og in zip(w, (o0, o1, o2)):
            acc = acc + jnp.broadcast_to(wg[:, h:h + 1], (rows, A_HEAD_DIM)) * og[h]
        comb_ref[slot, :, h * A_HEAD_DIM:(h + 1) * A_HEAD_DIM] = acc.astype(BF16)


def _comb_proj_a(outs, lses, w_o, res):
    tm = 256
    nt = TOKENS // tm
    hd = A_HEADS * A_HEAD_DIM
    o_spec = pl.BlockSpec((A_HEADS, tm, A_HEAD_DIM), lambda s: (0, jnp.minimum(s, nt - 1), 0))
    l_spec = pl.BlockSpec((tm, LANES), lambda s: (jnp.minimum(s, nt - 1), 0))
    row_spec = pl.BlockSpec((tm, D_MODEL), lambda s: (jnp.maximum(s - 1, 0), 0))
    return pl.pallas_call(
        _comb_proj_a_kernel,
        out_shape=jax.ShapeDtypeStruct((TOKENS, D_MODEL), F32),
        grid=(nt + 1,),
        in_specs=[o_spec] * 3 + [l_spec] * 3 + [pl.BlockSpec((hd, D_MODEL), lambda s: (0, 0)), row_spec],
        out_specs=row_spec,
        scratch_shapes=[pltpu.VMEM((2, tm, hd), BF16)],
        compiler_params=_cparams("arbitrary"),
        name="comb_proj_a",
    )(*outs, *lses, w_o, res)


def _mixer_a(h, gain, w_qkv, w_o, tabs_a):
    outs, lses = [], []
    for g, ((window, dilation), xn) in enumerate(zip(A_PATTERNS, _norm_perm_a(h, gain))):
        o, lse = _attn_a(_qkv_a(xn, w_qkv, tabs_a, g), window, dilation)
        outs.append(o)
        lses.append(lse)
    return _comb_proj_a(outs, lses, w_o, h)


def _b_in_kernel(x_ref, g_ref, w_ref, o_ref):
    o_ref[...] = _dot(_rms(x_ref[...], g_ref[...]).astype(BF16), w_ref[...])


def _b_in(h, gain, w_in):
    tm = 512
    return pl.pallas_call(
        _b_in_kernel,
        out_shape=jax.ShapeDtypeStruct((TOKENS, B_IN_COLS), F32),
        grid=(TOKENS // tm,),
        in_specs=[
            pl.BlockSpec((tm, D_MODEL), lambda i: (i, 0)),
            pl.BlockSpec((1, D_MODEL), lambda i: (0, 0)),
            pl.BlockSpec((D_MODEL, B_IN_COLS), lambda i: (0, 0)),
        ],
        out_specs=pl.BlockSpec((tm, B_IN_COLS), lambda i: (i, 0)),
        compiler_params=_cparams("parallel"),
        name="b_in",
    )(h, gain.reshape(1, D_MODEL), w_in)


def _b_q_kernel(c_ref, g_ref, w_ref, cos_ref, sin_ref, o_ref, xn_ref, *, slots):
    @pl.when(pl.program_id(1) == 0)
    def _():
        xn_ref[...] = _rms(c_ref[...], g_ref[...]).astype(BF16)

    y = _dot(xn_ref[...], w_ref[...])
    scale = (B_NOPE_DIM + B_ROPE_DIM) ** -0.5
    cos, sin = cos_ref[...], sin_ref[...]
    for s in range(slots):
        nope = y[:, s * B_SLOT:s * B_SLOT + LANES]
        rot = _rope128(y[:, s * B_SLOT + LANES:(s + 1) * B_SLOT], cos, sin)
        o_ref[s, :, :LANES] = (nope * scale).astype(BF16)
        o_ref[s, :, LANES:] = (rot * scale).astype(BF16)


def _b_q(c, gain, w_q, tabs_b):
    tm, slots = 1024, 8
    tn = slots * B_SLOT
    tab_spec = pl.BlockSpec((tm, LANES), lambda i, j: (i, 0))
    return pl.pallas_call(
        functools.partial(_b_q_kernel, slots=slots),
        out_shape=jax.ShapeDtypeStruct((B_HEADS, TOKENS, B_SLOT), BF16),
        grid=(TOKENS // tm, B_HEADS // slots),
        in_specs=[
            pl.BlockSpec((tm, B_Q_RANK), lambda i, j: (i, 0)),
            pl.BlockSpec((1, B_Q_RANK), lambda i, j: (0, 0)),
            pl.BlockSpec((B_Q_RANK, tn), lambda i, j: (0, j)),
            tab_spec, tab_spec,
        ],
        out_specs=pl.BlockSpec((slots, tm, B_SLOT), lambda i, j: (j, i, 0)),
        scratch_shapes=[pltpu.VMEM((tm, B_Q_RANK), BF16)],
        compiler_params=_cparams("parallel", "arbitrary"),
        name="b_q",
    )(c, gain.reshape(1, B_Q_RANK), w_q, *tabs_b)


def _b_kv_kernel(c_ref, g_ref, wk_ref, wv_ref, cr_ref, cos_ref, sin_ref, k_ref, v_ref, *, slots):
    xn = _rms(c_ref[...], g_ref[...]).astype(BF16)
    yk = _dot(xn, wk_ref[...])
    yv = _dot(xn, wv_ref[...])
    kr = _rope128(cr_ref[...], cos_ref[...], sin_ref[...]).astype(BF16)
    lane = lax.broadcasted_iota(jnp.int32, (c_ref.shape[0], B_VX - B_V_DIM), 1)
    ones_col = jnp.where(lane == 0, 1.0, 0.0).astype(BF16)
    for s in range(slots):
        sl = slice(s * LANES, (s + 1) * LANES)
        k_ref[s, :, :LANES] = yk[:, sl].astype(BF16)
        k_ref[s, :, LANES:] = kr
        v_ref[s, :, :B_V_DIM] = yv[:, sl].astype(BF16)
        v_ref[s, :, B_V_DIM:] = ones_col


def _b_kv(c, gain, w_k, w_v, tabs_b):
    tm, slots = 1024, 8
    tn = slots * LANES
    kv_block = B_Q_RANK // B_KV_RANK
    rope_block = (B_Q_RANK + B_KV_RANK) // LANES
    tab_spec = pl.BlockSpec((tm, LANES), lambda i, j: (i, 0))
    return pl.pallas_call(
        functools.partial(_b_kv_kernel, slots=slots),
        out_shape=(jax.ShapeDtypeStruct((B_HEADS, TOKENS, B_SLOT), BF16),
                   jax.ShapeDtypeStruct((B_HEADS, TOKENS, B_VX), BF16)),
        grid=(TOKENS // tm, B_HEADS // slots),
        in_specs=[
            pl.BlockSpec((tm, B_KV_RANK), lambda i, j: (i, kv_block)),
            pl.BlockSpec((1, B_KV_RANK), lambda i, j: (0, 0)),
            pl.BlockSpec((B_KV_RANK, tn), lambda i, j: (0, j)),
            pl.BlockSpec((B_KV_RANK, tn), lambda i, j: (0, j)),
            pl.BlockSpec((tm, LANES), lambda i, j: (i, rope_block)),
            tab_spec, tab_spec,
        ],
        out_specs=(pl.BlockSpec((slots, tm, B_SLOT), lambda i, j: (j, i, 0)),
                   pl.BlockSpec((slots, tm, B_VX), lambda i, j: (j, i, 0))),
        compiler_params=_cparams("parallel", "parallel"),
        name="b_kv",
    )(c, gain.reshape(1, B_KV_RANK), w_k, w_v, c, *tabs_b)


def _attn_b_kernel(q_ref, k_ref, v_ref, o_ref, s0_ref, s1_ref, m_ref, acc_ref, *, tq, heads):
    qi = pl.program_id(2)
    m_ref[...] = jnp.full_like(m_ref, NEG_INF)
    acc_ref[...] = jnp.zeros_like(acc_ref)

    def scores(s_ref, ki):
        for h in range(heads):
            k = k_ref[h, pl.ds(pl.multiple_of(ki * tq, tq), tq), :]
            s_ref[h] = _dot_t(q_ref[h], k)

    def fold(s_ref, ki, diagonal):
        for h in range(heads):
            s = s_ref[h]
            if diagonal:
                row = lax.broadcasted_iota(jnp.int32, (tq, tq), 0)
                col = lax.broadcasted_iota(jnp.int32, (tq, tq), 1)
                s = jnp.where(row >= col, s, NEG_INF)
            m_prev = m_ref[h]
            m_new = jnp.maximum(m_prev, jnp.max(s, axis=-1, keepdims=True))
            alpha = jnp.exp(m_prev - m_new)
            p = jnp.exp((s - jnp.concatenate([m_new] * (tq // LANES), axis=1)).astype(BF16))
            v = v_ref[h, pl.ds(pl.multiple_of(ki * tq, tq), tq), :]
            acc_ref[h] = jnp.concatenate([alpha] * (B_VX // LANES), axis=1) * acc_ref[h] + _dot(p, v)
            m_ref[h] = m_new

    pairs = qi // 2
    scores(s0_ref, 0)

    def body(t, carry):
        scores(s1_ref, 2 * t + 1)
        fold(s0_ref, 2 * t, False)
        scores(s0_ref, 2 * t + 2)
        fold(s1_ref, 2 * t + 1, False)
        return carry

    lax.fori_loop(0, pairs, body, 0)

    @pl.when(qi % 2 == 1)
    def _():
        scores(s1_ref, qi)
        fold(s0_ref, qi - 1, False)
        fold(s1_ref, qi, True)

    @pl.when(qi % 2 == 0)
    def _():
        fold(s0_ref, qi, True)

    for h in range(heads):
        acc = acc_ref[h]
        l = jnp.broadcast_to(acc[:, B_V_DIM:B_V_DIM + 1], (tq, B_V_DIM))
        o_ref[:, h * B_V_DIM:(h + 1) * B_V_DIM] = (acc[:, :B_V_DIM] / l).astype(BF16)


def _attn_b(q3, k3, v3):
    tq, heads = 512, 2
    nq = SEQ // tq
    return pl.pallas_call(
        functools.partial(_attn_b_kernel, tq=tq, heads=heads),
        out_shape=jax.ShapeDtypeStruct((TOKENS, B_HEADS * B_V_DIM), BF16),
        grid=(BATCH, B_HEADS // heads, nq),
        in_specs=[
            pl.BlockSpec((heads, tq, B_SLOT), lambda b, g, qi: (g, b * nq + qi, 0)),
            pl.BlockSpec((heads, SEQ, B_SLOT), lambda b, g, qi: (g, b, 0)),
            pl.BlockSpec((heads, SEQ, B_VX), lambda b, g, qi: (g, b, 0)),
        ],
        out_specs=pl.BlockSpec((tq, heads * B_V_DIM), lambda b, g, qi: (b * nq + qi, g)),
        scratch_shapes=[pltpu.VMEM((heads, tq, tq), F32), pltpu.VMEM((heads, tq, tq), F32),
                        pltpu.VMEM((heads, tq, LANES), F32), pltpu.VMEM((heads, tq, B_VX), F32)],
        compiler_params=_cparams("parallel", "parallel", "arbitrary"),
        name="attn_b",
    )(q3, k3, v3)


def _mixer_b(h, gain, w_in, q_norm, w_q, kv_norm, w_k, w_v, w_o, tabs_b):
    c = _b_in(h, gain, w_in)
    q3 = _b_q(c, q_norm, w_q, tabs_b)
    k3, v3 = _b_kv(c, kv_norm, w_k, w_v, tabs_b)
    return _out_proj(_attn_b(q3, k3, v3), w_o, h)


def _c_in_kernel(x_ref, g_ref, wb_ref, wc_ref, wu_ref, cw_ref, o_ref, xn_ref, carry_ref, *, tm):
    i = pl.program_id(0)
    j = pl.program_id(1)

    @pl.when(j == 0)
    def _():
        xn_ref[...] = _rms(x_ref[...], g_ref[...]).astype(BF16)

    xn = xn_ref[...]
    first = (i * tm) % SEQ == 0
    for cc in range(o_ref.shape[1] // MXU_COLS):
        cs = slice(cc * MXU_COLS, (cc + 1) * MXU_COLS)
        b_gate = _dot(xn, wb_ref[:, cs])
        cu = _dot(xn, wc_ref[:, cs]) * _dot(xn, wu_ref[:, cs])
        w0, w1, w2 = cw_ref[0:1, cs], cw_ref[1:2, cs], cw_ref[2:3, cs]
        z = w2 * cu + w1 * pltpu.roll(cu, 1, 0) + w0 * pltpu.roll(cu, 2, 0)
        o_ref[:, cs] = (b_gate * z).astype(BF16)

        tail = jnp.where(first, 0.0, carry_ref[j, :, cs])
        both = jnp.concatenate([tail, cu[0:8, :]], axis=0)
        z8 = w2 * both[8:16] + w1 * both[7:15] + w0 * both[6:14]
        o_ref[0:8, cs] = (b_gate[0:8, :] * z8).astype(BF16)
        carry_ref[j, :, cs] = cu[tm - 8:tm, :]


def _c_in(h, gain, w_in, conv_w):
    tm, tn = 1024, 512
    nj = D_MODEL // tn
    return pl.pallas_call(
        functools.partial(_c_in_kernel, tm=tm),
        out_shape=jax.ShapeDtypeStruct((TOKENS, D_MODEL), BF16),
        grid=(TOKENS // tm, nj),
        in_specs=[
            pl.BlockSpec((tm, D_MODEL), lambda i, j: (i, 0)),
            pl.BlockSpec((1, D_MODEL), lambda i, j: (0, 0)),
            pl.BlockSpec((D_MODEL, tn), lambda i, j: (0, j)),
            pl.BlockSpec((D_MODEL, tn), lambda i, j: (0, j + nj)),
            pl.BlockSpec((D_MODEL, tn), lambda i, j: (0, j + 2 * nj)),
            pl.BlockSpec((C_CONV, tn), lambda i, j: (0, j)),
        ],
        out_specs=pl.BlockSpec((tm, tn), lambda i, j: (i, j)),
        scratch_shapes=[pltpu.VMEM((tm, D_MODEL), BF16), pltpu.VMEM((nj, 8, tn), F32)],
        compiler_params=_cparams("arbitrary", "arbitrary"),
        name="c_in",
    )(h, gain.reshape(1, D_MODEL), w_in, w_in, w_in, conv_w)


def _mixer_c(h, gain, w_in, conv_w, w_out):
    return _out_proj(_c_in(h, gain, w_in, conv_w), w_out, h)


def kernel(x, positions, l0_ffn1_norm, l0_ffn1_w13, l0_ffn1_w2, l0_mix_norm, l0_a_w_qkv, l0_a_w_o, l0_ffn2_norm, l0_ffn2_w13, l0_ffn2_w2, l1_ffn1_norm, l1_ffn1_w13, l1_ffn1_w2, l1_mix_norm, l1_b_w_in, l1_b_q_norm, l1_b_w_uq, l1_b_kv_norm, l1_b_w_ukv, l1_b_w_o, l1_ffn2_norm, l1_ffn2_w13, l1_ffn2_w2, l2_ffn1_norm, l2_ffn1_w13, l2_ffn1_w2, l2_mix_norm, l2_c_w_in, l2_c_conv_w, l2_c_w_out, l2_ffn2_norm, l2_ffn2_w13, l2_ffn2_w2, l3_ffn1_norm, l3_ffn1_w13, l3_ffn1_w2, l3_mix_norm, l3_a_w_qkv, l3_a_w_o, l3_ffn2_norm, l3_ffn2_w13, l3_ffn2_w2, final_norm):
    bf = lambda w: w.astype(BF16)
    pos_tiles = positions.reshape(TOKENS // A_TILE, A_TILE)
    pos_a = [pos_tiles.reshape(-1, A_TILE // d, d).swapaxes(1, 2).reshape(TOKENS) for _, d in A_PATTERNS]
    tabs_a = _rope_tables(jnp.concatenate(pos_a + [jnp.zeros((A_TILE,), positions.dtype)]), A_ROT_HALF)
    tabs_b = _rope_tables(positions.reshape(TOKENS), B_ROT_HALF)

    def rope_group(w):
        return _rope_lanes(jnp.pad(w, [(0, 0)] * (w.ndim - 1) + [(0, LANES - B_ROPE_DIM)]), B_ROT_HALF)

    w_uq = l1_b_w_uq.reshape(B_Q_RANK, B_HEADS, B_NOPE_DIM + B_ROPE_DIM)
    w_q = jnp.concatenate([w_uq[..., :B_NOPE_DIM], rope_group(w_uq[..., B_NOPE_DIM:])], axis=-1)
    w_q = bf(w_q.reshape(B_Q_RANK, B_HEADS * B_SLOT))
    w_ukv = l1_b_w_ukv.reshape(B_KV_RANK, B_HEADS, B_NOPE_DIM + B_V_DIM)
    w_k = bf(w_ukv[:, :, :B_NOPE_DIM].reshape(B_KV_RANK, B_HEADS * B_NOPE_DIM))
    w_v = bf(w_ukv[:, :, B_NOPE_DIM:].reshape(B_KV_RANK, B_HEADS * B_V_DIM))
    n_c = B_Q_RANK + B_KV_RANK
    w_b_in = bf(jnp.concatenate([l1_b_w_in[:, :n_c], rope_group(l1_b_w_in[:, n_c:])], axis=-1))

    h = x.reshape(TOKENS, D_MODEL)

    h, w13, w2 = _ffn(h, l0_ffn1_norm, bf(l0_ffn1_w13), bf(l0_ffn1_w2), next_f32=(l0_ffn2_w13, l0_ffn2_w2))
    h = _mixer_a(h, l0_mix_norm, _qkv_w(l0_a_w_qkv), bf(l0_a_w_o), tabs_a)
    h, w13, w2 = _ffn(h, l0_ffn2_norm, w13, w2, next_f32=(l1_ffn1_w13, l1_ffn1_w2))

    h, w13, w2 = _ffn(h, l1_ffn1_norm, w13, w2, next_f32=(l1_ffn2_w13, l1_ffn2_w2))
    h = _mixer_b(h, l1_mix_norm, w_b_in, l1_b_q_norm, w_q, l1_b_kv_norm, w_k, w_v, bf(l1_b_w_o), tabs_b)
    h, w13, w2 = _ffn(h, l1_ffn2_norm, w13, w2, next_f32=(l2_ffn1_w13, l2_ffn1_w2))

    h, w13, w2 = _ffn(h, l2_ffn1_norm, w13, w2, next_f32=(l2_ffn2_w13, l2_ffn2_w2))
    h = _mixer_c(h, l2_mix_norm, bf(l2_c_w_in), l2_c_conv_w, bf(l2_c_w_out))
    h, w13, w2 = _ffn(h, l2_ffn2_norm, w13, w2, next_f32=(l3_ffn1_w13, l3_ffn1_w2))

    h, w13, w2 = _ffn(h, l3_ffn1_norm, w13, w2, next_f32=(l3_ffn2_w13, l3_ffn2_w2))
    h = _mixer_a(h, l3_mix_norm, _qkv_w(l3_a_w_qkv), bf(l3_a_w_o), tabs_a)
    h = _ffn(h, l3_ffn2_norm, w13, w2, final_gain=final_norm)

    return h.reshape(BATCH, SEQ, D_MODEL)
```

```python
import functools

import jax
import jax.numpy as jnp
from jax import lax
from jax.experimental import pallas as pl
from jax.experimental.pallas import tpu as pltpu

F32 = jnp.float32
BF16 = jnp.bfloat16

D_MODEL = 2048
BATCH = 2
SEQ = 8192
TOKENS = BATCH * SEQ
D_FF = 5632
HALF_STEP = 0.5
NORM_EPS = 1e-6
NEG_INF = -1e30
ROPE_THETA = 500000.0

A_HEADS = 16
A_HEAD_DIM = 128
A_ROT_HALF = A_HEAD_DIM // 8
A_PATTERNS = ((128, 1), (512, 4), (2048, 16))
A_BLOCK = 128
A_TILE = 2048
A_MID_DILATION = 4
assert [d for _, d in A_PATTERNS] == [1, A_MID_DILATION, A_MID_DILATION ** 2]

B_HEADS = 16
B_Q_RANK = 1536
B_KV_RANK = 512
B_NOPE_DIM = 128
B_ROPE_DIM = 64
B_ROT_HALF = B_ROPE_DIM // 2
B_V_DIM = 128
B_SLOT = 256
B_VX = 256
B_IN_COLS = 2176

C_CONV = 3

LANES = 128
MXU_COLS = 256
ROPE_PAIR = LANES // 2
VMEM_LIMIT = 56 * 1024 * 1024


def _cparams(*sem):
    return pltpu.CompilerParams(dimension_semantics=sem, vmem_limit_bytes=VMEM_LIMIT)


def _rms(x, gain):
    ms = jnp.mean(x * x, axis=-1, keepdims=True)
    return x * lax.rsqrt(ms + NORM_EPS) * gain


def _dot(a, b):
    return jnp.dot(a, b, preferred_element_type=F32)


def _dot_t(a, b):
    return lax.dot_general(a, b, (((1,), (1,)), ((), ())), preferred_element_type=F32)


def _rope128(y, cos, sin):
    return y * cos + pltpu.roll(y, ROPE_PAIR, 1) * sin


def _rope_lanes(w, half):
    return jnp.concatenate([w[..., :half], w[..., 2 * half:ROPE_PAIR + half], w[..., half:2 * half],
                            w[..., ROPE_PAIR + half:]], axis=-1)


def _rope_tab_kernel(pos_ref, inv_ref, c_ref, s_ref):
    ang = pos_ref[...].astype(F32) * inv_ref[...]
    lane = lax.broadcasted_iota(jnp.int32, (1, LANES), 1)
    sn = jnp.sin(ang)
    c_ref[...] = jnp.cos(ang)
    s_ref[...] = jnp.where(lane < ROPE_PAIR, -sn, sn)


def _rope_tables(pos_flat, half):
    inv = ROPE_THETA ** (-jnp.arange(half, dtype=F32) / half)
    gap = jnp.zeros((ROPE_PAIR - half,), F32)
    inv = jnp.concatenate([inv, gap, inv, gap]).reshape(1, LANES)
    n = pos_flat.shape[0]
    rows = 2048
    tab = jax.ShapeDtypeStruct((n, LANES), F32)
    row_spec = pl.BlockSpec((rows, LANES), lambda i: (i, 0))
    return pl.pallas_call(
        _rope_tab_kernel,
        out_shape=(tab,) * 2,
        grid=(n // rows,),
        in_specs=[pl.BlockSpec((rows, 1), lambda i: (i, 0)), pl.BlockSpec((1, LANES), lambda i: (0, 0))],
        out_specs=(row_spec,) * 2,
        compiler_params=_cparams("parallel"),
        name="rope_tables",
    )(pos_flat.reshape(n, 1), inv)


FFN_SLOTS = 3


def _ffn_kernel(x_ref, g_ref, w13_hbm, w2_hbm, *refs, final, nj, tf, cast_next):
    i = pl.program_id(0)
    if cast_next:
        (n13_hbm, n2_hbm, o_ref, c13_hbm, c2_hbm, xn_ref, acc_ref, w1_buf, w3_buf, w2_buf, sems,
         a13, a2, b13, b2, csems) = refs
        r13, c13w = a13.shape[1:]
        r2 = a2.shape[1]

        def slab_in(j):
            s = j % 2
            return (pltpu.make_async_copy(n13_hbm.at[pl.ds(i * r13, r13), pl.ds(j * c13w, c13w)], a13.at[s],
                                          csems.at[0, s]),
                    pltpu.make_async_copy(n2_hbm.at[pl.ds((i * nj + j) * r2, r2), :], a2.at[s], csems.at[1, s]))

        def slab_out(j):
            s = j % 2
            return (pltpu.make_async_copy(b13.at[s], c13_hbm.at[pl.ds(i * r13, r13), pl.ds(j * c13w, c13w)],
                                          csems.at[2, s]),
                    pltpu.make_async_copy(b2.at[s], c2_hbm.at[pl.ds((i * nj + j) * r2, r2), :], csems.at[3, s]))

        for c in slab_in(0):
            c.start()
    else:
        o_ref, xn_ref, acc_ref, w1_buf, w3_buf, w2_buf, sems = refs

    def copies(j):
        slot = j % FFN_SLOTS
        return (pltpu.make_async_copy(w13_hbm.at[:, pl.ds(j * tf, tf)], w1_buf.at[slot], sems.at[0, slot]),
                pltpu.make_async_copy(w13_hbm.at[:, pl.ds((j + nj) * tf, tf)], w3_buf.at[slot], sems.at[1, slot]),
                pltpu.make_async_copy(w2_hbm.at[pl.ds(j * tf, tf), :], w2_buf.at[slot], sems.at[2, slot]))

    def start(j):
        for c in copies(j):
            c.start()

    @pl.when(i == 0)
    def _():
        start(0)

    xn_ref[...] = _rms(x_ref[...], g_ref[0:1, :]).astype(BF16)
    xn = xn_ref[...]
    for j in range(nj):
        slot = j % FFN_SLOTS
        if j + 1 < nj:
            start(j + 1)
        else:
            pl.when(i < pl.num_programs(0) - 1)(functools.partial(start, 0))
        for c in copies(j):
            c.wait()
        gate = _dot(xn, w1_buf[slot])
        up = _dot(xn, w3_buf[slot])
        act = (gate * (1.0 / (1.0 + jnp.exp(-gate))) * up).astype(BF16)
        part = _dot(act, w2_buf[slot])
        acc_ref[...] = part if j == 0 else acc_ref[...] + part
        if cast_next:
            s = j % 2
            for c in slab_in(j):
                c.wait()
            if j + 1 < nj:
                for c in slab_in(j + 1):
                    c.start()
            if j >= 2:
                for c in slab_out(j - 2):
                    c.wait()
            b13[s] = a13[s].astype(BF16)
            b2[s] = a2[s].astype(BF16)
            for c in slab_out(j):
                c.start()
    if cast_next:
        for j in (nj - 2, nj - 1):
            for c in slab_out(j):
                c.wait()
    y = x_ref[...] + HALF_STEP * acc_ref[...]
    if final:
        y = _rms(y, g_ref[1:2, :])
    o_ref[...] = y


def _ffn(h, gain, w13, w2, final_gain=None, next_f32=None):
    tm, tf = 512, 512
    nj = D_FF // tf
    ni = TOKENS // tm
    assert nj % FFN_SLOTS != 1, "tile 0 of the next row tile must not land in the last tile's slot"
    final = final_gain is not None
    cast_next = next_f32 is not None
    gains = jnp.stack([gain, final_gain if final else gain])
    any_spec = pl.BlockSpec(memory_space=pl.ANY)
    in_specs = [pl.BlockSpec((tm, D_MODEL), lambda i: (i, 0)), pl.BlockSpec((2, D_MODEL), lambda i: (0, 0)),
                any_spec, any_spec]
    out_shape = [jax.ShapeDtypeStruct((TOKENS, D_MODEL), F32)]
    out_specs = [pl.BlockSpec((tm, D_MODEL), lambda i: (i, 0))]
    scratch = [pltpu.VMEM((tm, D_MODEL), BF16), pltpu.VMEM((tm, D_MODEL), F32),
               pltpu.VMEM((FFN_SLOTS, D_MODEL, tf), BF16), pltpu.VMEM((FFN_SLOTS, D_MODEL, tf), BF16),
               pltpu.VMEM((FFN_SLOTS, tf, D_MODEL), BF16), pltpu.SemaphoreType.DMA((3, FFN_SLOTS))]
    args = [h, gains, w13, w2]
    if cast_next:
        slab13, slab2 = (D_MODEL // ni, 2 * D_FF // nj), (D_FF // (ni * nj), D_MODEL)
        in_specs += [any_spec, any_spec]
        out_specs += [any_spec, any_spec]
        out_shape += [jax.ShapeDtypeStruct(w.shape, BF16) for w in next_f32]
        scratch += [pltpu.VMEM((2,) + slab13, F32), pltpu.VMEM((2,) + slab2, F32),
                    pltpu.VMEM((2,) + slab13, BF16), pltpu.VMEM((2,) + slab2, BF16),
                    pltpu.SemaphoreType.DMA((4, 2))]
        args += list(next_f32)
    outs = pl.pallas_call(
        functools.partial(_ffn_kernel, final=final, nj=nj, tf=tf, cast_next=cast_next),
        out_shape=tuple(out_shape),
        grid=(ni,),
        in_specs=in_specs,
        out_specs=tuple(out_specs),
        scratch_shapes=scratch,
        compiler_params=_cparams("arbitrary"),
        name="ffn",
    )(*args)
    return outs if cast_next else outs[0]


def _out_proj_kernel(x_ref, w_ref, r_ref, o_ref):
    o_ref[...] = r_ref[...] + _dot(x_ref[...], w_ref[...])


def _out_proj(x, w, res):
    k = x.shape[1]
    tm = 512
    return pl.pallas_call(
        _out_proj_kernel,
        out_shape=jax.ShapeDtypeStruct((TOKENS, D_MODEL), F32),
        grid=(TOKENS // tm,),
        in_specs=[
            pl.BlockSpec((tm, k), lambda i: (i, 0)),
            pl.BlockSpec((k, D_MODEL), lambda i: (0, 0)),
            pl.BlockSpec((tm, D_MODEL), lambda i: (i, 0)),
        ],
        out_specs=pl.BlockSpec((tm, D_MODEL), lambda i: (i, 0)),
        compiler_params=_cparams("parallel"),
        name="out_proj",
    )(x, w, res)


def _norm_perm_a_kernel(x_ref, g_ref, *refs, tm):
    o_refs, xs_ref, mid_ref = refs[:-2], refs[-2], refs[-1]
    xn = _rms(x_ref[...], g_ref[...])
    nchunk = D_MODEL // LANES
    for k in range(nchunk):
        xs_ref[k] = xn[:, k * LANES:(k + 1) * LANES]
    dp = 1
    for (_, d), o_ref in zip(A_PATTERNS, o_refs):
        if d == 1:
            o_ref[...] = xn.astype(BF16)
            continue
        step = d // dp
        for c in range(d):
            for k in range(nchunk):
                rows = pl.ds(c // dp, tm // d, stride=step)
                piece = xs_ref[k, rows, :] if dp == 1 else mid_ref[k, c % dp, rows, :]
                o_ref[0, c, 0, :, k * LANES:(k + 1) * LANES] = piece.astype(BF16)
                if d == A_MID_DILATION:
                    mid_ref[k, c] = piece
        dp = d


def _norm_perm_a(h, gain):
    tm = 512
    per_tile = A_TILE // tm
    shapes, specs = [], []
    for _, d in A_PATTERNS:
        if d == 1:
            shapes.append(jax.ShapeDtypeStruct((TOKENS, D_MODEL), BF16))
            specs.append(pl.BlockSpec((tm, D_MODEL), lambda i: (i, 0)))
        else:
            shapes.append(jax.ShapeDtypeStruct((TOKENS // A_TILE, d, per_tile, tm // d, D_MODEL), BF16))
            specs.append(pl.BlockSpec((1, d, 1, tm // d, D_MODEL),
                                      lambda i: (i // per_tile, 0, i % per_tile, 0, 0)))
    outs = pl.pallas_call(
        functools.partial(_norm_perm_a_kernel, tm=tm),
        out_shape=tuple(shapes),
        grid=(TOKENS // tm,),
        in_specs=[pl.BlockSpec((tm, D_MODEL), lambda i: (i, 0)), pl.BlockSpec((1, D_MODEL), lambda i: (0, 0))],
        out_specs=tuple(specs),
        scratch_shapes=[pltpu.VMEM((D_MODEL // LANES, tm, LANES), F32),
                        pltpu.VMEM((D_MODEL // LANES, A_MID_DILATION, tm // A_MID_DILATION, LANES), F32)],
        compiler_params=_cparams("parallel"),
        name="norm_perm_a",
    )(h, gain.reshape(1, D_MODEL))
    return [o.reshape(TOKENS, D_MODEL) for o in outs]


def _qkv_w_kernel(w_ref, o_ref):
    part = pl.program_id(1) % 3

    @pl.when(part == 2)
    def _():
        o_ref[...] = w_ref[...].astype(BF16)

    @pl.when(part != 2)
    def _():
        lane = lax.broadcasted_iota(jnp.int32, (1, LANES), 1)
        stay = (lane < A_ROT_HALF) | (lane >= ROPE_PAIR + A_ROT_HALF)
        for hh in range(A_HEADS):
            sl = slice(hh * A_HEAD_DIM, (hh + 1) * A_HEAD_DIM)
            y = w_ref[:, sl]
            moved = jnp.where(lane < ROPE_PAIR, pltpu.roll(y, LANES - A_ROT_HALF, 1),
                              pltpu.roll(y, ROPE_PAIR - A_ROT_HALF, 1))
            o_ref[:, sl] = jnp.where(stay, y, moved).astype(BF16)


def _qkv_w(w_qkv):
    rows = 512
    hd = A_HEADS * A_HEAD_DIM
    spec = pl.BlockSpec((rows, hd), lambda i, j: (i, j))
    return pl.pallas_call(
        _qkv_w_kernel,
        out_shape=jax.ShapeDtypeStruct(w_qkv.shape, BF16),
        grid=(D_MODEL // rows, w_qkv.shape[1] // hd),
        in_specs=[spec],
        out_specs=spec,
        compiler_params=_cparams("parallel", "parallel"),
        name="qkv_w",
    )(w_qkv)


def _qkv_a_kernel(x_ref, w_ref, cos_ref, sin_ref, o_ref, *, tn):
    part = pl.program_id(1) // ((A_HEADS * A_HEAD_DIM) // tn)
    scale = jnp.where(part == 0, A_HEAD_DIM ** -0.5, 1.0).astype(F32)
    half = x_ref.shape[0] // 2
    for cc in range(tn // MXU_COLS):
        for rows in (slice(0, half), slice(half, 2 * half)):
            y = _dot(x_ref[rows, :], w_ref[:, cc * MXU_COLS:(cc + 1) * MXU_COLS])
            for hh in range(MXU_COLS // LANES):
                sl = slice(cc * MXU_COLS + hh * LANES, cc * MXU_COLS + (hh + 1) * LANES)
                y_h = y[:, hh * LANES:(hh + 1) * LANES]
                o_ref[rows, sl] = (_rope128(y_h, cos_ref[rows, :], sin_ref[rows, :]) * scale).astype(BF16)


def _qkv_a(xn, w_qkv, tabs, group):
    tm, tn = 1024, 2048
    cols = 3 * A_HEADS * A_HEAD_DIM
    nj = cols // tn
    ni = TOKENS // tm
    v_first = 2 * (A_HEADS * A_HEAD_DIM) // tn
    tab_spec = pl.BlockSpec((tm, LANES),
                            lambda i, j: (jnp.where(j >= v_first, len(A_PATTERNS) * ni, group * ni + i), 0))
    return pl.pallas_call(
        functools.partial(_qkv_a_kernel, tn=tn),
        out_shape=jax.ShapeDtypeStruct((TOKENS, cols), BF16),
        grid=(ni, nj),
        in_specs=[
            pl.BlockSpec((tm, D_MODEL), lambda i, j: (i, 0)),
            pl.BlockSpec((D_MODEL, tn), lambda i, j: (0, group * nj + j)),
            tab_spec, tab_spec,
        ],
        out_specs=pl.BlockSpec((tm, tn), lambda i, j: (i, j)),
        compiler_params=_cparams("parallel", "parallel"),
        name="qkv_a",
    )(xn, w_qkv, *tabs)


def _attn_a_kernel(q_ref, kc_ref, kp_ref, vc_ref, vp_ref, o_ref, lse_ref, *scratch, d, heads, per_trip):
    rows = A_TILE // d
    nblk = rows // A_BLOCK
    two_stage = d > A_MID_DILATION
    if two_stage:
        os_ref, mid_ref, lse_mid_ref = scratch
        stage_ref, lse_stage_ref, mid_step = mid_ref, lse_mid_ref, d // A_MID_DILATION
    else:
        (os_ref,) = scratch
        stage_ref, lse_stage_ref = os_ref, lse_ref
    first_tile = pl.program_id(1) == 0
    r = lax.broadcasted_iota(jnp.int32, (A_BLOCK, 2 * A_BLOCK), 0)
    kk = lax.broadcasted_iota(jnp.int32, (A_BLOCK, 2 * A_BLOCK), 1)
    band = (kk >= r) & (kk <= r + A_BLOCK)
    lane = lax.broadcasted_iota(jnp.int32, (A_BLOCK, LANES), 1)
    hs = [slice(h * A_HEAD_DIM, (h + 1) * A_HEAD_DIM) for h in range(heads)]
    first_head = pl.program_id(2) * heads

    @pl.when(first_head == 0)
    def _():
        lse_stage_ref[...] = jnp.zeros_like(lse_stage_ref)

    def trip(t, carry):
        loaded = []
        for i in range(per_trip):
            u = t * per_trip + i
            c = u // nblk
            blk = u % nblk
            start = pl.multiple_of(blk * A_BLOCK, A_BLOCK)
            before = pl.multiple_of(jnp.maximum(start - A_BLOCK, 0), A_BLOCK)
            q = q_ref[0, c, pl.ds(start, A_BLOCK), :]
            k = jnp.concatenate([jnp.where(blk == 0, kp_ref[0, c], kc_ref[0, c, pl.ds(before, A_BLOCK), :]),
                                 kc_ref[0, c, pl.ds(start, A_BLOCK), :]], axis=0)
            v = jnp.concatenate([jnp.where(blk == 0, vp_ref[0, c], vc_ref[0, c, pl.ds(before, A_BLOCK), :]),
                                 vc_ref[0, c, pl.ds(start, A_BLOCK), :]], axis=0)
            no_prev = jnp.logical_and(first_tile, blk == 0)
            mask = jnp.logical_and(band, jnp.logical_or(kk >= A_BLOCK, jnp.logical_not(no_prev)))
            out_rows = (pl.ds(blk * (A_BLOCK * d) + c, A_BLOCK, stride=d) if d > 1
                        else pl.ds(start, A_BLOCK))
            if two_stage:
                o_at = (c % A_MID_DILATION,
                        pl.ds(blk * (A_BLOCK * mid_step) + c // A_MID_DILATION, A_BLOCK, stride=mid_step))
            else:
                o_at = (out_rows,)
            loaded.append(([_dot_t(q[:, sl], k[:, sl]) for sl in hs], v, mask, out_rows, o_at))
        for scores, v, mask, out_rows, o_at in loaded:
            lse_tile = lse_stage_ref[o_at + (slice(None),)]
            for h, sl in enumerate(hs):
                s = jnp.where(mask, scores[h], NEG_INF)
                m = jnp.max(s, axis=-1, keepdims=True)
                p = jnp.exp(s - m)
                l = jnp.sum(p, axis=-1, keepdims=True)
                stage_ref[(h,) + o_at + (slice(None),)] = _dot(p.astype(BF16), v[:, sl]) / l
                lse_tile = jnp.where(lane == first_head + h, m + jnp.log(l), lse_tile)
            lse_stage_ref[o_at + (slice(None),)] = lse_tile
        return carry

    lax.fori_loop(0, d * nblk // per_trip, trip, 0)
    if two_stage:
        for h in range(heads):
            for c4 in range(A_MID_DILATION):
                os_ref[h, pl.ds(c4, A_TILE // A_MID_DILATION, stride=A_MID_DILATION), :] = mid_ref[h, c4]
        for c4 in range(A_MID_DILATION):
            lse_ref[pl.ds(c4, A_TILE // A_MID_DILATION, stride=A_MID_DILATION), :] = lse_mid_ref[c4]
    o_ref[...] = os_ref[...].astype(BF16)


def _attn_a(qkv, window, dilation):
    assert window // dilation == A_BLOCK, "band width must equal the query block"
    d = dilation
    heads = 4
    hw = heads * A_HEAD_DIM
    ngrp = A_HEADS // heads
    rows = A_TILE // d
    tiles = SEQ // A_TILE
    qkv_v = qkv.reshape(BATCH, tiles, d, rows, 3 * A_HEADS * A_HEAD_DIM)
    last_blk = rows // A_BLOCK - 1

    def cur(part):
        return pl.BlockSpec((None, 1, d, rows, hw), lambda b, n, g: (b, n, 0, 0, part * ngrp + g))

    def prev(part):
        return pl.BlockSpec((None, 1, d, A_BLOCK, hw),
                            lambda b, n, g: (b, jnp.maximum(n - 1, 0), 0, last_blk, part * ngrp + g))

    return pl.pallas_call(
        functools.partial(_attn_a_kernel, d=d, heads=heads, per_trip=4),
        out_shape=(jax.ShapeDtypeStruct((A_HEADS, TOKENS, A_HEAD_DIM), BF16),
                   jax.ShapeDtypeStruct((TOKENS, LANES), F32)),
        grid=(BATCH, tiles, ngrp),
        in_specs=[cur(0), cur(1), prev(1), cur(2), prev(2)],
        out_specs=(pl.BlockSpec((heads, A_TILE, A_HEAD_DIM), lambda b, n, g: (g, b * tiles + n, 0)),
                   pl.BlockSpec((A_TILE, LANES), lambda b, n, g: (b * tiles + n, 0))),
        scratch_shapes=[pltpu.VMEM((heads, A_TILE, A_HEAD_DIM), F32)] + (
            [pltpu.VMEM((heads, A_MID_DILATION, A_TILE // A_MID_DILATION, A_HEAD_DIM), F32),
             pltpu.VMEM((A_MID_DILATION, A_TILE // A_MID_DILATION, LANES), F32)]
            if d > A_MID_DILATION else []),
        compiler_params=_cparams("parallel", "parallel", "arbitrary"),
        name="attn_a_d%d" % d,
    )(qkv_v, qkv_v, qkv_v, qkv_v, qkv_v)


def _comb_proj_a_kernel(o0, o1, o2, l0, l1, l2, w_ref, r_ref, out_ref, comb_ref):
    s = pl.program_id(0)
    slot = s % 2

    @pl.when(s == 0)
    def _():
        comb_ref[1] = jnp.zeros(comb_ref.shape[1:], BF16)

    out_ref[...] = r_ref[...] + _dot(comb_ref[1 - slot], w_ref[...])
    a = [l0[...], l1[...], l2[...]]
    m = jnp.maximum(jnp.maximum(a[0], a[1]), a[2])
    e = [jnp.exp(t - m) for t in a]
    den = e[0] + e[1] + e[2]
    w = [t / den for t in e]
    rows = out_ref.shape[0]
    for h in range(A_HEADS):
        acc = jnp.zeros((rows, A_HEAD_DIM), F32)
        for wg, og in zip(w, (o0, o1, o2)):
            acc = acc + jnp.broadcast_to(wg[:, h:h + 1], (rows, A_HEAD_DIM)) * og[h]
        comb_ref[slot, :, h * A_HEAD_DIM:(h + 1) * A_HEAD_DIM] = acc.astype(BF16)


def _comb_proj_a(outs, lses, w_o, res):
    tm = 256
    nt = TOKENS // tm
    hd = A_HEADS * A_HEAD_DIM
    o_spec = pl.BlockSpec((A_HEADS, tm, A_HEAD_DIM), lambda s: (0, jnp.minimum(s, nt - 1), 0))
    l_spec = pl.BlockSpec((tm, LANES), lambda s: (jnp.minimum(s, nt - 1), 0))
    row_spec = pl.BlockSpec((tm, D_MODEL), lambda s: (jnp.maximum(s - 1, 0), 0))
    return pl.pallas_call(
        _comb_proj_a_kernel,
        out_shape=jax.ShapeDtypeStruct((TOKENS, D_MODEL), F32),
        grid=(nt + 1,),
        in_specs=[o_spec] * 3 + [l_spec] * 3 + [pl.BlockSpec((hd, D_MODEL), lambda s: (0, 0)), row_spec],
        out_specs=row_spec,
        scratch_shapes=[pltpu.VMEM((2, tm, hd), BF16)],
        compiler_params=_cparams("arbitrary"),
        name="comb_proj_a",
    )(*outs, *lses, w_o, res)


def _mixer_a(h, gain, w_qkv, w_o, tabs_a):
    outs, lses = [], []
    for g, ((window, dilation), xn) in enumerate(zip(A_PATTERNS, _norm_perm_a(h, gain))):
        o, lse = _attn_a(_qkv_a(xn, w_qkv, tabs_a, g), window, dilation)
        outs.append(o)
        lses.append(lse)
    return _comb_proj_a(outs, lses, w_o, h)


def _b_in_kernel(x_ref, g_ref, w_ref, o_ref):
    o_ref[...] = _dot(_rms(x_ref[...], g_ref[...]).astype(BF16), w_ref[...])


def _b_in(h, gain, w_in):
    tm = 512
    return pl.pallas_call(
        _b_in_kernel,
        out_shape=jax.ShapeDtypeStruct((TOKENS, B_IN_COLS), F32),
        grid=(TOKENS // tm,),
        in_specs=[
            pl.BlockSpec((tm, D_MODEL), lambda i: (i, 0)),
            pl.BlockSpec((1, D_MODEL), lambda i: (0, 0)),
            pl.BlockSpec((D_MODEL, B_IN_COLS), lambda i: (0, 0)),
        ],
        out_specs=pl.BlockSpec((tm, B_IN_COLS), lambda i: (i, 0)),
        compiler_params=_cparams("parallel"),
        name="b_in",
    )(h, gain.reshape(1, D_MODEL), w_in)


def _b_q_kernel(c_ref, g_ref, w_ref, cos_ref, sin_ref, o_ref, xn_ref, *, slots):
    @pl.when(pl.program_id(1) == 0)
    def _():
        xn_ref[...] = _rms(c_ref[...], g_ref[...]).astype(BF16)

    y = _dot(xn_ref[...], w_ref[...])
    scale = (B_NOPE_DIM + B_ROPE_DIM) ** -0.5
    cos, sin = cos_ref[...], sin_ref[...]
    for s in range(slots):
        nope = y[:, s * B_SLOT:s * B_SLOT + LANES]
        rot = _rope128(y[:, s * B_SLOT + LANES:(s + 1) * B_SLOT], cos, sin)
        o_ref[s, :, :LANES] = (nope * scale).astype(BF16)
        o_ref[s, :, LANES:] = (rot * scale).astype(BF16)


def _b_q(c, gain, w_q, tabs_b):
    tm, slots = 1024, 8
    tn = slots * B_SLOT
    tab_spec = pl.BlockSpec((tm, LANES), lambda i, j: (i, 0))
    return pl.pallas_call(
        functools.partial(_b_q_kernel, slots=slots),
        out_shape=jax.ShapeDtypeStruct((B_HEADS, TOKENS, B_SLOT), BF16),
        grid=(TOKENS // tm, B_HEADS // slots),
        in_specs=[
            pl.BlockSpec((tm, B_Q_RANK), lambda i, j: (i, 0)),
            pl.BlockSpec((1, B_Q_RANK), lambda i, j: (0, 0)),
            pl.BlockSpec((B_Q_RANK, tn), lambda i, j: (0, j)),
            tab_spec, tab_spec,
        ],
        out_specs=pl.BlockSpec((slots, tm, B_SLOT), lambda i, j: (j, i, 0)),
        scratch_shapes=[pltpu.VMEM((tm, B_Q_RANK), BF16)],
        compiler_params=_cparams("parallel", "arbitrary"),
        name="b_q",
    )(c, gain.reshape(1, B_Q_RANK), w_q, *tabs_b)


def _b_kv_kernel(c_ref, g_ref, wk_ref, wv_ref, cr_ref, cos_ref, sin_ref, k_ref, v_ref, *, slots):
    xn = _rms(c_ref[...], g_ref[...]).astype(BF16)
    yk = _dot(xn, wk_ref[...])
    yv = _dot(xn, wv_ref[...])
    kr = _rope128(cr_ref[...], cos_ref[...], sin_ref[...]).astype(BF16)
    lane = lax.broadcasted_iota(jnp.int32, (c_ref.shape[0], B_VX - B_V_DIM), 1)
    ones_col = jnp.where(lane == 0, 1.0, 0.0).astype(BF16)
    for s in range(slots):
        sl = slice(s * LANES, (s + 1) * LANES)
        k_ref[s, :, :LANES] = yk[:, sl].astype(BF16)
        k_ref[s, :, LANES:] = kr
        v_ref[s, :, :B_V_DIM] = yv[:, sl].astype(BF16)
        v_ref[s, :, B_V_DIM:] = ones_col


def _b_kv(c, gain, w_k, w_v, tabs_b):
    tm, slots = 1024, 8
    tn = slots * LANES
    kv_block = B_Q_RANK // B_KV_RANK
    rope_block = (B_Q_RANK + B_KV_RANK) // LANES
    tab_spec = pl.BlockSpec((tm, LANES), lambda i, j: (i, 0))
    return pl.pallas_call(
        functools.partial(_b_kv_kernel, slots=slots),
        out_shape=(jax.ShapeDtypeStruct((B_HEADS, TOKENS, B_SLOT), BF16),
                   jax.ShapeDtypeStruct((B_HEADS, TOKENS, B_VX), BF16)),
        grid=(TOKENS // tm, B_HEADS // slots),
        in_specs=[
            pl.BlockSpec((tm, B_KV_RANK), lambda i, j: (i, kv_block)),
            pl.BlockSpec((1, B_KV_RANK), lambda i, j: (0, 0)),
            pl.BlockSpec((B_KV_RANK, tn), lambda i, j: (0, j)),
            pl.BlockSpec((B_KV_RANK, tn), lambda i, j: (0, j)),
            pl.BlockSpec((tm, LANES), lambda i, j: (i, rope_block)),
            tab_spec, tab_spec,
        ],
        out_specs=(pl.BlockSpec((slots, tm, B_SLOT), lambda i, j: (j, i, 0)),
                   pl.BlockSpec((slots, tm, B_VX), lambda i, j: (j, i, 0))),
        compiler_params=_cparams("parallel", "parallel"),
        name="b_kv",
    )(c, gain.reshape(1, B_KV_RANK), w_k, w_v, c, *tabs_b)


def _attn_b_kernel(q_ref, k_ref, v_ref, o_ref, s0_ref, s1_ref, m_ref, acc_ref, *, tq, heads):
    qi = pl.program_id(2)
    m_ref[...] = jnp.full_like(m_ref, NEG_INF)
    acc_ref[...] = jnp.zeros_like(acc_ref)

    def scores(s_ref, ki):
        for h in range(heads):
            k = k_ref[h, pl.ds(pl.multiple_of(ki * tq, tq), tq), :]
            s_ref[h] = _dot_t(q_ref[h], k)

    def fold(s_ref, ki, diagonal):
        for h in range(heads):
            s = s_ref[h]
            if diagonal:
                row = lax.broadcasted_iota(jnp.int32, (tq, tq), 0)
                col = lax.broadcasted_iota(jnp.int32, (tq, tq), 1)
                s = jnp.where(row >= col, s, NEG_INF)
            m_prev = m_ref[h]
            m_new = jnp.maximum(m_prev, jnp.max(s, axis=-1, keepdims=True))
            alpha = jnp.exp(m_prev - m_new)
            p = jnp.exp((s - jnp.concatenate([m_new] * (tq // LANES), axis=1)).astype(BF16))
            v = v_ref[h, pl.ds(pl.multiple_of(ki * tq, tq), tq), :]
            acc_ref[h] = jnp.concatenate([alpha] * (B_VX // LANES), axis=1) * acc_ref[h] + _dot(p, v)
            m_ref[h] = m_new

    pairs = qi // 2
    scores(s0_ref, 0)

    def body(t, carry):
        scores(s1_ref, 2 * t + 1)
        fold(s0_ref, 2 * t, False)
        scores(s0_ref, 2 * t + 2)
        fold(s1_ref, 2 * t + 1, False)
        return carry

    lax.fori_loop(0, pairs, body, 0)

    @pl.when(qi % 2 == 1)
    def _():
        scores(s1_ref, qi)
        fold(s0_ref, qi - 1, False)
        fold(s1_ref, qi, True)

    @pl.when(qi % 2 == 0)
    def _():
        fold(s0_ref, qi, True)

    for h in range(heads):
        acc = acc_ref[h]
        l = jnp.broadcast_to(acc[:, B_V_DIM:B_V_DIM + 1], (tq, B_V_DIM))
        o_ref[:, h * B_V_DIM:(h + 1) * B_V_DIM] = (acc[:, :B_V_DIM] / l).astype(BF16)


def _attn_b(q3, k3, v3):
    tq, heads = 512, 2
    nq = SEQ // tq
    return pl.pallas_call(
        functools.partial(_attn_b_kernel, tq=tq, heads=heads),
        out_shape=jax.ShapeDtypeStruct((TOKENS, B_HEADS * B_V_DIM), BF16),
        grid=(BATCH, B_HEADS // heads, nq),
        in_specs=[
            pl.BlockSpec((heads, tq, B_SLOT), lambda b, g, qi: (g, b * nq + qi, 0)),
            pl.BlockSpec((heads, SEQ, B_SLOT), lambda b, g, qi: (g, b, 0)),
            pl.BlockSpec((heads, SEQ, B_VX), lambda b, g, qi: (g, b, 0)),
        ],
        out_specs=pl.BlockSpec((tq, heads * B_V_DIM), lambda b, g, qi: (b * nq + qi, g)),
        scratch_shapes=[pltpu.VMEM((heads, tq, tq), F32), pltpu.VMEM((heads, tq, tq), F32),
                        pltpu.VMEM((heads, tq, LANES), F32), pltpu.VMEM((heads, tq, B_VX), F32)],
        compiler_params=_cparams("parallel", "parallel", "arbitrary"),
        name="attn_b",
    )(q3, k3, v3)


def _mixer_b(h, gain, w_in, q_norm, w_q, kv_norm, w_k, w_v, w_o, tabs_b):
    c = _b_in(h, gain, w_in)
    q3 = _b_q(c, q_norm, w_q, tabs_b)
    k3, v3 = _b_kv(c, kv_norm, w_k, w_v, tabs_b)
    return _out_proj(_attn_b(q3, k3, v3), w_o, h)


def _c_in_kernel(x_ref, g_ref, wb_ref, wc_ref, wu_ref, cw_ref, o_ref, xn_ref, carry_ref, *, tm):
    i = pl.program_id(0)
    j = pl.program_id(1)

    @pl.when(j == 0)
    def _():
        xn_ref[...] = _rms(x_ref[...], g_ref[...]).astype(BF16)

    xn = xn_ref[...]
    first = (i * tm) % SEQ == 0
    for cc in range(o_ref.shape[1] // MXU_COLS):
        cs = slice(cc * MXU_COLS, (cc + 1) * MXU_COLS)
        b_gate = _dot(xn, wb_ref[:, cs])
        cu = _dot(xn, wc_ref[:, cs]) * _dot(xn, wu_ref[:, cs])
        w0, w1, w2 = cw_ref[0:1, cs], cw_ref[1:2, cs], cw_ref[2:3, cs]
        z = w2 * cu + w1 * pltpu.roll(cu, 1, 0) + w0 * pltpu.roll(cu, 2, 0)
        o_ref[:, cs] = (b_gate * z).astype(BF16)

        tail = jnp.where(first, 0.0, carry_ref[j, :, cs])
        both = jnp.concatenate([tail, cu[0:8, :]], axis=0)
        z8 = w2 * both[8:16] + w1 * both[7:15] + w0 * both[6:14]
        o_ref[0:8, cs] = (b_gate[0:8, :] * z8).astype(BF16)
        carry_ref[j, :, cs] = cu[tm - 8:tm, :]


def _c_in(h, gain, w_in, conv_w):
    tm, tn = 1024, 512
    nj = D_MODEL // tn
    return pl.pallas_call(
        functools.partial(_c_in_kernel, tm=tm),
        out_shape=jax.ShapeDtypeStruct((TOKENS, D_MODEL), BF16),
        grid=(TOKENS // tm, nj),
        in_specs=[
            pl.BlockSpec((tm, D_MODEL), lambda i, j: (i, 0)),
            pl.BlockSpec((1, D_MODEL), lambda i, j: (0, 0)),
            pl.BlockSpec((D_MODEL, tn), lambda i, j: (0, j)),
            pl.BlockSpec((D_MODEL, tn), lambda i, j: (0, j + nj)),
            pl.BlockSpec((D_MODEL, tn), lambda i, j: (0, j + 2 * nj)),
            pl.BlockSpec((C_CONV, tn), lambda i, j: (0, j)),
        ],
        out_specs=pl.BlockSpec((tm, tn), lambda i, j: (i, j)),
        scratch_shapes=[pltpu.VMEM((tm, D_MODEL), BF16), pltpu.VMEM((nj, 8, tn), F32)],
        compiler_params=_cparams("arbitrary", "arbitrary"),
        name="c_in",
    )(h, gain.reshape(1, D_MODEL), w_in, w_in, w_in, conv_w)


def _mixer_c(h, gain, w_in, conv_w, w_out):
    return _out_proj(_c_in(h, gain, w_in, conv_w), w_out, h)


def kernel(x, positions, l0_ffn1_norm, l0_ffn1_w13, l0_ffn1_w2, l0_mix_norm, l0_a_w_qkv, l0_a_w_o, l0_ffn2_norm, l0_ffn2_w13, l0_ffn2_w2, l1_ffn1_norm, l1_ffn1_w13, l1_ffn1_w2, l1_mix_norm, l1_b_w_in, l1_b_q_norm, l1_b_w_uq, l1_b_kv_norm, l1_b_w_ukv, l1_b_w_o, l1_ffn2_norm, l1_ffn2_w13, l1_ffn2_w2, l2_ffn1_norm, l2_ffn1_w13, l2_ffn1_w2, l2_mix_norm, l2_c_w_in, l2_c_conv_w, l2_c_w_out, l2_ffn2_norm, l2_ffn2_w13, l2_ffn2_w2, l3_ffn1_norm, l3_ffn1_w13, l3_ffn1_w2, l3_mix_norm, l3_a_w_qkv, l3_a_w_o, l3_ffn2_norm, l3_ffn2_w13, l3_ffn2_w2, final_norm):
    bf = lambda w: w.astype(BF16)
    pos_tiles = positions.reshape(TOKENS // A_TILE, A_TILE)
    pos_a = [pos_tiles.reshape(-1, A_TILE // d, d).swapaxes(1, 2).reshape(TOKENS) for _, d in A_PATTERNS]
    tabs_a = _rope_tables(jnp.concatenate(pos_a + [jnp.zeros((A_TILE,), positions.dtype)]), A_ROT_HALF)
    tabs_b = _rope_tables(positions.reshape(TOKENS), B_ROT_HALF)

    def rope_group(w):
        return _rope_lanes(jnp.pad(w, [(0, 0)] * (w.ndim - 1) + [(0, LANES - B_ROPE_DIM)]), B_ROT_HALF)

    w_uq = l1_b_w_uq.reshape(B_Q_RANK, B_HEADS, B_NOPE_DIM + B_ROPE_DIM)
    w_q = jnp.concatenate([w_uq[..., :B_NOPE_DIM], rope_group(w_uq[..., B_NOPE_DIM:])], axis=-1)
    w_q = bf(w_q.reshape(B_Q_RANK, B_HEADS * B_SLOT))
    w_ukv = l1_b_w_ukv.reshape(B_KV_RANK, B_HEADS, B_NOPE_DIM + B_V_DIM)
    w_k = bf(w_ukv[:, :, :B_NOPE_DIM].reshape(B_KV_RANK, B_HEADS * B_NOPE_DIM))
    w_v = bf(w_ukv[:, :, B_NOPE_DIM:].reshape(B_KV_RANK, B_HEADS * B_V_DIM))
    n_c = B_Q_RANK + B_KV_RANK
    w_b_in = bf(jnp.concatenate([l1_b_w_in[:, :n_c], rope_group(l1_b_w_in[:, n_c:])], axis=-1))

    h = x.reshape(TOKENS, D_MODEL)

    h, w13, w2 = _ffn(h, l0_ffn1_norm, bf(l0_ffn1_w13), bf(l0_ffn1_w2), next_f32=(l0_ffn2_w13, l0_ffn2_w2))
    h = _mixer_a(h, l0_mix_norm, _qkv_w(l0_a_w_qkv), bf(l0_a_w_o), tabs_a)
    h, w13, w2 = _ffn(h, l0_ffn2_norm, w13, w2, next_f32=(l1_ffn1_w13, l1_ffn1_w2))

    h, w13, w2 = _ffn(h, l1_ffn1_norm, w13, w2, next_f32=(l1_ffn2_w13, l1_ffn2_w2))
    h = _mixer_b(h, l1_mix_norm, w_b_in, l1_b_q_norm, w_q, l1_b_kv_norm, w_k, w_v, bf(l1_b_w_o), tabs_b)
    h, w13, w2 = _ffn(h, l1_ffn2_norm, w13, w2, next_f32=(l2_ffn1_w13, l2_ffn1_w2))

    h, w13, w2 = _ffn(h, l2_ffn1_norm, w13, w2, next_f32=(l2_ffn2_w13, l2_ffn2_w2))
    h = _mixer_c(h, l2_mix_norm, bf(l2_c_w_in), l2_c_conv_w, bf(l2_c_w_out))
    h, w13, w2 = _ffn(h, l2_ffn2_norm, w13, w2, next_f32=(l3_ffn1_w13, l3_ffn1_w2))

    h, w13, w2 = _ffn(h, l3_ffn1_norm, w13, w2, next_f32=(l3_ffn2_w13, l3_ffn2_w2))
    h = _mixer_a(h, l3_mix_norm, _qkv_w(l3_a_w_qkv), bf(l3_a_w_o), tabs_a)
    h = _ffn(h, l3_ffn2_norm, w13, w2, final_gain=final_norm)

    return h.reshape(BATCH, SEQ, D_MODEL)
```
